```python
import jax
import jax.numpy as jnp
from jax import lax
import numpy as np

D_MODEL = 1024
BATCH = 4
SEQ = 4096
DEPTH = 2
DEC_BATCH = 32
DEC_SEQ = 4
PAST_LEN = 8192
PAGE_SIZE = 128

HEAD_DIM = 64
A_HEADS = 8
A_WIDTH = A_HEADS * HEAD_DIM
A_PATTERNS = ((128, 1), (512, 4), (2048, 16))
A_WIN_MAX = 2048
B_HEADS = 4
B_DK = 64
B_DV = 128
B_WIDTH = B_HEADS * B_DV
B_CHUNK = 128
ROPE_BASE = 10000.0
GN_EPS = 1e-5
C_HEADS = 16
C_KV_HEADS = 2
C_WIDTH = C_HEADS * HEAD_DIM
C_KV_WIDTH = C_KV_HEADS * HEAD_DIM
CMP_LEN = 32
CMP_STRIDE = 16
CMP_HIDDEN = 256
SEL_BLOCK = 64
SEL_TOPK = 16
SEL_FORCE = 1e9
C_WINDOW = 512
Q_BLOCK = 128
N_EVEN = (DEPTH + 1) // 2
N_ODD = DEPTH // 2
EVEN_SPLITS = (A_WIDTH, A_WIDTH, A_WIDTH, A_WIDTH, B_HEADS * B_DK, B_HEADS * B_DK, B_WIDTH, B_WIDTH)
D_IN_EVEN = 4 * A_WIDTH + 2 * B_HEADS * B_DK + 2 * B_WIDTH
D_MIX_EVEN = A_WIDTH + B_WIDTH
ODD_SPLITS = (C_WIDTH, C_KV_WIDTH, C_KV_WIDTH, C_KV_WIDTH, C_KV_WIDTH, C_KV_WIDTH, C_KV_WIDTH, 3 * C_HEADS, C_WIDTH)
D_IN_ODD = 2 * C_WIDTH + 6 * C_KV_WIDTH + 3 * C_HEADS
D_MIX_ODD = C_WIDTH
RMS_EPS = 1e-6
NEG_INF = -1e30

kernel_name = 'hybrid_dilated_retention_nsa_step'


def rmsnorm(x, g):
    xf = x.astype(jnp.float32)
    y = xf * lax.rsqrt(jnp.mean(xf * xf, axis=-1, keepdims=True) + RMS_EPS)
    return (y * g.astype(jnp.float32)).astype(x.dtype)


def split_cols(h, sizes):
    return jnp.split(h, np.cumsum(sizes)[:-1].tolist(), axis=-1)


def masked_softmax(s, mask):
    s = jnp.where(mask, s, NEG_INF)
    m = jnp.max(s, axis=-1, keepdims=True)
    p = jnp.where(mask, jnp.exp(s - m), 0.0)
    l = jnp.sum(p, axis=-1, keepdims=True)
    l = jnp.where(l > 0, l, 1.0)
    return p / l, (m + jnp.log(l))[..., 0]


def banded_attention(q, k, v, lookback):
    n, t, h, hd = q.shape
    g = k.shape[2]
    r = h // g
    blk = Q_BLOCK
    n_prev = -(-lookback // blk)
    t_pad = -(-t // blk) * blk
    span = (n_prev + 1) * blk
    q = jnp.pad(q, ((0, 0), (0, t_pad - t), (0, 0), (0, 0)))
    kv_pad = ((0, 0), (n_prev * blk, t_pad - t), (0, 0), (0, 0))
    k = jnp.pad(k, kv_pad)
    v = jnp.pad(v, kv_pad)
    scale = hd ** -0.5

    def one_block(start):
        qb = lax.dynamic_slice_in_dim(q, start, blk, axis=1).reshape(n, blk, g, r, hd)
        kb = lax.dynamic_slice_in_dim(k, start, span, axis=1)
        vb = lax.dynamic_slice_in_dim(v, start, span, axis=1)
        qpos = start + jnp.arange(blk)
        kpos = start - n_prev * blk + jnp.arange(span)
        dist = qpos[:, None] - kpos[None, :]
        mask = (kpos[None, :] >= 0) & (dist >= 0) & (dist <= lookback)
        s = jnp.einsum('nqgrd,nkgd->ngrqk', qb, kb, preferred_element_type=jnp.float32) * scale
        p, lse = masked_softmax(s, mask)
        o = jnp.einsum('ngrqk,nkgd->nqgrd', p, vb.astype(jnp.float32))
        return o.reshape(n, blk, h, hd), lse.transpose(0, 3, 1, 2).reshape(n, blk, h)

    o, lse = lax.map(one_block, jnp.arange(t_pad // blk) * blk)
    o = o.transpose(1, 0, 2, 3, 4).reshape(n, t_pad, h, hd)[:, :t]
    lse = lse.transpose(1, 0, 2, 3).reshape(n, t_pad, h)[:, :t]
    return o, lse


def gathered_window_attention(q, k_all, v_all, n_past, lookback, dilation):
    n, s_len, h, hd = q.shape
    g = k_all.shape[2]
    r = h // g
    n_keys = lookback // dilation + 1
    idx = n_past + np.arange(s_len)[:, None] - dilation * np.arange(n_keys)[None, :]
    valid = jnp.asarray(idx >= 0)
    idx = np.maximum(idx, 0)
    kg = k_all[:, idx]
    vg = v_all[:, idx]
    s = jnp.einsum('nqgrd,nqkgd->ngrqk', q.reshape(n, s_len, g, r, hd), kg,
                   preferred_element_type=jnp.float32) * hd ** -0.5
    p, lse = masked_softmax(s, valid)
    o = jnp.einsum('ngrqk,nqkgd->nqgrd', p, vg.astype(jnp.float32))
    return o.reshape(n, s_len, h, hd), lse.transpose(0, 3, 1, 2).reshape(n, s_len, h)


def merge_by_denominators(outs, lses):
    w = jax.nn.softmax(jnp.stack(lses, axis=0), axis=0)
    return jnp.einsum('pnth,pnthd->nthd', w, jnp.stack(outs, axis=0))


def dilated_attention_prompt(q, k, v):
    b, t, h, hd = q.shape
    outs, lses = [], []
    for window, dil in A_PATTERNS:
        n_sub = t // dil

        def to_residue(u):
            return u.reshape(b, n_sub, dil, h, hd).transpose(0, 2, 1, 3, 4).reshape(b * dil, n_sub, h, hd)

        o, lse = banded_attention(to_residue(q), to_residue(k), to_residue(v), window // dil)
        outs.append(o.reshape(b, dil, n_sub, h, hd).transpose(0, 2, 1, 3, 4).reshape(b, t, h, hd))
        lses.append(lse.reshape(b, dil, n_sub, h).transpose(0, 2, 1, 3).reshape(b, t, h))
    return merge_by_denominators(outs, lses)


def dilated_attention_sample(q, k_all, v_all, n_past):
    outs, lses = [], []
    for window, dil in A_PATTERNS:
        o, lse = gathered_window_attention(q, k_all, v_all, n_past, window, dil)
        outs.append(o)
        lses.append(lse)
    return merge_by_denominators(outs, lses)


def rotary(x, pos):
    half = x.shape[-1] // 2
    freqs = ROPE_BASE ** (-jnp.arange(half, dtype=jnp.float32) / half)
    ang = pos.astype(jnp.float32)[:, None] * freqs[None, :]
    cos = jnp.cos(ang)[None, :, None, :]
    sin = jnp.sin(ang)[None, :, None, :]
    xf = x.astype(jnp.float32)
    x1, x2 = xf[..., :half], xf[..., half:]
    return jnp.concatenate([x1 * cos - x2 * sin, x1 * sin + x2 * cos], axis=-1)


def retention(q, k, v, s0):
    n, t, h, dk = q.shape
    dv = v.shape[-1]
    c = B_CHUNK if t % B_CHUNK == 0 else t
    nc = t // c
    log_g = jnp.log1p(-jnp.exp2(-5.0 - jnp.arange(h, dtype=jnp.float32)))
    i = jnp.arange(c, dtype=jnp.float32)
    diff = i[:, None] - i[None, :]
    intra = jnp.where(diff >= 0, jnp.exp(log_g[:, None, None] * jnp.maximum(diff, 0.0)), 0.0)
    q_decay = jnp.exp(log_g[None, :] * (i[:, None] + 1.0))[None, :, :, None]
    k_decay = jnp.exp(log_g[None, :] * (c - 1.0 - i[:, None]))[None, :, :, None]
    chunk_decay = jnp.exp(log_g * c)[None, :, None, None]

    def to_chunks(u):
        return u.astype(jnp.float32).reshape(n, nc, c, h, u.shape[-1]).transpose(1, 0, 2, 3, 4)

    def step(state, inp):
        qc, kc, vc = inp
        scores = jnp.einsum('nqhd,nkhd->nhqk', qc, kc) * intra
        o = jnp.einsum('nhqk,nkhe->nqhe', scores, vc) + jnp.einsum('nqhd,nhde->nqhe', qc, state) * q_decay
        state = state * chunk_decay + jnp.einsum('nkhd,nkhe->nhde', kc * k_decay, vc)
        return state, o

    state, o = lax.scan(step, s0.astype(jnp.float32), (to_chunks(q), to_chunks(k), to_chunks(v)))
    return o.transpose(1, 0, 2, 3, 4).reshape(n, t, h, dv), state


def head_norm(o):
    mu = jnp.mean(o, axis=-1, keepdims=True)
    var = jnp.mean(jnp.square(o - mu), axis=-1, keepdims=True)
    return (o - mu) * lax.rsqrt(var + GN_EPS)


def layer_output(x, branches, w_out):
    mixed = jnp.concatenate(branches, axis=-1).astype(x.dtype)
    return x + mixed @ w_out


def even_project(x, norm_g, w_in):
    b, t, _ = x.shape
    qa, ka, va, za, qb, kb, vb, zb = split_cols(rmsnorm(x, norm_g) @ w_in, EVEN_SPLITS)
    a_heads = lambda u: u.reshape(b, t, A_HEADS, HEAD_DIM)
    b_heads = lambda u: u.reshape(b, t, B_HEADS, -1)
    return a_heads(qa), a_heads(ka), a_heads(va), za, b_heads(qb), b_heads(kb), b_heads(vb), zb


def even_output(x, oa, ob, za, zb, w_out):
    b, t, _ = x.shape
    ga = oa.reshape(b, t, A_WIDTH) * jax.nn.silu(za.astype(jnp.float32))
    gb = head_norm(ob).reshape(b, t, B_WIDTH) * jax.nn.silu(zb.astype(jnp.float32))
    return layer_output(x, [ga, gb], w_out)


def even_layer_prompt(x, norm_g, w_in, w_out):
    b, t, _ = x.shape
    qa, ka, va, za, qb, kb, vb, zb = even_project(x, norm_g, w_in)
    oa = dilated_attention_prompt(qa, ka, va)
    pos = jnp.arange(t)
    s0 = jnp.zeros((b, B_HEADS, B_DK, B_DV), jnp.float32)
    ob, s_fin = retention(rotary(qb, pos), rotary(kb, pos) * B_DK ** -0.5, vb, s0)
    keep = min(A_WIN_MAX, t)
    return even_output(x, oa, ob, za, zb, w_out), ka[:, t - keep:], va[:, t - keep:], s_fin


def even_layer_sample(x, cache_k, cache_v, state, norm_g, w_in, w_out):
    b, s_len, _ = x.shape
    qa, ka, va, za, qb, kb, vb, zb = even_project(x, norm_g, w_in)
    n_past = cache_k.shape[1]
    oa = dilated_attention_sample(qa, jnp.concatenate([cache_k, ka], axis=1),
                                  jnp.concatenate([cache_v, va], axis=1), n_past)
    pos = PAST_LEN + jnp.arange(s_len)
    ob, s_new = retention(rotary(qb, pos), rotary(kb, pos) * B_DK ** -0.5, vb, state)
    return even_output(x, oa, ob, za, zb, w_out), ka, va, s_new


def odd_project(x, norm_g, w_in):
    b, t, _ = x.shape
    q, kc, vc, ks, vs, kw, vw, gl, z = split_cols(rmsnorm(x, norm_g) @ w_in, ODD_SPLITS)
    kv = lambda u: u.reshape(b, t, C_KV_HEADS, HEAD_DIM)
    return q.reshape(b, t, C_HEADS, HEAD_DIM), kv(kc), kv(vc), kv(ks), kv(vs), kv(kw), kv(vw), gl, z


def odd_output(x, o_cmp, o_sel, o_win, gl, z, w_out):
    b, t, _ = x.shape
    gates = jax.nn.sigmoid(gl.astype(jnp.float32)).reshape(b, t, C_HEADS, 3, 1)
    o = gates[:, :, :, 0] * o_cmp + gates[:, :, :, 1] * o_sel + gates[:, :, :, 2] * o_win
    return layer_output(x, [o.reshape(b, t, C_WIDTH) * jax.nn.silu(z.astype(jnp.float32))], w_out)


def compress_blocks(kv, pos_emb, w1, w2):
    n, t, g, hd = kv.shape
    per = CMP_LEN // CMP_STRIDE
    n_cmp = (t - CMP_LEN) // CMP_STRIDE + 1
    pieces = kv[:, :CMP_STRIDE * (n_cmp + per - 1)].reshape(n, n_cmp + per - 1, CMP_STRIDE, g, hd)
    blocks = jnp.concatenate([pieces[:, j:j + n_cmp] for j in range(per)], axis=2)
    hid = jax.nn.silu(jnp.einsum('nilgd,ldf->nigf', blocks + pos_emb[None, None, :, None, :],
                                 w1.reshape(CMP_LEN, hd, CMP_HIDDEN)))
    return jnp.einsum('nigf,fd->nigd', hid, w2)


def select_blocks(kv):
    n, t, g, hd = kv.shape
    n_slc = -(-t // SEL_BLOCK)
    kv = jnp.pad(kv, ((0, 0), (0, n_slc * SEL_BLOCK - t), (0, 0), (0, 0)))
    return kv.reshape(n, n_slc, SEL_BLOCK, g, hd).transpose(0, 3, 1, 2, 4)


def nsa_compressed_selected(q, qpos, kc, vc, ks_b, vs_b):
    n, nq, h, hd = q.shape
    g = kc.shape[2]
    r = h // g
    n_cmp = kc.shape[1]
    n_slc = ks_b.shape[2]
    scale = hd ** -0.5
    qg = q.reshape(n, nq, g, r, hd)
    c_end = jnp.arange(n_cmp) * CMP_STRIDE + CMP_LEN - 1
    c_mask = c_end[None, :] <= qpos[:, None]
    s = jnp.einsum('nqgrd,ncgd->ngrqc', qg, kc, preferred_element_type=jnp.float32) * scale
    p, _ = masked_softmax(s, c_mask)
    o_cmp = jnp.einsum('ngrqc,ncgd->nqgrd', p, vc.astype(jnp.float32)).reshape(n, nq, h, hd)
    per = CMP_LEN // CMP_STRIDE
    ratio = SEL_BLOCK // CMP_STRIDE
    j_np = np.arange(n_slc)[:, None]
    i_np = ratio * j_np - (per - 1) + np.arange(ratio + per - 1)[None, :]
    overlap = (np.minimum(i_np * CMP_STRIDE + CMP_LEN, (j_np + 1) * SEL_BLOCK)
               - np.maximum(i_np * CMP_STRIDE, j_np * SEL_BLOCK))
    w_ov = np.where((i_np >= 0) & (i_np < n_cmp), np.clip(overlap, 0, None) / CMP_LEN, 0.0).astype(np.float32)
    imp = jnp.sum(p, axis=2)
    imp_g = jnp.take(imp, np.clip(i_np, 0, n_cmp - 1).reshape(-1), axis=-1).reshape(n, g, nq, n_slc, -1)
    p_slc = jnp.einsum('ngqjo,jo->ngqj', imp_g, w_ov)
    j = jnp.arange(n_slc)
    cur = qpos // SEL_BLOCK
    forced = (j[None, :] == 0) | (j[None, :] == cur[:, None]) | (j[None, :] == cur[:, None] - 1)
    causal = j[None, :] * SEL_BLOCK <= qpos[:, None]
    score = jnp.where(forced, SEL_FORCE, jnp.where(causal, p_slc, -SEL_FORCE))
    top_s, top_i = lax.top_k(score, min(SEL_TOPK, n_slc))
    n_sel = top_i.shape[-1]
    ni = jnp.arange(n)[:, None, None, None]
    gi = jnp.arange(g)[None, :, None, None]
    kg = ks_b[ni, gi, top_i].reshape(n, g, nq, n_sel * SEL_BLOCK, hd)
    vg = vs_b[ni, gi, top_i].reshape(n, g, nq, n_sel * SEL_BLOCK, hd)
    kpos = (top_i[..., None] * SEL_BLOCK + jnp.arange(SEL_BLOCK)).reshape(n, g, nq, n_sel * SEL_BLOCK)
    mask = jnp.repeat(top_s > -0.5 * SEL_FORCE, SEL_BLOCK, axis=-1) & (kpos <= qpos[None, None, :, None])
    s2 = jnp.einsum('nqgrd,ngqkd->ngrqk', qg, kg, preferred_element_type=jnp.float32) * scale
    p2, _ = masked_softmax(s2, mask[:, :, None])
    o_sel = jnp.einsum('ngrqk,ngqkd->nqgrd', p2, vg.astype(jnp.float32)).reshape(n, nq, h, hd)
    return o_cmp, o_sel


def odd_layer_prompt(x, norm_g, w_in, w_out, pos_k, w1_k, w2_k, pos_v, w1_v, w2_v):
    b, t, _ = x.shape
    q, kc, vc, ks, vs, kw, vw, gl, z = odd_project(x, norm_g, w_in)
    kcb = compress_blocks(kc, pos_k, w1_k, w2_k)
    vcb = compress_blocks(vc, pos_v, w1_v, w2_v)
    ksb = select_blocks(ks)
    vsb = select_blocks(vs)
    nb = t // Q_BLOCK
    q_blocks = q.reshape(b, nb, Q_BLOCK, C_HEADS, HEAD_DIM).transpose(1, 0, 2, 3, 4)
    pos_blocks = jnp.arange(t).reshape(nb, Q_BLOCK)
    o_cmp, o_sel = lax.map(lambda a: nsa_compressed_selected(a[0], a[1], kcb, vcb, ksb, vsb),
                           (q_blocks, pos_blocks))
    unblock = lambda o: o.transpose(1, 0, 2, 3, 4).reshape(b, t, C_HEADS, HEAD_DIM)
    o_win, _ = banded_attention(q, kw, vw, C_WINDOW)
    keep = min(C_WINDOW, t)
    y = odd_output(x, unblock(o_cmp), unblock(o_sel), o_win, gl, z, w_out)
    return y, kc, vc, ks, vs, kw[:, t - keep:], vw[:, t - keep:]


def odd_layer_sample(x, page_table, pool_kc, pool_vc, pool_ks, pool_vs, buf_kw, buf_vw,
                     norm_g, w_in, w_out, pos_k, w1_k, w2_k, pos_v, w1_v, w2_v):
    b, s_len, _ = x.shape
    q, kc, vc, ks, vs, kw, vw, gl, z = odd_project(x, norm_g, w_in)
    n_past = page_table.shape[1] * PAGE_SIZE

    def with_past(pool, new):
        past = pool[page_table].reshape(b, n_past, C_KV_HEADS, HEAD_DIM)
        return jnp.concatenate([past, new], axis=1)

    kcb = compress_blocks(with_past(pool_kc, kc), pos_k, w1_k, w2_k)
    vcb = compress_blocks(with_past(pool_vc, vc), pos_v, w1_v, w2_v)
    ksb = select_blocks(with_past(pool_ks, ks))
    vsb = select_blocks(with_past(pool_vs, vs))
    qpos = n_past + jnp.arange(s_len)
    o_cmp, o_sel = nsa_compressed_selected(q, qpos, kcb, vcb, ksb, vsb)
    o_win, _ = gathered_window_attention(q, jnp.concatenate([buf_kw, kw], axis=1),
                                         jnp.concatenate([buf_vw, vw], axis=1), buf_kw.shape[1], C_WINDOW, 1)
    y = odd_output(x, o_cmp, o_sel, o_win, gl, z, w_out)
    return y, kc, vc, ks, vs, kw, vw


def setup_inputs(seed: int = 0) -> dict:
    key = jax.random.key(seed)
    ks = jax.random.split(key, 25)
    f32 = jnp.float32
    n_pages = PAST_LEN // PAGE_SIZE
    n_used = DEC_BATCH * n_pages
    n_pool = n_used + n_used // 4
    a_buf = min(A_WIN_MAX, PAST_LEN)
    c_buf = min(C_WINDOW, PAST_LEN)

    def rnd(k, shape, scale):
        return scale * jax.random.normal(k, shape, f32)

    page_table = jax.random.permutation(ks[11], n_pool)[:n_used].reshape(DEC_BATCH, n_pages).astype(jnp.int32)
    kv_page = (N_ODD, n_pool, PAGE_SIZE, C_KV_HEADS, HEAD_DIM)
    a_shape = (N_EVEN, DEC_BATCH, a_buf, A_HEADS, HEAD_DIM)
    w_shape = (N_ODD, DEC_BATCH, c_buf, C_KV_HEADS, HEAD_DIM)
    return {
        'x_prompt': rnd(ks[0], (BATCH, SEQ, D_MODEL), 1.0),
        'x_sample': rnd(ks[1], (DEC_BATCH, DEC_SEQ, D_MODEL), 1.0),
        'cache_a_k': rnd(ks[2], a_shape, 1.0),
        'cache_a_v': rnd(ks[3], a_shape, 1.0),
        'state_ret': rnd(ks[4], (N_EVEN, DEC_BATCH, B_HEADS, B_DK, B_DV), 0.5),
        'cache_c_kcmp': rnd(ks[5], kv_page, 1.0),
        'cache_c_vcmp': rnd(ks[6], kv_page, 1.0),
        'cache_c_ksel': rnd(ks[7], kv_page, 1.0),
        'cache_c_vsel': rnd(ks[8], kv_page, 1.0),
        'cache_c_kwin': rnd(ks[9], w_shape, 1.0),
        'cache_c_vwin': rnd(ks[10], w_shape, 1.0),
        'page_table': page_table,
        'even_norm': 1.0 + rnd(ks[12], (N_EVEN, D_MODEL), 0.05),
        'even_w_in': rnd(ks[13], (N_EVEN, D_MODEL, D_IN_EVEN), D_MODEL ** -0.5),
        'even_w_out': rnd(ks[14], (N_EVEN, D_MIX_EVEN, D_MODEL), D_MIX_EVEN ** -0.5),
        'odd_norm': 1.0 + rnd(ks[15], (N_ODD, D_MODEL), 0.05),
        'odd_w_in': rnd(ks[16], (N_ODD, D_MODEL, D_IN_ODD), D_MODEL ** -0.5),
        'odd_w_out': rnd(ks[17], (N_ODD, D_MIX_ODD, D_MODEL), D_MIX_ODD ** -0.5),
        'cmp_pos_k': rnd(ks[18], (N_ODD, CMP_LEN, HEAD_DIM), 0.5),
        'cmp_w1_k': rnd(ks[19], (N_ODD, CMP_LEN * HEAD_DIM, CMP_HIDDEN), (CMP_LEN * HEAD_DIM) ** -0.5),
        'cmp_w2_k': rnd(ks[20], (N_ODD, CMP_HIDDEN, HEAD_DIM), CMP_HIDDEN ** -0.5),
        'cmp_pos_v': rnd(ks[21], (N_ODD, CMP_LEN, HEAD_DIM), 0.5),
        'cmp_w1_v': rnd(ks[22], (N_ODD, CMP_LEN * HEAD_DIM, CMP_HIDDEN), (CMP_LEN * HEAD_DIM) ** -0.5),
        'cmp_w2_v': rnd(ks[23], (N_ODD, CMP_HIDDEN, HEAD_DIM), CMP_HIDDEN ** -0.5),
        'final_norm': 1.0 + rnd(ks[24], (D_MODEL,), 0.05),
    }


def reference(x_prompt, x_sample, cache_a_k, cache_a_v, state_ret, cache_c_kcmp, cache_c_vcmp,
              cache_c_ksel, cache_c_vsel, cache_c_kwin, cache_c_vwin, page_table,
              even_norm, even_w_in, even_w_out, odd_norm, odd_w_in, odd_w_out,
              cmp_pos_k, cmp_w1_k, cmp_w2_k, cmp_pos_v, cmp_w1_v, cmp_w2_v, final_norm):
    hp, hs = x_prompt, x_sample
    ak_p, ak_s, av_p, av_s, ret_p, ret_s = [], [], [], [], [], []
    kc_p, kc_s, vc_p, vc_s, ks_p, ks_s, vs_p, vs_s, kw_p, kw_s, vw_p, vw_s = ([] for _ in range(12))
    for layer in range(DEPTH):
        i = layer // 2
        if layer % 2 == 0:
            hp, k_new, v_new, st = even_layer_prompt(hp, even_norm[i], even_w_in[i], even_w_out[i])
            ak_p.append(k_new)
            av_p.append(v_new)
            ret_p.append(st)
            hs, k_new, v_new, st = even_layer_sample(hs, cache_a_k[i], cache_a_v[i], state_ret[i],
                                                     even_norm[i], even_w_in[i], even_w_out[i])
            ak_s.append(k_new)
            av_s.append(v_new)
            ret_s.append(st)
        else:
            cmp_args = (cmp_pos_k[i], cmp_w1_k[i], cmp_w2_k[i], cmp_pos_v[i], cmp_w1_v[i], cmp_w2_v[i])
            hp, *rows = odd_layer_prompt(hp, odd_norm[i], odd_w_in[i], odd_w_out[i], *cmp_args)
            for lst, row in zip((kc_p, vc_p, ks_p, vs_p, kw_p, vw_p), rows):
                lst.append(row)
            hs, *rows = odd_layer_sample(hs, page_table, cache_c_kcmp[i], cache_c_vcmp[i], cache_c_ksel[i],
                                         cache_c_vsel[i], cache_c_kwin[i], cache_c_vwin[i],
                                         odd_norm[i], odd_w_in[i], odd_w_out[i], *cmp_args)
            for lst, row in zip((kc_s, vc_s, ks_s, vs_s, kw_s, vw_s), rows):
                lst.append(row)
    y_prompt = rmsnorm(hp, final_norm)
    y_sample = rmsnorm(hs, final_norm)
    stk = lambda lst: jnp.stack(lst, axis=0)
    return (y_prompt, y_sample, stk(ak_p), stk(ak_s), stk(av_p), stk(av_s), stk(ret_p), stk(ret_s),
            stk(kc_p), stk(kc_s), stk(vc_p), stk(vc_s), stk(ks_p), stk(ks_s), stk(vs_p), stk(vs_s),
            stk(kw_p), stk(kw_s), stk(vw_p), stk(vw_s))
```

```python
import functools

import numpy as np
import jax
import jax.numpy as jnp
from jax import lax
from jax.experimental import pallas as pl
from jax.experimental.pallas import tpu as pltpu

F32 = jnp.float32
BF16 = jnp.bfloat16

HEAD_DIM = 64
A_HEADS = 8
A_WIDTH = A_HEADS * HEAD_DIM
A_PATTERNS = ((128, 1), (512, 4), (2048, 16))
B_HEADS = 4
B_DK = 64
B_DV = 128
B_WIDTH = B_HEADS * B_DV
B_CHUNK = 128
ROPE_BASE = 10000.0
GN_EPS = 1e-5
C_HEADS = 16
C_KV_HEADS = 2
C_WIDTH = C_HEADS * HEAD_DIM
C_KV_WIDTH = C_KV_HEADS * HEAD_DIM
CMP_LEN = 32
CMP_STRIDE = 16
CMP_HIDDEN = 256
SEL_BLOCK = 64
SEL_TOPK = 16
SEL_FORCE = 1e9
C_WINDOW = 512
PAGE_SIZE = 128
RMS_EPS = 1e-6
NEG_INF = -1e30

LANES = 128
ATTN_TQ = 128
VMEM_LIMIT = 56 * 1024 * 1024


def _params(*sem):
    return pltpu.CompilerParams(dimension_semantics=sem, vmem_limit_bytes=VMEM_LIMIT)


def _dot(a, b):
    return jnp.dot(a, b, preferred_element_type=F32)


def _dot_nt(a, b):
    return lax.dot_general(a, b, (((1,), (1,)), ((), ())), preferred_element_type=F32)


def _dot_tn(a, b):
    return lax.dot_general(a, b, (((0,), (0,)), ((), ())), preferred_element_type=F32)


def _idiv(x, n):
    assert n & (n - 1) == 0
    return jnp.right_shift(x, n.bit_length() - 1)


def _imod(x, n):
    assert n & (n - 1) == 0
    return jnp.bitwise_and(x, n - 1)


def _split3_dot(a, w):
    hi = a.astype(BF16)
    r1 = a - hi.astype(F32)
    mid = r1.astype(BF16)
    lo = (r1 - mid.astype(F32)).astype(BF16)
    return _dot(hi, w) + _dot(mid, w) + _dot(lo, w)


def _norm_proj_body(x_ref, g_ref, w_ref, *o_refs, outs):
    x = x_ref[...]
    y = x * lax.rsqrt(jnp.mean(x * x, axis=-1, keepdims=True) + RMS_EPS) * g_ref[...]
    yb = y.astype(BF16)
    done = {}
    for o_ref, (off, width, _) in zip(o_refs, outs):
        if (off, width) not in done:
            done[(off, width)] = _dot(yb, w_ref[:, off:off + width])
        o_ref[...] = done[(off, width)].astype(o_ref.dtype)


def _norm_proj(x2d, gain, w, outs, tm):
    m, d = x2d.shape
    n = w.shape[1]
    return pl.pallas_call(
        functools.partial(_norm_proj_body, outs=outs),
        grid=(m // tm,),
        in_specs=[pl.BlockSpec((tm, d), lambda i: (i, 0)),
                  pl.BlockSpec((1, d), lambda i: (0, 0)),
                  pl.BlockSpec((d, n), lambda i: (0, 0))],
        out_specs=[pl.BlockSpec((tm, width), lambda i: (i, 0)) for _, width, _ in outs],
        out_shape=[jax.ShapeDtypeStruct((m, width), dt) for _, width, dt in outs],
        compiler_params=_params("parallel"),
        name="norm_proj",
    )(x2d, gain.reshape(1, d), w)


def _banded_body(*refs, n_heads, n_groups, lookback, tqb, with_lse, gate_col):
    q_ref, k_ref, v_ref = refs[:3]
    refs = refs[3:]
    g_ref = None
    if gate_col is not None:
        g_ref, refs = refs[0], refs[1:]
    o_ref = refs[0]
    lse_ref = refs[1] if with_lse else None
    tq = ATTN_TQ
    n_prev = -(-lookback // tq)
    span = (n_prev + 1) * tq
    rep = n_heads // n_groups
    blk = pl.program_id(2)

    def sub(u, carry):
        row0 = pl.multiple_of(u * tq, tq)
        s0 = blk * tqb + row0
        kstart = pl.multiple_of(jnp.maximum(s0 - n_prev * tq, 0), tq)
        qpos = s0 + lax.broadcasted_iota(jnp.int32, (tq, span), 0)
        kpos = kstart + lax.broadcasted_iota(jnp.int32, (tq, span), 1)
        dist = qpos - kpos
        bias = jnp.where(dist >= 0, jnp.where(dist <= lookback, 0.0, NEG_INF), NEG_INF)
        if g_ref is not None:
            sig = jax.nn.sigmoid(g_ref[0, pl.ds(row0, tq), :])
        for g in range(n_groups):
            kg = k_ref[0, pl.ds(kstart, span), g * HEAD_DIM:(g + 1) * HEAD_DIM]
            vg = v_ref[0, pl.ds(kstart, span), g * HEAD_DIM:(g + 1) * HEAD_DIM]
            for r in range(rep):
                h = g * rep + r
                cols = slice(h * HEAD_DIM, (h + 1) * HEAD_DIM)
                qh = q_ref[0, pl.ds(row0, tq), cols] * (HEAD_DIM ** -0.5)
                s = _dot_nt(qh, kg) + bias
                m = jnp.max(s, axis=-1, keepdims=True)
                p = jnp.exp(s - m)
                l = jnp.sum(p, axis=-1, keepdims=True)
                o = _dot(p.astype(BF16), vg) / l
                if g_ref is not None:
                    c = gate_col(h)
                    o = o * sig[:, c:c + 1]
                o_ref[0, pl.ds(row0, tq), cols] = o
                if lse_ref is not None:
                    lse_ref[0, pl.ds(row0, tq), cols] = jnp.broadcast_to(m + jnp.log(l), (tq, HEAD_DIM))
        return carry

    lax.fori_loop(0, tqb // tq, sub, 0)


def _banded_attention(q, k, v, *, dil, n_heads, n_groups, lookback, tqb, with_lse=False, gate=None, gate_col=None):
    b, t, wq = q.shape
    wk = k.shape[-1]
    n_sub = t // dil
    assert n_sub % tqb == 0 and tqb % ATTN_TQ == 0
    assert n_sub >= (-(-lookback // ATTN_TQ) + 1) * ATTN_TQ
    args = [q.reshape(b, n_sub, dil * wq), k.reshape(b, n_sub, dil * wk), v.reshape(b, n_sub, dil * wk)]
    in_specs = [pl.BlockSpec((1, tqb, wq), lambda bi, r, i: (bi, i, r)),
                pl.BlockSpec((1, n_sub, wk), lambda bi, r, i: (bi, 0, r)),
                pl.BlockSpec((1, n_sub, wk), lambda bi, r, i: (bi, 0, r))]
    if gate is not None:
        assert dil == 1
        args.append(gate)
        in_specs.append(pl.BlockSpec((1, tqb, gate.shape[-1]), lambda bi, r, i: (bi, i, 0)))
    n_out = 2 if with_lse else 1
    res = pl.pallas_call(
        functools.partial(_banded_body, n_heads=n_heads, n_groups=n_groups, lookback=lookback, tqb=tqb,
                          with_lse=with_lse, gate_col=gate_col if gate is not None else None),
        grid=(b, dil, n_sub // tqb),
        in_specs=in_specs,
        out_specs=[pl.BlockSpec((1, tqb, wq), lambda bi, r, i: (bi, i, r))] * n_out,
        out_shape=[jax.ShapeDtypeStruct((b, n_sub, dil * wq), F32)] * n_out,
        compiler_params=_params("parallel", "parallel", "arbitrary"),
        name=f"banded_attention_d{dil}",
    )(*args)
    return [a.reshape(b, t, wq) for a in res]


def _retention_body(q_ref, k_ref, v_ref, cos_ref, sin_ref, qd_ref, kd_ref, intra_ref, cd_ref, s0_ref,
                    o_ref, sfin_ref, state):
    c = pl.program_id(1)

    @pl.when(c == 0)
    def _():
        state[...] = s0_ref[0]

    cos = cos_ref[...]
    sin = sin_ref[...]
    width = cos.shape[-1]
    half = B_DK // 2
    lane = _imod(lax.broadcasted_iota(jnp.int32, cos.shape, 1), B_DK)

    def rotary(x):
        fwd = pltpu.roll(x, half, 1)
        bwd = pltpu.roll(x, width - half, 1)
        return x * cos + jnp.where(lane < half, -bwd, fwd) * sin

    q = rotary(q_ref[0])
    k = rotary(k_ref[0]) * (B_DK ** -0.5)
    q_in = q.astype(BF16)
    k_in = k.astype(BF16)
    q_st = (q * qd_ref[...]).astype(BF16)
    k_st = (k * kd_ref[...]).astype(BF16)
    v = v_ref[0].astype(BF16)
    for h in range(B_HEADS):
        kc = slice(h * B_DK, (h + 1) * B_DK)
        vc = slice(h * B_DV, (h + 1) * B_DV)
        st = state[h]
        scores = _dot_nt(q_in[:, kc], k_in[:, kc]) * intra_ref[h]
        o_ref[0, :, vc] = _dot(scores.astype(BF16), v[:, vc]) + _dot(q_st[:, kc], st.astype(BF16))
        state[h] = st * cd_ref[h] + _dot_tn(k_st[:, kc], v[:, vc])

    @pl.when(c == pl.num_programs(1) - 1)
    def _():
        sfin_ref[0] = state[...]


def _retention(q, k, v, s0, pos, *, chunk, c_real):
    n, t, _ = q.shape
    half = B_DK // 2
    freqs = ROPE_BASE ** (-jnp.arange(half, dtype=F32) / half)
    ang = pos.astype(F32)[:, None] * freqs[None, :]
    cos = jnp.tile(jnp.cos(ang), (1, 2 * B_HEADS))
    sin = jnp.tile(jnp.sin(ang), (1, 2 * B_HEADS))
    log_g = jnp.log1p(-jnp.exp2(-5.0 - jnp.arange(B_HEADS, dtype=F32)))
    i = jnp.arange(chunk, dtype=F32)
    diff = i[:, None] - i[None, :]
    intra = jnp.where(diff >= 0, jnp.exp(log_g[:, None, None] * jnp.maximum(diff, 0.0)), 0.0)
    q_decay = jnp.repeat(jnp.exp(log_g[None, :] * (i[:, None] + 1.0)), B_DK, axis=1)
    k_decay = jnp.repeat(jnp.exp(log_g[None, :] * (c_real - 1.0 - i[:, None])), B_DK, axis=1)
    chunk_decay = jnp.broadcast_to(jnp.exp(log_g * c_real)[:, None, None], (B_HEADS, 1, B_DV))
    wqk = B_HEADS * B_DK
    return pl.pallas_call(
        _retention_body,
        grid=(n, t // chunk),
        in_specs=[pl.BlockSpec((1, chunk, wqk), lambda b, c: (b, c, 0)),
                  pl.BlockSpec((1, chunk, wqk), lambda b, c: (b, c, 0)),
                  pl.BlockSpec((1, chunk, B_WIDTH), lambda b, c: (b, c, 0)),
                  pl.BlockSpec((chunk, wqk), lambda b, c: (c, 0)),
                  pl.BlockSpec((chunk, wqk), lambda b, c: (c, 0)),
                  pl.BlockSpec((chunk, wqk), lambda b, c: (0, 0)),
                  pl.BlockSpec((chunk, wqk), lambda b, c: (0, 0)),
                  pl.BlockSpec((B_HEADS, chunk, chunk), lambda b, c: (0, 0, 0)),
                  pl.BlockSpec((B_HEADS, 1, B_DV), lambda b, c: (0, 0, 0)),
                  pl.BlockSpec((1, B_HEADS, B_DK, B_DV), lambda b, c: (b, 0, 0, 0))],
        out_specs=[pl.BlockSpec((1, chunk, B_WIDTH), lambda b, c: (b, c, 0)),
                   pl.BlockSpec((1, B_HEADS, B_DK, B_DV), lambda b, c: (b, 0, 0, 0))],
        out_shape=[jax.ShapeDtypeStruct((n, t, B_WIDTH), F32),
                   jax.ShapeDtypeStruct((n, B_HEADS, B_DK, B_DV), F32)],
        scratch_shapes=[pltpu.VMEM((B_HEADS, B_DK, B_DV), F32)],
        compiler_params=_params("parallel", "arbitrary"),
        name="retention",
    )(q, k, v, cos, sin, q_decay, k_decay, intra, chunk_decay, s0)


def _silu(z):
    return z * jax.nn.sigmoid(z)


def _even_out_body(x_ref, o1, l1, o2, l2, o3, l3, za_ref, ob_ref, zb_ref, w_ref, y_ref):
    m = jnp.maximum(jnp.maximum(l1[...], l2[...]), l3[...])
    e1 = jnp.exp(l1[...] - m)
    e2 = jnp.exp(l2[...] - m)
    e3 = jnp.exp(l3[...] - m)
    oa = (e1 * o1[...] + e2 * o2[...] + e3 * o3[...]) / (e1 + e2 + e3)
    ga = (oa * _silu(za_ref[...])).astype(BF16)
    acc = x_ref[...] + _dot(ga, w_ref[0:A_WIDTH, :])
    ob = ob_ref[...]
    zb = zb_ref[...]
    for h in range(B_HEADS):
        cols = slice(h * B_DV, (h + 1) * B_DV)
        seg = ob[:, cols]
        mu = jnp.mean(seg, axis=-1, keepdims=True)
        cen = seg - mu
        var = jnp.mean(cen * cen, axis=-1, keepdims=True)
        gb = (cen * lax.rsqrt(var + GN_EPS) * _silu(zb[:, cols])).astype(BF16)
        acc = acc + _dot(gb, w_ref[A_WIDTH + h * B_DV:A_WIDTH + (h + 1) * B_DV, :])
    y_ref[...] = acc


def _even_out(x2d, pats, za, ob, zb, w_out, tm):
    m, d = x2d.shape
    row = lambda width: pl.BlockSpec((tm, width), lambda i: (i, 0))
    flat = [a for pair in pats for a in pair]
    return pl.pallas_call(
        _even_out_body,
        grid=(m // tm,),
        in_specs=[row(d)] + [row(A_WIDTH)] * 6 + [row(A_WIDTH), row(B_WIDTH), row(B_WIDTH),
                                                   pl.BlockSpec(w_out.shape, lambda i: (0, 0))],
        out_specs=row(d),
        out_shape=jax.ShapeDtypeStruct((m, d), F32),
        compiler_params=_params("parallel"),
        name="even_out",
    )(x2d, *flat, za, ob, zb, w_out)


def _compress_math(x, p2_ref, wbig_ref, w2_ref):
    n_piece = x.shape[0]
    a = _dot(x.astype(BF16), wbig_ref[...])
    pc = _dot(p2_ref[...], wbig_ref[...])
    hid = []
    for g in range(C_KV_HEADS):
        lo = slice(g * 2 * CMP_HIDDEN, g * 2 * CMP_HIDDEN + CMP_HIDDEN)
        hi = slice(g * 2 * CMP_HIDDEN + CMP_HIDDEN, (g + 1) * 2 * CMP_HIDDEN)
        nxt = pltpu.roll(a[:, hi], n_piece - 1, 0)
        pos = (pc[0:1, lo] + pc[2:3, lo]) + (pc[1:2, hi] + pc[3:4, hi])
        hid.append(_silu(a[:, lo] + nxt + pos))
    return _dot(jnp.concatenate(hid, axis=-1).astype(BF16), w2_ref[...])


def _compress_body(x_ref, p2_ref, wbig_ref, w2_ref, o_ref):
    o_ref[0] = _compress_math(x_ref[0], p2_ref, wbig_ref, w2_ref)


def _compress_weights(pos_emb, w1, w2):
    per = CMP_LEN // CMP_STRIDE
    w1r = w1.reshape(per, CMP_STRIDE, HEAD_DIM, CMP_HIDDEN)
    eye = jnp.eye(C_KV_HEADS, dtype=w1.dtype)
    wbig = jnp.einsum('jldf,gh->lgdhjf', w1r, eye).reshape(CMP_STRIDE * C_KV_WIDTH, C_KV_HEADS * per * CMP_HIDDEN)
    w2bd = jnp.einsum('fd,gh->gfhd', w2, eye).reshape(C_KV_HEADS * CMP_HIDDEN, C_KV_WIDTH)
    halves = jnp.tile(pos_emb.reshape(per, CMP_STRIDE, 1, HEAD_DIM), (1, 1, C_KV_HEADS, 1)).reshape(per, -1)
    hi = halves.astype(BF16)
    lo = (halves - hi.astype(F32)).astype(BF16)
    p2 = jnp.concatenate([hi, lo, jnp.zeros((16 - 2 * per, halves.shape[1]), BF16)], axis=0)
    return p2, wbig.astype(BF16), w2bd.astype(BF16)


def _compress(pieces, cw):
    n, n_piece, width = pieces.shape
    p2, wbig, w2bd = cw
    const = lambda a: pl.BlockSpec(a.shape, lambda i: (0,) * a.ndim)
    return pl.pallas_call(
        _compress_body,
        grid=(n,),
        in_specs=[pl.BlockSpec((1, n_piece, width), lambda i: (i, 0, 0)), const(p2), const(wbig), const(w2bd)],
        out_specs=pl.BlockSpec((1, n_piece, C_KV_WIDTH), lambda i: (i, 0, 0)),
        out_shape=jax.ShapeDtypeStruct((n, n_piece, C_KV_WIDTH), F32),
        compiler_params=_params("parallel"),
        name="compress",
    )(pieces, p2, wbig, w2bd)


def _gather_dma(pt_ref, pool_ref, buf_ref, sem_ref, n, slot, n_pages, rows, start):
    for p in range(n_pages):
        page = pt_ref[n, p] if start else 0
        cp = pltpu.make_async_copy(pool_ref.at[page], buf_ref.at[slot, pl.ds(p * rows, rows)], sem_ref.at[slot])
        if start:
            cp.start()
        else:
            cp.wait()


def _compress_paged_body(pt_ref, pool_ref, p2_ref, wbig_ref, w2_ref, o_ref, buf, sem, *, n_pages, rows):
    n = pl.program_id(0)
    slot = n % 2

    @pl.when(n == 0)
    def _():
        _gather_dma(pt_ref, pool_ref, buf, sem, 0, 0, n_pages, rows, True)

    @pl.when(n + 1 < pl.num_programs(0))
    def _():
        _gather_dma(pt_ref, pool_ref, buf, sem, n + 1, 1 - slot, n_pages, rows, True)

    _gather_dma(pt_ref, pool_ref, buf, sem, n, slot, n_pages, rows, False)
    o_ref[0] = _compress_math(buf[slot], p2_ref, wbig_ref, w2_ref)


def _compress_paged(pool, page_table, cw):
    n, n_pages = page_table.shape
    rows = PAGE_SIZE // CMP_STRIDE
    width = CMP_STRIDE * C_KV_WIDTH
    n_piece = n_pages * rows
    p2, wbig, w2bd = cw
    const = lambda a: pl.BlockSpec(a.shape, lambda i, pt: (0,) * a.ndim)
    return pl.pallas_call(
        functools.partial(_compress_paged_body, n_pages=n_pages, rows=rows),
        grid_spec=pltpu.PrefetchScalarGridSpec(
            num_scalar_prefetch=1,
            grid=(n,),
            in_specs=[pl.BlockSpec(memory_space=pl.ANY), const(p2), const(wbig), const(w2bd)],
            out_specs=pl.BlockSpec((1, n_piece, C_KV_WIDTH), lambda i, pt: (i, 0, 0)),
            scratch_shapes=[pltpu.VMEM((2, n_piece, width), F32), pltpu.SemaphoreType.DMA((2,))]),
        out_shape=jax.ShapeDtypeStruct((n, n_piece, C_KV_WIDTH), F32),
        compiler_params=_params("arbitrary"),
        name="compress_paged",
    )(page_table, pool.reshape(pool.shape[0], rows, width), p2, wbig, w2bd)


def _overlap_weights(n_cmp, n_cmp_pad, n_slc, n_slc_pad):
    i = np.arange(n_cmp_pad)[:, None]
    j = np.arange(n_slc_pad)[None, :]
    ov = (np.minimum(i * CMP_STRIDE + CMP_LEN, (j + 1) * SEL_BLOCK) - np.maximum(i * CMP_STRIDE, j * SEL_BLOCK))
    w = np.where((i < n_cmp) & (j < n_slc), np.clip(ov, 0, None) / CMP_LEN, 0.0)
    return jnp.asarray(w, dtype=BF16)


def _select_blocks(p_slc, qpos, n_slc):
    shape = p_slc.shape
    lane = lax.broadcasted_iota(jnp.int32, shape, 1)
    cur = _idiv(qpos, SEL_BLOCK)
    forced = (lane == 0) | (lane == cur) | (lane == cur - 1)
    causal = lane * SEL_BLOCK <= qpos
    score = jnp.where(forced, SEL_FORCE, jnp.where(causal, p_slc, -SEL_FORCE))
    score = jnp.where(lane < n_slc, score, -3.0 * SEL_FORCE)
    lane_f = lane.astype(F32)
    sel = jnp.zeros(shape, F32)
    for _ in range(min(SEL_TOPK, n_slc)):
        m = jnp.max(score, axis=-1, keepdims=True)
        first = jnp.min(jnp.where(score == m, lane_f, 1e9), axis=-1, keepdims=True)
        hit = lane_f == first
        sel = jnp.where(hit, jnp.where(m > -0.5 * SEL_FORCE, 1.0, 0.0), sel)
        score = jnp.where(hit, -4.0 * SEL_FORCE, score)
    return sel


def _nsa_prompt_body(q_ref, kc_ref, vc_ref, ks_ref, vs_ref, g_ref, wov_ref, o_ref, m_sc, l_sc, acc_sc,
                     *, n_cmp, n_slc, tk):
    tq = ATTN_TQ
    rep = C_HEADS // C_KV_HEADS
    rows = rep * tq
    q0 = pl.program_id(1) * tq
    n_cmp_pad = kc_ref.shape[1]
    sig = jax.nn.sigmoid(g_ref[0])
    qpos = q0 + lax.broadcasted_iota(jnp.int32, (tq, 1), 0)
    cmp_i = lax.broadcasted_iota(jnp.int32, (tq, n_cmp_pad), 1)
    cmp_ok = (cmp_i * CMP_STRIDE + (CMP_LEN - 1) <= qpos) & (cmp_i < n_cmp)
    cmp_bias = jnp.where(cmp_ok, 0.0, NEG_INF)
    cmp_keep = jnp.where(cmp_ok, 1.0, 0.0)
    n_kt = (q0 + tq + tk - 1) // tk
    for g in range(C_KV_HEADS):
        gc = slice(g * HEAD_DIM, (g + 1) * HEAD_DIM)
        qg = jnp.concatenate([q_ref[0, :, (g * rep + r) * HEAD_DIM:(g * rep + r + 1) * HEAD_DIM]
                              for r in range(rep)], axis=0) * (HEAD_DIM ** -0.5)
        s = _dot_nt(qg, kc_ref[0, :, gc].astype(BF16)).reshape(rep, tq, n_cmp_pad) + cmp_bias[None]
        m = jnp.max(s, axis=-1, keepdims=True)
        p = jnp.exp(s - m) * cmp_keep[None]
        l = jnp.sum(p, axis=-1, keepdims=True)
        p = p / jnp.where(l > 0, l, 1.0)
        o_cmp = _dot(p.reshape(rows, n_cmp_pad).astype(BF16), vc_ref[0, :, gc].astype(BF16))
        p_slc = _split3_dot(jnp.sum(p, axis=0), wov_ref[...])
        sel = _select_blocks(p_slc, qpos, n_slc).astype(BF16)
        m_sc[...] = jnp.full(m_sc.shape, NEG_INF, F32)
        l_sc[...] = jnp.zeros(l_sc.shape, F32)
        acc_sc[...] = jnp.zeros(acc_sc.shape, F32)

        def key_tile(kt, carry):
            k0 = pl.multiple_of(kt * tk, tk)
            blk_of_key = _idiv(k0 + lax.broadcasted_iota(jnp.int32, (sel.shape[1], tk), 1), SEL_BLOCK)
            expand = jnp.where(blk_of_key == lax.broadcasted_iota(jnp.int32, (sel.shape[1], tk), 0), 1.0, 0.0)
            picked = _dot(sel, expand.astype(BF16))
            kpos = k0 + lax.broadcasted_iota(jnp.int32, (tq, tk), 1)
            bias = jnp.where(picked > 0.5, jnp.where(kpos <= qpos, 0.0, NEG_INF), NEG_INF)
            s2 = _dot_nt(qg, ks_ref[0, pl.ds(k0, tk), gc]).reshape(rep, tq, tk) + bias[None]
            m_old = m_sc[...]
            m_new = jnp.maximum(m_old, jnp.max(s2, axis=-1, keepdims=True))
            alpha = jnp.exp(m_old - m_new)
            p2 = jnp.exp(s2 - m_new)
            l_sc[...] = alpha * l_sc[...] + jnp.sum(p2, axis=-1, keepdims=True)
            pv = _dot(p2.reshape(rows, tk).astype(BF16), vs_ref[0, pl.ds(k0, tk), gc])
            acc_sc[...] = alpha * acc_sc[...] + pv.reshape(rep, tq, HEAD_DIM)
            m_sc[...] = m_new
            return carry

        lax.fori_loop(0, n_kt, key_tile, 0)
        o_sel = acc_sc[...] / l_sc[...]
        for r in range(rep):
            h = g * rep + r
            o = (sig[:, 3 * h:3 * h + 1] * o_cmp[r * tq:(r + 1) * tq]
                 + sig[:, 3 * h + 1:3 * h + 2] * o_sel[r])
            o_ref[0, :, h * HEAD_DIM:(h + 1) * HEAD_DIM] = o


def _nsa_prompt(q, kcb, vcb, ks, vs, gates, n_cmp):
    b, t, wq = q.shape
    n_cmp_pad = kcb.shape[1]
    n_slc = -(-t // SEL_BLOCK)
    tk = min(512, t)
    assert t % tk == 0 and t % SEL_BLOCK == 0 and n_slc <= LANES
    wov = _overlap_weights(n_cmp, n_cmp_pad, n_slc, LANES)
    rep = C_HEADS // C_KV_HEADS
    whole = lambda a: pl.BlockSpec((1,) + a.shape[1:], lambda bi, i: (bi, 0, 0))
    return pl.pallas_call(
        functools.partial(_nsa_prompt_body, n_cmp=n_cmp, n_slc=n_slc, tk=tk),
        grid=(b, t // ATTN_TQ),
        in_specs=[pl.BlockSpec((1, ATTN_TQ, wq), lambda bi, i: (bi, i, 0)),
                  whole(kcb), whole(vcb), whole(ks), whole(vs),
                  pl.BlockSpec((1, ATTN_TQ, gates.shape[-1]), lambda bi, i: (bi, i, 0)),
                  pl.BlockSpec(wov.shape, lambda bi, i: (0, 0))],
        out_specs=pl.BlockSpec((1, ATTN_TQ, wq), lambda bi, i: (bi, i, 0)),
        out_shape=jax.ShapeDtypeStruct((b, t, wq), F32),
        scratch_shapes=[pltpu.VMEM((rep, ATTN_TQ, 1), F32), pltpu.VMEM((rep, ATTN_TQ, 1), F32),
                        pltpu.VMEM((rep, ATTN_TQ, HEAD_DIM), F32)],
        compiler_params=_params("parallel", "arbitrary"),
        name="nsa_prompt",
    )(q, kcb, vcb, ks, vs, gates, wov)


def _block_diag(q, blk_of_row, n_blk):
    return jnp.concatenate([jnp.where(blk_of_row == b, q, 0.0) for b in range(n_blk)], axis=-1)


def _diag_blocks(r, blk_of_row, n_blk):
    out = jnp.where(blk_of_row == 0, r[:, 0:HEAD_DIM], 0.0)
    for b in range(1, n_blk):
        out = out + jnp.where(blk_of_row == b, r[:, b * HEAD_DIM:(b + 1) * HEAD_DIM], 0.0)
    return out


def _row_ids(n_rows, n_heads, n_blk):
    row = lax.broadcasted_iota(jnp.int32, (n_rows, 1), 0)
    step = _idiv(row, n_heads)
    blk = _idiv(_imod(row, n_heads), n_heads // n_blk)
    return step, blk


def _pattern_weight(dist, patterns):
    w = jnp.zeros(dist.shape, F32)
    for window, dil in patterns:
        w = w + jnp.where(_imod(dist, dil) == 0, jnp.where(dist <= window, 1.0, 0.0), 0.0)
    return jnp.where(dist >= 0, w, 0.0)


def _decode_body(*refs, n_heads, n_blk, patterns, gate_col):
    q_ref, kc_ref, vc_ref, kn_ref, vn_ref = refs[:5]
    g_ref = refs[5] if gate_col is not None else None
    o_ref = refs[-1]
    n_rows = q_ref.shape[1]
    n_cache = kc_ref.shape[1]
    n_new = kn_ref.shape[1]
    step, blk = _row_ids(n_rows, n_heads, n_blk)
    qbd = _block_diag(q_ref[0] * (HEAD_DIM ** -0.5), blk, n_blk)
    s_c = _dot_nt(qbd.astype(BF16), kc_ref[0].astype(BF16))
    w_c = _pattern_weight(n_cache + step - lax.broadcasted_iota(jnp.int32, (n_rows, n_cache), 1), patterns)
    s_c = jnp.where(w_c > 0, s_c, NEG_INF)
    kn = kn_ref[0]
    vn = vn_ref[0]
    s_n, w_n = [], []
    for c in range(n_new):
        s = jnp.sum(qbd * kn[c:c + 1, :], axis=-1, keepdims=True)
        w = _pattern_weight(step - c, patterns)
        s_n.append(jnp.where(w > 0, s, NEG_INF))
        w_n.append(w)
    m = jnp.max(s_c, axis=-1, keepdims=True)
    for s in s_n:
        m = jnp.maximum(m, s)
    p_c = w_c * jnp.exp(s_c - m)
    l = jnp.sum(p_c, axis=-1, keepdims=True)
    r = _dot(p_c.astype(BF16), vc_ref[0].astype(BF16))
    for c in range(n_new):
        p = w_n[c] * jnp.exp(s_n[c] - m)
        l = l + p
        r = r + p * vn[c:c + 1, :]
    o = _diag_blocks(r, blk, n_blk) / l
    if g_ref is not None:
        o = o * jax.nn.sigmoid(g_ref[0])[:, gate_col:gate_col + 1]
    o_ref[0] = o


def _decode_attention(q, kc, vc, kn, vn, *, n_heads, n_blk, patterns, gate=None, gate_col=None):
    n, n_rows, _ = q.shape
    args = [q, kc, vc, kn, vn]
    if gate is not None:
        args.append(gate)
    whole = lambda a: pl.BlockSpec((1,) + a.shape[1:], lambda i: (i, 0, 0))
    return pl.pallas_call(
        functools.partial(_decode_body, n_heads=n_heads, n_blk=n_blk, patterns=patterns,
                          gate_col=gate_col if gate is not None else None),
        grid=(n,),
        in_specs=[whole(a) for a in args],
        out_specs=pl.BlockSpec((1, n_rows, HEAD_DIM), lambda i: (i, 0, 0)),
        out_shape=jax.ShapeDtypeStruct((n, n_rows, HEAD_DIM), F32),
        compiler_params=_params("parallel"),
        name="decode_attention",
    )(*args)


def _nsa_decode_body(pt_ref, q_ref, kc_ref, vc_ref, ksp_ref, vsp_ref, kn_ref, vn_ref, g_ref, wov_ref, o_ref,
                     kbuf, vbuf, expand, ksem, vsem, *, n_pages, n_cmp, n_slc):
    n = pl.program_id(0)
    slot = n % 2
    n_rows = q_ref.shape[1]
    n_new = kn_ref.shape[1]
    n_past = kbuf.shape[1]
    n_slc_pad = wov_ref.shape[1]
    rep = C_HEADS // C_KV_HEADS

    @pl.when(n == 0)
    def _():
        _gather_dma(pt_ref, ksp_ref, kbuf, ksem, 0, 0, n_pages, PAGE_SIZE, True)
        _gather_dma(pt_ref, vsp_ref, vbuf, vsem, 0, 0, n_pages, PAGE_SIZE, True)
        step_cols = 1024
        for c0 in range(0, n_past, step_cols):
            shape = (n_slc_pad, min(step_cols, n_past - c0))
            blk_of_key = _idiv(c0 + lax.broadcasted_iota(jnp.int32, shape, 1), SEL_BLOCK)
            hit = blk_of_key == lax.broadcasted_iota(jnp.int32, shape, 0)
            expand[:, c0:c0 + shape[1]] = jnp.where(hit, 1.0, 0.0).astype(BF16)

    @pl.when(n + 1 < pl.num_programs(0))
    def _():
        _gather_dma(pt_ref, ksp_ref, kbuf, ksem, n + 1, 1 - slot, n_pages, PAGE_SIZE, True)
        _gather_dma(pt_ref, vsp_ref, vbuf, vsem, n + 1, 1 - slot, n_pages, PAGE_SIZE, True)

    step, blk = _row_ids(n_rows, C_HEADS, C_KV_HEADS)
    qpos = n_past + step
    qbd = _block_diag(q_ref[0] * (HEAD_DIM ** -0.5), blk, C_KV_HEADS)
    qbd_in = qbd.astype(BF16)
    n_cmp_pad = kc_ref.shape[1]
    cmp_i = lax.broadcasted_iota(jnp.int32, (n_rows, n_cmp_pad), 1)
    cmp_ok = (cmp_i * CMP_STRIDE + (CMP_LEN - 1) <= qpos) & (cmp_i < n_cmp)
    s = jnp.where(cmp_ok, _dot_nt(qbd_in, kc_ref[0].astype(BF16)), NEG_INF)
    m = jnp.max(s, axis=-1, keepdims=True)
    p = jnp.where(cmp_ok, jnp.exp(s - m), 0.0)
    l = jnp.sum(p, axis=-1, keepdims=True)
    p = p / jnp.where(l > 0, l, 1.0)
    o_cmp = _diag_blocks(_dot(p.astype(BF16), vc_ref[0].astype(BF16)), blk, C_KV_HEADS)
    grp = (n_rows // rep, rep, n_cmp_pad)
    imp = jnp.broadcast_to(jnp.sum(p.reshape(grp), axis=1, keepdims=True), grp).reshape(n_rows, n_cmp_pad)
    sel_rows = _select_blocks(_split3_dot(imp, wov_ref[...]), qpos, n_slc)
    _gather_dma(pt_ref, ksp_ref, kbuf, ksem, n, slot, n_pages, PAGE_SIZE, False)
    _gather_dma(pt_ref, vsp_ref, vbuf, vsem, n, slot, n_pages, PAGE_SIZE, False)
    picked = _dot(sel_rows.astype(BF16), expand[...])
    kpos = lax.broadcasted_iota(jnp.int32, (n_rows, n_past), 1)
    s_p = jnp.where(picked > 0.5, jnp.where(kpos <= qpos, _dot_nt(qbd_in, kbuf[slot].astype(BF16)), NEG_INF), NEG_INF)
    kn = kn_ref[0]
    vn = vn_ref[0]
    s_n = []
    for c in range(n_new):
        blk_c = (n_past + c) // SEL_BLOCK
        ok = (sel_rows[:, blk_c:blk_c + 1] > 0.5) & (n_past + c <= qpos)
        s_n.append(jnp.where(ok, jnp.sum(qbd * kn[c:c + 1, :], axis=-1, keepdims=True), NEG_INF))
    m = jnp.max(s_p, axis=-1, keepdims=True)
    for sc in s_n:
        m = jnp.maximum(m, sc)
    p_p = jnp.exp(s_p - m)
    l = jnp.sum(p_p, axis=-1, keepdims=True)
    r = _dot(p_p.astype(BF16), vbuf[slot].astype(BF16))
    for c in range(n_new):
        pn = jnp.exp(s_n[c] - m)
        l = l + pn
        r = r + pn * vn[c:c + 1, :]
    o_sel = _diag_blocks(r, blk, C_KV_HEADS) / l
    sig = jax.nn.sigmoid(g_ref[0])
    o_ref[0] = sig[:, 0:1] * o_cmp + sig[:, 1:2] * o_sel


def _nsa_decode(q, kcb, vcb, pool_ks, pool_vs, page_table, kn, vn, gate, n_cmp):
    n, n_rows, _ = q.shape
    n_pages = page_table.shape[1]
    n_past = n_pages * PAGE_SIZE
    n_new = kn.shape[1]
    n_slc = -(-(n_past + n_new) // SEL_BLOCK)
    n_slc_pad = -(-n_slc // LANES) * LANES
    wov = _overlap_weights(n_cmp, kcb.shape[1], n_slc, n_slc_pad)
    whole = lambda a: pl.BlockSpec((1,) + a.shape[1:], lambda i, pt: (i, 0, 0))
    pools = [a.reshape(a.shape[0], PAGE_SIZE, C_KV_WIDTH) for a in (pool_ks, pool_vs)]
    return pl.pallas_call(
        functools.partial(_nsa_decode_body, n_pages=n_pages, n_cmp=n_cmp, n_slc=n_slc),
        grid_spec=pltpu.PrefetchScalarGridSpec(
            num_scalar_prefetch=1,
            grid=(n,),
            in_specs=[whole(q), whole(kcb), whole(vcb), pl.BlockSpec(memory_space=pl.ANY),
                      pl.BlockSpec(memory_space=pl.ANY), whole(kn), whole(vn), whole(gate),
                      pl.BlockSpec(wov.shape, lambda i, pt: (0, 0))],
            out_specs=pl.BlockSpec((1, n_rows, HEAD_DIM), lambda i, pt: (i, 0, 0)),
            scratch_shapes=[pltpu.VMEM((2, n_past, C_KV_WIDTH), F32), pltpu.VMEM((2, n_past, C_KV_WIDTH), F32),
                            pltpu.VMEM((n_slc_pad, n_past), BF16),
                            pltpu.SemaphoreType.DMA((2,)), pltpu.SemaphoreType.DMA((2,))]),
        out_shape=jax.ShapeDtypeStruct((n, n_rows, HEAD_DIM), F32),
        compiler_params=_params("arbitrary"),
        name="nsa_decode",
    )(page_table, q, kcb, vcb, *pools, kn, vn, gate, wov)


def _odd_out_body(x_ref, a_ref, b_ref, z_ref, w_ref, gf_ref, y_ref):
    mixed = ((a_ref[...] + b_ref[...]) * _silu(z_ref[...])).astype(BF16)
    h = x_ref[...] + _dot(mixed, w_ref[...])
    y_ref[...] = h * lax.rsqrt(jnp.mean(h * h, axis=-1, keepdims=True) + RMS_EPS) * gf_ref[...]


def _odd_out(x2d, o_cs, o_win, z, w_out, final_gain, tm):
    m, d = x2d.shape
    row = pl.BlockSpec((tm, d), lambda i: (i, 0))
    return pl.pallas_call(
        _odd_out_body,
        grid=(m // tm,),
        in_specs=[row, row, row, row, pl.BlockSpec(w_out.shape, lambda i: (0, 0)),
                  pl.BlockSpec((1, d), lambda i: (0, 0))],
        out_specs=row,
        out_shape=jax.ShapeDtypeStruct((m, d), F32),
        compiler_params=_params("parallel"),
        name="odd_out",
    )(x2d, o_cs, o_win, z, w_out, final_gain.reshape(1, d))


def _even_outs():
    a, bk = A_WIDTH, B_HEADS * B_DK
    return [(0, a, BF16),
            (a, a, BF16), (a, a, F32),
            (2 * a, a, BF16), (2 * a, a, F32),
            (3 * a, a, F32),
            (4 * a, bk, F32), (4 * a + bk, bk, F32),
            (4 * a + 2 * bk, B_WIDTH, F32), (4 * a + 2 * bk + B_WIDTH, B_WIDTH, F32)]


def _even_layer_prompt(x, gain, w_in, w_out):
    b, t, d = x.shape
    x2d = x.reshape(b * t, d)
    qa, ka, ka32, va, va32, za, qb, kb, vb, zb = _norm_proj(x2d, gain, w_in, _even_outs(), 512)
    seq = lambda u: u.reshape(b, t, u.shape[-1])
    pats = []
    for window, dil in A_PATTERNS:
        n_sub = t // dil
        pats.append(_banded_attention(seq(qa), seq(ka), seq(va), dil=dil, n_heads=A_HEADS, n_groups=A_HEADS,
                                      lookback=window // dil, tqb=min(512, n_sub), with_lse=True))
    pats = [[a.reshape(b * t, A_WIDTH) for a in pair] for pair in pats]
    chunk = B_CHUNK if t % B_CHUNK == 0 else t
    s0 = jnp.zeros((b, B_HEADS, B_DK, B_DV), F32)
    ob, s_fin = _retention(seq(qb), seq(kb), seq(vb), s0, jnp.arange(t), chunk=chunk, c_real=chunk)
    y = _even_out(x2d, pats, za, ob.reshape(b * t, B_WIDTH), zb, w_out, 256)
    keep = min(A_PATTERNS[-1][0], t)
    heads = lambda u: seq(u)[:, t - keep:].reshape(b, keep, A_HEADS, HEAD_DIM)
    return y.reshape(b, t, d), heads(ka32), heads(va32), s_fin


def _even_layer_sample(x, cache_k, cache_v, state, gain, w_in, w_out, past_len):
    n, s_len, d = x.shape
    x2d = x.reshape(n * s_len, d)
    qa, _, ka32, _, va32, za, qb, kb, vb, zb = _norm_proj(x2d, gain, w_in, _even_outs(), n * s_len)
    n_past = cache_k.shape[1]
    flat = lambda u: u.reshape(n, n_past, A_WIDTH)
    seq = lambda u: u.reshape(n, s_len, u.shape[-1])
    oa = _decode_attention(qa.astype(F32).reshape(n, s_len * A_HEADS, HEAD_DIM), flat(cache_k), flat(cache_v),
                           seq(ka32), seq(va32), n_heads=A_HEADS, n_blk=A_HEADS, patterns=A_PATTERNS)
    oa = oa.reshape(n * s_len, A_WIDTH)
    chunk = 16
    pad = lambda u: jnp.pad(seq(u), ((0, 0), (0, chunk - s_len), (0, 0)))
    pos = past_len + jnp.arange(chunk)
    ob, s_new = _retention(pad(qb), pad(kb), pad(vb), state, pos, chunk=chunk, c_real=s_len)
    ob = ob[:, :s_len].reshape(n * s_len, B_WIDTH)
    zero = jnp.zeros_like(oa)
    y = _even_out(x2d, [[oa, zero]] * 3, za, ob, zb, w_out, n * s_len)
    heads = lambda u: u.reshape(n, s_len, A_HEADS, HEAD_DIM)
    return y.reshape(n, s_len, d), heads(ka32), heads(va32), s_new


def _odd_weights(w_in):
    kv0 = C_WIDTH
    g0 = kv0 + 6 * C_KV_WIDTH
    z0 = g0 + 3 * C_HEADS
    gl = jnp.pad(w_in[:, g0:z0], ((0, 0), (0, LANES - 3 * C_HEADS)))
    return jnp.concatenate([w_in[:, :g0], w_in[:, z0:], gl], axis=1).astype(BF16)


def _odd_outs():
    outs = [(0, C_WIDTH, BF16)]
    off = C_WIDTH
    for name in ("kc", "vc", "ks", "vs", "kw", "vw"):
        outs.append((off, C_KV_WIDTH, F32))
        if name not in ("kc", "vc"):
            outs.append((off, C_KV_WIDTH, BF16))
        off += C_KV_WIDTH
    outs.append((off, C_WIDTH, F32))
    outs.append((off + C_WIDTH, LANES, F32))
    return outs


def _odd_layer_prompt(x, gain, w_in, w_out, cw_k, cw_v, final_gain):
    b, t, d = x.shape
    x2d = x.reshape(b * t, d)
    q, kc, vc, ks32, ks, vs32, vs, kw32, kw, vw32, vw, z, gl = _norm_proj(x2d, gain, w_in, _odd_outs(), 512)
    seq = lambda u: u.reshape(b, t, u.shape[-1])
    n_cmp = (t - CMP_LEN) // CMP_STRIDE + 1
    pieces = lambda u: u.reshape(b, t // CMP_STRIDE, CMP_STRIDE * C_KV_WIDTH)
    kcb = _compress(pieces(kc), cw_k)
    vcb = _compress(pieces(vc), cw_v)
    o_cs = _nsa_prompt(seq(q), kcb, vcb, seq(ks), seq(vs), seq(gl), n_cmp)
    o_win, = _banded_attention(seq(q), seq(kw), seq(vw), dil=1, n_heads=C_HEADS, n_groups=C_KV_HEADS,
                               lookback=C_WINDOW, tqb=min(512, t), gate=seq(gl), gate_col=lambda h: 3 * h + 2)
    y = _odd_out(x2d, o_cs.reshape(b * t, C_WIDTH), o_win.reshape(b * t, C_WIDTH), z, w_out, final_gain, 256)
    keep = min(C_WINDOW, t)
    kv = lambda u: seq(u).reshape(b, t, C_KV_HEADS, HEAD_DIM)
    return (y.reshape(b, t, d), kv(kc), kv(vc), kv(ks32), kv(vs32), kv(kw32)[:, t - keep:], kv(vw32)[:, t - keep:])


def _odd_layer_sample(x, page_table, pool_kc, pool_vc, pool_ks, pool_vs, buf_kw, buf_vw, gain, w_in, w_out,
                      cw_k, cw_v, final_gain):
    n, s_len, d = x.shape
    x2d = x.reshape(n * s_len, d)
    q, kc, vc, ks32, _, vs32, _, kw32, _, vw32, _, z, gl = _norm_proj(x2d, gain, w_in, _odd_outs(), n * s_len)
    seq = lambda u: u.reshape(n, s_len, u.shape[-1])
    n_past = page_table.shape[1] * PAGE_SIZE
    n_cmp = (n_past + s_len - CMP_LEN) // CMP_STRIDE + 1
    assert CMP_STRIDE * (n_cmp - 1) + CMP_LEN <= n_past
    kcb = _compress_paged(pool_kc, page_table, cw_k)
    vcb = _compress_paged(pool_vc, page_table, cw_v)
    rows = lambda u: u.astype(F32).reshape(n, s_len * C_HEADS, HEAD_DIM)
    gate = gl[:, :3 * C_HEADS].reshape(n, s_len * C_HEADS, 3)
    o_cs = _nsa_decode(rows(q), kcb, vcb, pool_ks, pool_vs, page_table, seq(ks32), seq(vs32), gate, n_cmp)
    n_win = buf_kw.shape[1]
    o_win = _decode_attention(rows(q), buf_kw.reshape(n, n_win, C_KV_WIDTH), buf_vw.reshape(n, n_win, C_KV_WIDTH),
                              seq(kw32), seq(vw32), n_heads=C_HEADS, n_blk=C_KV_HEADS,
                              patterns=((C_WINDOW, 1),), gate=gate, gate_col=2)
    y = _odd_out(x2d, o_cs.reshape(n * s_len, C_WIDTH), o_win.reshape(n * s_len, C_WIDTH), z, w_out, final_gain,
                 n * s_len)
    kv = lambda u: u.reshape(n, s_len, C_KV_HEADS, HEAD_DIM)
    return y.reshape(n, s_len, d), kv(kc), kv(vc), kv(ks32), kv(vs32), kv(kw32), kv(vw32)


def kernel(x_prompt, x_sample, cache_a_k, cache_a_v, state_ret, cache_c_kcmp, cache_c_vcmp, cache_c_ksel,
           cache_c_vsel, cache_c_kwin, cache_c_vwin, page_table, even_norm, even_w_in, even_w_out, odd_norm,
           odd_w_in, odd_w_out, cmp_pos_k, cmp_w1_k, cmp_w2_k, cmp_pos_v, cmp_w1_v, cmp_w2_v, final_norm):
    assert even_norm.shape[0] == 1 and odd_norm.shape[0] == 1
    past_len = page_table.shape[1] * PAGE_SIZE
    w_in_e = even_w_in[0].astype(BF16)
    w_out_e = even_w_out[0].astype(BF16)
    w_in_o = _odd_weights(odd_w_in[0])
    w_out_o = odd_w_out[0].astype(BF16)
    cw_k = _compress_weights(cmp_pos_k[0], cmp_w1_k[0], cmp_w2_k[0])
    cw_v = _compress_weights(cmp_pos_v[0], cmp_w1_v[0], cmp_w2_v[0])

    hp, ak_p, av_p, ret_p = _even_layer_prompt(x_prompt, even_norm[0], w_in_e, w_out_e)
    hs, ak_s, av_s, ret_s = _even_layer_sample(x_sample, cache_a_k[0], cache_a_v[0], state_ret[0], even_norm[0],
                                               w_in_e, w_out_e, past_len)
    yp, *rows_p = _odd_layer_prompt(hp, odd_norm[0], w_in_o, w_out_o, cw_k, cw_v, final_norm)
    ys, *rows_s = _odd_layer_sample(hs, page_table, cache_c_kcmp[0], cache_c_vcmp[0], cache_c_ksel[0],
                                    cache_c_vsel[0], cache_c_kwin[0], cache_c_vwin[0], odd_norm[0], w_in_o, w_out_o,
                                    cw_k, cw_v, final_norm)
    lead = lambda u: u[None]
    out = [yp, ys, lead(ak_p), lead(ak_s), lead(av_p), lead(av_s), lead(ret_p), lead(ret_s)]
    for rp, rs in zip(rows_p, rows_s):
        out += [lead(rp), lead(rs)]
    return tuple(out)
```

```python
import functools

import numpy as np
import jax
import jax.numpy as jnp
from jax import lax
from jax.experimental import pallas as pl
from jax.experimental.pallas import tpu as pltpu

F32 = jnp.float32
BF16 = jnp.bfloat16

HEAD_DIM = 64
A_HEADS = 8
A_WIDTH = A_HEADS * HEAD_DIM
A_PATTERNS = ((128, 1), (512, 4), (2048, 16))
B_HEADS = 4
B_DK = 64
B_DV = 128
B_WIDTH = B_HEADS * B_DV
B_CHUNK = 128
ROPE_BASE = 10000.0
GN_EPS = 1e-5
C_HEADS = 16
C_KV_HEADS = 2
C_WIDTH = C_HEADS * HEAD_DIM
C_KV_WIDTH = C_KV_HEADS * HEAD_DIM
CMP_LEN = 32
CMP_STRIDE = 16
CMP_HIDDEN = 256
SEL_BLOCK = 64
SEL_TOPK = 16
SEL_FORCE = 1e9
C_WINDOW = 512
PAGE_SIZE = 128
RMS_EPS = 1e-6
NEG_INF = -1e30

LANES = 128
ATTN_TQ = 128
VMEM_LIMIT = 56 * 1024 * 1024


def _params(*sem):
    return pltpu.CompilerParams(dimension_semantics=sem, vmem_limit_bytes=VMEM_LIMIT)


def _dot(a, b):
    return jnp.dot(a, b, preferred_element_type=F32)


def _dot_nt(a, b):
    return lax.dot_general(a, b, (((1,), (1,)), ((), ())), preferred_element_type=F32)


def _dot_tn(a, b):
    return lax.dot_general(a, b, (((0,), (0,)), ((), ())), preferred_element_type=F32)


def _idiv(x, n):
    assert n & (n - 1) == 0
    return jnp.right_shift(x, n.bit_length() - 1)


def _imod(x, n):
    assert n & (n - 1) == 0
    return jnp.bitwise_and(x, n - 1)


def _split3_dot(a, w):
    hi = a.astype(BF16)
    r1 = a - hi.astype(F32)
    mid = r1.astype(BF16)
    lo = (r1 - mid.astype(F32)).astype(BF16)
    return _dot(hi, w) + _dot(mid, w) + _dot(lo, w)


def _norm_proj_body(x_ref, g_ref, w_ref, *o_refs, outs):
    x = x_ref[...]
    y = x * lax.rsqrt(jnp.mean(x * x, axis=-1, keepdims=True) + RMS_EPS) * g_ref[...]
    yb = y.astype(BF16)
    done = {}
    for o_ref, (off, width, _) in zip(o_refs, outs):
        if (off, width) not in done:
            done[(off, width)] = _dot(yb, w_ref[:, off:off + width])
        o_ref[...] = done[(off, width)].astype(o_ref.dtype)


def _norm_proj(x2d, gain, w, outs, tm):
    m, d = x2d.shape
    n = w.shape[1]
    return pl.pallas_call(
        functools.partial(_norm_proj_body, outs=outs),
        grid=(m // tm,),
        in_specs=[pl.BlockSpec((tm, d), lambda i: (i, 0)),
                  pl.BlockSpec((1, d), lambda i: (0, 0)),
                  pl.BlockSpec((d, n), lambda i: (0, 0))],
        out_specs=[pl.BlockSpec((tm, width), lambda i: (i, 0)) for _, width, _ in outs],
        out_shape=[jax.ShapeDtypeStruct((m, width), dt) for _, width, dt in outs],
        compiler_params=_params("parallel"),
        name="norm_proj",
    )(x2d, gain.reshape(1, d), w)


def _banded_body(*refs, n_heads, n_groups, lookback, tqb, with_lse, gate_col):
    q_ref, k_ref, v_ref = refs[:3]
    refs = refs[3:]
    g_ref = None
    if gate_col is not None:
        g_ref, refs = refs[0], refs[1:]
    o_ref = refs[0]
    lse_ref = refs[1] if with_lse else None
    tq = ATTN_TQ
    n_prev = -(-lookback // tq)
    span = (n_prev + 1) * tq
    rep = n_heads // n_groups
    blk = pl.program_id(2)

    def sub(u, carry):
        row0 = pl.multiple_of(u * tq, tq)
        s0 = blk * tqb + row0
        kstart = pl.multiple_of(jnp.maximum(s0 - n_prev * tq, 0), tq)
        qpos = s0 + lax.broadcasted_iota(jnp.int32, (tq, span), 0)
        kpos = kstart + lax.broadcasted_iota(jnp.int32, (tq, span), 1)
        dist = qpos - kpos
        bias = jnp.where(dist >= 0, jnp.where(dist <= lookback, 0.0, NEG_INF), NEG_INF)
        if g_ref is not None:
            sig = jax.nn.sigmoid(g_ref[0, pl.ds(row0, tq), :])
        kv_cols = [slice(g * HEAD_DIM, (g + 1) * HEAD_DIM) for g in range(n_groups)]
        keys = [k_ref[0, pl.ds(kstart, span), c] for c in kv_cols]
        vals = [v_ref[0, pl.ds(kstart, span), c] for c in kv_cols]
        cols = [slice(h * HEAD_DIM, (h + 1) * HEAD_DIM) for h in range(n_heads)]
        scores = [_dot_nt(q_ref[0, pl.ds(row0, tq), c] * (HEAD_DIM ** -0.5), keys[h // rep]) + bias
                  for h, c in enumerate(cols)]
        maxes = [jnp.max(s, axis=-1, keepdims=True) for s in scores]
        probs = [jnp.exp(s - m) for s, m in zip(scores, maxes)]
        sums = [jnp.sum(p, axis=-1, keepdims=True) for p in probs]
        outs = [_dot(p.astype(BF16), vals[h // rep]) for h, p in enumerate(probs)]
        for h, (c, m, l, o) in enumerate(zip(cols, maxes, sums, outs)):
            o = o / l
            if g_ref is not None:
                gc = gate_col(h)
                o = o * sig[:, gc:gc + 1]
            o_ref[0, pl.ds(row0, tq), c] = o
            if lse_ref is not None:
                lse_ref[0, pl.ds(row0, tq), c] = jnp.broadcast_to(m + jnp.log(l), (tq, HEAD_DIM))
        return carry

    lax.fori_loop(0, tqb // tq, sub, 0)


def _banded_attention(q, k, v, *, dil, n_heads, n_groups, lookback, tqb, with_lse=False, gate=None, gate_col=None):
    b, t, wq = q.shape
    wk = k.shape[-1]
    n_sub = t // dil
    assert n_sub % tqb == 0 and tqb % ATTN_TQ == 0
    assert n_sub >= (-(-lookback // ATTN_TQ) + 1) * ATTN_TQ
    args = [q.reshape(b, n_sub, dil * wq), k.reshape(b, n_sub, dil * wk), v.reshape(b, n_sub, dil * wk)]
    in_specs = [pl.BlockSpec((1, tqb, wq), lambda bi, r, i: (bi, i, r)),
                pl.BlockSpec((1, n_sub, wk), lambda bi, r, i: (bi, 0, r)),
                pl.BlockSpec((1, n_sub, wk), lambda bi, r, i: (bi, 0, r))]
    if gate is not None:
        assert dil == 1
        args.append(gate)
        in_specs.append(pl.BlockSpec((1, tqb, gate.shape[-1]), lambda bi, r, i: (bi, i, 0)))
    n_out = 2 if with_lse else 1
    res = pl.pallas_call(
        functools.partial(_banded_body, n_heads=n_heads, n_groups=n_groups, lookback=lookback, tqb=tqb,
                          with_lse=with_lse, gate_col=gate_col if gate is not None else None),
        grid=(b, dil, n_sub // tqb),
        in_specs=in_specs,
        out_specs=[pl.BlockSpec((1, tqb, wq), lambda bi, r, i: (bi, i, r))] * n_out,
        out_shape=[jax.ShapeDtypeStruct((b, n_sub, dil * wq), F32)] * n_out,
        compiler_params=_params("parallel", "parallel", "arbitrary"),
        name=f"banded_attention_d{dil}",
    )(*args)
    return [a.reshape(b, t, wq) for a in res]


def _retention_body(q_ref, k_ref, v_ref, cos_ref, sin_ref, qd_ref, kd_ref, intra_ref, cd_ref, s0_ref,
                    o_ref, sfin_ref, state):
    c = pl.program_id(1)

    @pl.when(c == 0)
    def _():
        state[...] = s0_ref[0]

    cos = cos_ref[...]
    sin = sin_ref[...]
    width = cos.shape[-1]
    half = B_DK // 2
    lane = _imod(lax.broadcasted_iota(jnp.int32, cos.shape, 1), B_DK)

    def rotary(x):
        fwd = pltpu.roll(x, half, 1)
        bwd = pltpu.roll(x, width - half, 1)
        return x * cos + jnp.where(lane < half, -bwd, fwd) * sin

    q = rotary(q_ref[0])
    k = rotary(k_ref[0]) * (B_DK ** -0.5)
    q_in = q.astype(BF16)
    k_in = k.astype(BF16)
    q_st = (q * qd_ref[...]).astype(BF16)
    k_st = (k * kd_ref[...]).astype(BF16)
    v = v_ref[0].astype(BF16)
    for h in range(B_HEADS):
        kc = slice(h * B_DK, (h + 1) * B_DK)
        vc = slice(h * B_DV, (h + 1) * B_DV)
        st = state[h]
        scores = _dot_nt(q_in[:, kc], k_in[:, kc]) * intra_ref[h]
        o_ref[0, :, vc] = _dot(scores.astype(BF16), v[:, vc]) + _dot(q_st[:, kc], st.astype(BF16))
        state[h] = st * cd_ref[h] + _dot_tn(k_st[:, kc], v[:, vc])

    @pl.when(c == pl.num_programs(1) - 1)
    def _():
        sfin_ref[0] = state[...]


def _retention(q, k, v, s0, pos, *, chunk, c_real):
    n, t, _ = q.shape
    half = B_DK // 2
    freqs = ROPE_BASE ** (-jnp.arange(half, dtype=F32) / half)
    ang = pos.astype(F32)[:, None] * freqs[None, :]
    cos = jnp.tile(jnp.cos(ang), (1, 2 * B_HEADS))
    sin = jnp.tile(jnp.sin(ang), (1, 2 * B_HEADS))
    log_g = jnp.log1p(-jnp.exp2(-5.0 - jnp.arange(B_HEADS, dtype=F32)))
    i = jnp.arange(chunk, dtype=F32)
    diff = i[:, None] - i[None, :]
    intra = jnp.where(diff >= 0, jnp.exp(log_g[:, None, None] * jnp.maximum(diff, 0.0)), 0.0)
    q_decay = jnp.repeat(jnp.exp(log_g[None, :] * (i[:, None] + 1.0)), B_DK, axis=1)
    k_decay = jnp.repeat(jnp.exp(log_g[None, :] * (c_real - 1.0 - i[:, None])), B_DK, axis=1)
    chunk_decay = jnp.broadcast_to(jnp.exp(log_g * c_real)[:, None, None], (B_HEADS, 1, B_DV))
    wqk = B_HEADS * B_DK
    return pl.pallas_call(
        _retention_body,
        grid=(n, t // chunk),
        in_specs=[pl.BlockSpec((1, chunk, wqk), lambda b, c: (b, c, 0)),
                  pl.BlockSpec((1, chunk, wqk), lambda b, c: (b, c, 0)),
                  pl.BlockSpec((1, chunk, B_WIDTH), lambda b, c: (b, c, 0)),
                  pl.BlockSpec((chunk, wqk), lambda b, c: (c, 0)),
                  pl.BlockSpec((chunk, wqk), lambda b, c: (c, 0)),
                  pl.BlockSpec((chunk, wqk), lambda b, c: (0, 0)),
                  pl.BlockSpec((chunk, wqk), lambda b, c: (0, 0)),
                  pl.BlockSpec((B_HEADS, chunk, chunk), lambda b, c: (0, 0, 0)),
                  pl.BlockSpec((B_HEADS, 1, B_DV), lambda b, c: (0, 0, 0)),
                  pl.BlockSpec((1, B_HEADS, B_DK, B_DV), lambda b, c: (b, 0, 0, 0))],
        out_specs=[pl.BlockSpec((1, chunk, B_WIDTH), lambda b, c: (b, c, 0)),
                   pl.BlockSpec((1, B_HEADS, B_DK, B_DV), lambda b, c: (b, 0, 0, 0))],
        out_shape=[jax.ShapeDtypeStruct((n, t, B_WIDTH), F32),
                   jax.ShapeDtypeStruct((n, B_HEADS, B_DK, B_DV), F32)],
        scratch_shapes=[pltpu.VMEM((B_HEADS, B_DK, B_DV), F32)],
        compiler_params=_params("parallel", "arbitrary"),
        name="retention",
    )(q, k, v, cos, sin, q_decay, k_decay, intra, chunk_decay, s0)


def _silu(z):
    return z * jax.nn.sigmoid(z)


def _even_out_body(x_ref, o1, l1, o2, l2, o3, l3, za_ref, ob_ref, zb_ref, w_ref, y_ref):
    m = jnp.maximum(jnp.maximum(l1[...], l2[...]), l3[...])
    e1 = jnp.exp(l1[...] - m)
    e2 = jnp.exp(l2[...] - m)
    e3 = jnp.exp(l3[...] - m)
    oa = (e1 * o1[...] + e2 * o2[...] + e3 * o3[...]) / (e1 + e2 + e3)
    ga = (oa * _silu(za_ref[...])).astype(BF16)
    acc = x_ref[...] + _dot(ga, w_ref[0:A_WIDTH, :])
    ob = ob_ref[...]
    zb = zb_ref[...]
    for h in range(B_HEADS):
        cols = slice(h * B_DV, (h + 1) * B_DV)
        seg = ob[:, cols]
        mu = jnp.mean(seg, axis=-1, keepdims=True)
        cen = seg - mu
        var = jnp.mean(cen * cen, axis=-1, keepdims=True)
        gb = (cen * lax.rsqrt(var + GN_EPS) * _silu(zb[:, cols])).astype(BF16)
        acc = acc + _dot(gb, w_ref[A_WIDTH + h * B_DV:A_WIDTH + (h + 1) * B_DV, :])
    y_ref[...] = acc


def _even_out(x2d, pats, za, ob, zb, w_out, tm):
    m, d = x2d.shape
    row = lambda width: pl.BlockSpec((tm, width), lambda i: (i, 0))
    flat = [a for pair in pats for a in pair]
    return pl.pallas_call(
        _even_out_body,
        grid=(m // tm,),
        in_specs=[row(d)] + [row(A_WIDTH)] * 6 + [row(A_WIDTH), row(B_WIDTH), row(B_WIDTH),
                                                   pl.BlockSpec(w_out.shape, lambda i: (0, 0))],
        out_specs=row(d),
        out_shape=jax.ShapeDtypeStruct((m, d), F32),
        compiler_params=_params("parallel"),
        name="even_out",
    )(x2d, *flat, za, ob, zb, w_out)


def _compress_math(x, p2_ref, wbig_ref, w2_ref):
    n_piece = x.shape[0]
    a = _dot(x.astype(BF16), wbig_ref[...])
    pc = _dot(p2_ref[...], wbig_ref[...])
    hid = []
    for g in range(C_KV_HEADS):
        lo = slice(g * 2 * CMP_HIDDEN, g * 2 * CMP_HIDDEN + CMP_HIDDEN)
        hi = slice(g * 2 * CMP_HIDDEN + CMP_HIDDEN, (g + 1) * 2 * CMP_HIDDEN)
        nxt = pltpu.roll(a[:, hi], n_piece - 1, 0)
        pos = (pc[0:1, lo] + pc[2:3, lo]) + (pc[1:2, hi] + pc[3:4, hi])
        hid.append(_silu(a[:, lo] + nxt + pos))
    return _dot(jnp.concatenate(hid, axis=-1).astype(BF16), w2_ref[...])


def _compress_body(x_ref, p2_ref, wbig_ref, w2_ref, o_ref):
    o_ref[0] = _compress_math(x_ref[0], p2_ref, wbig_ref, w2_ref)


def _compress_weights(pos_emb, w1, w2):
    per = CMP_LEN // CMP_STRIDE
    w1r = w1.reshape(per, CMP_STRIDE, HEAD_DIM, CMP_HIDDEN)
    eye = jnp.eye(C_KV_HEADS, dtype=w1.dtype)
    wbig = jnp.einsum('jldf,gh->lgdhjf', w1r, eye).reshape(CMP_STRIDE * C_KV_WIDTH, C_KV_HEADS * per * CMP_HIDDEN)
    w2bd = jnp.einsum('fd,gh->gfhd', w2, eye).reshape(C_KV_HEADS * CMP_HIDDEN, C_KV_WIDTH)
    halves = jnp.tile(pos_emb.reshape(per, CMP_STRIDE, 1, HEAD_DIM), (1, 1, C_KV_HEADS, 1)).reshape(per, -1)
    hi = halves.astype(BF16)
    lo = (halves - hi.astype(F32)).astype(BF16)
    p2 = jnp.concatenate([hi, lo, jnp.zeros((16 - 2 * per, halves.shape[1]), BF16)], axis=0)
    return p2, wbig.astype(BF16), w2bd.astype(BF16)


def _compress(pieces, cw):
    n, n_piece, width = pieces.shape
    p2, wbig, w2bd = cw
    const = lambda a: pl.BlockSpec(a.shape, lambda i: (0,) * a.ndim)
    return pl.pallas_call(
        _compress_body,
        grid=(n,),
        in_specs=[pl.BlockSpec((1, n_piece, width), lambda i: (i, 0, 0)), const(p2), const(wbig), const(w2bd)],
        out_specs=pl.BlockSpec((1, n_piece, C_KV_WIDTH), lambda i: (i, 0, 0)),
        out_shape=jax.ShapeDtypeStruct((n, n_piece, C_KV_WIDTH), F32),
        compiler_params=_params("parallel"),
        name="compress",
    )(pieces, p2, wbig, w2bd)


def _gather_dma(pt_ref, pool_ref, buf_ref, sem_ref, n, slot, n_pages, rows, start):
    for p in range(n_pages):
        page = pt_ref[n, p] if start else 0
        cp = pltpu.make_async_copy(pool_ref.at[page], buf_ref.at[slot, pl.ds(p * rows, rows)], sem_ref.at[slot])
        if start:
            cp.start()
        else:
            cp.wait()


def _compress_paged_body(pt_ref, pool_ref, p2_ref, wbig_ref, w2_ref, o_ref, buf, sem, *, n_pages, rows):
    n = pl.program_id(0)
    slot = n % 2

    @pl.when(n == 0)
    def _():
        _gather_dma(pt_ref, pool_ref, buf, sem, 0, 0, n_pages, rows, True)

    @pl.when(n + 1 < pl.num_programs(0))
    def _():
        _gather_dma(pt_ref, pool_ref, buf, sem, n + 1, 1 - slot, n_pages, rows, True)

    _gather_dma(pt_ref, pool_ref, buf, sem, n, slot, n_pages, rows, False)
    o_ref[0] = _compress_math(buf[slot], p2_ref, wbig_ref, w2_ref)


def _compress_paged(pool, page_table, cw):
    n, n_pages = page_table.shape
    rows = PAGE_SIZE // CMP_STRIDE
    width = CMP_STRIDE * C_KV_WIDTH
    n_piece = n_pages * rows
    p2, wbig, w2bd = cw
    const = lambda a: pl.BlockSpec(a.shape, lambda i, pt: (0,) * a.ndim)
    return pl.pallas_call(
        functools.partial(_compress_paged_body, n_pages=n_pages, rows=rows),
        grid_spec=pltpu.PrefetchScalarGridSpec(
            num_scalar_prefetch=1,
            grid=(n,),
            in_specs=[pl.BlockSpec(memory_space=pl.ANY), const(p2), const(wbig), const(w2bd)],
            out_specs=pl.BlockSpec((1, n_piece, C_KV_WIDTH), lambda i, pt: (i, 0, 0)),
            scratch_shapes=[pltpu.VMEM((2, n_piece, width), F32), pltpu.SemaphoreType.DMA((2,))]),
        out_shape=jax.ShapeDtypeStruct((n, n_piece, C_KV_WIDTH), F32),
        compiler_params=_params("arbitrary"),
        name="compress_paged",
    )(page_table, pool.reshape(pool.shape[0], rows, width), p2, wbig, w2bd)


def _overlap_weights(n_cmp, n_cmp_pad, n_slc, n_slc_pad):
    i = np.arange(n_cmp_pad)[:, None]
    j = np.arange(n_slc_pad)[None, :]
    ov = (np.minimum(i * CMP_STRIDE + CMP_LEN, (j + 1) * SEL_BLOCK) - np.maximum(i * CMP_STRIDE, j * SEL_BLOCK))
    w = np.where((i < n_cmp) & (j < n_slc), np.clip(ov, 0, None) / CMP_LEN, 0.0)
    return jnp.asarray(w, dtype=BF16)


def _select_blocks(p_slc, qpos, n_slc):
    shape = p_slc.shape
    lane = lax.broadcasted_iota(jnp.int32, shape, 1)
    cur = _idiv(qpos, SEL_BLOCK)
    forced = (lane == 0) | (lane == cur) | (lane == cur - 1)
    causal = lane * SEL_BLOCK <= qpos
    score = jnp.where(forced, SEL_FORCE, jnp.where(causal, p_slc, -SEL_FORCE))
    score = jnp.where(lane < n_slc, score, -3.0 * SEL_FORCE)
    lane_f = lane.astype(F32)
    sel = jnp.zeros(shape, F32)
    for _ in range(min(SEL_TOPK, n_slc)):
        m = jnp.max(score, axis=-1, keepdims=True)
        first = jnp.min(jnp.where(score == m, lane_f, 1e9), axis=-1, keepdims=True)
        hit = lane_f == first
        sel = jnp.where(hit, jnp.where(m > -0.5 * SEL_FORCE, 1.0, 0.0), sel)
        score = jnp.where(hit, -4.0 * SEL_FORCE, score)
    return sel


def _select_blocks_t(p_slc, qpos, n_slc):
    n_rows = -(-n_slc // 8) * 8
    shape = (n_rows, p_slc.shape[1])
    blk = lax.broadcasted_iota(jnp.int32, shape, 0)
    cur = _idiv(qpos, SEL_BLOCK)
    forced = (blk == 0) | (blk == cur) | (blk == cur - 1)
    causal = blk * SEL_BLOCK <= qpos
    score = jnp.where(forced, SEL_FORCE, jnp.where(causal, p_slc[:n_rows], -SEL_FORCE))
    score = jnp.where(blk < n_slc, score, -3.0 * SEL_FORCE)
    tiles = [score[8 * v:8 * v + 8] for v in range(n_rows // 8)]
    sub = lax.broadcasted_iota(jnp.int32, (8, shape[1]), 0)
    ranks = [jnp.zeros((8, shape[1]), F32) for _ in tiles]
    for i in range(n_slc):
        row = tiles[i // 8][i % 8:i % 8 + 1]
        for v, tile in enumerate(tiles):
            ge = jnp.where(row >= tile, 1.0, 0.0)
            gt = jnp.where(row > tile, 1.0, 0.0)
            if 8 * v > i:
                ranks[v] = ranks[v] + ge
            elif 8 * v + 7 <= i:
                ranks[v] = ranks[v] + gt
            else:
                ranks[v] = ranks[v] + jnp.where(sub > i - 8 * v, ge, gt)
    rank = jnp.concatenate(ranks, axis=0)
    return jnp.where(rank < min(SEL_TOPK, n_slc), jnp.where(score > -0.5 * SEL_FORCE, 1.0, 0.0), 0.0)


def _nsa_prompt_body(q_ref, kc_ref, vct_ref, ks_ref, vst_ref, g_ref, wovt_ref, o_ref, bias_sc, m_sc, l_sc, acc_sc,
                     *, n_cmp, n_slc, tk):
    tq = ATTN_TQ
    rep = C_HEADS // C_KV_HEADS
    q0 = pl.program_id(1) * tq
    n_cmp_pad = kc_ref.shape[1]
    n_blk_tile = tk // SEL_BLOCK
    sig_t = jnp.transpose(jax.nn.sigmoid(g_ref[0]))
    q_t = [jnp.transpose(q_ref[0, :, c * LANES:(c + 1) * LANES].astype(F32)) * (HEAD_DIM ** -0.5)
           for c in range(C_WIDTH // LANES)]
    qpos = q0 + lax.broadcasted_iota(jnp.int32, (1, tq), 1)
    cmp_i = lax.broadcasted_iota(jnp.int32, (n_cmp_pad, tq), 0)
    cmp_ok = (cmp_i * CMP_STRIDE + (CMP_LEN - 1) <= qpos) & (cmp_i < n_cmp)
    cmp_bias = jnp.where(cmp_ok, 0.0, NEG_INF)
    cmp_keep = jnp.where(cmp_ok, 1.0, 0.0)
    n_full = q0 // tk
    outs = []
    for g in range(C_KV_HEADS):
        gc = slice(g * HEAD_DIM, (g + 1) * HEAD_DIM)
        heads = [g * rep + r for r in range(rep)]
        qh = [q_t[h // 2][(h % 2) * HEAD_DIM:(h % 2 + 1) * HEAD_DIM].astype(BF16) for h in heads]
        kcb = kc_ref[0, :, gc].astype(BF16)
        vct = vct_ref[0, gc, :].astype(BF16)
        scores = [_dot(kcb, qh[r]) + cmp_bias for r in range(rep)]
        probs = []
        for s in scores:
            p = jnp.exp(s - jnp.max(s, axis=0, keepdims=True)) * cmp_keep
            l = jnp.sum(p, axis=0, keepdims=True)
            probs.append(p / jnp.where(l > 0, l, 1.0))
        o_cmp = [_dot(vct, p.astype(BF16)) for p in probs]
        imp = probs[0]
        for p in probs[1:]:
            imp = imp + p
        hi = imp.astype(BF16)
        r1 = imp - hi.astype(F32)
        mid = r1.astype(BF16)
        lo = (r1 - mid.astype(F32)).astype(BF16)
        p_slc = _dot(wovt_ref[...], hi) + _dot(wovt_ref[...], mid) + _dot(wovt_ref[...], lo)
        bias_sc[...] = jnp.where(_select_blocks_t(p_slc, qpos, n_slc) > 0.5, 0.0, NEG_INF)
        m_sc[...] = jnp.full(m_sc.shape, NEG_INF, F32)
        l_sc[...] = jnp.zeros(l_sc.shape, F32)
        acc_sc[...] = jnp.zeros(acc_sc.shape, F32)

        def key_tile(k0, diagonal):
            k0 = pl.multiple_of(k0, tk)
            blk0 = k0 // SEL_BLOCK
            bias = jnp.concatenate([jnp.broadcast_to(bias_sc[pl.ds(blk0 + j, 1), :], (SEL_BLOCK, tq))
                                    for j in range(n_blk_tile)], axis=0)
            if diagonal:
                kpos = k0 + lax.broadcasted_iota(jnp.int32, (tk, tq), 0)
                bias = jnp.where(kpos <= qpos, bias, NEG_INF)
            k_tile = ks_ref[0, pl.ds(k0, tk), gc]
            v_tile = vst_ref[0, gc, pl.ds(k0, tk)]
            scores = [_dot(k_tile, qh[r]) + bias for r in range(rep)]
            m_old = m_sc[...]
            m_new = jnp.maximum(m_old, jnp.concatenate([jnp.max(s, axis=0, keepdims=True) for s in scores], axis=0))
            alpha = jnp.exp(m_old - m_new)
            probs = [jnp.exp(scores[r] - m_new[r:r + 1]) for r in range(rep)]
            l_sc[...] = alpha * l_sc[...] + jnp.concatenate([jnp.sum(p, axis=0, keepdims=True) for p in probs], axis=0)
            m_sc[...] = m_new
            pv = [_dot(v_tile, p.astype(BF16)) for p in probs]
            for r in range(rep):
                acc_sc[r] = alpha[r:r + 1] * acc_sc[r] + pv[r]

        def full_tile(kt, carry):
            key_tile(kt * tk, False)
            return carry

        lax.fori_loop(0, n_full, full_tile, 0)
        key_tile(n_full * tk, True)
        for r, h in enumerate(heads):
            o_sel = acc_sc[r] / l_sc[r:r + 1, :]
            outs.append(sig_t[3 * h:3 * h + 1] * o_cmp[r] + sig_t[3 * h + 1:3 * h + 2] * o_sel)
    for c in range(C_WIDTH // LANES):
        o_ref[0, :, c * LANES:(c + 1) * LANES] = jnp.transpose(jnp.concatenate(outs[2 * c:2 * c + 2], axis=0))


def _nsa_prompt(q, kcb, vcb, ks, vs, gates, n_cmp):
    b, t, wq = q.shape
    n_cmp_pad = kcb.shape[1]
    n_slc = -(-t // SEL_BLOCK)
    n_slc_rows = -(-n_slc // 8) * 8
    tk = min(2 * LANES, t)
    assert t % tk == 0 and tk % ATTN_TQ == 0 and t % SEL_BLOCK == 0 and n_slc <= LANES
    wov_t = _overlap_weights(n_cmp, n_cmp_pad, n_slc, LANES).T
    vcb_t = jnp.swapaxes(vcb, 1, 2)
    vs_t = jnp.swapaxes(vs, 1, 2)
    rep = C_HEADS // C_KV_HEADS
    whole = lambda a: pl.BlockSpec((1,) + a.shape[1:], lambda bi, i: (bi, 0, 0))
    return pl.pallas_call(
        functools.partial(_nsa_prompt_body, n_cmp=n_cmp, n_slc=n_slc, tk=tk),
        grid=(b, t // ATTN_TQ),
        in_specs=[pl.BlockSpec((1, ATTN_TQ, wq), lambda bi, i: (bi, i, 0)),
                  whole(kcb), whole(vcb_t), whole(ks), whole(vs_t),
                  pl.BlockSpec((1, ATTN_TQ, gates.shape[-1]), lambda bi, i: (bi, i, 0)),
                  pl.BlockSpec(wov_t.shape, lambda bi, i: (0, 0))],
        out_specs=pl.BlockSpec((1, ATTN_TQ, wq), lambda bi, i: (bi, i, 0)),
        out_shape=jax.ShapeDtypeStruct((b, t, wq), F32),
        scratch_shapes=[pltpu.VMEM((n_slc_rows, ATTN_TQ), F32), pltpu.VMEM((rep, ATTN_TQ), F32),
                        pltpu.VMEM((rep, ATTN_TQ), F32), pltpu.VMEM((rep, HEAD_DIM, ATTN_TQ), F32)],
        compiler_params=_params("parallel", "arbitrary"),
        name="nsa_prompt",
    )(q, kcb, vcb_t, ks, vs_t, gates, wov_t)


def _block_diag(q, blk_of_row, n_blk):
    return jnp.concatenate([jnp.where(blk_of_row == b, q, 0.0) for b in range(n_blk)], axis=-1)


def _diag_blocks(r, blk_of_row, n_blk):
    out = jnp.where(blk_of_row == 0, r[:, 0:HEAD_DIM], 0.0)
    for b in range(1, n_blk):
        out = out + jnp.where(blk_of_row == b, r[:, b * HEAD_DIM:(b + 1) * HEAD_DIM], 0.0)
    return out


def _row_ids(n_rows, n_heads, n_blk):
    row = lax.broadcasted_iota(jnp.int32, (n_rows, 1), 0)
    step = _idiv(row, n_heads)
    blk = _idiv(_imod(row, n_heads), n_heads // n_blk)
    return step, blk


def _pattern_weight(dist, patterns):
    w = jnp.zeros(dist.shape, F32)
    for window, dil in patterns:
        w = w + jnp.where(_imod(dist, dil) == 0, jnp.where(dist <= window, 1.0, 0.0), 0.0)
    return jnp.where(dist >= 0, w, 0.0)


def _decode_body(*refs, n_heads, n_blk, patterns, gate_col):
    q_ref, kc_ref, vc_ref, kn_ref, vn_ref = refs[:5]
    g_ref = refs[5] if gate_col is not None else None
    o_ref = refs[-1]
    n_rows = q_ref.shape[1]
    n_cache = kc_ref.shape[1]
    n_new = kn_ref.shape[1]
    step, blk = _row_ids(n_rows, n_heads, n_blk)
    qbd = _block_diag(q_ref[0] * (HEAD_DIM ** -0.5), blk, n_blk)
    s_c = _dot_nt(qbd.astype(BF16), kc_ref[0].astype(BF16))
    w_c = _pattern_weight(n_cache + step - lax.broadcasted_iota(jnp.int32, (n_rows, n_cache), 1), patterns)
    s_c = jnp.where(w_c > 0, s_c, NEG_INF)
    kn = kn_ref[0]
    vn = vn_ref[0]
    s_n, w_n = [], []
    for c in range(n_new):
        s = jnp.sum(qbd * kn[c:c + 1, :], axis=-1, keepdims=True)
        w = _pattern_weight(step - c, patterns)
        s_n.append(jnp.where(w > 0, s, NEG_INF))
        w_n.append(w)
    m = jnp.max(s_c, axis=-1, keepdims=True)
    for s in s_n:
        m = jnp.maximum(m, s)
    p_c = w_c * jnp.exp(s_c - m)
    l = jnp.sum(p_c, axis=-1, keepdims=True)
    r = _dot(p_c.astype(BF16), vc_ref[0].astype(BF16))
    for c in range(n_new):
        p = w_n[c] * jnp.exp(s_n[c] - m)
        l = l + p
        r = r + p * vn[c:c + 1, :]
    o = _diag_blocks(r, blk, n_blk) / l
    if g_ref is not None:
        o = o * jax.nn.sigmoid(g_ref[0])[:, gate_col:gate_col + 1]
    o_ref[0] = o


def _decode_attention(q, kc, vc, kn, vn, *, n_heads, n_blk, patterns, gate=None, gate_col=None):
    n, n_rows, _ = q.shape
    args = [q, kc, vc, kn, vn]
    if gate is not None:
        args.append(gate)
    whole = lambda a: pl.BlockSpec((1,) + a.shape[1:], lambda i: (i, 0, 0))
    return pl.pallas_call(
        functools.partial(_decode_body, n_heads=n_heads, n_blk=n_blk, patterns=patterns,
                          gate_col=gate_col if gate is not None else None),
        grid=(n,),
        in_specs=[whole(a) for a in args],
        out_specs=pl.BlockSpec((1, n_rows, HEAD_DIM), lambda i: (i, 0, 0)),
        out_shape=jax.ShapeDtypeStruct((n, n_rows, HEAD_DIM), F32),
        compiler_params=_params("parallel"),
        name="decode_attention",
    )(*args)


def _nsa_decode_body(pt_ref, q_ref, kc_ref, vc_ref, ksp_ref, vsp_ref, kn_ref, vn_ref, g_ref, wov_ref, o_ref,
                     kbuf, vbuf, expand, ksem, vsem, *, n_pages, n_cmp, n_slc):
    n = pl.program_id(0)
    slot = n % 2
    n_rows = q_ref.shape[1]
    n_new = kn_ref.shape[1]
    n_past = kbuf.shape[1]
    n_slc_pad = wov_ref.shape[1]
    rep = C_HEADS // C_KV_HEADS

    @pl.when(n == 0)
    def _():
        _gather_dma(pt_ref, ksp_ref, kbuf, ksem, 0, 0, n_pages, PAGE_SIZE, True)
        _gather_dma(pt_ref, vsp_ref, vbuf, vsem, 0, 0, n_pages, PAGE_SIZE, True)
        step_cols = 1024
        for c0 in range(0, n_past, step_cols):
            shape = (n_slc_pad, min(step_cols, n_past - c0))
            blk_of_key = _idiv(c0 + lax.broadcasted_iota(jnp.int32, shape, 1), SEL_BLOCK)
            hit = blk_of_key == lax.broadcasted_iota(jnp.int32, shape, 0)
            expand[:, c0:c0 + shape[1]] = jnp.where(hit, 1.0, 0.0).astype(BF16)

    @pl.when(n + 1 < pl.num_programs(0))
    def _():
        _gather_dma(pt_ref, ksp_ref, kbuf, ksem, n + 1, 1 - slot, n_pages, PAGE_SIZE, True)
        _gather_dma(pt_ref, vsp_ref, vbuf, vsem, n + 1, 1 - slot, n_pages, PAGE_SIZE, True)

    step, blk = _row_ids(n_rows, C_HEADS, C_KV_HEADS)
    qpos = n_past + step
    qbd = _block_diag(q_ref[0] * (HEAD_DIM ** -0.5), blk, C_KV_HEADS)
    qbd_in = qbd.astype(BF16)
    n_cmp_pad = kc_ref.shape[1]
    cmp_i = lax.broadcasted_iota(jnp.int32, (n_rows, n_cmp_pad), 1)
    cmp_ok = (cmp_i * CMP_STRIDE + (CMP_LEN - 1) <= qpos) & (cmp_i < n_cmp)
    s = jnp.where(cmp_ok, _dot_nt(qbd_in, kc_ref[0].astype(BF16)), NEG_INF)
    m = jnp.max(s, axis=-1, keepdims=True)
    p = jnp.where(cmp_ok, jnp.exp(s - m), 0.0)
    l = jnp.sum(p, axis=-1, keepdims=True)
    p = p / jnp.where(l > 0, l, 1.0)
    o_cmp = _diag_blocks(_dot(p.astype(BF16), vc_ref[0].astype(BF16)), blk, C_KV_HEADS)
    grp = (n_rows // rep, rep, n_cmp_pad)
    imp = jnp.broadcast_to(jnp.sum(p.reshape(grp), axis=1, keepdims=True), grp).reshape(n_rows, n_cmp_pad)
    sel_rows = _select_blocks(_split3_dot(imp, wov_ref[...]), qpos, n_slc)
    _gather_dma(pt_ref, ksp_ref, kbuf, ksem, n, slot, n_pages, PAGE_SIZE, False)
    _gather_dma(pt_ref, vsp_ref, vbuf, vsem, n, slot, n_pages, PAGE_SIZE, False)
    picked = _dot(sel_rows.astype(BF16), expand[...])
    kpos = lax.broadcasted_iota(jnp.int32, (n_rows, n_past), 1)
    s_p = jnp.where(picked > 0.5, jnp.where(kpos <= qpos, _dot_nt(qbd_in, kbuf[slot].astype(BF16)), NEG_INF), NEG_INF)
    kn = kn_ref[0]
    vn = vn_ref[0]
    s_n = []
    for c in range(n_new):
        blk_c = (n_past + c) // SEL_BLOCK
        ok = (sel_rows[:, blk_c:blk_c + 1] > 0.5) & (n_past + c <= qpos)
        s_n.append(jnp.where(ok, jnp.sum(qbd * kn[c:c + 1, :], axis=-1, keepdims=True), NEG_INF))
    m = jnp.max(s_p, axis=-1, keepdims=True)
    for sc in s_n:
        m = jnp.maximum(m, sc)
    p_p = jnp.exp(s_p - m)
    l = jnp.sum(p_p, axis=-1, keepdims=True)
    r = _dot(p_p.astype(BF16), vbuf[slot].astype(BF16))
    for c in range(n_new):
        pn = jnp.exp(s_n[c] - m)
        l = l + pn
        r = r + pn * vn[c:c + 1, :]
    o_sel = _diag_blocks(r, blk, C_KV_HEADS) / l
    sig = jax.nn.sigmoid(g_ref[0])
    o_ref[0] = sig[:, 0:1] * o_cmp + sig[:, 1:2] * o_sel


def _nsa_decode(q, kcb, vcb, pool_ks, pool_vs, page_table, kn, vn, gate, n_cmp):
    n, n_rows, _ = q.shape
    n_pages = page_table.shape[1]
    n_past = n_pages * PAGE_SIZE
    n_new = kn.shape[1]
    n_slc = -(-(n_past + n_new) // SEL_BLOCK)
    n_slc_pad = -(-n_slc // LANES) * LANES
    wov = _overlap_weights(n_cmp, kcb.shape[1], n_slc, n_slc_pad)
    whole = lambda a: pl.BlockSpec((1,) + a.shape[1:], lambda i, pt: (i, 0, 0))
    pools = [a.reshape(a.shape[0], PAGE_SIZE, C_KV_WIDTH) for a in (pool_ks, pool_vs)]
    return pl.pallas_call(
        functools.partial(_nsa_decode_body, n_pages=n_pages, n_cmp=n_cmp, n_slc=n_slc),
        grid_spec=pltpu.PrefetchScalarGridSpec(
            num_scalar_prefetch=1,
            grid=(n,),
            in_specs=[whole(q), whole(kcb), whole(vcb), pl.BlockSpec(memory_space=pl.ANY),
                      pl.BlockSpec(memory_space=pl.ANY), whole(kn), whole(vn), whole(gate),
                      pl.BlockSpec(wov.shape, lambda i, pt: (0, 0))],
            out_specs=pl.BlockSpec((1, n_rows, HEAD_DIM), lambda i, pt: (i, 0, 0)),
            scratch_shapes=[pltpu.VMEM((2, n_past, C_KV_WIDTH), F32), pltpu.VMEM((2, n_past, C_KV_WIDTH), F32),
                            pltpu.VMEM((n_slc_pad, n_past), BF16),
                            pltpu.SemaphoreType.DMA((2,)), pltpu.SemaphoreType.DMA((2,))]),
        out_shape=jax.ShapeDtypeStruct((n, n_rows, HEAD_DIM), F32),
        compiler_params=_params("arbitrary"),
        name="nsa_decode",
    )(page_table, q, kcb, vcb, *pools, kn, vn, gate, wov)


def _odd_out_body(x_ref, a_ref, b_ref, z_ref, w_ref, gf_ref, y_ref):
    mixed = ((a_ref[...] + b_ref[...]) * _silu(z_ref[...])).astype(BF16)
    h = x_ref[...] + _dot(mixed, w_ref[...])
    y_ref[...] = h * lax.rsqrt(jnp.mean(h * h, axis=-1, keepdims=True) + RMS_EPS) * gf_ref[...]


def _odd_out(x2d, o_cs, o_win, z, w_out, final_gain, tm):
    m, d = x2d.shape
    row = pl.BlockSpec((tm, d), lambda i: (i, 0))
    return pl.pallas_call(
        _odd_out_body,
        grid=(m // tm,),
        in_specs=[row, row, row, row, pl.BlockSpec(w_out.shape, lambda i: (0, 0)),
                  pl.BlockSpec((1, d), lambda i: (0, 0))],
        out_specs=row,
        out_shape=jax.ShapeDtypeStruct((m, d), F32),
        compiler_params=_params("parallel"),
        name="odd_out",
    )(x2d, o_cs, o_win, z, w_out, final_gain.reshape(1, d))


def _even_outs():
    a, bk = A_WIDTH, B_HEADS * B_DK
    return [(0, a, BF16),
            (a, a, BF16), (a, a, F32),
            (2 * a, a, BF16), (2 * a, a, F32),
            (3 * a, a, F32),
            (4 * a, bk, F32), (4 * a + bk, bk, F32),
            (4 * a + 2 * bk, B_WIDTH, F32), (4 * a + 2 * bk + B_WIDTH, B_WIDTH, F32)]


def _even_layer_prompt(x, gain, w_in, w_out):
    b, t, d = x.shape
    x2d = x.reshape(b * t, d)
    qa, ka, ka32, va, va32, za, qb, kb, vb, zb = _norm_proj(x2d, gain, w_in, _even_outs(), 512)
    seq = lambda u: u.reshape(b, t, u.shape[-1])
    pats = []
    for window, dil in A_PATTERNS:
        n_sub = t // dil
        pats.append(_banded_attention(seq(qa), seq(ka), seq(va), dil=dil, n_heads=A_HEADS, n_groups=A_HEADS,
                                      lookback=window // dil, tqb=min(512, n_sub), with_lse=True))
    pats = [[a.reshape(b * t, A_WIDTH) for a in pair] for pair in pats]
    chunk = B_CHUNK if t % B_CHUNK == 0 else t
    s0 = jnp.zeros((b, B_HEADS, B_DK, B_DV), F32)
    ob, s_fin = _retention(seq(qb), seq(kb), seq(vb), s0, jnp.arange(t), chunk=chunk, c_real=chunk)
    y = _even_out(x2d, pats, za, ob.reshape(b * t, B_WIDTH), zb, w_out, 256)
    keep = min(A_PATTERNS[-1][0], t)
    heads = lambda u: seq(u)[:, t - keep:].reshape(b, keep, A_HEADS, HEAD_DIM)
    return y.reshape(b, t, d), heads(ka32), heads(va32), s_fin


def _even_layer_sample(x, cache_k, cache_v, state, gain, w_in, w_out, past_len):
    n, s_len, d = x.shape
    x2d = x.reshape(n * s_len, d)
    qa, _, ka32, _, va32, za, qb, kb, vb, zb = _norm_proj(x2d, gain, w_in, _even_outs(), n * s_len)
    n_past = cache_k.shape[1]
    flat = lambda u: u.reshape(n, n_past, A_WIDTH)
    seq = lambda u: u.reshape(n, s_len, u.shape[-1])
    oa = _decode_attention(qa.astype(F32).reshape(n, s_len * A_HEADS, HEAD_DIM), flat(cache_k), flat(cache_v),
                           seq(ka32), seq(va32), n_heads=A_HEADS, n_blk=A_HEADS, patterns=A_PATTERNS)
    oa = oa.reshape(n * s_len, A_WIDTH)
    chunk = 16
    pad = lambda u: jnp.pad(seq(u), ((0, 0), (0, chunk - s_len), (0, 0)))
    pos = past_len + jnp.arange(chunk)
    ob, s_new = _retention(pad(qb), pad(kb), pad(vb), state, pos, chunk=chunk, c_real=s_len)
    ob = ob[:, :s_len].reshape(n * s_len, B_WIDTH)
    zero = jnp.zeros_like(oa)
    y = _even_out(x2d, [[oa, zero]] * 3, za, ob, zb, w_out, n * s_len)
    heads = lambda u: u.reshape(n, s_len, A_HEADS, HEAD_DIM)
    return y.reshape(n, s_len, d), heads(ka32), heads(va32), s_new


def _odd_weights(w_in):
    kv0 = C_WIDTH
    g0 = kv0 + 6 * C_KV_WIDTH
    z0 = g0 + 3 * C_HEADS
    gl = jnp.pad(w_in[:, g0:z0], ((0, 0), (0, LANES - 3 * C_HEADS)))
    return jnp.concatenate([w_in[:, :g0], w_in[:, z0:], gl], axis=1).astype(BF16)


def _odd_outs():
    outs = [(0, C_WIDTH, BF16)]
    off = C_WIDTH
    for name in ("kc", "vc", "ks", "vs", "kw", "vw"):
        outs.append((off, C_KV_WIDTH, F32))
        if name not in ("kc", "vc"):
            outs.append((off, C_KV_WIDTH, BF16))
        off += C_KV_WIDTH
    outs.append((off, C_WIDTH, F32))
    outs.append((off + C_WIDTH, LANES, F32))
    return outs


def _odd_layer_prompt(x, gain, w_in, w_out, cw_k, cw_v, final_gain):
    b, t, d = x.shape
    x2d = x.reshape(b * t, d)
    q, kc, vc, ks32, ks, vs32, vs, kw32, kw, vw32, vw, z, gl = _norm_proj(x2d, gain, w_in, _odd_outs(), 512)
    seq = lambda u: u.reshape(b, t, u.shape[-1])
    n_cmp = (t - CMP_LEN) // CMP_STRIDE + 1
    pieces = lambda u: u.reshape(b, t // CMP_STRIDE, CMP_STRIDE * C_KV_WIDTH)
    kcb = _compress(pieces(kc), cw_k)
    vcb = _compress(pieces(vc), cw_v)
    o_cs = _nsa_prompt(seq(q), kcb, vcb, seq(ks), seq(vs), seq(gl), n_cmp)
    o_win, = _banded_attention(seq(q), seq(kw), seq(vw), dil=1, n_heads=C_HEADS, n_groups=C_KV_HEADS,
                               lookback=C_WINDOW, tqb=min(512, t), gate=seq(gl), gate_col=lambda h: 3 * h + 2)
    y = _odd_out(x2d, o_cs.reshape(b * t, C_WIDTH), o_win.reshape(b * t, C_WIDTH), z, w_out, final_gain, 256)
    keep = min(C_WINDOW, t)
    kv = lambda u: seq(u).reshape(b, t, C_KV_HEADS, HEAD_DIM)
    return (y.reshape(b, t, d), kv(kc), kv(vc), kv(ks32), kv(vs32), kv(kw32)[:, t - keep:], kv(vw32)[:, t - keep:])


def _odd_layer_sample(x, page_table, pool_kc, pool_vc, pool_ks, pool_vs, buf_kw, buf_vw, gain, w_in, w_out,
                      cw_k, cw_v, final_gain):
    n, s_len, d = x.shape
    x2d = x.reshape(n * s_len, d)
    q, kc, vc, ks32, _, vs32, _, kw32, _, vw32, _, z, gl = _norm_proj(x2d, gain, w_in, _odd_outs(), n * s_len)
    seq = lambda u: u.reshape(n, s_len, u.shape[-1])
    n_past = page_table.shape[1] * PAGE_SIZE
    n_cmp = (n_past + s_len - CMP_LEN) // CMP_STRIDE + 1
    assert CMP_STRIDE * (n_cmp - 1) + CMP_LEN <= n_past
    kcb = _compress_paged(pool_kc, page_table, cw_k)
    vcb = _compress_paged(pool_vc, page_table, cw_v)
    rows = lambda u: u.astype(F32).reshape(n, s_len * C_HEADS, HEAD_DIM)
    gate = gl[:, :3 * C_HEADS].reshape(n, s_len * C_HEADS, 3)
    o_cs = _nsa_decode(rows(q), kcb, vcb, pool_ks, pool_vs, page_table, seq(ks32), seq(vs32), gate, n_cmp)
    n_win = buf_kw.shape[1]
    o_win = _decode_attention(rows(q), buf_kw.reshape(n, n_win, C_KV_WIDTH), buf_vw.reshape(n, n_win, C_KV_WIDTH),
                              seq(kw32), seq(vw32), n_heads=C_HEADS, n_blk=C_KV_HEADS,
                              patterns=((C_WINDOW, 1),), gate=gate, gate_col=2)
    y = _odd_out(x2d, o_cs.reshape(n * s_len, C_WIDTH), o_win.reshape(n * s_len, C_WIDTH), z, w_out, final_gain,
                 n * s_len)
    kv = lambda u: u.reshape(n, s_len, C_KV_HEADS, HEAD_DIM)
    return y.reshape(n, s_len, d), kv(kc), kv(vc), kv(ks32), kv(vs32), kv(kw32), kv(vw32)


def kernel(x_prompt, x_sample, cache_a_k, cache_a_v, state_ret, cache_c_kcmp, cache_c_vcmp, cache_c_ksel,
           cache_c_vsel, cache_c_kwin, cache_c_vwin, page_table, even_norm, even_w_in, even_w_out, odd_norm,
           odd_w_in, odd_w_out, cmp_pos_k, cmp_w1_k, cmp_w2_k, cmp_pos_v, cmp_w1_v, cmp_w2_v, final_norm):
    assert even_norm.shape[0] == 1 and odd_norm.shape[0] == 1
    past_len = page_table.shape[1] * PAGE_SIZE
    w_in_e = even_w_in[0].astype(BF16)
    w_out_e = even_w_out[0].astype(BF16)
    w_in_o = _odd_weights(odd_w_in[0])
    w_out_o = odd_w_out[0].astype(BF16)
    cw_k = _compress_weights(cmp_pos_k[0], cmp_w1_k[0], cmp_w2_k[0])
    cw_v = _compress_weights(cmp_pos_v[0], cmp_w1_v[0], cmp_w2_v[0])

    hp, ak_p, av_p, ret_p = _even_layer_prompt(x_prompt, even_norm[0], w_in_e, w_out_e)
    hs, ak_s, av_s, ret_s = _even_layer_sample(x_sample, cache_a_k[0], cache_a_v[0], state_ret[0], even_norm[0],
                                               w_in_e, w_out_e, past_len)
    yp, *rows_p = _odd_layer_prompt(hp, odd_norm[0], w_in_o, w_out_o, cw_k, cw_v, final_norm)
    ys, *rows_s = _odd_layer_sample(hs, page_table, cache_c_kcmp[0], cache_c_vcmp[0], cache_c_ksel[0],
                                    cache_c_vsel[0], cache_c_kwin[0], cache_c_vwin[0], odd_norm[0], w_in_o, w_out_o,
                                    cw_k, cw_v, final_norm)
    lead = lambda u: u[None]
    out = [yp, ys, lead(ak_p), lead(ak_s), lead(av_p), lead(av_s), lead(ret_p), lead(ret_s)]
    for rp, rs in zip(rows_p, rows_s):
        out += [lead(rp), lead(rs)]
    return tuple(out)
```

```python
import functools

import numpy as np
import jax
import jax.numpy as jnp
from jax import lax
from jax.experimental import pallas as pl
from jax.experimental.pallas import tpu as pltpu

F32 = jnp.float32
BF16 = jnp.bfloat16

HEAD_DIM = 64
A_HEADS = 8
A_WIDTH = A_HEADS * HEAD_DIM
A_PATTERNS = ((128, 1), (512, 4), (2048, 16))
B_HEADS = 4
B_DK = 64
B_DV = 128
B_WIDTH = B_HEADS * B_DV
B_CHUNK = 128
ROPE_BASE = 10000.0
GN_EPS = 1e-5
C_HEADS = 16
C_KV_HEADS = 2
C_WIDTH = C_HEADS * HEAD_DIM
C_KV_WIDTH = C_KV_HEADS * HEAD_DIM
CMP_LEN = 32
CMP_STRIDE = 16
CMP_HIDDEN = 256
SEL_BLOCK = 64
SEL_TOPK = 16
SEL_FORCE = 1e9
C_WINDOW = 512
PAGE_SIZE = 128
RMS_EPS = 1e-6
NEG_INF = -1e30

LANES = 128
ATTN_TQ = 128
VMEM_LIMIT = 56 * 1024 * 1024


def _params(*sem):
    return pltpu.CompilerParams(dimension_semantics=sem, vmem_limit_bytes=VMEM_LIMIT)


def _dot(a, b):
    return jnp.dot(a, b, preferred_element_type=F32)


def _dot_nt(a, b):
    return lax.dot_general(a, b, (((1,), (1,)), ((), ())), preferred_element_type=F32)


def _dot_tn(a, b):
    return lax.dot_general(a, b, (((0,), (0,)), ((), ())), preferred_element_type=F32)


def _idiv(x, n):
    assert n & (n - 1) == 0
    return jnp.right_shift(x, n.bit_length() - 1)


def _imod(x, n):
    assert n & (n - 1) == 0
    return jnp.bitwise_and(x, n - 1)


def _split3_dot(a, w):
    hi = a.astype(BF16)
    r1 = a - hi.astype(F32)
    mid = r1.astype(BF16)
    lo = (r1 - mid.astype(F32)).astype(BF16)
    return _dot(hi, w) + _dot(mid, w) + _dot(lo, w)


def _norm_proj_body(x_ref, g_ref, wt_ref, *o_refs, outs):
    x = x_ref[...]
    y = x * lax.rsqrt(jnp.mean(x * x, axis=-1, keepdims=True) + RMS_EPS) * g_ref[...]
    yb = y.astype(BF16)
    done = {}
    for o_ref, (off, width, _, feature_major) in zip(o_refs, outs):
        key = (off, width, feature_major)
        if key not in done:
            w_rows = wt_ref[off:off + width, :]
            done[key] = _dot_nt(w_rows, yb) if feature_major else _dot_nt(yb, w_rows)
        if feature_major:
            o_ref[0] = done[key].astype(o_ref.dtype)
        else:
            o_ref[...] = done[key].astype(o_ref.dtype)


def _norm_proj(x2d, gain, w_t, outs, tm, seq_len):
    m, d = x2d.shape
    n = w_t.shape[0]
    tiles = seq_len // tm
    assert seq_len % tm == 0 and m % seq_len == 0
    out_specs, out_shape = [], []
    for _, width, dt, feature_major in outs:
        if feature_major:
            out_specs.append(pl.BlockSpec((1, width, tm), lambda i: (i // tiles, 0, i % tiles)))
            out_shape.append(jax.ShapeDtypeStruct((m // seq_len, width, seq_len), dt))
        else:
            out_specs.append(pl.BlockSpec((tm, width), lambda i: (i, 0)))
            out_shape.append(jax.ShapeDtypeStruct((m, width), dt))
    return pl.pallas_call(
        functools.partial(_norm_proj_body, outs=outs),
        grid=(m // tm,),
        in_specs=[pl.BlockSpec((tm, d), lambda i: (i, 0)),
                  pl.BlockSpec((1, d), lambda i: (0, 0)),
                  pl.BlockSpec((n, d), lambda i: (0, 0))],
        out_specs=out_specs,
        out_shape=out_shape,
        compiler_params=_params("parallel"),
        name="norm_proj",
    )(x2d, gain.reshape(1, d), w_t)


def _banded_body(*refs, n_heads, n_groups, lookback, tqb, with_lse, gate_col):
    q_ref, k_ref, v_ref = refs[:3]
    refs = refs[3:]
    g_ref = None
    if gate_col is not None:
        g_ref, refs = refs[0], refs[1:]
    o_ref = refs[0]
    lse_ref = refs[1] if with_lse else None
    tq = ATTN_TQ
    n_prev = -(-lookback // tq)
    span = (n_prev + 1) * tq
    rep = n_heads // n_groups
    blk = pl.program_id(2)

    def sub(u, carry):
        row0 = pl.multiple_of(u * tq, tq)
        s0 = blk * tqb + row0
        kstart = pl.multiple_of(jnp.maximum(s0 - n_prev * tq, 0), tq)
        qpos = s0 + lax.broadcasted_iota(jnp.int32, (tq, span), 0)
        kpos = kstart + lax.broadcasted_iota(jnp.int32, (tq, span), 1)
        dist = qpos - kpos
        bias = jnp.where(dist >= 0, jnp.where(dist <= lookback, 0.0, NEG_INF), NEG_INF)
        if g_ref is not None:
            sig = jax.nn.sigmoid(g_ref[0, pl.ds(row0, tq), :])
        kv_cols = [slice(g * HEAD_DIM, (g + 1) * HEAD_DIM) for g in range(n_groups)]
        keys = [k_ref[0, pl.ds(kstart, span), c] for c in kv_cols]
        vals = [v_ref[0, pl.ds(kstart, span), c] for c in kv_cols]
        cols = [slice(h * HEAD_DIM, (h + 1) * HEAD_DIM) for h in range(n_heads)]
        scores = [_dot_nt(q_ref[0, pl.ds(row0, tq), c] * (HEAD_DIM ** -0.5), keys[h // rep]) + bias
                  for h, c in enumerate(cols)]
        maxes = [jnp.max(s, axis=-1, keepdims=True) for s in scores]
        probs = [jnp.exp(s - m) for s, m in zip(scores, maxes)]
        sums = [jnp.sum(p, axis=-1, keepdims=True) for p in probs]
        outs = [_dot(p.astype(BF16), vals[h // rep]) for h, p in enumerate(probs)]
        for h, (c, m, l, o) in enumerate(zip(cols, maxes, sums, outs)):
            o = o / l
            if g_ref is not None:
                gc = gate_col(h)
                o = o * sig[:, gc:gc + 1]
            o_ref[0, pl.ds(row0, tq), c] = o
            if lse_ref is not None:
                lse_ref[0, pl.ds(row0, tq), c] = jnp.broadcast_to(m + jnp.log(l), (tq, HEAD_DIM))
        return carry

    lax.fori_loop(0, tqb // tq, sub, 0)


def _banded_attention(q, k, v, *, dil, n_heads, n_groups, lookback, tqb, with_lse=False, gate=None, gate_col=None):
    b, t, wq = q.shape
    wk = k.shape[-1]
    n_sub = t // dil
    assert n_sub % tqb == 0 and tqb % ATTN_TQ == 0
    assert n_sub >= (-(-lookback // ATTN_TQ) + 1) * ATTN_TQ
    args = [q.reshape(b, n_sub, dil * wq), k.reshape(b, n_sub, dil * wk), v.reshape(b, n_sub, dil * wk)]
    in_specs = [pl.BlockSpec((1, tqb, wq), lambda bi, r, i: (bi, i, r)),
                pl.BlockSpec((1, n_sub, wk), lambda bi, r, i: (bi, 0, r)),
                pl.BlockSpec((1, n_sub, wk), lambda bi, r, i: (bi, 0, r))]
    if gate is not None:
        assert dil == 1
        args.append(gate)
        in_specs.append(pl.BlockSpec((1, tqb, gate.shape[-1]), lambda bi, r, i: (bi, i, 0)))
    n_out = 2 if with_lse else 1
    res = pl.pallas_call(
        functools.partial(_banded_body, n_heads=n_heads, n_groups=n_groups, lookback=lookback, tqb=tqb,
                          with_lse=with_lse, gate_col=gate_col if gate is not None else None),
        grid=(b, dil, n_sub // tqb),
        in_specs=in_specs,
        out_specs=[pl.BlockSpec((1, tqb, wq), lambda bi, r, i: (bi, i, r))] * n_out,
        out_shape=[jax.ShapeDtypeStruct((b, n_sub, dil * wq), F32)] * n_out,
        compiler_params=_params("parallel", "parallel", "arbitrary"),
        name=f"banded_attention_d{dil}",
    )(*args)
    return [a.reshape(b, t, wq) for a in res]


def _retention_body(q_ref, k_ref, v_ref, cos_ref, sin_ref, qd_ref, kd_ref, intra_ref, cd_ref, s0_ref,
                    o_ref, sfin_ref, state):
    c = pl.program_id(1)

    @pl.when(c == 0)
    def _():
        state[...] = s0_ref[0]

    cos = cos_ref[...]
    sin = sin_ref[...]
    width = cos.shape[-1]
    half = B_DK // 2
    lane = _imod(lax.broadcasted_iota(jnp.int32, cos.shape, 1), B_DK)

    def rotary(x):
        fwd = pltpu.roll(x, half, 1)
        bwd = pltpu.roll(x, width - half, 1)
        return x * cos + jnp.where(lane < half, -bwd, fwd) * sin

    q = rotary(q_ref[0])
    k = rotary(k_ref[0]) * (B_DK ** -0.5)
    q_in = q.astype(BF16)
    k_in = k.astype(BF16)
    q_st = (q * qd_ref[...]).astype(BF16)
    k_st = (k * kd_ref[...]).astype(BF16)
    v = v_ref[0].astype(BF16)
    for h in range(B_HEADS):
        kc = slice(h * B_DK, (h + 1) * B_DK)
        vc = slice(h * B_DV, (h + 1) * B_DV)
        st = state[h]
        scores = _dot_nt(q_in[:, kc], k_in[:, kc]) * intra_ref[h]
        o_ref[0, :, vc] = _dot(scores.astype(BF16), v[:, vc]) + _dot(q_st[:, kc], st.astype(BF16))
        state[h] = st * cd_ref[h] + _dot_tn(k_st[:, kc], v[:, vc])

    @pl.when(c == pl.num_programs(1) - 1)
    def _():
        sfin_ref[0] = state[...]


def _retention(q, k, v, s0, pos, *, chunk, c_real):
    n, t, _ = q.shape
    half = B_DK // 2
    freqs = ROPE_BASE ** (-jnp.arange(half, dtype=F32) / half)
    ang = pos.astype(F32)[:, None] * freqs[None, :]
    cos = jnp.tile(jnp.cos(ang), (1, 2 * B_HEADS))
    sin = jnp.tile(jnp.sin(ang), (1, 2 * B_HEADS))
    log_g = jnp.log1p(-jnp.exp2(-5.0 - jnp.arange(B_HEADS, dtype=F32)))
    i = jnp.arange(chunk, dtype=F32)
    diff = i[:, None] - i[None, :]
    intra = jnp.where(diff >= 0, jnp.exp(log_g[:, None, None] * jnp.maximum(diff, 0.0)), 0.0)
    q_decay = jnp.repeat(jnp.exp(log_g[None, :] * (i[:, None] + 1.0)), B_DK, axis=1)
    k_decay = jnp.repeat(jnp.exp(log_g[None, :] * (c_real - 1.0 - i[:, None])), B_DK, axis=1)
    chunk_decay = jnp.broadcast_to(jnp.exp(log_g * c_real)[:, None, None], (B_HEADS, 1, B_DV))
    wqk = B_HEADS * B_DK
    return pl.pallas_call(
        _retention_body,
        grid=(n, t // chunk),
        in_specs=[pl.BlockSpec((1, chunk, wqk), lambda b, c: (b, c, 0)),
                  pl.BlockSpec((1, chunk, wqk), lambda b, c: (b, c, 0)),
                  pl.BlockSpec((1, chunk, B_WIDTH), lambda b, c: (b, c, 0)),
                  pl.BlockSpec((chunk, wqk), lambda b, c: (c, 0)),
                  pl.BlockSpec((chunk, wqk), lambda b, c: (c, 0)),
                  pl.BlockSpec((chunk, wqk), lambda b, c: (0, 0)),
                  pl.BlockSpec((chunk, wqk), lambda b, c: (0, 0)),
                  pl.BlockSpec((B_HEADS, chunk, chunk), lambda b, c: (0, 0, 0)),
                  pl.BlockSpec((B_HEADS, 1, B_DV), lambda b, c: (0, 0, 0)),
                  pl.BlockSpec((1, B_HEADS, B_DK, B_DV), lambda b, c: (b, 0, 0, 0))],
        out_specs=[pl.BlockSpec((1, chunk, B_WIDTH), lambda b, c: (b, c, 0)),
                   pl.BlockSpec((1, B_HEADS, B_DK, B_DV), lambda b, c: (b, 0, 0, 0))],
        out_shape=[jax.ShapeDtypeStruct((n, t, B_WIDTH), F32),
                   jax.ShapeDtypeStruct((n, B_HEADS, B_DK, B_DV), F32)],
        scratch_shapes=[pltpu.VMEM((B_HEADS, B_DK, B_DV), F32)],
        compiler_params=_params("parallel", "arbitrary"),
        name="retention",
    )(q, k, v, cos, sin, q_decay, k_decay, intra, chunk_decay, s0)


def _silu(z):
    return z * jax.nn.sigmoid(z)


def _even_out_body(x_ref, o1, l1, o2, l2, o3, l3, za_ref, ob_ref, zb_ref, w_ref, y_ref):
    m = jnp.maximum(jnp.maximum(l1[...], l2[...]), l3[...])
    e1 = jnp.exp(l1[...] - m)
    e2 = jnp.exp(l2[...] - m)
    e3 = jnp.exp(l3[...] - m)
    oa = (e1 * o1[...] + e2 * o2[...] + e3 * o3[...]) / (e1 + e2 + e3)
    ga = (oa * _silu(za_ref[...])).astype(BF16)
    acc = x_ref[...] + _dot(ga, w_ref[0:A_WIDTH, :])
    ob = ob_ref[...]
    zb = zb_ref[...]
    for h in range(B_HEADS):
        cols = slice(h * B_DV, (h + 1) * B_DV)
        seg = ob[:, cols]
        mu = jnp.mean(seg, axis=-1, keepdims=True)
        cen = seg - mu
        var = jnp.mean(cen * cen, axis=-1, keepdims=True)
        gb = (cen * lax.rsqrt(var + GN_EPS) * _silu(zb[:, cols])).astype(BF16)
        acc = acc + _dot(gb, w_ref[A_WIDTH + h * B_DV:A_WIDTH + (h + 1) * B_DV, :])
    y_ref[...] = acc


def _even_out(x2d, pats, za, ob, zb, w_out, tm):
    m, d = x2d.shape
    row = lambda width: pl.BlockSpec((tm, width), lambda i: (i, 0))
    flat = [a for pair in pats for a in pair]
    return pl.pallas_call(
        _even_out_body,
        grid=(m // tm,),
        in_specs=[row(d)] + [row(A_WIDTH)] * 6 + [row(A_WIDTH), row(B_WIDTH), row(B_WIDTH),
                                                   pl.BlockSpec(w_out.shape, lambda i: (0, 0))],
        out_specs=row(d),
        out_shape=jax.ShapeDtypeStruct((m, d), F32),
        compiler_params=_params("parallel"),
        name="even_out",
    )(x2d, *flat, za, ob, zb, w_out)


def _compress_math(x, p2_ref, wbig_ref, w2_ref):
    n_piece = x.shape[0]
    a = _dot(x.astype(BF16), wbig_ref[...])
    pc = _dot(p2_ref[...], wbig_ref[...])
    hid = []
    for g in range(C_KV_HEADS):
        lo = slice(g * 2 * CMP_HIDDEN, g * 2 * CMP_HIDDEN + CMP_HIDDEN)
        hi = slice(g * 2 * CMP_HIDDEN + CMP_HIDDEN, (g + 1) * 2 * CMP_HIDDEN)
        nxt = pltpu.roll(a[:, hi], n_piece - 1, 0)
        pos = (pc[0:1, lo] + pc[2:3, lo]) + (pc[1:2, hi] + pc[3:4, hi])
        hid.append(_silu(a[:, lo] + nxt + pos))
    return _dot(jnp.concatenate(hid, axis=-1).astype(BF16), w2_ref[...])


def _compress_body(x_ref, p2_ref, wbig_ref, w2_ref, o_ref):
    o_ref[0] = _compress_math(x_ref[0], p2_ref, wbig_ref, w2_ref)


def _compress_weights(pos_emb, w1, w2):
    per = CMP_LEN // CMP_STRIDE
    w1r = w1.reshape(per, CMP_STRIDE, HEAD_DIM, CMP_HIDDEN)
    eye = jnp.eye(C_KV_HEADS, dtype=w1.dtype)
    wbig = jnp.einsum('jldf,gh->lgdhjf', w1r, eye).reshape(CMP_STRIDE * C_KV_WIDTH, C_KV_HEADS * per * CMP_HIDDEN)
    w2bd = jnp.einsum('fd,gh->gfhd', w2, eye).reshape(C_KV_HEADS * CMP_HIDDEN, C_KV_WIDTH)
    halves = jnp.tile(pos_emb.reshape(per, CMP_STRIDE, 1, HEAD_DIM), (1, 1, C_KV_HEADS, 1)).reshape(per, -1)
    hi = halves.astype(BF16)
    lo = (halves - hi.astype(F32)).astype(BF16)
    p2 = jnp.concatenate([hi, lo, jnp.zeros((16 - 2 * per, halves.shape[1]), BF16)], axis=0)
    return p2, wbig.astype(BF16), w2bd.astype(BF16)


def _compress(pieces, cw):
    n, n_piece, width = pieces.shape
    p2, wbig, w2bd = cw
    const = lambda a: pl.BlockSpec(a.shape, lambda i: (0,) * a.ndim)
    return pl.pallas_call(
        _compress_body,
        grid=(n,),
        in_specs=[pl.BlockSpec((1, n_piece, width), lambda i: (i, 0, 0)), const(p2), const(wbig), const(w2bd)],
        out_specs=pl.BlockSpec((1, n_piece, C_KV_WIDTH), lambda i: (i, 0, 0)),
        out_shape=jax.ShapeDtypeStruct((n, n_piece, C_KV_WIDTH), F32),
        compiler_params=_params("parallel"),
        name="compress",
    )(pieces, p2, wbig, w2bd)


def _page_dma(pt_ref, pool_ref, buf_ref, sem_ref, n, slot, n_pages, start):
    for p in range(n_pages):
        page = pt_ref[n, p] if start else 0
        cp = pltpu.make_async_copy(pool_ref.at[page], buf_ref.at[slot, p], sem_ref.at[slot])
        if start:
            cp.start()
        else:
            cp.wait()


def _compress_paged_body(pt_ref, pool_ref, perm_ref, p2_ref, wbig_ref, w2_ref, o_ref, buf, pieces, sem, *, n_pages):
    n = pl.program_id(0)
    slot = n % 2
    rows = PAGE_SIZE // CMP_STRIDE

    @pl.when(n == 0)
    def _():
        _page_dma(pt_ref, pool_ref, buf, sem, 0, 0, n_pages, True)

    @pl.when(n + 1 < pl.num_programs(0))
    def _():
        _page_dma(pt_ref, pool_ref, buf, sem, n + 1, 1 - slot, n_pages, True)

    _page_dma(pt_ref, pool_ref, buf, sem, n, slot, n_pages, False)
    perm = perm_ref[...]

    group = 8
    assert n_pages % group == 0

    def unfold(i, carry):
        by_row = [_dot_nt(perm, buf[slot, i * group + j].astype(BF16)) for j in range(group)]
        for j in range(group):
            r0 = pl.multiple_of((i * group + j) * rows, rows)
            for l in range(CMP_STRIDE):
                pieces[pl.ds(r0, rows), l * C_KV_WIDTH:(l + 1) * C_KV_WIDTH] = by_row[j][l * rows:(l + 1) * rows, :]
        return carry

    lax.fori_loop(0, n_pages // group, unfold, 0)
    o_ref[0] = _compress_math(pieces[...], p2_ref, wbig_ref, w2_ref)


def _compress_paged(pool, page_table, cw):
    n, n_pages = page_table.shape
    rows = PAGE_SIZE // CMP_STRIDE
    n_piece = n_pages * rows
    p2, wbig, w2bd = cw
    perm = np.zeros((PAGE_SIZE, PAGE_SIZE), np.float32)
    for l in range(CMP_STRIDE):
        for piece in range(rows):
            perm[l * rows + piece, CMP_STRIDE * piece + l] = 1.0
    perm = jnp.asarray(perm, BF16)
    pages = jnp.transpose(pool, (0, 2, 3, 1)).reshape(pool.shape[0], C_KV_WIDTH, PAGE_SIZE)
    const = lambda a: pl.BlockSpec(a.shape, lambda i, pt: (0,) * a.ndim)
    return pl.pallas_call(
        functools.partial(_compress_paged_body, n_pages=n_pages),
        grid_spec=pltpu.PrefetchScalarGridSpec(
            num_scalar_prefetch=1,
            grid=(n,),
            in_specs=[pl.BlockSpec(memory_space=pl.ANY), const(perm), const(p2), const(wbig), const(w2bd)],
            out_specs=pl.BlockSpec((1, n_piece, C_KV_WIDTH), lambda i, pt: (i, 0, 0)),
            scratch_shapes=[pltpu.VMEM((2, n_pages, C_KV_WIDTH, PAGE_SIZE), F32),
                            pltpu.VMEM((n_piece, CMP_STRIDE * C_KV_WIDTH), F32),
                            pltpu.SemaphoreType.DMA((2,))]),
        out_shape=jax.ShapeDtypeStruct((n, n_piece, C_KV_WIDTH), F32),
        compiler_params=_params("arbitrary"),
        name="compress_paged",
    )(page_table, pages, perm, p2, wbig, w2bd)


def _overlap_weights(n_cmp, n_cmp_pad, n_slc, n_slc_pad):
    i = np.arange(n_cmp_pad)[:, None]
    j = np.arange(n_slc_pad)[None, :]
    ov = (np.minimum(i * CMP_STRIDE + CMP_LEN, (j + 1) * SEL_BLOCK) - np.maximum(i * CMP_STRIDE, j * SEL_BLOCK))
    w = np.where((i < n_cmp) & (j < n_slc), np.clip(ov, 0, None) / CMP_LEN, 0.0)
    return jnp.asarray(w, dtype=BF16)


def _select_blocks(p_slc, qpos, n_slc):
    shape = p_slc.shape
    lane = lax.broadcasted_iota(jnp.int32, shape, 1)
    cur = _idiv(qpos, SEL_BLOCK)
    forced = (lane == 0) | (lane == cur) | (lane == cur - 1)
    causal = lane * SEL_BLOCK <= qpos
    score = jnp.where(forced, SEL_FORCE, jnp.where(causal, p_slc, -SEL_FORCE))
    score = jnp.where(lane < n_slc, score, -3.0 * SEL_FORCE)
    lane_f = lane.astype(F32)
    sel = jnp.zeros(shape, F32)
    for _ in range(min(SEL_TOPK, n_slc)):
        m = jnp.max(score, axis=-1, keepdims=True)
        first = jnp.min(jnp.where(score == m, lane_f, 1e9), axis=-1, keepdims=True)
        hit = lane_f == first
        sel = jnp.where(hit, jnp.where(m > -0.5 * SEL_FORCE, 1.0, 0.0), sel)
        score = jnp.where(hit, -4.0 * SEL_FORCE, score)
    return sel


def _select_blocks_t(p_slc, qpos, n_slc):
    n_rows = -(-n_slc // 8) * 8
    shape = (n_rows, p_slc.shape[1])
    blk = lax.broadcasted_iota(jnp.int32, shape, 0)
    cur = _idiv(qpos, SEL_BLOCK)
    forced = (blk == 0) | (blk == cur) | (blk == cur - 1)
    causal = blk * SEL_BLOCK <= qpos
    score = jnp.where(forced, SEL_FORCE, jnp.where(causal, p_slc[:n_rows], -SEL_FORCE))
    score = jnp.where(blk < n_slc, score, -3.0 * SEL_FORCE)
    tiles = [score[8 * v:8 * v + 8] for v in range(n_rows // 8)]
    sub = lax.broadcasted_iota(jnp.int32, (8, shape[1]), 0)
    ranks = [jnp.zeros((8, shape[1]), F32) for _ in tiles]
    for i in range(n_slc):
        row = tiles[i // 8][i % 8:i % 8 + 1]
        for v, tile in enumerate(tiles):
            ge = jnp.where(row >= tile, 1.0, 0.0)
            gt = jnp.where(row > tile, 1.0, 0.0)
            if 8 * v > i:
                ranks[v] = ranks[v] + ge
            elif 8 * v + 7 <= i:
                ranks[v] = ranks[v] + gt
            else:
                ranks[v] = ranks[v] + jnp.where(sub > i - 8 * v, ge, gt)
    rank = jnp.concatenate(ranks, axis=0)
    return jnp.where(rank < min(SEL_TOPK, n_slc), jnp.where(score > -0.5 * SEL_FORCE, 1.0, 0.0), 0.0)


def _nsa_prompt_body(q_ref, kc_ref, vct_ref, ks_ref, vst_ref, g_ref, wovt_ref, o_ref, bias_sc, m_sc, l_sc, acc_sc,
                     *, n_cmp, n_slc, tk):
    tq = ATTN_TQ
    rep = C_HEADS // C_KV_HEADS
    q0 = pl.program_id(1) * tq
    n_cmp_pad = kc_ref.shape[1]
    n_blk_tile = tk // SEL_BLOCK
    sig_t = jax.nn.sigmoid(g_ref[0])
    q_t = [jnp.transpose(q_ref[0, :, c * LANES:(c + 1) * LANES].astype(F32)) * (HEAD_DIM ** -0.5)
           for c in range(C_WIDTH // LANES)]
    qpos = q0 + lax.broadcasted_iota(jnp.int32, (1, tq), 1)
    cmp_i = lax.broadcasted_iota(jnp.int32, (n_cmp_pad, tq), 0)
    cmp_ok = (cmp_i * CMP_STRIDE + (CMP_LEN - 1) <= qpos) & (cmp_i < n_cmp)
    cmp_bias = jnp.where(cmp_ok, 0.0, NEG_INF)
    cmp_keep = jnp.where(cmp_ok, 1.0, 0.0)
    n_full = q0 // tk
    outs = []
    for g in range(C_KV_HEADS):
        gc = slice(g * HEAD_DIM, (g + 1) * HEAD_DIM)
        heads = [g * rep + r for r in range(rep)]
        qh = [q_t[h // 2][(h % 2) * HEAD_DIM:(h % 2 + 1) * HEAD_DIM].astype(BF16) for h in heads]
        kcb = kc_ref[0, :, gc].astype(BF16)
        vct = vct_ref[0, gc, :].astype(BF16)
        scores = [_dot(kcb, qh[r]) + cmp_bias for r in range(rep)]
        probs = []
        for s in scores:
            p = jnp.exp(s - jnp.max(s, axis=0, keepdims=True)) * cmp_keep
            l = jnp.sum(p, axis=0, keepdims=True)
            probs.append(p / jnp.where(l > 0, l, 1.0))
        o_cmp = [_dot(vct, p.astype(BF16)) for p in probs]
        imp = probs[0]
        for p in probs[1:]:
            imp = imp + p
        hi = imp.astype(BF16)
        r1 = imp - hi.astype(F32)
        mid = r1.astype(BF16)
        lo = (r1 - mid.astype(F32)).astype(BF16)
        p_slc = _dot(wovt_ref[...], hi) + _dot(wovt_ref[...], mid) + _dot(wovt_ref[...], lo)
        bias_sc[...] = jnp.where(_select_blocks_t(p_slc, qpos, n_slc) > 0.5, 0.0, NEG_INF)
        m_sc[...] = jnp.full(m_sc.shape, NEG_INF, F32)
        l_sc[...] = jnp.zeros(l_sc.shape, F32)
        acc_sc[...] = jnp.zeros(acc_sc.shape, F32)

        def key_tile(k0, diagonal):
            k0 = pl.multiple_of(k0, tk)
            blk0 = k0 // SEL_BLOCK
            bias = jnp.concatenate([jnp.broadcast_to(bias_sc[pl.ds(blk0 + j, 1), :], (SEL_BLOCK, tq))
                                    for j in range(n_blk_tile)], axis=0)
            if diagonal:
                kpos = k0 + lax.broadcasted_iota(jnp.int32, (tk, tq), 0)
                bias = jnp.where(kpos <= qpos, bias, NEG_INF)
            k_tile = ks_ref[0, pl.ds(k0, tk), gc]
            v_tile = vst_ref[0, gc, pl.ds(k0, tk)]
            scores = [_dot(k_tile, qh[r]) + bias for r in range(rep)]
            m_old = m_sc[...]
            m_new = jnp.maximum(m_old, jnp.concatenate([jnp.max(s, axis=0, keepdims=True) for s in scores], axis=0))
            alpha = jnp.exp(m_old - m_new)
            probs = [jnp.exp(scores[r] - m_new[r:r + 1]) for r in range(rep)]
            l_sc[...] = alpha * l_sc[...] + jnp.concatenate([jnp.sum(p, axis=0, keepdims=True) for p in probs], axis=0)
            m_sc[...] = m_new
            pv = [_dot(v_tile, p.astype(BF16)) for p in probs]
            for r in range(rep):
                acc_sc[r] = alpha[r:r + 1] * acc_sc[r] + pv[r]

        def full_tile(kt, carry):
            key_tile(kt * tk, False)
            return carry

        lax.fori_loop(0, n_full, full_tile, 0)
        key_tile(n_full * tk, True)
        for r, h in enumerate(heads):
            o_sel = acc_sc[r] / l_sc[r:r + 1, :]
            outs.append(sig_t[3 * h:3 * h + 1] * o_cmp[r] + sig_t[3 * h + 1:3 * h + 2] * o_sel)
    for c in range(C_WIDTH // LANES):
        o_ref[0, :, c * LANES:(c + 1) * LANES] = jnp.transpose(jnp.concatenate(outs[2 * c:2 * c + 2], axis=0))


def _nsa_prompt(q, kcb, vcb, ks, vs_t, gates_t, n_cmp):
    b, t, wq = q.shape
    n_cmp_pad = kcb.shape[1]
    n_slc = -(-t // SEL_BLOCK)
    n_slc_rows = -(-n_slc // 8) * 8
    tk = min(2 * LANES, t)
    assert t % tk == 0 and tk % ATTN_TQ == 0 and t % SEL_BLOCK == 0 and n_slc <= LANES
    wov_t = _overlap_weights(n_cmp, n_cmp_pad, n_slc, LANES).T
    vcb_t = jnp.swapaxes(vcb, 1, 2)
    rep = C_HEADS // C_KV_HEADS
    whole = lambda a: pl.BlockSpec((1,) + a.shape[1:], lambda bi, i: (bi, 0, 0))
    return pl.pallas_call(
        functools.partial(_nsa_prompt_body, n_cmp=n_cmp, n_slc=n_slc, tk=tk),
        grid=(b, t // ATTN_TQ),
        in_specs=[pl.BlockSpec((1, ATTN_TQ, wq), lambda bi, i: (bi, i, 0)),
                  whole(kcb), whole(vcb_t), whole(ks), whole(vs_t),
                  pl.BlockSpec((1, gates_t.shape[1], ATTN_TQ), lambda bi, i: (bi, 0, i)),
                  pl.BlockSpec(wov_t.shape, lambda bi, i: (0, 0))],
        out_specs=pl.BlockSpec((1, ATTN_TQ, wq), lambda bi, i: (bi, i, 0)),
        out_shape=jax.ShapeDtypeStruct((b, t, wq), F32),
        scratch_shapes=[pltpu.VMEM((n_slc_rows, ATTN_TQ), F32), pltpu.VMEM((rep, ATTN_TQ), F32),
                        pltpu.VMEM((rep, ATTN_TQ), F32), pltpu.VMEM((rep, HEAD_DIM, ATTN_TQ), F32)],
        compiler_params=_params("parallel", "arbitrary"),
        name="nsa_prompt",
    )(q, kcb, vcb_t, ks, vs_t, gates_t, wov_t)


def _block_diag(q, blk_of_row, n_blk):
    return jnp.concatenate([jnp.where(blk_of_row == b, q, 0.0) for b in range(n_blk)], axis=-1)


def _diag_blocks(r, blk_of_row, n_blk):
    out = jnp.where(blk_of_row == 0, r[:, 0:HEAD_DIM], 0.0)
    for b in range(1, n_blk):
        out = out + jnp.where(blk_of_row == b, r[:, b * HEAD_DIM:(b + 1) * HEAD_DIM], 0.0)
    return out


def _row_ids(n_rows, n_heads, n_blk):
    row = lax.broadcasted_iota(jnp.int32, (n_rows, 1), 0)
    step = _idiv(row, n_heads)
    blk = _idiv(_imod(row, n_heads), n_heads // n_blk)
    return step, blk


def _pattern_weight(dist, patterns):
    w = jnp.zeros(dist.shape, F32)
    for window, dil in patterns:
        w = w + jnp.where(_imod(dist, dil) == 0, jnp.where(dist <= window, 1.0, 0.0), 0.0)
    return jnp.where(dist >= 0, w, 0.0)


def _decode_body(*refs, n_heads, n_blk, patterns, gate_col):
    q_ref, kc_ref, vc_ref, kn_ref, vn_ref = refs[:5]
    g_ref = refs[5] if gate_col is not None else None
    o_ref = refs[-1]
    n_rows = q_ref.shape[1]
    n_cache = kc_ref.shape[2]
    n_new = kn_ref.shape[1]
    step, blk = _row_ids(n_rows, n_heads, n_blk)
    qbd = _block_diag(q_ref[0] * (HEAD_DIM ** -0.5), blk, n_blk)
    s_c = _dot(qbd.astype(BF16), kc_ref[0].astype(BF16))
    w_c = _pattern_weight(n_cache + step - lax.broadcasted_iota(jnp.int32, (n_rows, n_cache), 1), patterns)
    s_c = jnp.where(w_c > 0, s_c, NEG_INF)
    kn = kn_ref[0]
    vn = vn_ref[0]
    s_n, w_n = [], []
    for c in range(n_new):
        s = jnp.sum(qbd * kn[c:c + 1, :], axis=-1, keepdims=True)
        w = _pattern_weight(step - c, patterns)
        s_n.append(jnp.where(w > 0, s, NEG_INF))
        w_n.append(w)
    m = jnp.max(s_c, axis=-1, keepdims=True)
    for s in s_n:
        m = jnp.maximum(m, s)
    p_c = w_c * jnp.exp(s_c - m)
    l = jnp.sum(p_c, axis=-1, keepdims=True)
    r = _dot_nt(p_c.astype(BF16), vc_ref[0].astype(BF16))
    for c in range(n_new):
        p = w_n[c] * jnp.exp(s_n[c] - m)
        l = l + p
        r = r + p * vn[c:c + 1, :]
    o = _diag_blocks(r, blk, n_blk) / l
    if g_ref is not None:
        o = o * jax.nn.sigmoid(g_ref[0])[:, gate_col:gate_col + 1]
    o_ref[0] = o


def _decode_attention(q, kc, vc, kn, vn, *, n_heads, n_blk, patterns, gate=None, gate_col=None):
    n, n_rows, _ = q.shape
    args = [q, kc, vc, kn, vn]
    if gate is not None:
        args.append(gate)
    whole = lambda a: pl.BlockSpec((1,) + a.shape[1:], lambda i: (i, 0, 0))
    return pl.pallas_call(
        functools.partial(_decode_body, n_heads=n_heads, n_blk=n_blk, patterns=patterns,
                          gate_col=gate_col if gate is not None else None),
        grid=(n,),
        in_specs=[whole(a) for a in args],
        out_specs=pl.BlockSpec((1, n_rows, HEAD_DIM), lambda i: (i, 0, 0)),
        out_shape=jax.ShapeDtypeStruct((n, n_rows, HEAD_DIM), F32),
        compiler_params=_params("parallel"),
        name="decode_attention",
    )(*args)


def _nsa_decode_body(pt_ref, q_ref, kc_ref, vc_ref, ksp_ref, vsp_ref, kn_ref, vn_ref, g_ref, wov_ref, o_ref,
                     kbuf, vbuf, expand, ksem, vsem, *, n_pages, n_cmp, n_slc):
    n = pl.program_id(0)
    slot = n % 2
    n_rows = q_ref.shape[1]
    n_new = kn_ref.shape[1]
    n_past = kbuf.shape[2]
    n_slc_pad = wov_ref.shape[1]
    rep = C_HEADS // C_KV_HEADS

    def gather(seq, dst_slot, start):
        for pool_ref, buf, sem in ((ksp_ref, kbuf, ksem), (vsp_ref, vbuf, vsem)):
            for p in range(n_pages):
                page = pt_ref[seq, p] if start else 0
                cp = pltpu.make_async_copy(pool_ref.at[page], buf.at[dst_slot, :, pl.ds(p * PAGE_SIZE, PAGE_SIZE)],
                                           sem.at[dst_slot])
                if start:
                    cp.start()
                else:
                    cp.wait()

    @pl.when(n == 0)
    def _():
        gather(0, 0, True)
        step_cols = 1024
        for c0 in range(0, n_past, step_cols):
            shape = (n_slc_pad, min(step_cols, n_past - c0))
            blk_of_key = _idiv(c0 + lax.broadcasted_iota(jnp.int32, shape, 1), SEL_BLOCK)
            hit = blk_of_key == lax.broadcasted_iota(jnp.int32, shape, 0)
            expand[:, c0:c0 + shape[1]] = jnp.where(hit, 1.0, 0.0).astype(BF16)

    @pl.when(n + 1 < pl.num_programs(0))
    def _():
        gather(n + 1, 1 - slot, True)

    step, blk = _row_ids(n_rows, C_HEADS, C_KV_HEADS)
    qpos = n_past + step
    qbd = _block_diag(q_ref[0] * (HEAD_DIM ** -0.5), blk, C_KV_HEADS)
    qbd_in = qbd.astype(BF16)
    n_cmp_pad = kc_ref.shape[1]
    cmp_i = lax.broadcasted_iota(jnp.int32, (n_rows, n_cmp_pad), 1)
    cmp_ok = (cmp_i * CMP_STRIDE + (CMP_LEN - 1) <= qpos) & (cmp_i < n_cmp)
    s = jnp.where(cmp_ok, _dot_nt(qbd_in, kc_ref[0].astype(BF16)), NEG_INF)
    m = jnp.max(s, axis=-1, keepdims=True)
    p = jnp.where(cmp_ok, jnp.exp(s - m), 0.0)
    l = jnp.sum(p, axis=-1, keepdims=True)
    p = p / jnp.where(l > 0, l, 1.0)
    o_cmp = _diag_blocks(_dot(p.astype(BF16), vc_ref[0].astype(BF16)), blk, C_KV_HEADS)
    grp = (n_rows // rep, rep, n_cmp_pad)
    imp = jnp.broadcast_to(jnp.sum(p.reshape(grp), axis=1, keepdims=True), grp).reshape(n_rows, n_cmp_pad)
    sel_rows = _select_blocks(_split3_dot(imp, wov_ref[...]), qpos, n_slc)
    gather(n, slot, False)
    picked = _dot(sel_rows.astype(BF16), expand[...])
    kpos = lax.broadcasted_iota(jnp.int32, (n_rows, n_past), 1)
    s_p = jnp.where(picked > 0.5, jnp.where(kpos <= qpos, _dot(qbd_in, kbuf[slot].astype(BF16)), NEG_INF), NEG_INF)
    kn = kn_ref[0]
    vn = vn_ref[0]
    s_n = []
    for c in range(n_new):
        blk_c = (n_past + c) // SEL_BLOCK
        ok = (sel_rows[:, blk_c:blk_c + 1] > 0.5) & (n_past + c <= qpos)
        s_n.append(jnp.where(ok, jnp.sum(qbd * kn[c:c + 1, :], axis=-1, keepdims=True), NEG_INF))
    m = jnp.max(s_p, axis=-1, keepdims=True)
    for sc in s_n:
        m = jnp.maximum(m, sc)
    p_p = jnp.exp(s_p - m)
    l = jnp.sum(p_p, axis=-1, keepdims=True)
    r = _dot_nt(p_p.astype(BF16), vbuf[slot].astype(BF16))
    for c in range(n_new):
        pn = jnp.exp(s_n[c] - m)
        l = l + pn
        r = r + pn * vn[c:c + 1, :]
    o_sel = _diag_blocks(r, blk, C_KV_HEADS) / l
    sig = jax.nn.sigmoid(g_ref[0])
    o_ref[0] = sig[:, 0:1] * o_cmp + sig[:, 1:2] * o_sel


def _nsa_decode(q, kcb, vcb, pool_ks, pool_vs, page_table, kn, vn, gate, n_cmp):
    n, n_rows, _ = q.shape
    n_pages = page_table.shape[1]
    n_past = n_pages * PAGE_SIZE
    n_new = kn.shape[1]
    n_slc = -(-(n_past + n_new) // SEL_BLOCK)
    n_slc_pad = -(-n_slc // LANES) * LANES
    wov = _overlap_weights(n_cmp, kcb.shape[1], n_slc, n_slc_pad)
    whole = lambda a: pl.BlockSpec((1,) + a.shape[1:], lambda i, pt: (i, 0, 0))
    pools = [jnp.transpose(a, (0, 2, 3, 1)).reshape(a.shape[0], C_KV_WIDTH, PAGE_SIZE) for a in (pool_ks, pool_vs)]
    return pl.pallas_call(
        functools.partial(_nsa_decode_body, n_pages=n_pages, n_cmp=n_cmp, n_slc=n_slc),
        grid_spec=pltpu.PrefetchScalarGridSpec(
            num_scalar_prefetch=1,
            grid=(n,),
            in_specs=[whole(q), whole(kcb), whole(vcb), pl.BlockSpec(memory_space=pl.ANY),
                      pl.BlockSpec(memory_space=pl.ANY), whole(kn), whole(vn), whole(gate),
                      pl.BlockSpec(wov.shape, lambda i, pt: (0, 0))],
            out_specs=pl.BlockSpec((1, n_rows, HEAD_DIM), lambda i, pt: (i, 0, 0)),
            scratch_shapes=[pltpu.VMEM((2, C_KV_WIDTH, n_past), F32), pltpu.VMEM((2, C_KV_WIDTH, n_past), F32),
                            pltpu.VMEM((n_slc_pad, n_past), BF16),
                            pltpu.SemaphoreType.DMA((2,)), pltpu.SemaphoreType.DMA((2,))]),
        out_shape=jax.ShapeDtypeStruct((n, n_rows, HEAD_DIM), F32),
        compiler_params=_params("arbitrary"),
        name="nsa_decode",
    )(page_table, q, kcb, vcb, *pools, kn, vn, gate, wov)


def _odd_out_body(x_ref, a_ref, b_ref, z_ref, w_ref, gf_ref, y_ref):
    mixed = ((a_ref[...] + b_ref[...]) * _silu(z_ref[...])).astype(BF16)
    h = x_ref[...] + _dot(mixed, w_ref[...])
    y_ref[...] = h * lax.rsqrt(jnp.mean(h * h, axis=-1, keepdims=True) + RMS_EPS) * gf_ref[...]


def _odd_out(x2d, o_cs, o_win, z, w_out, final_gain, tm):
    m, d = x2d.shape
    row = pl.BlockSpec((tm, d), lambda i: (i, 0))
    return pl.pallas_call(
        _odd_out_body,
        grid=(m // tm,),
        in_specs=[row, row, row, row, pl.BlockSpec(w_out.shape, lambda i: (0, 0)),
                  pl.BlockSpec((1, d), lambda i: (0, 0))],
        out_specs=row,
        out_shape=jax.ShapeDtypeStruct((m, d), F32),
        compiler_params=_params("parallel"),
        name="odd_out",
    )(x2d, o_cs, o_win, z, w_out, final_gain.reshape(1, d))


def _even_outs(prompt):
    a, bk = A_WIDTH, B_HEADS * B_DK
    return [(0, a, BF16, False),
            (a, a, BF16, False), (a, a, F32, prompt),
            (2 * a, a, BF16, False), (2 * a, a, F32, prompt),
            (3 * a, a, F32, False),
            (4 * a, bk, F32, False), (4 * a + bk, bk, F32, False),
            (4 * a + 2 * bk, B_WIDTH, F32, False), (4 * a + 2 * bk + B_WIDTH, B_WIDTH, F32, False)]


def _from_feature_major(u, t0, n_heads):
    b, _, t = u.shape
    return jnp.transpose(u[:, :, t0:].reshape(b, n_heads, HEAD_DIM, t - t0), (0, 3, 1, 2))


def _to_feature_major(u):
    n, length, n_heads, hd = u.shape
    return jnp.transpose(u, (0, 2, 3, 1)).reshape(n, n_heads * hd, length)


def _even_layer_prompt(x, gain, w_in, w_out):
    b, t, d = x.shape
    x2d = x.reshape(b * t, d)
    qa, ka, ka_t, va, va_t, za, qb, kb, vb, zb = _norm_proj(x2d, gain, w_in, _even_outs(True), 512, t)
    seq = lambda u: u.reshape(b, t, u.shape[-1])
    pats = []
    for window, dil in A_PATTERNS:
        n_sub = t // dil
        pats.append(_banded_attention(seq(qa), seq(ka), seq(va), dil=dil, n_heads=A_HEADS, n_groups=A_HEADS,
                                      lookback=window // dil, tqb=min(512, n_sub), with_lse=True))
    pats = [[a.reshape(b * t, A_WIDTH) for a in pair] for pair in pats]
    chunk = B_CHUNK if t % B_CHUNK == 0 else t
    s0 = jnp.zeros((b, B_HEADS, B_DK, B_DV), F32)
    ob, s_fin = _retention(seq(qb), seq(kb), seq(vb), s0, jnp.arange(t), chunk=chunk, c_real=chunk)
    y = _even_out(x2d, pats, za, ob.reshape(b * t, B_WIDTH), zb, w_out, 256)
    keep = min(A_PATTERNS[-1][0], t)
    return (y.reshape(b, t, d), _from_feature_major(ka_t, t - keep, A_HEADS),
            _from_feature_major(va_t, t - keep, A_HEADS), s_fin)


def _even_layer_sample(x, cache_k, cache_v, state, gain, w_in, w_out, past_len):
    n, s_len, d = x.shape
    x2d = x.reshape(n * s_len, d)
    qa, _, ka32, _, va32, za, qb, kb, vb, zb = _norm_proj(x2d, gain, w_in, _even_outs(False), n * s_len, n * s_len)
    seq = lambda u: u.reshape(n, s_len, u.shape[-1])
    oa = _decode_attention(qa.astype(F32).reshape(n, s_len * A_HEADS, HEAD_DIM), _to_feature_major(cache_k),
                           _to_feature_major(cache_v), seq(ka32), seq(va32), n_heads=A_HEADS, n_blk=A_HEADS,
                           patterns=A_PATTERNS)
    oa = oa.reshape(n * s_len, A_WIDTH)
    chunk = 16
    pad = lambda u: jnp.pad(seq(u), ((0, 0), (0, chunk - s_len), (0, 0)))
    pos = past_len + jnp.arange(chunk)
    ob, s_new = _retention(pad(qb), pad(kb), pad(vb), state, pos, chunk=chunk, c_real=s_len)
    ob = ob[:, :s_len].reshape(n * s_len, B_WIDTH)
    zero = jnp.zeros_like(oa)
    y = _even_out(x2d, [[oa, zero]] * 3, za, ob, zb, w_out, n * s_len)
    heads = lambda u: u.reshape(n, s_len, A_HEADS, HEAD_DIM)
    return y.reshape(n, s_len, d), heads(ka32), heads(va32), s_new


ODD_KV = ("kc", "vc", "ks", "vs", "kw", "vw")
ODD_GATE_COL = C_WIDTH + len(ODD_KV) * C_KV_WIDTH
ODD_Z_COL = ODD_GATE_COL + 3 * C_HEADS


def _odd_outs_prompt():
    outs = [(0, C_WIDTH, BF16, False)]
    for i, name in enumerate(ODD_KV):
        off = C_WIDTH + i * C_KV_WIDTH
        outs.append((off, C_KV_WIDTH, BF16, name == "vs"))
        outs.append((off, C_KV_WIDTH, F32, True))
    outs += [(ODD_GATE_COL, 3 * C_HEADS, F32, False), (ODD_GATE_COL, 3 * C_HEADS, F32, True),
             (ODD_Z_COL, C_WIDTH, F32, False)]
    return outs


def _odd_outs_sample():
    outs = [(0, C_WIDTH, BF16, False)]
    outs += [(C_WIDTH + i * C_KV_WIDTH, C_KV_WIDTH, F32, False) for i in range(len(ODD_KV))]
    outs += [(ODD_GATE_COL, 3 * C_HEADS, F32, False), (ODD_Z_COL, C_WIDTH, F32, False)]
    return outs


def _odd_layer_prompt(x, gain, w_in, w_out, cw_k, cw_v, final_gain):
    b, t, d = x.shape
    x2d = x.reshape(b * t, d)
    (q, kc, kc_t, vc, vc_t, ks, ks_t, vs_tb, vs_t, kw, kw_t, vw, vw_t, gl, gl_t, z) = _norm_proj(
        x2d, gain, w_in, _odd_outs_prompt(), 512, t)
    seq = lambda u: u.reshape(b, t, u.shape[-1])
    n_cmp = (t - CMP_LEN) // CMP_STRIDE + 1
    pieces = lambda u: u.reshape(b, t // CMP_STRIDE, CMP_STRIDE * C_KV_WIDTH)
    kcb = _compress(pieces(kc), cw_k)
    vcb = _compress(pieces(vc), cw_v)
    o_cs = _nsa_prompt(seq(q), kcb, vcb, seq(ks), vs_tb, gl_t, n_cmp)
    o_win, = _banded_attention(seq(q), seq(kw), seq(vw), dil=1, n_heads=C_HEADS, n_groups=C_KV_HEADS,
                               lookback=C_WINDOW, tqb=min(512, t), gate=seq(gl), gate_col=lambda h: 3 * h + 2)
    y = _odd_out(x2d, o_cs.reshape(b * t, C_WIDTH), o_win.reshape(b * t, C_WIDTH), z, w_out, final_gain, 256)
    keep = min(C_WINDOW, t)
    kv = lambda u, t0: _from_feature_major(u, t0, C_KV_HEADS)
    return (y.reshape(b, t, d), kv(kc_t, 0), kv(vc_t, 0), kv(ks_t, 0), kv(vs_t, 0), kv(kw_t, t - keep),
            kv(vw_t, t - keep))


def _odd_layer_sample(x, page_table, pool_kc, pool_vc, pool_ks, pool_vs, buf_kw, buf_vw, gain, w_in, w_out,
                      cw_k, cw_v, final_gain):
    n, s_len, d = x.shape
    x2d = x.reshape(n * s_len, d)
    q, kc, vc, ks32, vs32, kw32, vw32, gl, z = _norm_proj(x2d, gain, w_in, _odd_outs_sample(), n * s_len, n * s_len)
    seq = lambda u: u.reshape(n, s_len, u.shape[-1])
    n_past = page_table.shape[1] * PAGE_SIZE
    n_cmp = (n_past + s_len - CMP_LEN) // CMP_STRIDE + 1
    assert CMP_STRIDE * (n_cmp - 1) + CMP_LEN <= n_past
    kcb = _compress_paged(pool_kc, page_table, cw_k)
    vcb = _compress_paged(pool_vc, page_table, cw_v)
    rows = lambda u: u.astype(F32).reshape(n, s_len * C_HEADS, HEAD_DIM)
    gate = gl.reshape(n, s_len * C_HEADS, 3)
    o_cs = _nsa_decode(rows(q), kcb, vcb, pool_ks, pool_vs, page_table, seq(ks32), seq(vs32), gate, n_cmp)
    o_win = _decode_attention(rows(q), _to_feature_major(buf_kw), _to_feature_major(buf_vw),
                              seq(kw32), seq(vw32), n_heads=C_HEADS, n_blk=C_KV_HEADS,
                              patterns=((C_WINDOW, 1),), gate=gate, gate_col=2)
    y = _odd_out(x2d, o_cs.reshape(n * s_len, C_WIDTH), o_win.reshape(n * s_len, C_WIDTH), z, w_out, final_gain,
                 n * s_len)
    kv = lambda u: u.reshape(n, s_len, C_KV_HEADS, HEAD_DIM)
    return y.reshape(n, s_len, d), kv(kc), kv(vc), kv(ks32), kv(vs32), kv(kw32), kv(vw32)


def kernel(x_prompt, x_sample, cache_a_k, cache_a_v, state_ret, cache_c_kcmp, cache_c_vcmp, cache_c_ksel,
           cache_c_vsel, cache_c_kwin, cache_c_vwin, page_table, even_norm, even_w_in, even_w_out, odd_norm,
           odd_w_in, odd_w_out, cmp_pos_k, cmp_w1_k, cmp_w2_k, cmp_pos_v, cmp_w1_v, cmp_w2_v, final_norm):
    assert even_norm.shape[0] == 1 and odd_norm.shape[0] == 1
    past_len = page_table.shape[1] * PAGE_SIZE
    w_in_e = even_w_in[0].T.astype(BF16)
    w_out_e = even_w_out[0].astype(BF16)
    w_in_o = odd_w_in[0].T.astype(BF16)
    w_out_o = odd_w_out[0].astype(BF16)
    cw_k = _compress_weights(cmp_pos_k[0], cmp_w1_k[0], cmp_w2_k[0])
    cw_v = _compress_weights(cmp_pos_v[0], cmp_w1_v[0], cmp_w2_v[0])

    hp, ak_p, av_p, ret_p = _even_layer_prompt(x_prompt, even_norm[0], w_in_e, w_out_e)
    hs, ak_s, av_s, ret_s = _even_layer_sample(x_sample, cache_a_k[0], cache_a_v[0], state_ret[0], even_norm[0],
                                               w_in_e, w_out_e, past_len)
    yp, *rows_p = _odd_layer_prompt(hp, odd_norm[0], w_in_o, w_out_o, cw_k, cw_v, final_norm)
    ys, *rows_s = _odd_layer_sample(hs, page_table, cache_c_kcmp[0], cache_c_vcmp[0], cache_c_ksel[0],
                                    cache_c_vsel[0], cache_c_kwin[0], cache_c_vwin[0], odd_norm[0], w_in_o, w_out_o,
                                    cw_k, cw_v, final_norm)
    lead = lambda u: u[None]
    out = [yp, ys, lead(ak_p), lead(ak_s), lead(av_p), lead(av_s), lead(ret_p), lead(ret_s)]
    for rp, rs in zip(rows_p, rows_s):
        out += [lead(rp), lead(rs)]
    return tuple(out)
```

```python
import functools

import numpy as np
import jax
import jax.numpy as jnp
from jax import lax
from jax.experimental import pallas as pl
from jax.experimental.pallas import tpu as pltpu

F32 = jnp.float32
BF16 = jnp.bfloat16

HEAD_DIM = 64
A_HEADS = 8
A_WIDTH = A_HEADS * HEAD_DIM
A_PATTERNS = ((128, 1), (512, 4), (2048, 16))
B_HEADS = 4
B_DK = 64
B_DV = 128
B_WIDTH = B_HEADS * B_DV
B_CHUNK = 128
ROPE_BASE = 10000.0
GN_EPS = 1e-5
C_HEADS = 16
C_KV_HEADS = 2
C_WIDTH = C_HEADS * HEAD_DIM
C_KV_WIDTH = C_KV_HEADS * HEAD_DIM
CMP_LEN = 32
CMP_STRIDE = 16
CMP_HIDDEN = 256
SEL_BLOCK = 64
SEL_TOPK = 16
SEL_FORCE = 1e9
C_WINDOW = 512
PAGE_SIZE = 128
RMS_EPS = 1e-6
NEG_INF = -1e30

LANES = 128
ATTN_TQ = 128
VMEM_LIMIT = 56 * 1024 * 1024


def _params(*sem):
    return pltpu.CompilerParams(dimension_semantics=sem, vmem_limit_bytes=VMEM_LIMIT)


def _dot(a, b):
    return jnp.dot(a, b, preferred_element_type=F32)


def _dot_nt(a, b):
    return lax.dot_general(a, b, (((1,), (1,)), ((), ())), preferred_element_type=F32)


def _dot_tn(a, b):
    return lax.dot_general(a, b, (((0,), (0,)), ((), ())), preferred_element_type=F32)


def _idiv(x, n):
    assert n & (n - 1) == 0
    return jnp.right_shift(x, n.bit_length() - 1)


def _imod(x, n):
    assert n & (n - 1) == 0
    return jnp.bitwise_and(x, n - 1)


def _split3_dot(a, w):
    hi = a.astype(BF16)
    r1 = a - hi.astype(F32)
    mid = r1.astype(BF16)
    lo = (r1 - mid.astype(F32)).astype(BF16)
    return _dot(hi, w) + _dot(mid, w) + _dot(lo, w)


def _norm_proj_body(x_ref, g_ref, wt_ref, *o_refs, outs):
    x = x_ref[...]
    y = x * lax.rsqrt(jnp.mean(x * x, axis=-1, keepdims=True) + RMS_EPS) * g_ref[...]
    yb = y.astype(BF16)
    done = {}
    for o_ref, (off, width, _, feature_major) in zip(o_refs, outs):
        key = (off, width, feature_major)
        if key not in done:
            w_rows = wt_ref[off:off + width, :]
            done[key] = _dot_nt(w_rows, yb) if feature_major else _dot_nt(yb, w_rows)
        if feature_major:
            o_ref[0] = done[key].astype(o_ref.dtype)
        else:
            o_ref[...] = done[key].astype(o_ref.dtype)


def _norm_proj(x2d, gain, w_t, outs, tm, seq_len):
    m, d = x2d.shape
    n = w_t.shape[0]
    tiles = seq_len // tm
    assert seq_len % tm == 0 and m % seq_len == 0
    out_specs, out_shape = [], []
    for _, width, dt, feature_major in outs:
        if feature_major:
            out_specs.append(pl.BlockSpec((1, width, tm), lambda i: (i // tiles, 0, i % tiles)))
            out_shape.append(jax.ShapeDtypeStruct((m // seq_len, width, seq_len), dt))
        else:
            out_specs.append(pl.BlockSpec((tm, width), lambda i: (i, 0)))
            out_shape.append(jax.ShapeDtypeStruct((m, width), dt))
    return pl.pallas_call(
        functools.partial(_norm_proj_body, outs=outs),
        grid=(m // tm,),
        in_specs=[pl.BlockSpec((tm, d), lambda i: (i, 0)),
                  pl.BlockSpec((1, d), lambda i: (0, 0)),
                  pl.BlockSpec((n, d), lambda i: (0, 0))],
        out_specs=out_specs,
        out_shape=out_shape,
        compiler_params=_params("parallel"),
        name="norm_proj",
    )(x2d, gain.reshape(1, d), w_t)


def _banded_body(q_ref, k_ref, v_ref, o_ref, lse_ref, *, n_heads, n_groups, lookback, tqb):
    tq = ATTN_TQ
    n_prev = -(-lookback // tq)
    span = (n_prev + 1) * tq
    rep = n_heads // n_groups
    blk = pl.program_id(2)

    def sub(u, carry):
        row0 = pl.multiple_of(u * tq, tq)
        s0 = blk * tqb + row0
        kstart = pl.multiple_of(jnp.maximum(s0 - n_prev * tq, 0), tq)
        qpos = s0 + lax.broadcasted_iota(jnp.int32, (tq, span), 0)
        kpos = kstart + lax.broadcasted_iota(jnp.int32, (tq, span), 1)
        dist = qpos - kpos
        bias = jnp.where(dist >= 0, jnp.where(dist <= lookback, 0.0, NEG_INF), NEG_INF)
        kv_cols = [slice(g * HEAD_DIM, (g + 1) * HEAD_DIM) for g in range(n_groups)]
        keys = [k_ref[0, pl.ds(kstart, span), c] for c in kv_cols]
        vals = [v_ref[0, pl.ds(kstart, span), c] for c in kv_cols]
        cols = [slice(h * HEAD_DIM, (h + 1) * HEAD_DIM) for h in range(n_heads)]
        scores = [_dot_nt(q_ref[0, pl.ds(row0, tq), c] * (HEAD_DIM ** -0.5), keys[h // rep]) + bias
                  for h, c in enumerate(cols)]
        maxes = [jnp.max(s, axis=-1, keepdims=True) for s in scores]
        probs = [jnp.exp(s - m) for s, m in zip(scores, maxes)]
        sums = [jnp.sum(p, axis=-1, keepdims=True) for p in probs]
        outs = [_dot(p.astype(BF16), vals[h // rep]) for h, p in enumerate(probs)]
        for c, m, l, o in zip(cols, maxes, sums, outs):
            o_ref[0, pl.ds(row0, tq), c] = o / l
            lse_ref[0, pl.ds(row0, tq), c] = jnp.broadcast_to(m + jnp.log(l), (tq, HEAD_DIM))
        return carry

    lax.fori_loop(0, tqb // tq, sub, 0)


def _banded_attention(q, k, v, *, dil, n_heads, n_groups, lookback, tqb):
    b, t, wq = q.shape
    wk = k.shape[-1]
    n_sub = t // dil
    assert n_sub % tqb == 0 and tqb % ATTN_TQ == 0
    assert n_sub >= (-(-lookback // ATTN_TQ) + 1) * ATTN_TQ
    args = [q.reshape(b, n_sub, dil * wq), k.reshape(b, n_sub, dil * wk), v.reshape(b, n_sub, dil * wk)]
    in_specs = [pl.BlockSpec((1, tqb, wq), lambda bi, r, i: (bi, i, r)),
                pl.BlockSpec((1, n_sub, wk), lambda bi, r, i: (bi, 0, r)),
                pl.BlockSpec((1, n_sub, wk), lambda bi, r, i: (bi, 0, r))]
    res = pl.pallas_call(
        functools.partial(_banded_body, n_heads=n_heads, n_groups=n_groups, lookback=lookback, tqb=tqb),
        grid=(b, dil, n_sub // tqb),
        in_specs=in_specs,
        out_specs=[pl.BlockSpec((1, tqb, wq), lambda bi, r, i: (bi, i, r))] * 2,
        out_shape=[jax.ShapeDtypeStruct((b, n_sub, dil * wq), F32)] * 2,
        compiler_params=_params("parallel", "parallel", "arbitrary"),
        name=f"banded_attention_d{dil}",
    )(*args)
    return [a.reshape(b, t, wq) for a in res]


def _retention_body(q_ref, k_ref, v_ref, cos_ref, sin_ref, qd_ref, kd_ref, intra_ref, cd_ref, s0_ref,
                    o_ref, sfin_ref, state):
    c = pl.program_id(1)

    @pl.when(c == 0)
    def _():
        state[...] = s0_ref[0]

    cos = cos_ref[...]
    sin = sin_ref[...]
    width = cos.shape[-1]
    half = B_DK // 2
    lane = _imod(lax.broadcasted_iota(jnp.int32, cos.shape, 1), B_DK)

    def rotary(x):
        fwd = pltpu.roll(x, half, 1)
        bwd = pltpu.roll(x, width - half, 1)
        return x * cos + jnp.where(lane < half, -bwd, fwd) * sin

    q = rotary(q_ref[0])
    k = rotary(k_ref[0]) * (B_DK ** -0.5)
    q_in = q.astype(BF16)
    k_in = k.astype(BF16)
    q_st = (q * qd_ref[...]).astype(BF16)
    k_st = (k * kd_ref[...]).astype(BF16)
    v = v_ref[0].astype(BF16)
    for h in range(B_HEADS):
        kc = slice(h * B_DK, (h + 1) * B_DK)
        vc = slice(h * B_DV, (h + 1) * B_DV)
        st = state[h]
        scores = _dot_nt(q_in[:, kc], k_in[:, kc]) * intra_ref[h]
        o_ref[0, :, vc] = _dot(scores.astype(BF16), v[:, vc]) + _dot(q_st[:, kc], st.astype(BF16))
        state[h] = st * cd_ref[h] + _dot_tn(k_st[:, kc], v[:, vc])

    @pl.when(c == pl.num_programs(1) - 1)
    def _():
        sfin_ref[0] = state[...]


def _retention(q, k, v, s0, pos, *, chunk, c_real):
    n, t, _ = q.shape
    half = B_DK // 2
    freqs = ROPE_BASE ** (-jnp.arange(half, dtype=F32) / half)
    ang = pos.astype(F32)[:, None] * freqs[None, :]
    cos = jnp.tile(jnp.cos(ang), (1, 2 * B_HEADS))
    sin = jnp.tile(jnp.sin(ang), (1, 2 * B_HEADS))
    log_g = jnp.log1p(-jnp.exp2(-5.0 - jnp.arange(B_HEADS, dtype=F32)))
    i = jnp.arange(chunk, dtype=F32)
    diff = i[:, None] - i[None, :]
    intra = jnp.where(diff >= 0, jnp.exp(log_g[:, None, None] * jnp.maximum(diff, 0.0)), 0.0)
    q_decay = jnp.repeat(jnp.exp(log_g[None, :] * (i[:, None] + 1.0)), B_DK, axis=1)
    k_decay = jnp.repeat(jnp.exp(log_g[None, :] * (c_real - 1.0 - i[:, None])), B_DK, axis=1)
    chunk_decay = jnp.broadcast_to(jnp.exp(log_g * c_real)[:, None, None], (B_HEADS, 1, B_DV))
    wqk = B_HEADS * B_DK
    return pl.pallas_call(
        _retention_body,
        grid=(n, t // chunk),
        in_specs=[pl.BlockSpec((1, chunk, wqk), lambda b, c: (b, c, 0)),
                  pl.BlockSpec((1, chunk, wqk), lambda b, c: (b, c, 0)),
                  pl.BlockSpec((1, chunk, B_WIDTH), lambda b, c: (b, c, 0)),
                  pl.BlockSpec((chunk, wqk), lambda b, c: (c, 0)),
                  pl.BlockSpec((chunk, wqk), lambda b, c: (c, 0)),
                  pl.BlockSpec((chunk, wqk), lambda b, c: (0, 0)),
                  pl.BlockSpec((chunk, wqk), lambda b, c: (0, 0)),
                  pl.BlockSpec((B_HEADS, chunk, chunk), lambda b, c: (0, 0, 0)),
                  pl.BlockSpec((B_HEADS, 1, B_DV), lambda b, c: (0, 0, 0)),
                  pl.BlockSpec((1, B_HEADS, B_DK, B_DV), lambda b, c: (b, 0, 0, 0))],
        out_specs=[pl.BlockSpec((1, chunk, B_WIDTH), lambda b, c: (b, c, 0)),
                   pl.BlockSpec((1, B_HEADS, B_DK, B_DV), lambda b, c: (b, 0, 0, 0))],
        out_shape=[jax.ShapeDtypeStruct((n, t, B_WIDTH), F32),
                   jax.ShapeDtypeStruct((n, B_HEADS, B_DK, B_DV), F32)],
        scratch_shapes=[pltpu.VMEM((B_HEADS, B_DK, B_DV), F32)],
        compiler_params=_params("parallel", "arbitrary"),
        name="retention",
    )(q, k, v, cos, sin, q_decay, k_decay, intra, chunk_decay, s0)


def _silu(z):
    return z * jax.nn.sigmoid(z)


def _even_out_body(x_ref, o1, l1, o2, l2, o3, l3, za_ref, ob_ref, zb_ref, w_ref, y_ref):
    m = jnp.maximum(jnp.maximum(l1[...], l2[...]), l3[...])
    e1 = jnp.exp(l1[...] - m)
    e2 = jnp.exp(l2[...] - m)
    e3 = jnp.exp(l3[...] - m)
    oa = (e1 * o1[...] + e2 * o2[...] + e3 * o3[...]) / (e1 + e2 + e3)
    ga = (oa * _silu(za_ref[...])).astype(BF16)
    acc = x_ref[...] + _dot(ga, w_ref[0:A_WIDTH, :])
    ob = ob_ref[...]
    zb = zb_ref[...]
    for h in range(B_HEADS):
        cols = slice(h * B_DV, (h + 1) * B_DV)
        seg = ob[:, cols]
        mu = jnp.mean(seg, axis=-1, keepdims=True)
        cen = seg - mu
        var = jnp.mean(cen * cen, axis=-1, keepdims=True)
        gb = (cen * lax.rsqrt(var + GN_EPS) * _silu(zb[:, cols])).astype(BF16)
        acc = acc + _dot(gb, w_ref[A_WIDTH + h * B_DV:A_WIDTH + (h + 1) * B_DV, :])
    y_ref[...] = acc


def _even_out(x2d, pats, za, ob, zb, w_out, tm):
    m, d = x2d.shape
    row = lambda width: pl.BlockSpec((tm, width), lambda i: (i, 0))
    flat = [a for pair in pats for a in pair]
    return pl.pallas_call(
        _even_out_body,
        grid=(m // tm,),
        in_specs=[row(d)] + [row(A_WIDTH)] * 6 + [row(A_WIDTH), row(B_WIDTH), row(B_WIDTH),
                                                   pl.BlockSpec(w_out.shape, lambda i: (0, 0))],
        out_specs=row(d),
        out_shape=jax.ShapeDtypeStruct((m, d), F32),
        compiler_params=_params("parallel"),
        name="even_out",
    )(x2d, *flat, za, ob, zb, w_out)


def _compress_math(x, p2_ref, wbig_ref, w2_ref):
    n_piece = x.shape[0]
    a = _dot(x.astype(BF16), wbig_ref[...])
    pc = _dot(p2_ref[...], wbig_ref[...])
    hid = []
    for g in range(C_KV_HEADS):
        lo = slice(g * 2 * CMP_HIDDEN, g * 2 * CMP_HIDDEN + CMP_HIDDEN)
        hi = slice(g * 2 * CMP_HIDDEN + CMP_HIDDEN, (g + 1) * 2 * CMP_HIDDEN)
        nxt = pltpu.roll(a[:, hi], n_piece - 1, 0)
        pos = (pc[0:1, lo] + pc[2:3, lo]) + (pc[1:2, hi] + pc[3:4, hi])
        hid.append(_silu(a[:, lo] + nxt + pos))
    return _dot(jnp.concatenate(hid, axis=-1).astype(BF16), w2_ref[...])


def _compress_body(x_ref, p2_ref, wbig_ref, w2_ref, o_ref):
    o_ref[0] = _compress_math(x_ref[0], p2_ref, wbig_ref, w2_ref)


def _compress_weights(pos_emb, w1, w2):
    per = CMP_LEN // CMP_STRIDE
    w1r = w1.reshape(per, CMP_STRIDE, HEAD_DIM, CMP_HIDDEN)
    eye = jnp.eye(C_KV_HEADS, dtype=w1.dtype)
    wbig = jnp.einsum('jldf,gh->lgdhjf', w1r, eye).reshape(CMP_STRIDE * C_KV_WIDTH, C_KV_HEADS * per * CMP_HIDDEN)
    w2bd = jnp.einsum('fd,gh->gfhd', w2, eye).reshape(C_KV_HEADS * CMP_HIDDEN, C_KV_WIDTH)
    halves = jnp.tile(pos_emb.reshape(per, CMP_STRIDE, 1, HEAD_DIM), (1, 1, C_KV_HEADS, 1)).reshape(per, -1)
    hi = halves.astype(BF16)
    lo = (halves - hi.astype(F32)).astype(BF16)
    p2 = jnp.concatenate([hi, lo, jnp.zeros((16 - 2 * per, halves.shape[1]), BF16)], axis=0)
    return p2, wbig.astype(BF16), w2bd.astype(BF16)


def _compress(pieces, cw):
    n, n_piece, width = pieces.shape
    p2, wbig, w2bd = cw
    const = lambda a: pl.BlockSpec(a.shape, lambda i: (0,) * a.ndim)
    return pl.pallas_call(
        _compress_body,
        grid=(n,),
        in_specs=[pl.BlockSpec((1, n_piece, width), lambda i: (i, 0, 0)), const(p2), const(wbig), const(w2bd)],
        out_specs=pl.BlockSpec((1, n_piece, C_KV_WIDTH), lambda i: (i, 0, 0)),
        out_shape=jax.ShapeDtypeStruct((n, n_piece, C_KV_WIDTH), F32),
        compiler_params=_params("parallel"),
        name="compress",
    )(pieces, p2, wbig, w2bd)


def _page_dma(pt_ref, pool_ref, buf_ref, sem_ref, n, slot, n_pages, start):
    for p in range(n_pages):
        page = pt_ref[n, p] if start else 0
        cp = pltpu.make_async_copy(pool_ref.at[page], buf_ref.at[slot, p], sem_ref.at[slot])
        if start:
            cp.start()
        else:
            cp.wait()


def _compress_paged_body(pt_ref, pool_ref, perm_ref, p2_ref, wbig_ref, w2_ref, o_ref, buf, pieces, sem, *, n_pages):
    n = pl.program_id(0)
    slot = n % 2
    rows = PAGE_SIZE // CMP_STRIDE

    @pl.when(n == 0)
    def _():
        _page_dma(pt_ref, pool_ref, buf, sem, 0, 0, n_pages, True)

    @pl.when(n + 1 < pl.num_programs(0))
    def _():
        _page_dma(pt_ref, pool_ref, buf, sem, n + 1, 1 - slot, n_pages, True)

    _page_dma(pt_ref, pool_ref, buf, sem, n, slot, n_pages, False)
    perm = perm_ref[...]

    group = 8
    assert n_pages % group == 0

    def unfold(i, carry):
        by_row = [_dot_nt(perm, buf[slot, i * group + j].astype(BF16)) for j in range(group)]
        for j in range(group):
            r0 = pl.multiple_of((i * group + j) * rows, rows)
            for l in range(CMP_STRIDE):
                pieces[pl.ds(r0, rows), l * C_KV_WIDTH:(l + 1) * C_KV_WIDTH] = by_row[j][l * rows:(l + 1) * rows, :]
        return carry

    lax.fori_loop(0, n_pages // group, unfold, 0)
    o_ref[0] = _compress_math(pieces[...], p2_ref, wbig_ref, w2_ref)


def _compress_paged(pool, page_table, cw):
    n, n_pages = page_table.shape
    rows = PAGE_SIZE // CMP_STRIDE
    n_piece = n_pages * rows
    p2, wbig, w2bd = cw
    perm = np.zeros((PAGE_SIZE, PAGE_SIZE), np.float32)
    for l in range(CMP_STRIDE):
        for piece in range(rows):
            perm[l * rows + piece, CMP_STRIDE * piece + l] = 1.0
    perm = jnp.asarray(perm, BF16)
    pages = jnp.transpose(pool, (0, 2, 3, 1)).reshape(pool.shape[0], C_KV_WIDTH, PAGE_SIZE)
    const = lambda a: pl.BlockSpec(a.shape, lambda i, pt: (0,) * a.ndim)
    return pl.pallas_call(
        functools.partial(_compress_paged_body, n_pages=n_pages),
        grid_spec=pltpu.PrefetchScalarGridSpec(
            num_scalar_prefetch=1,
            grid=(n,),
            in_specs=[pl.BlockSpec(memory_space=pl.ANY), const(perm), const(p2), const(wbig), const(w2bd)],
            out_specs=pl.BlockSpec((1, n_piece, C_KV_WIDTH), lambda i, pt: (i, 0, 0)),
            scratch_shapes=[pltpu.VMEM((2, n_pages, C_KV_WIDTH, PAGE_SIZE), F32),
                            pltpu.VMEM((n_piece, CMP_STRIDE * C_KV_WIDTH), F32),
                            pltpu.SemaphoreType.DMA((2,))]),
        out_shape=jax.ShapeDtypeStruct((n, n_piece, C_KV_WIDTH), F32),
        compiler_params=_params("arbitrary"),
        name="compress_paged",
    )(page_table, pages, perm, p2, wbig, w2bd)


def _overlap_weights(n_cmp, n_cmp_pad, n_slc, n_slc_pad):
    i = np.arange(n_cmp_pad)[:, None]
    j = np.arange(n_slc_pad)[None, :]
    ov = (np.minimum(i * CMP_STRIDE + CMP_LEN, (j + 1) * SEL_BLOCK) - np.maximum(i * CMP_STRIDE, j * SEL_BLOCK))
    w = np.where((i < n_cmp) & (j < n_slc), np.clip(ov, 0, None) / CMP_LEN, 0.0)
    return jnp.asarray(w, dtype=BF16)


def _select_blocks(p_slc, qpos, n_slc):
    shape = p_slc.shape
    lane = lax.broadcasted_iota(jnp.int32, shape, 1)
    cur = _idiv(qpos, SEL_BLOCK)
    forced = (lane == 0) | (lane == cur) | (lane == cur - 1)
    causal = lane * SEL_BLOCK <= qpos
    score = jnp.where(forced, SEL_FORCE, jnp.where(causal, p_slc, -SEL_FORCE))
    score = jnp.where(lane < n_slc, score, -3.0 * SEL_FORCE)
    lane_f = lane.astype(F32)
    sel = jnp.zeros(shape, F32)
    for _ in range(min(SEL_TOPK, n_slc)):
        m = jnp.max(score, axis=-1, keepdims=True)
        first = jnp.min(jnp.where(score == m, lane_f, 1e9), axis=-1, keepdims=True)
        hit = lane_f == first
        sel = jnp.where(hit, jnp.where(m > -0.5 * SEL_FORCE, 1.0, 0.0), sel)
        score = jnp.where(hit, -4.0 * SEL_FORCE, score)
    return sel


def _select_blocks_t(p_slc, qpos, n_slc):
    n_rows = -(-n_slc // 8) * 8
    shape = (n_rows, p_slc.shape[1])
    blk = lax.broadcasted_iota(jnp.int32, shape, 0)
    cur = _idiv(qpos, SEL_BLOCK)
    forced = (blk == 0) | (blk == cur) | (blk == cur - 1)
    causal = blk * SEL_BLOCK <= qpos
    score = jnp.where(forced, SEL_FORCE, jnp.where(causal, p_slc[:n_rows], -SEL_FORCE))
    score = jnp.where(blk < n_slc, score, -3.0 * SEL_FORCE)
    tiles = [score[8 * v:8 * v + 8] for v in range(n_rows // 8)]
    sub = lax.broadcasted_iota(jnp.int32, (8, shape[1]), 0)
    ranks = [jnp.zeros((8, shape[1]), F32) for _ in tiles]
    for i in range(n_slc):
        row = tiles[i // 8][i % 8:i % 8 + 1]
        for v, tile in enumerate(tiles):
            ge = jnp.where(row >= tile, 1.0, 0.0)
            gt = jnp.where(row > tile, 1.0, 0.0)
            if 8 * v > i:
                ranks[v] = ranks[v] + ge
            elif 8 * v + 7 <= i:
                ranks[v] = ranks[v] + gt
            else:
                ranks[v] = ranks[v] + jnp.where(sub > i - 8 * v, ge, gt)
    rank = jnp.concatenate(ranks, axis=0)
    return jnp.where(rank < min(SEL_TOPK, n_slc), jnp.where(score > -0.5 * SEL_FORCE, 1.0, 0.0), 0.0)


NSA_TILE = 2 * LANES
ONES_ROWS = 16
LOG2E = 1.4426950408889634


def _nsa_prompt_body(q_ref, kc_ref, vct_ref, ks_ref, vst_ref, kw_ref, vwt_ref, g_ref, wovt_ref, o_ref, m_sc, acc_sc,
                     *, n_cmp, n_slc):
    tq = tk = NSA_TILE
    rep = C_HEADS // C_KV_HEADS
    blk = pl.program_id(1)
    q0 = blk * tq
    n_cmp_pad = kc_ref.shape[1]
    sig_t = jax.nn.sigmoid(g_ref[0])
    q_t = [jnp.transpose(q_ref[0, :, c * LANES:(c + 1) * LANES].astype(F32))
           for c in range(C_WIDTH // LANES)]
    qpos = q0 + lax.broadcasted_iota(jnp.int32, (1, tq), 1)
    cmp_i = lax.broadcasted_iota(jnp.int32, (n_cmp_pad, tq), 0)
    cmp_ok = (cmp_i * CMP_STRIDE + (CMP_LEN - 1) <= qpos) & (cmp_i < n_cmp)
    cmp_bias = jnp.where(cmp_ok, 0.0, NEG_INF)
    cmp_keep = jnp.where(cmp_ok, 1.0, 0.0)
    ahead = lax.broadcasted_iota(jnp.int32, (tk, tq), 0) - lax.broadcasted_iota(jnp.int32, (tk, tq), 1)
    diag_bias = jnp.where(ahead <= 0, 0.0, NEG_INF)
    far_bias = jnp.where(ahead >= 0, 0.0, NEG_INF)
    ones_rows = jnp.where(lax.broadcasted_iota(jnp.int32, (ONES_ROWS, tk), 0) == 0, 1.0, 0.0).astype(BF16)

    def flash_update(scores, v_t):
        v_ext = jnp.concatenate([v_t, ones_rows], axis=0)
        for r, s in enumerate(scores):
            m_old = m_sc[r:r + 1, :]
            m_new = jnp.maximum(m_old, jnp.max(s, axis=0, keepdims=True))
            m_sc[r:r + 1, :] = m_new
            pv = _dot(v_ext, jnp.exp2(s - m_new).astype(BF16))
            acc_sc[r] = jnp.exp2(m_old - m_new) * acc_sc[r] + pv

    def flash_reset():
        m_sc[...] = jnp.full(m_sc.shape, NEG_INF, F32)
        acc_sc[...] = jnp.zeros(acc_sc.shape, F32)

    def flash_result(r):
        acc = acc_sc[r]
        return acc[:HEAD_DIM] / acc[HEAD_DIM:HEAD_DIM + 1]

    outs = []
    for g in range(C_KV_HEADS):
        gc = slice(g * HEAD_DIM, (g + 1) * HEAD_DIM)
        heads = [g * rep + r for r in range(rep)]
        q_heads = [q_t[h // 2][(h % 2) * HEAD_DIM:(h % 2 + 1) * HEAD_DIM] for h in heads]
        q_nat = [(qh * (HEAD_DIM ** -0.5)).astype(BF16) for qh in q_heads]
        q_log2 = [(qh * (HEAD_DIM ** -0.5 * LOG2E)).astype(BF16) for qh in q_heads]
        kcb = kc_ref[0, :, gc].astype(BF16)
        vct = vct_ref[0, gc, :].astype(BF16)
        scores = [_dot(kcb, q_nat[r]) + cmp_bias for r in range(rep)]
        probs = []
        for s in scores:
            p = jnp.exp(s - jnp.max(s, axis=0, keepdims=True)) * cmp_keep
            l = jnp.sum(p, axis=0, keepdims=True)
            probs.append(p / jnp.where(l > 0, l, 1.0))
        o_cmp = [_dot(vct, p.astype(BF16)) for p in probs]
        imp = probs[0]
        for p in probs[1:]:
            imp = imp + p
        hi = imp.astype(BF16)
        r1 = imp - hi.astype(F32)
        mid = r1.astype(BF16)
        lo = (r1 - mid.astype(F32)).astype(BF16)
        p_slc = _dot(wovt_ref[...], hi) + _dot(wovt_ref[...], mid) + _dot(wovt_ref[...], lo)
        sel_bias = jnp.where(_select_blocks_t(p_slc, qpos, n_slc) > 0.5, 0.0, NEG_INF).astype(BF16)
        q_sel = [jnp.concatenate([qh, sel_bias], axis=0) for qh in q_log2]
        flash_reset()

        def sel_tile(k0, diagonal):
            k0 = pl.multiple_of(k0, tk)
            k_tile = ks_ref[0, pl.ds(k0, tk), g * LANES:(g + 1) * LANES]
            scores = [_dot(k_tile, qs) for qs in q_sel]
            if diagonal:
                scores = [s + diag_bias for s in scores]
            flash_update(scores, vst_ref[0, gc, pl.ds(k0, tk)])

        def full_tile(kt, carry):
            sel_tile(kt * tk, False)
            return carry

        lax.fori_loop(0, blk, full_tile, 0)
        sel_tile(q0, True)
        o_sel = [flash_result(r) for r in range(rep)]
        flash_reset()

        def win_tile(k0, bias):
            k0 = pl.multiple_of(k0, tk)
            k_tile = kw_ref[0, pl.ds(k0, tk), gc]
            scores = [_dot(k_tile, qh) for qh in q_log2]
            if bias is not None:
                scores = [s + bias for s in scores]
            flash_update(scores, vwt_ref[0, gc, pl.ds(k0, tk)])

        n_back = C_WINDOW // tk
        for back in range(n_back, 0, -1):
            @pl.when(blk >= back)
            def _():
                win_tile(q0 - back * tk, far_bias if back == n_back else None)

        win_tile(q0, diag_bias)
        for r, h in enumerate(heads):
            outs.append(sig_t[3 * h:3 * h + 1] * o_cmp[r] + sig_t[3 * h + 1:3 * h + 2] * o_sel[r]
                        + sig_t[3 * h + 2:3 * h + 3] * flash_result(r))
    for c in range(C_WIDTH // LANES):
        o_ref[0, :, c * LANES:(c + 1) * LANES] = jnp.transpose(jnp.concatenate(outs[2 * c:2 * c + 2], axis=0))


def _nsa_prompt(q, kcb, vcb, ks, vs_t, kw, vw_t, gates_t, n_cmp):
    b, t, wq = q.shape
    n_cmp_pad = kcb.shape[1]
    n_slc = -(-t // SEL_BLOCK)
    tile = NSA_TILE
    assert t % tile == 0 and n_slc == HEAD_DIM and C_WINDOW % tile == 0
    wov_t = _overlap_weights(n_cmp, n_cmp_pad, n_slc, LANES).T
    vcb_t = jnp.swapaxes(vcb, 1, 2)
    onehot = jnp.asarray(np.arange(t)[:, None] // SEL_BLOCK == np.arange(HEAD_DIM)[None, :], BF16)
    ks_ext = jnp.concatenate([ks.reshape(b, t, C_KV_HEADS, HEAD_DIM),
                              jnp.broadcast_to(onehot[None, :, None, :], (b, t, C_KV_HEADS, HEAD_DIM))],
                             axis=-1).reshape(b, t, C_KV_HEADS * LANES)
    rep = C_HEADS // C_KV_HEADS
    whole = lambda a: pl.BlockSpec((1,) + a.shape[1:], lambda bi, i: (bi, 0, 0))
    return pl.pallas_call(
        functools.partial(_nsa_prompt_body, n_cmp=n_cmp, n_slc=n_slc),
        grid=(b, t // tile),
        in_specs=[pl.BlockSpec((1, tile, wq), lambda bi, i: (bi, i, 0)),
                  whole(kcb), whole(vcb_t), whole(ks_ext), whole(vs_t), whole(kw), whole(vw_t),
                  pl.BlockSpec((1, gates_t.shape[1], tile), lambda bi, i: (bi, 0, i)),
                  pl.BlockSpec(wov_t.shape, lambda bi, i: (0, 0))],
        out_specs=pl.BlockSpec((1, tile, wq), lambda bi, i: (bi, i, 0)),
        out_shape=jax.ShapeDtypeStruct((b, t, wq), F32),
        scratch_shapes=[pltpu.VMEM((rep, tile), F32), pltpu.VMEM((rep, HEAD_DIM + ONES_ROWS, tile), F32)],
        compiler_params=_params("parallel", "arbitrary"),
        name="nsa_prompt",
    )(q, kcb, vcb_t, ks_ext, vs_t, kw, vw_t, gates_t, wov_t)


def _block_diag(q, blk_of_row, n_blk):
    return jnp.concatenate([jnp.where(blk_of_row == b, q, 0.0) for b in range(n_blk)], axis=-1)


def _diag_blocks(r, blk_of_row, n_blk):
    out = jnp.where(blk_of_row == 0, r[:, 0:HEAD_DIM], 0.0)
    for b in range(1, n_blk):
        out = out + jnp.where(blk_of_row == b, r[:, b * HEAD_DIM:(b + 1) * HEAD_DIM], 0.0)
    return out


def _row_ids(n_rows, n_heads, n_blk):
    row = lax.broadcasted_iota(jnp.int32, (n_rows, 1), 0)
    step = _idiv(row, n_heads)
    blk = _idiv(_imod(row, n_heads), n_heads // n_blk)
    return step, blk


def _pattern_weight(dist, patterns):
    w = jnp.zeros(dist.shape, F32)
    for window, dil in patterns:
        w = w + jnp.where(_imod(dist, dil) == 0, jnp.where(dist <= window, 1.0, 0.0), 0.0)
    return jnp.where(dist >= 0, w, 0.0)


def _decode_body(*refs, n_heads, n_blk, patterns, gate_col):
    q_ref, kc_ref, vc_ref, kn_ref, vn_ref = refs[:5]
    g_ref = refs[5] if gate_col is not None else None
    o_ref = refs[-1]
    n_rows = q_ref.shape[1]
    n_cache = kc_ref.shape[2]
    n_new = kn_ref.shape[1]
    step, blk = _row_ids(n_rows, n_heads, n_blk)
    qbd = _block_diag(q_ref[0] * (HEAD_DIM ** -0.5), blk, n_blk)
    s_c = _dot(qbd.astype(BF16), kc_ref[0].astype(BF16))
    w_c = _pattern_weight(n_cache + step - lax.broadcasted_iota(jnp.int32, (n_rows, n_cache), 1), patterns)
    s_c = jnp.where(w_c > 0, s_c, NEG_INF)
    kn = kn_ref[0]
    vn = vn_ref[0]
    s_n, w_n = [], []
    for c in range(n_new):
        s = jnp.sum(qbd * kn[c:c + 1, :], axis=-1, keepdims=True)
        w = _pattern_weight(step - c, patterns)
        s_n.append(jnp.where(w > 0, s, NEG_INF))
        w_n.append(w)
    m = jnp.max(s_c, axis=-1, keepdims=True)
    for s in s_n:
        m = jnp.maximum(m, s)
    p_c = w_c * jnp.exp(s_c - m)
    l = jnp.sum(p_c, axis=-1, keepdims=True)
    r = _dot_nt(p_c.astype(BF16), vc_ref[0].astype(BF16))
    for c in range(n_new):
        p = w_n[c] * jnp.exp(s_n[c] - m)
        l = l + p
        r = r + p * vn[c:c + 1, :]
    o = _diag_blocks(r, blk, n_blk) / l
    if g_ref is not None:
        o = o * jax.nn.sigmoid(g_ref[0])[:, gate_col:gate_col + 1]
    o_ref[0] = o


def _decode_attention(q, kc, vc, kn, vn, *, n_heads, n_blk, patterns, gate=None, gate_col=None):
    n, n_rows, _ = q.shape
    args = [q, kc, vc, kn, vn]
    if gate is not None:
        args.append(gate)
    whole = lambda a: pl.BlockSpec((1,) + a.shape[1:], lambda i: (i, 0, 0))
    return pl.pallas_call(
        functools.partial(_decode_body, n_heads=n_heads, n_blk=n_blk, patterns=patterns,
                          gate_col=gate_col if gate is not None else None),
        grid=(n,),
        in_specs=[whole(a) for a in args],
        out_specs=pl.BlockSpec((1, n_rows, HEAD_DIM), lambda i: (i, 0, 0)),
        out_shape=jax.ShapeDtypeStruct((n, n_rows, HEAD_DIM), F32),
        compiler_params=_params("parallel"),
        name="decode_attention",
    )(*args)


def _nsa_decode_body(pt_ref, q_ref, kc_ref, vc_ref, ksp_ref, vsp_ref, kn_ref, vn_ref, g_ref, wov_ref, o_ref,
                     kbuf, vbuf, expand, ksem, vsem, *, n_pages, n_cmp, n_slc):
    n = pl.program_id(0)
    slot = n % 2
    n_rows = q_ref.shape[1]
    n_new = kn_ref.shape[1]
    n_past = kbuf.shape[2]
    n_slc_pad = wov_ref.shape[1]
    rep = C_HEADS // C_KV_HEADS

    def gather(seq, dst_slot, start):
        for pool_ref, buf, sem in ((ksp_ref, kbuf, ksem), (vsp_ref, vbuf, vsem)):
            for p in range(n_pages):
                page = pt_ref[seq, p] if start else 0
                cp = pltpu.make_async_copy(pool_ref.at[page], buf.at[dst_slot, :, pl.ds(p * PAGE_SIZE, PAGE_SIZE)],
                                           sem.at[dst_slot])
                if start:
                    cp.start()
                else:
                    cp.wait()

    @pl.when(n == 0)
    def _():
        gather(0, 0, True)
        step_cols = 1024
        for c0 in range(0, n_past, step_cols):
            shape = (n_slc_pad, min(step_cols, n_past - c0))
            blk_of_key = _idiv(c0 + lax.broadcasted_iota(jnp.int32, shape, 1), SEL_BLOCK)
            hit = blk_of_key == lax.broadcasted_iota(jnp.int32, shape, 0)
            expand[:, c0:c0 + shape[1]] = jnp.where(hit, 1.0, 0.0).astype(BF16)

    @pl.when(n + 1 < pl.num_programs(0))
    def _():
        gather(n + 1, 1 - slot, True)

    step, blk = _row_ids(n_rows, C_HEADS, C_KV_HEADS)
    qpos = n_past + step
    qbd = _block_diag(q_ref[0] * (HEAD_DIM ** -0.5), blk, C_KV_HEADS)
    qbd_in = qbd.astype(BF16)
    n_cmp_pad = kc_ref.shape[1]
    cmp_i = lax.broadcasted_iota(jnp.int32, (n_rows, n_cmp_pad), 1)
    cmp_ok = (cmp_i * CMP_STRIDE + (CMP_LEN - 1) <= qpos) & (cmp_i < n_cmp)
    s = jnp.where(cmp_ok, _dot_nt(qbd_in, kc_ref[0].astype(BF16)), NEG_INF)
    m = jnp.max(s, axis=-1, keepdims=True)
    p = jnp.where(cmp_ok, jnp.exp(s - m), 0.0)
    l = jnp.sum(p, axis=-1, keepdims=True)
    p = p / jnp.where(l > 0, l, 1.0)
    o_cmp = _diag_blocks(_dot(p.astype(BF16), vc_ref[0].astype(BF16)), blk, C_KV_HEADS)
    grp = (n_rows // rep, rep, n_cmp_pad)
    imp = jnp.broadcast_to(jnp.sum(p.reshape(grp), axis=1, keepdims=True), grp).reshape(n_rows, n_cmp_pad)
    sel_rows = _select_blocks(_split3_dot(imp, wov_ref[...]), qpos, n_slc)
    gather(n, slot, False)
    picked = _dot(sel_rows.astype(BF16), expand[...])
    kpos = lax.broadcasted_iota(jnp.int32, (n_rows, n_past), 1)
    s_p = jnp.where(picked > 0.5, jnp.where(kpos <= qpos, _dot(qbd_in, kbuf[slot].astype(BF16)), NEG_INF), NEG_INF)
    kn = kn_ref[0]
    vn = vn_ref[0]
    s_n = []
    for c in range(n_new):
        blk_c = (n_past + c) // SEL_BLOCK
        ok = (sel_rows[:, blk_c:blk_c + 1] > 0.5) & (n_past + c <= qpos)
        s_n.append(jnp.where(ok, jnp.sum(qbd * kn[c:c + 1, :], axis=-1, keepdims=True), NEG_INF))
    m = jnp.max(s_p, axis=-1, keepdims=True)
    for sc in s_n:
        m = jnp.maximum(m, sc)
    p_p = jnp.exp(s_p - m)
    l = jnp.sum(p_p, axis=-1, keepdims=True)
    r = _dot_nt(p_p.astype(BF16), vbuf[slot].astype(BF16))
    for c in range(n_new):
        pn = jnp.exp(s_n[c] - m)
        l = l + pn
        r = r + pn * vn[c:c + 1, :]
    o_sel = _diag_blocks(r, blk, C_KV_HEADS) / l
    sig = jax.nn.sigmoid(g_ref[0])
    o_ref[0] = sig[:, 0:1] * o_cmp + sig[:, 1:2] * o_sel


def _nsa_decode(q, kcb, vcb, pool_ks, pool_vs, page_table, kn, vn, gate, n_cmp):
    n, n_rows, _ = q.shape
    n_pages = page_table.shape[1]
    n_past = n_pages * PAGE_SIZE
    n_new = kn.shape[1]
    n_slc = -(-(n_past + n_new) // SEL_BLOCK)
    n_slc_pad = -(-n_slc // LANES) * LANES
    wov = _overlap_weights(n_cmp, kcb.shape[1], n_slc, n_slc_pad)
    whole = lambda a: pl.BlockSpec((1,) + a.shape[1:], lambda i, pt: (i, 0, 0))
    pools = [jnp.transpose(a, (0, 2, 3, 1)).reshape(a.shape[0], C_KV_WIDTH, PAGE_SIZE) for a in (pool_ks, pool_vs)]
    return pl.pallas_call(
        functools.partial(_nsa_decode_body, n_pages=n_pages, n_cmp=n_cmp, n_slc=n_slc),
        grid_spec=pltpu.PrefetchScalarGridSpec(
            num_scalar_prefetch=1,
            grid=(n,),
            in_specs=[whole(q), whole(kcb), whole(vcb), pl.BlockSpec(memory_space=pl.ANY),
                      pl.BlockSpec(memory_space=pl.ANY), whole(kn), whole(vn), whole(gate),
                      pl.BlockSpec(wov.shape, lambda i, pt: (0, 0))],
            out_specs=pl.BlockSpec((1, n_rows, HEAD_DIM), lambda i, pt: (i, 0, 0)),
            scratch_shapes=[pltpu.VMEM((2, C_KV_WIDTH, n_past), F32), pltpu.VMEM((2, C_KV_WIDTH, n_past), F32),
                            pltpu.VMEM((n_slc_pad, n_past), BF16),
                            pltpu.SemaphoreType.DMA((2,)), pltpu.SemaphoreType.DMA((2,))]),
        out_shape=jax.ShapeDtypeStruct((n, n_rows, HEAD_DIM), F32),
        compiler_params=_params("arbitrary"),
        name="nsa_decode",
    )(page_table, q, kcb, vcb, *pools, kn, vn, gate, wov)


def _odd_out_body(x_ref, *refs):
    *branch_refs, z_ref, w_ref, gf_ref, y_ref = refs
    o = branch_refs[0][...]
    for ref in branch_refs[1:]:
        o = o + ref[...]
    mixed = (o * _silu(z_ref[...])).astype(BF16)
    h = x_ref[...] + _dot(mixed, w_ref[...])
    y_ref[...] = h * lax.rsqrt(jnp.mean(h * h, axis=-1, keepdims=True) + RMS_EPS) * gf_ref[...]


def _odd_out(x2d, branches, z, w_out, final_gain, tm):
    m, d = x2d.shape
    row = pl.BlockSpec((tm, d), lambda i: (i, 0))
    return pl.pallas_call(
        _odd_out_body,
        grid=(m // tm,),
        in_specs=[row] * (len(branches) + 2) + [pl.BlockSpec(w_out.shape, lambda i: (0, 0)),
                                                 pl.BlockSpec((1, d), lambda i: (0, 0))],
        out_specs=row,
        out_shape=jax.ShapeDtypeStruct((m, d), F32),
        compiler_params=_params("parallel"),
        name="odd_out",
    )(x2d, *branches, z, w_out, final_gain.reshape(1, d))


def _even_outs(prompt):
    a, bk = A_WIDTH, B_HEADS * B_DK
    return [(0, a, BF16, False),
            (a, a, BF16, False), (a, a, F32, prompt),
            (2 * a, a, BF16, False), (2 * a, a, F32, prompt),
            (3 * a, a, F32, False),
            (4 * a, bk, F32, False), (4 * a + bk, bk, F32, False),
            (4 * a + 2 * bk, B_WIDTH, F32, False), (4 * a + 2 * bk + B_WIDTH, B_WIDTH, F32, False)]


def _from_feature_major(u, t0, n_heads):
    b, _, t = u.shape
    return jnp.transpose(u[:, :, t0:].reshape(b, n_heads, HEAD_DIM, t - t0), (0, 3, 1, 2))


def _to_feature_major(u):
    n, length, n_heads, hd = u.shape
    return jnp.transpose(u, (0, 2, 3, 1)).reshape(n, n_heads * hd, length)


def _even_layer_prompt(x, gain, w_in, w_out):
    b, t, d = x.shape
    x2d = x.reshape(b * t, d)
    qa, ka, ka_t, va, va_t, za, qb, kb, vb, zb = _norm_proj(x2d, gain, w_in, _even_outs(True), 512, t)
    seq = lambda u: u.reshape(b, t, u.shape[-1])
    pats = []
    for window, dil in A_PATTERNS:
        n_sub = t // dil
        pats.append(_banded_attention(seq(qa), seq(ka), seq(va), dil=dil, n_heads=A_HEADS, n_groups=A_HEADS,
                                      lookback=window // dil, tqb=min(512, n_sub)))
    pats = [[a.reshape(b * t, A_WIDTH) for a in pair] for pair in pats]
    chunk = B_CHUNK if t % B_CHUNK == 0 else t
    s0 = jnp.zeros((b, B_HEADS, B_DK, B_DV), F32)
    ob, s_fin = _retention(seq(qb), seq(kb), seq(vb), s0, jnp.arange(t), chunk=chunk, c_real=chunk)
    y = _even_out(x2d, pats, za, ob.reshape(b * t, B_WIDTH), zb, w_out, 256)
    keep = min(A_PATTERNS[-1][0], t)
    return (y.reshape(b, t, d), _from_feature_major(ka_t, t - keep, A_HEADS),
            _from_feature_major(va_t, t - keep, A_HEADS), s_fin)


def _even_layer_sample(x, cache_k, cache_v, state, gain, w_in, w_out, past_len):
    n, s_len, d = x.shape
    x2d = x.reshape(n * s_len, d)
    qa, _, ka32, _, va32, za, qb, kb, vb, zb = _norm_proj(x2d, gain, w_in, _even_outs(False), n * s_len, n * s_len)
    seq = lambda u: u.reshape(n, s_len, u.shape[-1])
    oa = _decode_attention(qa.astype(F32).reshape(n, s_len * A_HEADS, HEAD_DIM), _to_feature_major(cache_k),
                           _to_feature_major(cache_v), seq(ka32), seq(va32), n_heads=A_HEADS, n_blk=A_HEADS,
                           patterns=A_PATTERNS)
    oa = oa.reshape(n * s_len, A_WIDTH)
    chunk = 16
    pad = lambda u: jnp.pad(seq(u), ((0, 0), (0, chunk - s_len), (0, 0)))
    pos = past_len + jnp.arange(chunk)
    ob, s_new = _retention(pad(qb), pad(kb), pad(vb), state, pos, chunk=chunk, c_real=s_len)
    ob = ob[:, :s_len].reshape(n * s_len, B_WIDTH)
    zero = jnp.zeros_like(oa)
    y = _even_out(x2d, [[oa, zero]] * 3, za, ob, zb, w_out, n * s_len)
    heads = lambda u: u.reshape(n, s_len, A_HEADS, HEAD_DIM)
    return y.reshape(n, s_len, d), heads(ka32), heads(va32), s_new


ODD_KV = ("kc", "vc", "ks", "vs", "kw", "vw")
ODD_GATE_COL = C_WIDTH + len(ODD_KV) * C_KV_WIDTH
ODD_Z_COL = ODD_GATE_COL + 3 * C_HEADS


def _odd_outs_prompt():
    outs = [(0, C_WIDTH, BF16, False)]
    for i, name in enumerate(ODD_KV):
        off = C_WIDTH + i * C_KV_WIDTH
        outs.append((off, C_KV_WIDTH, BF16, name in ("vs", "vw")))
        outs.append((off, C_KV_WIDTH, F32, True))
    outs += [(ODD_GATE_COL, 3 * C_HEADS, F32, True), (ODD_Z_COL, C_WIDTH, F32, False)]
    return outs


def _odd_outs_sample():
    outs = [(0, C_WIDTH, BF16, False)]
    outs += [(C_WIDTH + i * C_KV_WIDTH, C_KV_WIDTH, F32, False) for i in range(len(ODD_KV))]
    outs += [(ODD_GATE_COL, 3 * C_HEADS, F32, False), (ODD_Z_COL, C_WIDTH, F32, False)]
    return outs


def _odd_layer_prompt(x, gain, w_in, w_out, cw_k, cw_v, final_gain):
    b, t, d = x.shape
    x2d = x.reshape(b * t, d)
    (q, kc, kc_t, vc, vc_t, ks, ks_t, vs_tb, vs_t, kw, kw_t, vw_tb, vw_t, gl_t, z) = _norm_proj(
        x2d, gain, w_in, _odd_outs_prompt(), 512, t)
    seq = lambda u: u.reshape(b, t, u.shape[-1])
    n_cmp = (t - CMP_LEN) // CMP_STRIDE + 1
    pieces = lambda u: u.reshape(b, t // CMP_STRIDE, CMP_STRIDE * C_KV_WIDTH)
    kcb = _compress(pieces(kc), cw_k)
    vcb = _compress(pieces(vc), cw_v)
    o = _nsa_prompt(seq(q), kcb, vcb, seq(ks), vs_tb, seq(kw), vw_tb, gl_t, n_cmp)
    y = _odd_out(x2d, [o.reshape(b * t, C_WIDTH)], z, w_out, final_gain, 256)
    keep = min(C_WINDOW, t)
    kv = lambda u, t0: _from_feature_major(u, t0, C_KV_HEADS)
    return (y.reshape(b, t, d), kv(kc_t, 0), kv(vc_t, 0), kv(ks_t, 0), kv(vs_t, 0), kv(kw_t, t - keep),
            kv(vw_t, t - keep))


def _odd_layer_sample(x, page_table, pool_kc, pool_vc, pool_ks, pool_vs, buf_kw, buf_vw, gain, w_in, w_out,
                      cw_k, cw_v, final_gain):
    n, s_len, d = x.shape
    x2d = x.reshape(n * s_len, d)
    q, kc, vc, ks32, vs32, kw32, vw32, gl, z = _norm_proj(x2d, gain, w_in, _odd_outs_sample(), n * s_len, n * s_len)
    seq = lambda u: u.reshape(n, s_len, u.shape[-1])
    n_past = page_table.shape[1] * PAGE_SIZE
    n_cmp = (n_past + s_len - CMP_LEN) // CMP_STRIDE + 1
    assert CMP_STRIDE * (n_cmp - 1) + CMP_LEN <= n_past
    kcb = _compress_paged(pool_kc, page_table, cw_k)
    vcb = _compress_paged(pool_vc, page_table, cw_v)
    rows = lambda u: u.astype(F32).reshape(n, s_len * C_HEADS, HEAD_DIM)
    gate = gl.reshape(n, s_len * C_HEADS, 3)
    o_cs = _nsa_decode(rows(q), kcb, vcb, pool_ks, pool_vs, page_table, seq(ks32), seq(vs32), gate, n_cmp)
    o_win = _decode_attention(rows(q), _to_feature_major(buf_kw), _to_feature_major(buf_vw),
                              seq(kw32), seq(vw32), n_heads=C_HEADS, n_blk=C_KV_HEADS,
                              patterns=((C_WINDOW, 1),), gate=gate, gate_col=2)
    y = _odd_out(x2d, [o_cs.reshape(n * s_len, C_WIDTH), o_win.reshape(n * s_len, C_WIDTH)], z, w_out, final_gain,
                 n * s_len)
    kv = lambda u: u.reshape(n, s_len, C_KV_HEADS, HEAD_DIM)
    return y.reshape(n, s_len, d), kv(kc), kv(vc), kv(ks32), kv(vs32), kv(kw32), kv(vw32)


def kernel(x_prompt, x_sample, cache_a_k, cache_a_v, state_ret, cache_c_kcmp, cache_c_vcmp, cache_c_ksel,
           cache_c_vsel, cache_c_kwin, cache_c_vwin, page_table, even_norm, even_w_in, even_w_out, odd_norm,
           odd_w_in, odd_w_out, cmp_pos_k, cmp_w1_k, cmp_w2_k, cmp_pos_v, cmp_w1_v, cmp_w2_v, final_norm):
    assert even_norm.shape[0] == 1 and odd_norm.shape[0] == 1
    past_len = page_table.shape[1] * PAGE_SIZE
    w_in_e = even_w_in[0].T.astype(BF16)
    w_out_e = even_w_out[0].astype(BF16)
    w_in_o = odd_w_in[0].T.astype(BF16)
    w_out_o = odd_w_out[0].astype(BF16)
    cw_k = _compress_weights(cmp_pos_k[0], cmp_w1_k[0], cmp_w2_k[0])
    cw_v = _compress_weights(cmp_pos_v[0], cmp_w1_v[0], cmp_w2_v[0])

    hp, ak_p, av_p, ret_p = _even_layer_prompt(x_prompt, even_norm[0], w_in_e, w_out_e)
    hs, ak_s, av_s, ret_s = _even_layer_sample(x_sample, cache_a_k[0], cache_a_v[0], state_ret[0], even_norm[0],
                                               w_in_e, w_out_e, past_len)
    yp, *rows_p = _odd_layer_prompt(hp, odd_norm[0], w_in_o, w_out_o, cw_k, cw_v, final_norm)
    ys, *rows_s = _odd_layer_sample(hs, page_table, cache_c_kcmp[0], cache_c_vcmp[0], cache_c_ksel[0],
                                    cache_c_vsel[0], cache_c_kwin[0], cache_c_vwin[0], odd_norm[0], w_in_o, w_out_o,
                                    cw_k, cw_v, final_norm)
    lead = lambda u: u[None]
    out = [yp, ys, lead(ak_p), lead(ak_s), lead(av_p), lead(av_s), lead(ret_p), lead(ret_s)]
    for rp, rs in zip(rows_p, rows_s):
        out += [lead(rp), lead(rs)]
    return tuple(out)
```

```python
import functools

import numpy as np
import jax
import jax.numpy as jnp
from jax import lax
from jax.experimental import pallas as pl
from jax.experimental.pallas import tpu as pltpu

F32 = jnp.float32
BF16 = jnp.bfloat16

HEAD_DIM = 64
A_HEADS = 8
A_WIDTH = A_HEADS * HEAD_DIM
A_PATTERNS = ((128, 1), (512, 4), (2048, 16))
B_HEADS = 4
B_DK = 64
B_DV = 128
B_WIDTH = B_HEADS * B_DV
B_CHUNK = 128
ROPE_BASE = 10000.0
GN_EPS = 1e-5
C_HEADS = 16
C_KV_HEADS = 2
C_WIDTH = C_HEADS * HEAD_DIM
C_KV_WIDTH = C_KV_HEADS * HEAD_DIM
CMP_LEN = 32
CMP_STRIDE = 16
CMP_HIDDEN = 256
SEL_BLOCK = 64
SEL_TOPK = 16
SEL_FORCE = 1e9
C_WINDOW = 512
PAGE_SIZE = 128
RMS_EPS = 1e-6
NEG_INF = -1e30

LANES = 128
ATTN_TQ = 128
VMEM_LIMIT = 56 * 1024 * 1024


def _params(*sem):
    return pltpu.CompilerParams(dimension_semantics=sem, vmem_limit_bytes=VMEM_LIMIT)


def _dot(a, b):
    return jnp.dot(a, b, preferred_element_type=F32)


def _dot_nt(a, b):
    return lax.dot_general(a, b, (((1,), (1,)), ((), ())), preferred_element_type=F32)


def _dot_tn(a, b):
    return lax.dot_general(a, b, (((0,), (0,)), ((), ())), preferred_element_type=F32)


def _idiv(x, n):
    assert n & (n - 1) == 0
    return jnp.right_shift(x, n.bit_length() - 1)


def _imod(x, n):
    assert n & (n - 1) == 0
    return jnp.bitwise_and(x, n - 1)


def _split3_dot(a, w):
    hi = a.astype(BF16)
    r1 = a - hi.astype(F32)
    mid = r1.astype(BF16)
    lo = (r1 - mid.astype(F32)).astype(BF16)
    return _dot(hi, w) + _dot(mid, w) + _dot(lo, w)


def _norm_proj_body(x_ref, g_ref, wt_ref, *o_refs, outs):
    x = x_ref[...]
    y = x * lax.rsqrt(jnp.mean(x * x, axis=-1, keepdims=True) + RMS_EPS) * g_ref[...]
    yb = y.astype(BF16)
    done = {}
    for o_ref, (off, width, _, feature_major) in zip(o_refs, outs):
        key = (off, width, feature_major)
        if key not in done:
            w_rows = wt_ref[off:off + width, :]
            done[key] = _dot_nt(w_rows, yb) if feature_major else _dot_nt(yb, w_rows)
        if feature_major:
            o_ref[0] = done[key].astype(o_ref.dtype)
        else:
            o_ref[...] = done[key].astype(o_ref.dtype)


def _norm_proj(x2d, gain, w_t, outs, tm, seq_len):
    m, d = x2d.shape
    n = w_t.shape[0]
    tiles = seq_len // tm
    assert seq_len % tm == 0 and m % seq_len == 0
    out_specs, out_shape = [], []
    for _, width, dt, feature_major in outs:
        if feature_major:
            out_specs.append(pl.BlockSpec((1, width, tm), lambda i: (i // tiles, 0, i % tiles)))
            out_shape.append(jax.ShapeDtypeStruct((m // seq_len, width, seq_len), dt))
        else:
            out_specs.append(pl.BlockSpec((tm, width), lambda i: (i, 0)))
            out_shape.append(jax.ShapeDtypeStruct((m, width), dt))
    return pl.pallas_call(
        functools.partial(_norm_proj_body, outs=outs),
        grid=(m // tm,),
        in_specs=[pl.BlockSpec((tm, d), lambda i: (i, 0)),
                  pl.BlockSpec((1, d), lambda i: (0, 0)),
                  pl.BlockSpec((n, d), lambda i: (0, 0))],
        out_specs=out_specs,
        out_shape=out_shape,
        compiler_params=_params("parallel"),
        name="norm_proj",
    )(x2d, gain.reshape(1, d), w_t)


DIL_GROUP = 4


def _dilated_body(q_ref, k_ref, v_ref, o_ref, o_sc, lse_sc, *, patterns):
    tq = ATTN_TQ
    blk_len = q_ref.shape[1]
    base = pl.program_id(2) * blk_len
    head_cols = [slice(h * HEAD_DIM, (h + 1) * HEAD_DIM) for h in range(LANES // HEAD_DIM)]

    def rows(ref, start, size, dil):
        return ref[0, pl.ds(start, size) if dil == 1 else pl.ds(start, size, stride=dil), :]

    for p_idx, (window, dil) in enumerate(patterns):
        lookback = window // dil
        n_prev = -(-lookback // tq)
        span = (n_prev + 1) * tq
        n_u = blk_len // (tq * dil)
        n_sub_blocks = dil * n_u
        assert blk_len % (tq * dil) == 0 and n_sub_blocks % DIL_GROUP == 0

        def trip(i, carry, dil=dil, lookback=lookback, n_prev=n_prev, span=span, p_idx=p_idx):
            work = []
            for j in range(DIL_GROUP):
                idx = i * DIL_GROUP + j
                res = _imod(idx, dil)
                u = _idiv(idx, dil)
                s0 = base // dil + u * tq
                ks = jnp.maximum(s0 - n_prev * tq, 0)
                q_row = res + dil * (u * tq)
                k_row = res + dil * ks
                q2 = (rows(q_ref, q_row, tq, dil) * (HEAD_DIM ** -0.5)).astype(BF16)
                k2 = rows(k_ref, k_row, span, dil).astype(BF16)
                v2 = rows(v_ref, k_row, span, dil).astype(BF16)
                dist = (s0 + lax.broadcasted_iota(jnp.int32, (tq, span), 0)
                        - ks - lax.broadcasted_iota(jnp.int32, (tq, span), 1))
                bias = jnp.where(dist >= 0, jnp.where(dist <= lookback, 0.0, NEG_INF), NEG_INF)
                work.append((q_row, q2, k2, v2, bias))
            scores = [[_dot_nt(q2[:, c], k2[:, c]) + bias for c in head_cols] for _, q2, k2, _, bias in work]
            maxes = [[jnp.max(s, axis=-1, keepdims=True) for s in ss] for ss in scores]
            probs = [[jnp.exp(s - m) for s, m in zip(ss, mm)] for ss, mm in zip(scores, maxes)]
            sums = [[jnp.sum(p, axis=-1, keepdims=True) for p in pp] for pp in probs]
            outs = [[_dot(p.astype(BF16), w[3][:, c]) for p, c in zip(pp, head_cols)] for pp, w in zip(probs, work)]
            for w, oo, mm, ll in zip(work, outs, maxes, sums):
                o2 = jnp.concatenate([o / l for o, l in zip(oo, ll)], axis=-1)
                lse2 = jnp.concatenate([jnp.broadcast_to(m + jnp.log(l), (tq, HEAD_DIM)) for m, l in zip(mm, ll)],
                                       axis=-1)
                idx = pl.ds(w[0], tq) if dil == 1 else pl.ds(w[0], tq, stride=dil)
                o_sc[p_idx, idx, :] = o2
                lse_sc[p_idx, idx, :] = lse2
            return carry

        lax.fori_loop(0, n_sub_blocks // DIL_GROUP, trip, 0)

    lses = [lse_sc[p] for p in range(len(patterns))]
    top = lses[0]
    for l in lses[1:]:
        top = jnp.maximum(top, l)
    weights = [jnp.exp(l - top) for l in lses]
    num = weights[0] * o_sc[0]
    den = weights[0]
    for p in range(1, len(patterns)):
        num = num + weights[p] * o_sc[p]
        den = den + weights[p]
    o_ref[0] = num / den


def _dilated_attention(q, k, v, patterns):
    b, t, width = q.shape
    blk_len = min(t, max(ATTN_TQ * dil for _, dil in patterns))
    assert t % blk_len == 0
    for window, dil in patterns:
        assert t // dil >= (-(-(window // dil) // ATTN_TQ) + 1) * ATTN_TQ
    return pl.pallas_call(
        functools.partial(_dilated_body, patterns=patterns),
        grid=(b, width // LANES, t // blk_len),
        in_specs=[pl.BlockSpec((1, blk_len, LANES), lambda bi, hp, i: (bi, i, hp)),
                  pl.BlockSpec((1, t, LANES), lambda bi, hp, i: (bi, 0, hp)),
                  pl.BlockSpec((1, t, LANES), lambda bi, hp, i: (bi, 0, hp))],
        out_specs=pl.BlockSpec((1, blk_len, LANES), lambda bi, hp, i: (bi, i, hp)),
        out_shape=jax.ShapeDtypeStruct((b, t, width), F32),
        scratch_shapes=[pltpu.VMEM((len(patterns), blk_len, LANES), F32),
                        pltpu.VMEM((len(patterns), blk_len, LANES), F32)],
        compiler_params=_params("parallel", "parallel", "arbitrary"),
        name="dilated_attention",
    )(q, k, v)


def _retention_body(q_ref, k_ref, v_ref, cos_ref, sin_ref, qd_ref, kd_ref, intra_ref, cd_ref, s0_ref,
                    o_ref, sfin_ref, state):
    c = pl.program_id(1)

    @pl.when(c == 0)
    def _():
        state[...] = s0_ref[0]

    cos = cos_ref[...]
    sin = sin_ref[...]
    width = cos.shape[-1]
    half = B_DK // 2
    lane = _imod(lax.broadcasted_iota(jnp.int32, cos.shape, 1), B_DK)

    def rotary(x):
        fwd = pltpu.roll(x, half, 1)
        bwd = pltpu.roll(x, width - half, 1)
        return x * cos + jnp.where(lane < half, -bwd, fwd) * sin

    q = rotary(q_ref[0])
    k = rotary(k_ref[0]) * (B_DK ** -0.5)
    q_in = q.astype(BF16)
    k_in = k.astype(BF16)
    q_st = (q * qd_ref[...]).astype(BF16)
    k_st = (k * kd_ref[...]).astype(BF16)
    v = v_ref[0].astype(BF16)
    for h in range(B_HEADS):
        kc = slice(h * B_DK, (h + 1) * B_DK)
        vc = slice(h * B_DV, (h + 1) * B_DV)
        st = state[h]
        scores = _dot_nt(q_in[:, kc], k_in[:, kc]) * intra_ref[h]
        o_ref[0, :, vc] = _dot(scores.astype(BF16), v[:, vc]) + _dot(q_st[:, kc], st.astype(BF16))
        state[h] = st * cd_ref[h] + _dot_tn(k_st[:, kc], v[:, vc])

    @pl.when(c == pl.num_programs(1) - 1)
    def _():
        sfin_ref[0] = state[...]


def _retention(q, k, v, s0, pos, *, chunk, c_real):
    n, t, _ = q.shape
    half = B_DK // 2
    freqs = ROPE_BASE ** (-jnp.arange(half, dtype=F32) / half)
    ang = pos.astype(F32)[:, None] * freqs[None, :]
    cos = jnp.tile(jnp.cos(ang), (1, 2 * B_HEADS))
    sin = jnp.tile(jnp.sin(ang), (1, 2 * B_HEADS))
    log_g = jnp.log1p(-jnp.exp2(-5.0 - jnp.arange(B_HEADS, dtype=F32)))
    i = jnp.arange(chunk, dtype=F32)
    diff = i[:, None] - i[None, :]
    intra = jnp.where(diff >= 0, jnp.exp(log_g[:, None, None] * jnp.maximum(diff, 0.0)), 0.0)
    q_decay = jnp.repeat(jnp.exp(log_g[None, :] * (i[:, None] + 1.0)), B_DK, axis=1)
    k_decay = jnp.repeat(jnp.exp(log_g[None, :] * (c_real - 1.0 - i[:, None])), B_DK, axis=1)
    chunk_decay = jnp.broadcast_to(jnp.exp(log_g * c_real)[:, None, None], (B_HEADS, 1, B_DV))
    wqk = B_HEADS * B_DK
    return pl.pallas_call(
        _retention_body,
        grid=(n, t // chunk),
        in_specs=[pl.BlockSpec((1, chunk, wqk), lambda b, c: (b, c, 0)),
                  pl.BlockSpec((1, chunk, wqk), lambda b, c: (b, c, 0)),
                  pl.BlockSpec((1, chunk, B_WIDTH), lambda b, c: (b, c, 0)),
                  pl.BlockSpec((chunk, wqk), lambda b, c: (c, 0)),
                  pl.BlockSpec((chunk, wqk), lambda b, c: (c, 0)),
                  pl.BlockSpec((chunk, wqk), lambda b, c: (0, 0)),
                  pl.BlockSpec((chunk, wqk), lambda b, c: (0, 0)),
                  pl.BlockSpec((B_HEADS, chunk, chunk), lambda b, c: (0, 0, 0)),
                  pl.BlockSpec((B_HEADS, 1, B_DV), lambda b, c: (0, 0, 0)),
                  pl.BlockSpec((1, B_HEADS, B_DK, B_DV), lambda b, c: (b, 0, 0, 0))],
        out_specs=[pl.BlockSpec((1, chunk, B_WIDTH), lambda b, c: (b, c, 0)),
                   pl.BlockSpec((1, B_HEADS, B_DK, B_DV), lambda b, c: (b, 0, 0, 0))],
        out_shape=[jax.ShapeDtypeStruct((n, t, B_WIDTH), F32),
                   jax.ShapeDtypeStruct((n, B_HEADS, B_DK, B_DV), F32)],
        scratch_shapes=[pltpu.VMEM((B_HEADS, B_DK, B_DV), F32)],
        compiler_params=_params("parallel", "arbitrary"),
        name="retention",
    )(q, k, v, cos, sin, q_decay, k_decay, intra, chunk_decay, s0)


def _silu(z):
    return z * jax.nn.sigmoid(z)


def _even_out_body(x_ref, oa_ref, za_ref, ob_ref, zb_ref, w_ref, y_ref):
    ga = (oa_ref[...] * _silu(za_ref[...])).astype(BF16)
    acc = x_ref[...] + _dot(ga, w_ref[0:A_WIDTH, :])
    ob = ob_ref[...]
    zb = zb_ref[...]
    for h in range(B_HEADS):
        cols = slice(h * B_DV, (h + 1) * B_DV)
        seg = ob[:, cols]
        mu = jnp.mean(seg, axis=-1, keepdims=True)
        cen = seg - mu
        var = jnp.mean(cen * cen, axis=-1, keepdims=True)
        gb = (cen * lax.rsqrt(var + GN_EPS) * _silu(zb[:, cols])).astype(BF16)
        acc = acc + _dot(gb, w_ref[A_WIDTH + h * B_DV:A_WIDTH + (h + 1) * B_DV, :])
    y_ref[...] = acc


def _even_out(x2d, oa, za, ob, zb, w_out, tm):
    m, d = x2d.shape
    row = lambda width: pl.BlockSpec((tm, width), lambda i: (i, 0))
    return pl.pallas_call(
        _even_out_body,
        grid=(m // tm,),
        in_specs=[row(d), row(A_WIDTH), row(A_WIDTH), row(B_WIDTH), row(B_WIDTH),
                  pl.BlockSpec(w_out.shape, lambda i: (0, 0))],
        out_specs=row(d),
        out_shape=jax.ShapeDtypeStruct((m, d), F32),
        compiler_params=_params("parallel"),
        name="even_out",
    )(x2d, oa, za, ob, zb, w_out)


def _compress_math(x, p2_ref, wbig_ref, w2_ref):
    n_piece = x.shape[0]
    a = _dot(x.astype(BF16), wbig_ref[...])
    pc = _dot(p2_ref[...], wbig_ref[...])
    hid = []
    for g in range(C_KV_HEADS):
        lo = slice(g * 2 * CMP_HIDDEN, g * 2 * CMP_HIDDEN + CMP_HIDDEN)
        hi = slice(g * 2 * CMP_HIDDEN + CMP_HIDDEN, (g + 1) * 2 * CMP_HIDDEN)
        nxt = pltpu.roll(a[:, hi], n_piece - 1, 0)
        pos = (pc[0:1, lo] + pc[2:3, lo]) + (pc[1:2, hi] + pc[3:4, hi])
        hid.append(_silu(a[:, lo] + nxt + pos))
    return _dot(jnp.concatenate(hid, axis=-1).astype(BF16), w2_ref[...])


def _compress_body(x_ref, p2_ref, wbig_ref, w2_ref, o_ref):
    o_ref[0] = _compress_math(x_ref[0], p2_ref, wbig_ref, w2_ref)


def _compress_weights(pos_emb, w1, w2):
    per = CMP_LEN // CMP_STRIDE
    w1r = w1.reshape(per, CMP_STRIDE, HEAD_DIM, CMP_HIDDEN)
    eye = jnp.eye(C_KV_HEADS, dtype=w1.dtype)
    wbig = jnp.einsum('jldf,gh->lgdhjf', w1r, eye).reshape(CMP_STRIDE * C_KV_WIDTH, C_KV_HEADS * per * CMP_HIDDEN)
    w2bd = jnp.einsum('fd,gh->gfhd', w2, eye).reshape(C_KV_HEADS * CMP_HIDDEN, C_KV_WIDTH)
    halves = jnp.tile(pos_emb.reshape(per, CMP_STRIDE, 1, HEAD_DIM), (1, 1, C_KV_HEADS, 1)).reshape(per, -1)
    hi = halves.astype(BF16)
    lo = (halves - hi.astype(F32)).astype(BF16)
    p2 = jnp.concatenate([hi, lo, jnp.zeros((16 - 2 * per, halves.shape[1]), BF16)], axis=0)
    return p2, wbig.astype(BF16), w2bd.astype(BF16)


def _compress(pieces, cw):
    n, n_piece, width = pieces.shape
    p2, wbig, w2bd = cw
    const = lambda a: pl.BlockSpec(a.shape, lambda i: (0,) * a.ndim)
    return pl.pallas_call(
        _compress_body,
        grid=(n,),
        in_specs=[pl.BlockSpec((1, n_piece, width), lambda i: (i, 0, 0)), const(p2), const(wbig), const(w2bd)],
        out_specs=pl.BlockSpec((1, n_piece, C_KV_WIDTH), lambda i: (i, 0, 0)),
        out_shape=jax.ShapeDtypeStruct((n, n_piece, C_KV_WIDTH), F32),
        compiler_params=_params("parallel"),
        name="compress",
    )(pieces, p2, wbig, w2bd)


def _page_dma(pt_ref, pool_ref, buf_ref, sem_ref, n, slot, n_pages, start):
    for p in range(n_pages):
        page = pt_ref[n, p] if start else 0
        cp = pltpu.make_async_copy(pool_ref.at[page], buf_ref.at[slot, p], sem_ref.at[slot])
        if start:
            cp.start()
        else:
            cp.wait()


def _compress_paged_body(pt_ref, pool_ref, perm_ref, p2_ref, wbig_ref, w2_ref, o_ref, buf, pieces, sem, *, n_pages):
    n = pl.program_id(0)
    slot = n % 2
    rows = PAGE_SIZE // CMP_STRIDE

    @pl.when(n == 0)
    def _():
        _page_dma(pt_ref, pool_ref, buf, sem, 0, 0, n_pages, True)

    @pl.when(n + 1 < pl.num_programs(0))
    def _():
        _page_dma(pt_ref, pool_ref, buf, sem, n + 1, 1 - slot, n_pages, True)

    _page_dma(pt_ref, pool_ref, buf, sem, n, slot, n_pages, False)
    perm = perm_ref[...]

    group = 8
    assert n_pages % group == 0

    def unfold(i, carry):
        by_row = [_dot_nt(perm, buf[slot, i * group + j].astype(BF16)) for j in range(group)]
        for j in range(group):
            r0 = pl.multiple_of((i * group + j) * rows, rows)
            for l in range(CMP_STRIDE):
                pieces[pl.ds(r0, rows), l * C_KV_WIDTH:(l + 1) * C_KV_WIDTH] = by_row[j][l * rows:(l + 1) * rows, :]
        return carry

    lax.fori_loop(0, n_pages // group, unfold, 0)
    o_ref[0] = _compress_math(pieces[...], p2_ref, wbig_ref, w2_ref)


def _compress_paged(pool, page_table, cw):
    n, n_pages = page_table.shape
    rows = PAGE_SIZE // CMP_STRIDE
    n_piece = n_pages * rows
    p2, wbig, w2bd = cw
    perm = np.zeros((PAGE_SIZE, PAGE_SIZE), np.float32)
    for l in range(CMP_STRIDE):
        for piece in range(rows):
            perm[l * rows + piece, CMP_STRIDE * piece + l] = 1.0
    perm = jnp.asarray(perm, BF16)
    pages = jnp.transpose(pool, (0, 2, 3, 1)).reshape(pool.shape[0], C_KV_WIDTH, PAGE_SIZE)
    const = lambda a: pl.BlockSpec(a.shape, lambda i, pt: (0,) * a.ndim)
    return pl.pallas_call(
        functools.partial(_compress_paged_body, n_pages=n_pages),
        grid_spec=pltpu.PrefetchScalarGridSpec(
            num_scalar_prefetch=1,
            grid=(n,),
            in_specs=[pl.BlockSpec(memory_space=pl.ANY), const(perm), const(p2), const(wbig), const(w2bd)],
            out_specs=pl.BlockSpec((1, n_piece, C_KV_WIDTH), lambda i, pt: (i, 0, 0)),
            scratch_shapes=[pltpu.VMEM((2, n_pages, C_KV_WIDTH, PAGE_SIZE), F32),
                            pltpu.VMEM((n_piece, CMP_STRIDE * C_KV_WIDTH), F32),
                            pltpu.SemaphoreType.DMA((2,))]),
        out_shape=jax.ShapeDtypeStruct((n, n_piece, C_KV_WIDTH), F32),
        compiler_params=_params("arbitrary"),
        name="compress_paged",
    )(page_table, pages, perm, p2, wbig, w2bd)


def _overlap_weights(n_cmp, n_cmp_pad, n_slc, n_slc_pad):
    i = np.arange(n_cmp_pad)[:, None]
    j = np.arange(n_slc_pad)[None, :]
    ov = (np.minimum(i * CMP_STRIDE + CMP_LEN, (j + 1) * SEL_BLOCK) - np.maximum(i * CMP_STRIDE, j * SEL_BLOCK))
    w = np.where((i < n_cmp) & (j < n_slc), np.clip(ov, 0, None) / CMP_LEN, 0.0)
    return jnp.asarray(w, dtype=BF16)


def _select_blocks(p_slc, qpos, n_slc):
    shape = p_slc.shape
    lane = lax.broadcasted_iota(jnp.int32, shape, 1)
    cur = _idiv(qpos, SEL_BLOCK)
    forced = (lane == 0) | (lane == cur) | (lane == cur - 1)
    causal = lane * SEL_BLOCK <= qpos
    score = jnp.where(forced, SEL_FORCE, jnp.where(causal, p_slc, -SEL_FORCE))
    score = jnp.where(lane < n_slc, score, -3.0 * SEL_FORCE)
    lane_f = lane.astype(F32)
    sel = jnp.zeros(shape, F32)
    for _ in range(min(SEL_TOPK, n_slc)):
        m = jnp.max(score, axis=-1, keepdims=True)
        first = jnp.min(jnp.where(score == m, lane_f, 1e9), axis=-1, keepdims=True)
        hit = lane_f == first
        sel = jnp.where(hit, jnp.where(m > -0.5 * SEL_FORCE, 1.0, 0.0), sel)
        score = jnp.where(hit, -4.0 * SEL_FORCE, score)
    return sel


def _select_blocks_t(p_slc, qpos, n_slc):
    n_rows = -(-n_slc // 8) * 8
    shape = (n_rows, p_slc.shape[1])
    blk = lax.broadcasted_iota(jnp.int32, shape, 0)
    cur = _idiv(qpos, SEL_BLOCK)
    forced = (blk == 0) | (blk == cur) | (blk == cur - 1)
    causal = blk * SEL_BLOCK <= qpos
    score = jnp.where(forced, SEL_FORCE, jnp.where(causal, p_slc[:n_rows], -SEL_FORCE))
    score = jnp.where(blk < n_slc, score, -3.0 * SEL_FORCE)
    tiles = [score[8 * v:8 * v + 8] for v in range(n_rows // 8)]
    sub = lax.broadcasted_iota(jnp.int32, (8, shape[1]), 0)
    ranks = [jnp.zeros((8, shape[1]), F32) for _ in tiles]
    for i in range(n_slc):
        row = tiles[i // 8][i % 8:i % 8 + 1]
        for v, tile in enumerate(tiles):
            ge = jnp.where(row >= tile, 1.0, 0.0)
            gt = jnp.where(row > tile, 1.0, 0.0)
            if 8 * v > i:
                ranks[v] = ranks[v] + ge
            elif 8 * v + 7 <= i:
                ranks[v] = ranks[v] + gt
            else:
                ranks[v] = ranks[v] + jnp.where(sub > i - 8 * v, ge, gt)
    rank = jnp.concatenate(ranks, axis=0)
    return jnp.where(rank < min(SEL_TOPK, n_slc), jnp.where(score > -0.5 * SEL_FORCE, 1.0, 0.0), 0.0)


NSA_TILE = 2 * LANES
ONES_ROWS = 16
LOG2E = 1.4426950408889634


def _nsa_prompt_body(q_ref, kc_ref, vct_ref, ks_ref, vst_ref, kw_ref, vwt_ref, g_ref, wovt_ref, o_ref, m_sc, acc_sc,
                     *, n_cmp, n_slc):
    tq = tk = NSA_TILE
    rep = C_HEADS // C_KV_HEADS
    blk = pl.program_id(1)
    q0 = blk * tq
    n_cmp_pad = kc_ref.shape[1]
    sig_t = jax.nn.sigmoid(g_ref[0])
    q_t = [jnp.transpose(q_ref[0, :, c * LANES:(c + 1) * LANES].astype(F32))
           for c in range(C_WIDTH // LANES)]
    qpos = q0 + lax.broadcasted_iota(jnp.int32, (1, tq), 1)
    cmp_i = lax.broadcasted_iota(jnp.int32, (n_cmp_pad, tq), 0)
    cmp_ok = (cmp_i * CMP_STRIDE + (CMP_LEN - 1) <= qpos) & (cmp_i < n_cmp)
    cmp_bias = jnp.where(cmp_ok, 0.0, NEG_INF)
    cmp_keep = jnp.where(cmp_ok, 1.0, 0.0)
    ahead = lax.broadcasted_iota(jnp.int32, (tk, tq), 0) - lax.broadcasted_iota(jnp.int32, (tk, tq), 1)
    diag_bias = jnp.where(ahead <= 0, 0.0, NEG_INF)
    far_bias = jnp.where(ahead >= 0, 0.0, NEG_INF)
    ones_rows = jnp.where(lax.broadcasted_iota(jnp.int32, (ONES_ROWS, tk), 0) == 0, 1.0, 0.0).astype(BF16)

    def flash_update(scores, v_t):
        v_ext = jnp.concatenate([v_t, ones_rows], axis=0)
        for r, s in enumerate(scores):
            m_old = m_sc[r:r + 1, :]
            m_new = jnp.maximum(m_old, jnp.max(s, axis=0, keepdims=True))
            m_sc[r:r + 1, :] = m_new
            pv = _dot(v_ext, jnp.exp2(s - m_new).astype(BF16))
            acc_sc[r] = jnp.exp2(m_old - m_new) * acc_sc[r] + pv

    def flash_reset():
        m_sc[...] = jnp.full(m_sc.shape, NEG_INF, F32)
        acc_sc[...] = jnp.zeros(acc_sc.shape, F32)

    def flash_result(r):
        acc = acc_sc[r]
        return acc[:HEAD_DIM] / acc[HEAD_DIM:HEAD_DIM + 1]

    outs = []
    for g in range(C_KV_HEADS):
        gc = slice(g * HEAD_DIM, (g + 1) * HEAD_DIM)
        heads = [g * rep + r for r in range(rep)]
        q_heads = [q_t[h // 2][(h % 2) * HEAD_DIM:(h % 2 + 1) * HEAD_DIM] for h in heads]
        q_nat = [(qh * (HEAD_DIM ** -0.5)).astype(BF16) for qh in q_heads]
        q_log2 = [(qh * (HEAD_DIM ** -0.5 * LOG2E)).astype(BF16) for qh in q_heads]
        kcb = kc_ref[0, :, gc].astype(BF16)
        vct = vct_ref[0, gc, :].astype(BF16)
        scores = [_dot(kcb, q_nat[r]) + cmp_bias for r in range(rep)]
        probs = []
        for s in scores:
            p = jnp.exp(s - jnp.max(s, axis=0, keepdims=True)) * cmp_keep
            l = jnp.sum(p, axis=0, keepdims=True)
            probs.append(p / jnp.where(l > 0, l, 1.0))
        o_cmp = [_dot(vct, p.astype(BF16)) for p in probs]
        imp = probs[0]
        for p in probs[1:]:
            imp = imp + p
        hi = imp.astype(BF16)
        r1 = imp - hi.astype(F32)
        mid = r1.astype(BF16)
        lo = (r1 - mid.astype(F32)).astype(BF16)
        p_slc = _dot(wovt_ref[...], hi) + _dot(wovt_ref[...], mid) + _dot(wovt_ref[...], lo)
        sel_bias = jnp.where(_select_blocks_t(p_slc, qpos, n_slc) > 0.5, 0.0, NEG_INF).astype(BF16)
        q_sel = [jnp.concatenate([qh, sel_bias], axis=0) for qh in q_log2]
        flash_reset()

        def sel_tile(k0, diagonal):
            k0 = pl.multiple_of(k0, tk)
            k_tile = ks_ref[0, pl.ds(k0, tk), g * LANES:(g + 1) * LANES]
            scores = [_dot(k_tile, qs) for qs in q_sel]
            if diagonal:
                scores = [s + diag_bias for s in scores]
            flash_update(scores, vst_ref[0, gc, pl.ds(k0, tk)])

        def full_tile(kt, carry):
            sel_tile(kt * tk, False)
            return carry

        lax.fori_loop(0, blk, full_tile, 0)
        sel_tile(q0, True)
        o_sel = [flash_result(r) for r in range(rep)]
        flash_reset()

        def win_tile(k0, bias):
            k0 = pl.multiple_of(k0, tk)
            k_tile = kw_ref[0, pl.ds(k0, tk), gc]
            scores = [_dot(k_tile, qh) for qh in q_log2]
            if bias is not None:
                scores = [s + bias for s in scores]
            flash_update(scores, vwt_ref[0, gc, pl.ds(k0, tk)])

        n_back = C_WINDOW // tk
        for back in range(n_back, 0, -1):
            @pl.when(blk >= back)
            def _():
                win_tile(q0 - back * tk, far_bias if back == n_back else None)

        win_tile(q0, diag_bias)
        for r, h in enumerate(heads):
            outs.append(sig_t[3 * h:3 * h + 1] * o_cmp[r] + sig_t[3 * h + 1:3 * h + 2] * o_sel[r]
                        + sig_t[3 * h + 2:3 * h + 3] * flash_result(r))
    for c in range(C_WIDTH // LANES):
        o_ref[0, :, c * LANES:(c + 1) * LANES] = jnp.transpose(jnp.concatenate(outs[2 * c:2 * c + 2], axis=0))


def _nsa_prompt(q, kcb, vcb, ks, vs_t, kw, vw_t, gates_t, n_cmp):
    b, t, wq = q.shape
    n_cmp_pad = kcb.shape[1]
    n_slc = -(-t // SEL_BLOCK)
    tile = NSA_TILE
    assert t % tile == 0 and n_slc == HEAD_DIM and C_WINDOW % tile == 0
    wov_t = _overlap_weights(n_cmp, n_cmp_pad, n_slc, LANES).T
    vcb_t = jnp.swapaxes(vcb, 1, 2)
    onehot = jnp.asarray(np.arange(t)[:, None] // SEL_BLOCK == np.arange(HEAD_DIM)[None, :], BF16)
    ks_ext = jnp.concatenate([ks.reshape(b, t, C_KV_HEADS, HEAD_DIM),
                              jnp.broadcast_to(onehot[None, :, None, :], (b, t, C_KV_HEADS, HEAD_DIM))],
                             axis=-1).reshape(b, t, C_KV_HEADS * LANES)
    rep = C_HEADS // C_KV_HEADS
    whole = lambda a: pl.BlockSpec((1,) + a.shape[1:], lambda bi, i: (bi, 0, 0))
    return pl.pallas_call(
        functools.partial(_nsa_prompt_body, n_cmp=n_cmp, n_slc=n_slc),
        grid=(b, t // tile),
        in_specs=[pl.BlockSpec((1, tile, wq), lambda bi, i: (bi, i, 0)),
                  whole(kcb), whole(vcb_t), whole(ks_ext), whole(vs_t), whole(kw), whole(vw_t),
                  pl.BlockSpec((1, gates_t.shape[1], tile), lambda bi, i: (bi, 0, i)),
                  pl.BlockSpec(wov_t.shape, lambda bi, i: (0, 0))],
        out_specs=pl.BlockSpec((1, tile, wq), lambda bi, i: (bi, i, 0)),
        out_shape=jax.ShapeDtypeStruct((b, t, wq), F32),
        scratch_shapes=[pltpu.VMEM((rep, tile), F32), pltpu.VMEM((rep, HEAD_DIM + ONES_ROWS, tile), F32)],
        compiler_params=_params("parallel", "arbitrary"),
        name="nsa_prompt",
    )(q, kcb, vcb_t, ks_ext, vs_t, kw, vw_t, gates_t, wov_t)


def _block_diag(q, blk_of_row, n_blk):
    return jnp.concatenate([jnp.where(blk_of_row == b, q, 0.0) for b in range(n_blk)], axis=-1)


def _diag_blocks(r, blk_of_row, n_blk):
    out = jnp.where(blk_of_row == 0, r[:, 0:HEAD_DIM], 0.0)
    for b in range(1, n_blk):
        out = out + jnp.where(blk_of_row == b, r[:, b * HEAD_DIM:(b + 1) * HEAD_DIM], 0.0)
    return out


def _row_ids(n_rows, n_heads, n_blk):
    row = lax.broadcasted_iota(jnp.int32, (n_rows, 1), 0)
    step = _idiv(row, n_heads)
    blk = _idiv(_imod(row, n_heads), n_heads // n_blk)
    return step, blk


def _pattern_weight(dist, patterns):
    w = jnp.zeros(dist.shape, F32)
    for window, dil in patterns:
        w = w + jnp.where(_imod(dist, dil) == 0, jnp.where(dist <= window, 1.0, 0.0), 0.0)
    return jnp.where(dist >= 0, w, 0.0)


def _decode_body(*refs, n_heads, n_blk, patterns, gate_col):
    q_ref, kc_ref, vc_ref, kn_ref, vn_ref = refs[:5]
    g_ref = refs[5] if gate_col is not None else None
    o_ref = refs[-1]
    n_rows = q_ref.shape[1]
    n_cache = kc_ref.shape[2]
    n_new = kn_ref.shape[1]
    step, blk = _row_ids(n_rows, n_heads, n_blk)
    qbd = _block_diag(q_ref[0] * (HEAD_DIM ** -0.5), blk, n_blk)
    s_c = _dot(qbd.astype(BF16), kc_ref[0].astype(BF16))
    w_c = _pattern_weight(n_cache + step - lax.broadcasted_iota(jnp.int32, (n_rows, n_cache), 1), patterns)
    s_c = jnp.where(w_c > 0, s_c, NEG_INF)
    kn = kn_ref[0]
    vn = vn_ref[0]
    s_n, w_n = [], []
    for c in range(n_new):
        s = jnp.sum(qbd * kn[c:c + 1, :], axis=-1, keepdims=True)
        w = _pattern_weight(step - c, patterns)
        s_n.append(jnp.where(w > 0, s, NEG_INF))
        w_n.append(w)
    m = jnp.max(s_c, axis=-1, keepdims=True)
    for s in s_n:
        m = jnp.maximum(m, s)
    p_c = w_c * jnp.exp(s_c - m)
    l = jnp.sum(p_c, axis=-1, keepdims=True)
    r = _dot_nt(p_c.astype(BF16), vc_ref[0].astype(BF16))
    for c in range(n_new):
        p = w_n[c] * jnp.exp(s_n[c] - m)
        l = l + p
        r = r + p * vn[c:c + 1, :]
    o = _diag_blocks(r, blk, n_blk) / l
    if g_ref is not None:
        o = o * jax.nn.sigmoid(g_ref[0])[:, gate_col:gate_col + 1]
    o_ref[0] = o


def _decode_attention(q, kc, vc, kn, vn, *, n_heads, n_blk, patterns, gate=None, gate_col=None):
    n, n_rows, _ = q.shape
    args = [q, kc, vc, kn, vn]
    if gate is not None:
        args.append(gate)
    whole = lambda a: pl.BlockSpec((1,) + a.shape[1:], lambda i: (i, 0, 0))
    return pl.pallas_call(
        functools.partial(_decode_body, n_heads=n_heads, n_blk=n_blk, patterns=patterns,
                          gate_col=gate_col if gate is not None else None),
        grid=(n,),
        in_specs=[whole(a) for a in args],
        out_specs=pl.BlockSpec((1, n_rows, HEAD_DIM), lambda i: (i, 0, 0)),
        out_shape=jax.ShapeDtypeStruct((n, n_rows, HEAD_DIM), F32),
        compiler_params=_params("parallel"),
        name="decode_attention",
    )(*args)


def _nsa_decode_body(pt_ref, q_ref, kc_ref, vc_ref, ksp_ref, vsp_ref, kn_ref, vn_ref, g_ref, wov_ref, o_ref,
                     kbuf, vbuf, expand, ksem, vsem, *, n_pages, n_cmp, n_slc):
    n = pl.program_id(0)
    slot = n % 2
    n_rows = q_ref.shape[1]
    n_new = kn_ref.shape[1]
    n_past = kbuf.shape[2]
    n_slc_pad = wov_ref.shape[1]
    rep = C_HEADS // C_KV_HEADS

    def gather(seq, dst_slot, start):
        for pool_ref, buf, sem in ((ksp_ref, kbuf, ksem), (vsp_ref, vbuf, vsem)):
            for p in range(n_pages):
                page = pt_ref[seq, p] if start else 0
                cp = pltpu.make_async_copy(pool_ref.at[page], buf.at[dst_slot, :, pl.ds(p * PAGE_SIZE, PAGE_SIZE)],
                                           sem.at[dst_slot])
                if start:
                    cp.start()
                else:
                    cp.wait()

    @pl.when(n == 0)
    def _():
        gather(0, 0, True)
        step_cols = 1024
        for c0 in range(0, n_past, step_cols):
            shape = (n_slc_pad, min(step_cols, n_past - c0))
            blk_of_key = _idiv(c0 + lax.broadcasted_iota(jnp.int32, shape, 1), SEL_BLOCK)
            hit = blk_of_key == lax.broadcasted_iota(jnp.int32, shape, 0)
            expand[:, c0:c0 + shape[1]] = jnp.where(hit, 1.0, 0.0).astype(BF16)

    @pl.when(n + 1 < pl.num_programs(0))
    def _():
        gather(n + 1, 1 - slot, True)

    step, blk = _row_ids(n_rows, C_HEADS, C_KV_HEADS)
    qpos = n_past + step
    qbd = _block_diag(q_ref[0] * (HEAD_DIM ** -0.5), blk, C_KV_HEADS)
    qbd_in = qbd.astype(BF16)
    n_cmp_pad = kc_ref.shape[1]
    cmp_i = lax.broadcasted_iota(jnp.int32, (n_rows, n_cmp_pad), 1)
    cmp_ok = (cmp_i * CMP_STRIDE + (CMP_LEN - 1) <= qpos) & (cmp_i < n_cmp)
    s = jnp.where(cmp_ok, _dot_nt(qbd_in, kc_ref[0].astype(BF16)), NEG_INF)
    m = jnp.max(s, axis=-1, keepdims=True)
    p = jnp.where(cmp_ok, jnp.exp(s - m), 0.0)
    l = jnp.sum(p, axis=-1, keepdims=True)
    p = p / jnp.where(l > 0, l, 1.0)
    o_cmp = _diag_blocks(_dot(p.astype(BF16), vc_ref[0].astype(BF16)), blk, C_KV_HEADS)
    grp = (n_rows // rep, rep, n_cmp_pad)
    imp = jnp.broadcast_to(jnp.sum(p.reshape(grp), axis=1, keepdims=True), grp).reshape(n_rows, n_cmp_pad)
    sel_rows = _select_blocks(_split3_dot(imp, wov_ref[...]), qpos, n_slc)
    gather(n, slot, False)
    picked = _dot(sel_rows.astype(BF16), expand[...])
    kpos = lax.broadcasted_iota(jnp.int32, (n_rows, n_past), 1)
    s_p = jnp.where(picked > 0.5, jnp.where(kpos <= qpos, _dot(qbd_in, kbuf[slot].astype(BF16)), NEG_INF), NEG_INF)
    kn = kn_ref[0]
    vn = vn_ref[0]
    s_n = []
    for c in range(n_new):
        blk_c = (n_past + c) // SEL_BLOCK
        ok = (sel_rows[:, blk_c:blk_c + 1] > 0.5) & (n_past + c <= qpos)
        s_n.append(jnp.where(ok, jnp.sum(qbd * kn[c:c + 1, :], axis=-1, keepdims=True), NEG_INF))
    m = jnp.max(s_p, axis=-1, keepdims=True)
    for sc in s_n:
        m = jnp.maximum(m, sc)
    p_p = jnp.exp(s_p - m)
    l = jnp.sum(p_p, axis=-1, keepdims=True)
    r = _dot_nt(p_p.astype(BF16), vbuf[slot].astype(BF16))
    for c in range(n_new):
        pn = jnp.exp(s_n[c] - m)
        l = l + pn
        r = r + pn * vn[c:c + 1, :]
    o_sel = _diag_blocks(r, blk, C_KV_HEADS) / l
    sig = jax.nn.sigmoid(g_ref[0])
    o_ref[0] = sig[:, 0:1] * o_cmp + sig[:, 1:2] * o_sel


def _nsa_decode(q, kcb, vcb, pool_ks, pool_vs, page_table, kn, vn, gate, n_cmp):
    n, n_rows, _ = q.shape
    n_pages = page_table.shape[1]
    n_past = n_pages * PAGE_SIZE
    n_new = kn.shape[1]
    n_slc = -(-(n_past + n_new) // SEL_BLOCK)
    n_slc_pad = -(-n_slc // LANES) * LANES
    wov = _overlap_weights(n_cmp, kcb.shape[1], n_slc, n_slc_pad)
    whole = lambda a: pl.BlockSpec((1,) + a.shape[1:], lambda i, pt: (i, 0, 0))
    pools = [jnp.transpose(a, (0, 2, 3, 1)).reshape(a.shape[0], C_KV_WIDTH, PAGE_SIZE) for a in (pool_ks, pool_vs)]
    return pl.pallas_call(
        functools.partial(_nsa_decode_body, n_pages=n_pages, n_cmp=n_cmp, n_slc=n_slc),
        grid_spec=pltpu.PrefetchScalarGridSpec(
            num_scalar_prefetch=1,
            grid=(n,),
            in_specs=[whole(q), whole(kcb), whole(vcb), pl.BlockSpec(memory_space=pl.ANY),
                      pl.BlockSpec(memory_space=pl.ANY), whole(kn), whole(vn), whole(gate),
                      pl.BlockSpec(wov.shape, lambda i, pt: (0, 0))],
            out_specs=pl.BlockSpec((1, n_rows, HEAD_DIM), lambda i, pt: (i, 0, 0)),
            scratch_shapes=[pltpu.VMEM((2, C_KV_WIDTH, n_past), F32), pltpu.VMEM((2, C_KV_WIDTH, n_past), F32),
                            pltpu.VMEM((n_slc_pad, n_past), BF16),
                            pltpu.SemaphoreType.DMA((2,)), pltpu.SemaphoreType.DMA((2,))]),
        out_shape=jax.ShapeDtypeStruct((n, n_rows, HEAD_DIM), F32),
        compiler_params=_params("arbitrary"),
        name="nsa_decode",
    )(page_table, q, kcb, vcb, *pools, kn, vn, gate, wov)


def _odd_out_body(x_ref, *refs):
    *branch_refs, z_ref, w_ref, gf_ref, y_ref = refs
    o = branch_refs[0][...]
    for ref in branch_refs[1:]:
        o = o + ref[...]
    mixed = (o * _silu(z_ref[...])).astype(BF16)
    h = x_ref[...] + _dot(mixed, w_ref[...])
    y_ref[...] = h * lax.rsqrt(jnp.mean(h * h, axis=-1, keepdims=True) + RMS_EPS) * gf_ref[...]


def _odd_out(x2d, branches, z, w_out, final_gain, tm):
    m, d = x2d.shape
    row = pl.BlockSpec((tm, d), lambda i: (i, 0))
    return pl.pallas_call(
        _odd_out_body,
        grid=(m // tm,),
        in_specs=[row] * (len(branches) + 2) + [pl.BlockSpec(w_out.shape, lambda i: (0, 0)),
                                                 pl.BlockSpec((1, d), lambda i: (0, 0))],
        out_specs=row,
        out_shape=jax.ShapeDtypeStruct((m, d), F32),
        compiler_params=_params("parallel"),
        name="odd_out",
    )(x2d, *branches, z, w_out, final_gain.reshape(1, d))


def _even_outs(prompt):
    a, bk = A_WIDTH, B_HEADS * B_DK
    outs = [(0, a, F32, False), (a, a, F32, False), (2 * a, a, F32, False)]
    if prompt:
        outs += [(a, a, F32, True), (2 * a, a, F32, True)]
    outs += [(3 * a, a, F32, False),
             (4 * a, bk, F32, False), (4 * a + bk, bk, F32, False),
             (4 * a + 2 * bk, B_WIDTH, F32, False), (4 * a + 2 * bk + B_WIDTH, B_WIDTH, F32, False)]
    return outs


def _from_feature_major(u, t0, n_heads):
    b, _, t = u.shape
    return jnp.transpose(u[:, :, t0:].reshape(b, n_heads, HEAD_DIM, t - t0), (0, 3, 1, 2))


def _to_feature_major(u):
    n, length, n_heads, hd = u.shape
    return jnp.transpose(u, (0, 2, 3, 1)).reshape(n, n_heads * hd, length)


def _even_layer_prompt(x, gain, w_in, w_out):
    b, t, d = x.shape
    x2d = x.reshape(b * t, d)
    qa, ka, va, ka_t, va_t, za, qb, kb, vb, zb = _norm_proj(x2d, gain, w_in, _even_outs(True), 512, t)
    seq = lambda u: u.reshape(b, t, u.shape[-1])
    oa = _dilated_attention(seq(qa), seq(ka), seq(va), A_PATTERNS)
    chunk = B_CHUNK if t % B_CHUNK == 0 else t
    s0 = jnp.zeros((b, B_HEADS, B_DK, B_DV), F32)
    ob, s_fin = _retention(seq(qb), seq(kb), seq(vb), s0, jnp.arange(t), chunk=chunk, c_real=chunk)
    y = _even_out(x2d, oa.reshape(b * t, A_WIDTH), za, ob.reshape(b * t, B_WIDTH), zb, w_out, 256)
    keep = min(A_PATTERNS[-1][0], t)
    return (y.reshape(b, t, d), _from_feature_major(ka_t, t - keep, A_HEADS),
            _from_feature_major(va_t, t - keep, A_HEADS), s_fin)


def _even_layer_sample(x, cache_k, cache_v, state, gain, w_in, w_out, past_len):
    n, s_len, d = x.shape
    x2d = x.reshape(n * s_len, d)
    qa, ka32, va32, za, qb, kb, vb, zb = _norm_proj(x2d, gain, w_in, _even_outs(False), n * s_len, n * s_len)
    seq = lambda u: u.reshape(n, s_len, u.shape[-1])
    oa = _decode_attention(qa.reshape(n, s_len * A_HEADS, HEAD_DIM), _to_feature_major(cache_k),
                           _to_feature_major(cache_v), seq(ka32), seq(va32), n_heads=A_HEADS, n_blk=A_HEADS,
                           patterns=A_PATTERNS)
    oa = oa.reshape(n * s_len, A_WIDTH)
    chunk = 16
    pad = lambda u: jnp.pad(seq(u), ((0, 0), (0, chunk - s_len), (0, 0)))
    pos = past_len + jnp.arange(chunk)
    ob, s_new = _retention(pad(qb), pad(kb), pad(vb), state, pos, chunk=chunk, c_real=s_len)
    ob = ob[:, :s_len].reshape(n * s_len, B_WIDTH)
    y = _even_out(x2d, oa, za, ob, zb, w_out, n * s_len)
    heads = lambda u: u.reshape(n, s_len, A_HEADS, HEAD_DIM)
    return y.reshape(n, s_len, d), heads(ka32), heads(va32), s_new


ODD_KV = ("kc", "vc", "ks", "vs", "kw", "vw")
ODD_GATE_COL = C_WIDTH + len(ODD_KV) * C_KV_WIDTH
ODD_Z_COL = ODD_GATE_COL + 3 * C_HEADS


def _odd_outs_prompt():
    outs = [(0, C_WIDTH, BF16, False)]
    for i, name in enumerate(ODD_KV):
        off = C_WIDTH + i * C_KV_WIDTH
        outs.append((off, C_KV_WIDTH, BF16, name in ("vs", "vw")))
        outs.append((off, C_KV_WIDTH, F32, True))
    outs += [(ODD_GATE_COL, 3 * C_HEADS, F32, True), (ODD_Z_COL, C_WIDTH, F32, False)]
    return outs


def _odd_outs_sample():
    outs = [(0, C_WIDTH, BF16, False)]
    outs += [(C_WIDTH + i * C_KV_WIDTH, C_KV_WIDTH, F32, False) for i in range(len(ODD_KV))]
    outs += [(ODD_GATE_COL, 3 * C_HEADS, F32, False), (ODD_Z_COL, C_WIDTH, F32, False)]
    return outs


def _odd_layer_prompt(x, gain, w_in, w_out, cw_k, cw_v, final_gain):
    b, t, d = x.shape
    x2d = x.reshape(b * t, d)
    (q, kc, kc_t, vc, vc_t, ks, ks_t, vs_tb, vs_t, kw, kw_t, vw_tb, vw_t, gl_t, z) = _norm_proj(
        x2d, gain, w_in, _odd_outs_prompt(), 512, t)
    seq = lambda u: u.reshape(b, t, u.shape[-1])
    n_cmp = (t - CMP_LEN) // CMP_STRIDE + 1
    pieces = lambda u: u.reshape(b, t // CMP_STRIDE, CMP_STRIDE * C_KV_WIDTH)
    kcb = _compress(pieces(kc), cw_k)
    vcb = _compress(pieces(vc), cw_v)
    o = _nsa_prompt(seq(q), kcb, vcb, seq(ks), vs_tb, seq(kw), vw_tb, gl_t, n_cmp)
    y = _odd_out(x2d, [o.reshape(b * t, C_WIDTH)], z, w_out, final_gain, 256)
    keep = min(C_WINDOW, t)
    kv = lambda u, t0: _from_feature_major(u, t0, C_KV_HEADS)
    return (y.reshape(b, t, d), kv(kc_t, 0), kv(vc_t, 0), kv(ks_t, 0), kv(vs_t, 0), kv(kw_t, t - keep),
            kv(vw_t, t - keep))


def _odd_layer_sample(x, page_table, pool_kc, pool_vc, pool_ks, pool_vs, buf_kw, buf_vw, gain, w_in, w_out,
                      cw_k, cw_v, final_gain):
    n, s_len, d = x.shape
    x2d = x.reshape(n * s_len, d)
    q, kc, vc, ks32, vs32, kw32, vw32, gl, z = _norm_proj(x2d, gain, w_in, _odd_outs_sample(), n * s_len, n * s_len)
    seq = lambda u: u.reshape(n, s_len, u.shape[-1])
    n_past = page_table.shape[1] * PAGE_SIZE
    n_cmp = (n_past + s_len - CMP_LEN) // CMP_STRIDE + 1
    assert CMP_STRIDE * (n_cmp - 1) + CMP_LEN <= n_past
    kcb = _compress_paged(pool_kc, page_table, cw_k)
    vcb = _compress_paged(pool_vc, page_table, cw_v)
    rows = lambda u: u.astype(F32).reshape(n, s_len * C_HEADS, HEAD_DIM)
    gate = gl.reshape(n, s_len * C_HEADS, 3)
    o_cs = _nsa_decode(rows(q), kcb, vcb, pool_ks, pool_vs, page_table, seq(ks32), seq(vs32), gate, n_cmp)
    o_win = _decode_attention(rows(q), _to_feature_major(buf_kw), _to_feature_major(buf_vw),
                              seq(kw32), seq(vw32), n_heads=C_HEADS, n_blk=C_KV_HEADS,
                              patterns=((C_WINDOW, 1),), gate=gate, gate_col=2)
    y = _odd_out(x2d, [o_cs.reshape(n * s_len, C_WIDTH), o_win.reshape(n * s_len, C_WIDTH)], z, w_out, final_gain,
                 n * s_len)
    kv = lambda u: u.reshape(n, s_len, C_KV_HEADS, HEAD_DIM)
    return y.reshape(n, s_len, d), kv(kc), kv(vc), kv(ks32), kv(vs32), kv(kw32), kv(vw32)


def kernel(x_prompt, x_sample, cache_a_k, cache_a_v, state_ret, cache_c_kcmp, cache_c_vcmp, cache_c_ksel,
           cache_c_vsel, cache_c_kwin, cache_c_vwin, page_table, even_norm, even_w_in, even_w_out, odd_norm,
           odd_w_in, odd_w_out, cmp_pos_k, cmp_w1_k, cmp_w2_k, cmp_pos_v, cmp_w1_v, cmp_w2_v, final_norm):
    assert even_norm.shape[0] == 1 and odd_norm.shape[0] == 1
    past_len = page_table.shape[1] * PAGE_SIZE
    w_in_e = even_w_in[0].T.astype(BF16)
    w_out_e = even_w_out[0].astype(BF16)
    w_in_o = odd_w_in[0].T.astype(BF16)
    w_out_o = odd_w_out[0].astype(BF16)
    cw_k = _compress_weights(cmp_pos_k[0], cmp_w1_k[0], cmp_w2_k[0])
    cw_v = _compress_weights(cmp_pos_v[0], cmp_w1_v[0], cmp_w2_v[0])

    hp, ak_p, av_p, ret_p = _even_layer_prompt(x_prompt, even_norm[0], w_in_e, w_out_e)
    hs, ak_s, av_s, ret_s = _even_layer_sample(x_sample, cache_a_k[0], cache_a_v[0], state_ret[0], even_norm[0],
                                               w_in_e, w_out_e, past_len)
    yp, *rows_p = _odd_layer_prompt(hp, odd_norm[0], w_in_o, w_out_o, cw_k, cw_v, final_norm)
    ys, *rows_s = _odd_layer_sample(hs, page_table, cache_c_kcmp[0], cache_c_vcmp[0], cache_c_ksel[0],
                                    cache_c_vsel[0], cache_c_kwin[0], cache_c_vwin[0], odd_norm[0], w_in_o, w_out_o,
                                    cw_k, cw_v, final_norm)
    lead = lambda u: u[None]
    out = [yp, ys, lead(ak_p), lead(ak_s), lead(av_p), lead(av_s), lead(ret_p), lead(ret_s)]
    for rp, rs in zip(rows_p, rows_s):
        out += [lead(rp), lead(rs)]
    return tuple(out)
```

```python
import functools

import numpy as np
import jax
import jax.numpy as jnp
from jax import lax
from jax.experimental import pallas as pl
from jax.experimental.pallas import tpu as pltpu

F32 = jnp.float32
BF16 = jnp.bfloat16

HEAD_DIM = 64
A_HEADS = 8
A_WIDTH = A_HEADS * HEAD_DIM
A_PATTERNS = ((128, 1), (512, 4), (2048, 16))
B_HEADS = 4
B_DK = 64
B_DV = 128
B_WIDTH = B_HEADS * B_DV
B_CHUNK = 128
ROPE_BASE = 10000.0
GN_EPS = 1e-5
C_HEADS = 16
C_KV_HEADS = 2
C_WIDTH = C_HEADS * HEAD_DIM
C_KV_WIDTH = C_KV_HEADS * HEAD_DIM
CMP_LEN = 32
CMP_STRIDE = 16
CMP_HIDDEN = 256
SEL_BLOCK = 64
SEL_TOPK = 16
SEL_FORCE = 1e9
C_WINDOW = 512
PAGE_SIZE = 128
RMS_EPS = 1e-6
NEG_INF = -1e30

LANES = 128
ATTN_TQ = 128
VMEM_LIMIT = 56 * 1024 * 1024


def _params(*sem):
    return pltpu.CompilerParams(dimension_semantics=sem, vmem_limit_bytes=VMEM_LIMIT)


def _dot(a, b):
    return jnp.dot(a, b, preferred_element_type=F32)


def _dot_nt(a, b):
    return lax.dot_general(a, b, (((1,), (1,)), ((), ())), preferred_element_type=F32)


def _dot_tn(a, b):
    return lax.dot_general(a, b, (((0,), (0,)), ((), ())), preferred_element_type=F32)


def _idiv(x, n):
    assert n & (n - 1) == 0
    return jnp.right_shift(x, n.bit_length() - 1)


def _imod(x, n):
    assert n & (n - 1) == 0
    return jnp.bitwise_and(x, n - 1)


def _split3_dot(a, w):
    hi = a.astype(BF16)
    r1 = a - hi.astype(F32)
    mid = r1.astype(BF16)
    lo = (r1 - mid.astype(F32)).astype(BF16)
    return _dot(hi, w) + _dot(mid, w) + _dot(lo, w)


def _norm_proj_body(x_ref, g_ref, wt_ref, *o_refs, outs):
    x = x_ref[...]
    y = x * lax.rsqrt(jnp.mean(x * x, axis=-1, keepdims=True) + RMS_EPS) * g_ref[...]
    yb = y.astype(BF16)
    done = {}
    for o_ref, (off, width, _, feature_major) in zip(o_refs, outs):
        key = (off, width, feature_major)
        if key not in done:
            w_rows = wt_ref[off:off + width, :]
            done[key] = _dot_nt(w_rows, yb) if feature_major else _dot_nt(yb, w_rows)
        if feature_major:
            o_ref[0] = done[key].astype(o_ref.dtype)
        else:
            o_ref[...] = done[key].astype(o_ref.dtype)


def _norm_proj(x2d, gain, w_t, outs, tm, seq_len):
    m, d = x2d.shape
    n = w_t.shape[0]
    tiles = seq_len // tm
    assert seq_len % tm == 0 and m % seq_len == 0
    out_specs, out_shape = [], []
    for _, width, dt, feature_major in outs:
        if feature_major:
            out_specs.append(pl.BlockSpec((1, width, tm), lambda i: (i // tiles, 0, i % tiles)))
            out_shape.append(jax.ShapeDtypeStruct((m // seq_len, width, seq_len), dt))
        else:
            out_specs.append(pl.BlockSpec((tm, width), lambda i: (i, 0)))
            out_shape.append(jax.ShapeDtypeStruct((m, width), dt))
    return pl.pallas_call(
        functools.partial(_norm_proj_body, outs=outs),
        grid=(m // tm,),
        in_specs=[pl.BlockSpec((tm, d), lambda i: (i, 0)),
                  pl.BlockSpec((1, d), lambda i: (0, 0)),
                  pl.BlockSpec((n, d), lambda i: (0, 0))],
        out_specs=out_specs,
        out_shape=out_shape,
        compiler_params=_params("parallel"),
        name="norm_proj",
    )(x2d, gain.reshape(1, d), w_t)


DIL_GROUP = 4


def _dilated_body(q_ref, k_ref, v_ref, o_ref, o_sc, lse_sc, *, patterns):
    tq = ATTN_TQ
    blk_len = q_ref.shape[1]
    base = pl.program_id(2) * blk_len
    head_cols = [slice(h * HEAD_DIM, (h + 1) * HEAD_DIM) for h in range(LANES // HEAD_DIM)]

    def rows(ref, start, size, dil):
        return ref[0, pl.ds(start, size) if dil == 1 else pl.ds(start, size, stride=dil), :]

    for p_idx, (window, dil) in enumerate(patterns):
        lookback = window // dil
        n_prev = -(-lookback // tq)
        span = (n_prev + 1) * tq
        n_u = blk_len // (tq * dil)
        n_sub_blocks = dil * n_u
        assert blk_len % (tq * dil) == 0 and n_sub_blocks % DIL_GROUP == 0

        def trip(i, carry, dil=dil, lookback=lookback, n_prev=n_prev, span=span, p_idx=p_idx):
            work = []
            for j in range(DIL_GROUP):
                idx = i * DIL_GROUP + j
                res = _imod(idx, dil)
                u = _idiv(idx, dil)
                s0 = base // dil + u * tq
                ks = jnp.maximum(s0 - n_prev * tq, 0)
                q_row = res + dil * (u * tq)
                k_row = res + dil * ks
                q2 = (rows(q_ref, q_row, tq, dil) * (HEAD_DIM ** -0.5)).astype(BF16)
                k2 = rows(k_ref, k_row, span, dil).astype(BF16)
                v2 = rows(v_ref, k_row, span, dil).astype(BF16)
                dist = (s0 + lax.broadcasted_iota(jnp.int32, (tq, span), 0)
                        - ks - lax.broadcasted_iota(jnp.int32, (tq, span), 1))
                bias = jnp.where(dist >= 0, jnp.where(dist <= lookback, 0.0, NEG_INF), NEG_INF)
                work.append((q_row, q2, k2, v2, bias))
            scores = [[_dot_nt(q2[:, c], k2[:, c]) + bias for c in head_cols] for _, q2, k2, _, bias in work]
            maxes = [[jnp.max(s, axis=-1, keepdims=True) for s in ss] for ss in scores]
            probs = [[jnp.exp(s - m) for s, m in zip(ss, mm)] for ss, mm in zip(scores, maxes)]
            sums = [[jnp.sum(p, axis=-1, keepdims=True) for p in pp] for pp in probs]
            outs = [[_dot(p.astype(BF16), w[3][:, c]) for p, c in zip(pp, head_cols)] for pp, w in zip(probs, work)]
            for w, oo, mm, ll in zip(work, outs, maxes, sums):
                o2 = jnp.concatenate([o / l for o, l in zip(oo, ll)], axis=-1)
                lse2 = jnp.concatenate([jnp.broadcast_to(m + jnp.log(l), (tq, HEAD_DIM)) for m, l in zip(mm, ll)],
                                       axis=-1)
                idx = pl.ds(w[0], tq) if dil == 1 else pl.ds(w[0], tq, stride=dil)
                o_sc[p_idx, idx, :] = o2
                lse_sc[p_idx, idx, :] = lse2
            return carry

        lax.fori_loop(0, n_sub_blocks // DIL_GROUP, trip, 0)

    lses = [lse_sc[p] for p in range(len(patterns))]
    top = lses[0]
    for l in lses[1:]:
        top = jnp.maximum(top, l)
    weights = [jnp.exp(l - top) for l in lses]
    num = weights[0] * o_sc[0]
    den = weights[0]
    for p in range(1, len(patterns)):
        num = num + weights[p] * o_sc[p]
        den = den + weights[p]
    o_ref[0] = num / den


def _dilated_attention(q, k, v, patterns):
    b, t, width = q.shape
    blk_len = min(t, max(ATTN_TQ * dil for _, dil in patterns))
    assert t % blk_len == 0
    for window, dil in patterns:
        assert t // dil >= (-(-(window // dil) // ATTN_TQ) + 1) * ATTN_TQ
    return pl.pallas_call(
        functools.partial(_dilated_body, patterns=patterns),
        grid=(b, width // LANES, t // blk_len),
        in_specs=[pl.BlockSpec((1, blk_len, LANES), lambda bi, hp, i: (bi, i, hp)),
                  pl.BlockSpec((1, t, LANES), lambda bi, hp, i: (bi, 0, hp)),
                  pl.BlockSpec((1, t, LANES), lambda bi, hp, i: (bi, 0, hp))],
        out_specs=pl.BlockSpec((1, blk_len, LANES), lambda bi, hp, i: (bi, i, hp)),
        out_shape=jax.ShapeDtypeStruct((b, t, width), F32),
        scratch_shapes=[pltpu.VMEM((len(patterns), blk_len, LANES), F32),
                        pltpu.VMEM((len(patterns), blk_len, LANES), F32)],
        compiler_params=_params("parallel", "parallel", "arbitrary"),
        name="dilated_attention",
    )(q, k, v)


def _retention_body(q_ref, k_ref, v_ref, cos_ref, sin_ref, qd_ref, kd_ref, intra_ref, cd_ref, s0_ref,
                    o_ref, sfin_ref, state):
    c = pl.program_id(1)

    @pl.when(c == 0)
    def _():
        state[...] = s0_ref[0]

    cos = cos_ref[...]
    sin = sin_ref[...]
    width = cos.shape[-1]
    half = B_DK // 2
    lane = _imod(lax.broadcasted_iota(jnp.int32, cos.shape, 1), B_DK)

    def rotary(x):
        fwd = pltpu.roll(x, half, 1)
        bwd = pltpu.roll(x, width - half, 1)
        return x * cos + jnp.where(lane < half, -bwd, fwd) * sin

    q = rotary(q_ref[0])
    k = rotary(k_ref[0]) * (B_DK ** -0.5)
    q_in = q.astype(BF16)
    k_in = k.astype(BF16)
    q_st = (q * qd_ref[...]).astype(BF16)
    k_st = (k * kd_ref[...]).astype(BF16)
    v = v_ref[0].astype(BF16)
    kcs = [slice(h * B_DK, (h + 1) * B_DK) for h in range(B_HEADS)]
    vcs = [slice(h * B_DV, (h + 1) * B_DV) for h in range(B_HEADS)]
    states = [state[h] for h in range(B_HEADS)]
    scores = [_dot_nt(q_in[:, kc], k_in[:, kc]) * intra_ref[h] for h, kc in enumerate(kcs)]
    carried = [_dot(q_st[:, kc], st.astype(BF16)) for kc, st in zip(kcs, states)]
    updates = [_dot_tn(k_st[:, kc], v[:, vc]) for kc, vc in zip(kcs, vcs)]
    inner = [_dot(s.astype(BF16), v[:, vc]) for s, vc in zip(scores, vcs)]
    for h in range(B_HEADS):
        o_ref[0, :, vcs[h]] = inner[h] + carried[h]
        state[h] = states[h] * cd_ref[h] + updates[h]

    @pl.when(c == pl.num_programs(1) - 1)
    def _():
        sfin_ref[0] = state[...]


def _retention(q, k, v, s0, pos, *, chunk, c_real):
    n, t, _ = q.shape
    half = B_DK // 2
    freqs = ROPE_BASE ** (-jnp.arange(half, dtype=F32) / half)
    ang = pos.astype(F32)[:, None] * freqs[None, :]
    cos = jnp.tile(jnp.cos(ang), (1, 2 * B_HEADS))
    sin = jnp.tile(jnp.sin(ang), (1, 2 * B_HEADS))
    log_g = jnp.log1p(-jnp.exp2(-5.0 - jnp.arange(B_HEADS, dtype=F32)))
    i = jnp.arange(chunk, dtype=F32)
    diff = i[:, None] - i[None, :]
    intra = jnp.where(diff >= 0, jnp.exp(log_g[:, None, None] * jnp.maximum(diff, 0.0)), 0.0)
    q_decay = jnp.repeat(jnp.exp(log_g[None, :] * (i[:, None] + 1.0)), B_DK, axis=1)
    k_decay = jnp.repeat(jnp.exp(log_g[None, :] * (c_real - 1.0 - i[:, None])), B_DK, axis=1)
    chunk_decay = jnp.broadcast_to(jnp.exp(log_g * c_real)[:, None, None], (B_HEADS, 1, B_DV))
    wqk = B_HEADS * B_DK
    return pl.pallas_call(
        _retention_body,
        grid=(n, t // chunk),
        in_specs=[pl.BlockSpec((1, chunk, wqk), lambda b, c: (b, c, 0)),
                  pl.BlockSpec((1, chunk, wqk), lambda b, c: (b, c, 0)),
                  pl.BlockSpec((1, chunk, B_WIDTH), lambda b, c: (b, c, 0)),
                  pl.BlockSpec((chunk, wqk), lambda b, c: (c, 0)),
                  pl.BlockSpec((chunk, wqk), lambda b, c: (c, 0)),
                  pl.BlockSpec((chunk, wqk), lambda b, c: (0, 0)),
                  pl.BlockSpec((chunk, wqk), lambda b, c: (0, 0)),
                  pl.BlockSpec((B_HEADS, chunk, chunk), lambda b, c: (0, 0, 0)),
                  pl.BlockSpec((B_HEADS, 1, B_DV), lambda b, c: (0, 0, 0)),
                  pl.BlockSpec((1, B_HEADS, B_DK, B_DV), lambda b, c: (b, 0, 0, 0))],
        out_specs=[pl.BlockSpec((1, chunk, B_WIDTH), lambda b, c: (b, c, 0)),
                   pl.BlockSpec((1, B_HEADS, B_DK, B_DV), lambda b, c: (b, 0, 0, 0))],
        out_shape=[jax.ShapeDtypeStruct((n, t, B_WIDTH), F32),
                   jax.ShapeDtypeStruct((n, B_HEADS, B_DK, B_DV), F32)],
        scratch_shapes=[pltpu.VMEM((B_HEADS, B_DK, B_DV), F32)],
        compiler_params=_params("parallel", "arbitrary"),
        name="retention",
    )(q, k, v, cos, sin, q_decay, k_decay, intra, chunk_decay, s0)


def _silu(z):
    return z * jax.nn.sigmoid(z)


def _even_out_body(x_ref, oa_ref, za_ref, ob_ref, zb_ref, w_ref, y_ref):
    ga = (oa_ref[...] * _silu(za_ref[...])).astype(BF16)
    acc = x_ref[...] + _dot(ga, w_ref[0:A_WIDTH, :])
    ob = ob_ref[...]
    zb = zb_ref[...]
    for h in range(B_HEADS):
        cols = slice(h * B_DV, (h + 1) * B_DV)
        seg = ob[:, cols]
        mu = jnp.mean(seg, axis=-1, keepdims=True)
        cen = seg - mu
        var = jnp.mean(cen * cen, axis=-1, keepdims=True)
        gb = (cen * lax.rsqrt(var + GN_EPS) * _silu(zb[:, cols])).astype(BF16)
        acc = acc + _dot(gb, w_ref[A_WIDTH + h * B_DV:A_WIDTH + (h + 1) * B_DV, :])
    y_ref[...] = acc


def _even_out(x2d, oa, za, ob, zb, w_out, tm):
    m, d = x2d.shape
    row = lambda width: pl.BlockSpec((tm, width), lambda i: (i, 0))
    return pl.pallas_call(
        _even_out_body,
        grid=(m // tm,),
        in_specs=[row(d), row(A_WIDTH), row(A_WIDTH), row(B_WIDTH), row(B_WIDTH),
                  pl.BlockSpec(w_out.shape, lambda i: (0, 0))],
        out_specs=row(d),
        out_shape=jax.ShapeDtypeStruct((m, d), F32),
        compiler_params=_params("parallel"),
        name="even_out",
    )(x2d, oa, za, ob, zb, w_out)


def _compress_math(x, p2_ref, wbig_ref, w2_ref):
    n_piece = x.shape[0]
    a = _dot(x.astype(BF16), wbig_ref[...])
    pc = _dot(p2_ref[...], wbig_ref[...])
    hid = []
    for g in range(C_KV_HEADS):
        lo = slice(g * 2 * CMP_HIDDEN, g * 2 * CMP_HIDDEN + CMP_HIDDEN)
        hi = slice(g * 2 * CMP_HIDDEN + CMP_HIDDEN, (g + 1) * 2 * CMP_HIDDEN)
        nxt = pltpu.roll(a[:, hi], n_piece - 1, 0)
        pos = (pc[0:1, lo] + pc[2:3, lo]) + (pc[1:2, hi] + pc[3:4, hi])
        hid.append(_silu(a[:, lo] + nxt + pos))
    return _dot(jnp.concatenate(hid, axis=-1).astype(BF16), w2_ref[...])


def _compress_body(x_ref, p2_ref, wbig_ref, w2_ref, o_ref):
    o_ref[0] = _compress_math(x_ref[0], p2_ref, wbig_ref, w2_ref)


def _compress_weights(pos_emb, w1, w2):
    per = CMP_LEN // CMP_STRIDE
    w1r = w1.reshape(per, CMP_STRIDE, HEAD_DIM, CMP_HIDDEN)
    eye = jnp.eye(C_KV_HEADS, dtype=w1.dtype)
    wbig = jnp.einsum('jldf,gh->lgdhjf', w1r, eye).reshape(CMP_STRIDE * C_KV_WIDTH, C_KV_HEADS * per * CMP_HIDDEN)
    w2bd = jnp.einsum('fd,gh->gfhd', w2, eye).reshape(C_KV_HEADS * CMP_HIDDEN, C_KV_WIDTH)
    halves = jnp.tile(pos_emb.reshape(per, CMP_STRIDE, 1, HEAD_DIM), (1, 1, C_KV_HEADS, 1)).reshape(per, -1)
    hi = halves.astype(BF16)
    lo = (halves - hi.astype(F32)).astype(BF16)
    p2 = jnp.concatenate([hi, lo, jnp.zeros((16 - 2 * per, halves.shape[1]), BF16)], axis=0)
    w1g = jnp.transpose(w1r, (1, 2, 0, 3)).reshape(CMP_STRIDE * HEAD_DIM, per * CMP_HIDDEN)
    halves_g = pos_emb.reshape(per, CMP_STRIDE * HEAD_DIM)
    hi_g = halves_g.astype(BF16)
    lo_g = (halves_g - hi_g.astype(F32)).astype(BF16)
    p2g = jnp.concatenate([hi_g, lo_g, jnp.zeros((16 - 2 * per, halves_g.shape[1]), BF16)], axis=0)
    return p2, wbig.astype(BF16), w2bd.astype(BF16), p2g, w1g.astype(BF16)


def _compress(pieces, cw):
    n, n_piece, width = pieces.shape
    p2, wbig, w2bd = cw[:3]
    const = lambda a: pl.BlockSpec(a.shape, lambda i: (0,) * a.ndim)
    return pl.pallas_call(
        _compress_body,
        grid=(n,),
        in_specs=[pl.BlockSpec((1, n_piece, width), lambda i: (i, 0, 0)), const(p2), const(wbig), const(w2bd)],
        out_specs=pl.BlockSpec((1, n_piece, C_KV_WIDTH), lambda i: (i, 0, 0)),
        out_shape=jax.ShapeDtypeStruct((n, n_piece, C_KV_WIDTH), F32),
        compiler_params=_params("parallel"),
        name="compress",
    )(pieces, p2, wbig, w2bd)


def _page_dma(pt_ref, pool_ref, buf_ref, sem_ref, n, slot, n_pages, start):
    for p in range(n_pages):
        page = pt_ref[n, p] if start else 0
        cp = pltpu.make_async_copy(pool_ref.at[page], buf_ref.at[slot, p], sem_ref.at[slot])
        if start:
            cp.start()
        else:
            cp.wait()


def _compress_paged_body(pt_ref, pool_ref, perm_ref, p2_ref, w1_ref, w2_ref, o_ref, buf, pieces, sem, *, n_pages):
    n = pl.program_id(0)
    slot = n % 2
    rows = PAGE_SIZE // CMP_STRIDE
    n_piece = n_pages * rows

    @pl.when(n == 0)
    def _():
        _page_dma(pt_ref, pool_ref, buf, sem, 0, 0, n_pages, True)

    @pl.when(n + 1 < pl.num_programs(0))
    def _():
        _page_dma(pt_ref, pool_ref, buf, sem, n + 1, 1 - slot, n_pages, True)

    _page_dma(pt_ref, pool_ref, buf, sem, n, slot, n_pages, False)
    perm = perm_ref[...]

    group = 8
    assert n_pages % group == 0 and CMP_STRIDE % 2 == 0
    low_half = lax.broadcasted_iota(jnp.int32, (rows, LANES), 1) < HEAD_DIM

    def unfold(i, carry):
        pairs = []
        for j in range(0, group, 2):
            two = buf[slot, pl.ds(i * group + j, 2)].reshape(2 * C_KV_WIDTH, PAGE_SIZE)
            pairs.append(_dot_nt(perm, two.astype(BF16)))
        for j in range(group):
            by_row = pairs[j // 2][:, (j % 2) * C_KV_WIDTH:(j % 2 + 1) * C_KV_WIDTH]
            r0 = pl.multiple_of((i * group + j) * rows, rows)
            for l in range(0, CMP_STRIDE, 2):
                even = by_row[l * rows:(l + 1) * rows, :]
                odd = by_row[(l + 1) * rows:(l + 2) * rows, :]
                lanes = slice(l * HEAD_DIM, (l + 2) * HEAD_DIM)
                pieces[0, pl.ds(r0, rows), lanes] = jnp.where(low_half, even, pltpu.roll(odd, HEAD_DIM, 1))
                pieces[1, pl.ds(r0, rows), lanes] = jnp.where(low_half, pltpu.roll(even, HEAD_DIM, 1), odd)
        return carry

    lax.fori_loop(0, n_pages // group, unfold, 0)
    hidden = CMP_HIDDEN
    pc = _dot(p2_ref[...], w1_ref[...])
    pos = (pc[0:1, :hidden] + pc[2:3, :hidden]) + (pc[1:2, hidden:] + pc[3:4, hidden:])
    acts = [_dot(pieces[g].astype(BF16), w1_ref[...]) for g in range(C_KV_HEADS)]
    hid = [_silu(a[:, :hidden] + pltpu.roll(a[:, hidden:], n_piece - 1, 0) + pos) for a in acts]
    o_ref[0] = _dot(jnp.concatenate(hid, axis=-1).astype(BF16), w2_ref[...])


def _compress_paged(pool, page_table, cw):
    n, n_pages = page_table.shape
    rows = PAGE_SIZE // CMP_STRIDE
    n_piece = n_pages * rows
    _, _, w2bd, p2g, w1g = cw
    perm = np.zeros((PAGE_SIZE, PAGE_SIZE), np.float32)
    for l in range(CMP_STRIDE):
        for piece in range(rows):
            perm[l * rows + piece, CMP_STRIDE * piece + l] = 1.0
    perm = jnp.asarray(perm, BF16)
    pages = jnp.transpose(pool, (0, 2, 3, 1)).reshape(pool.shape[0], C_KV_WIDTH, PAGE_SIZE)
    const = lambda a: pl.BlockSpec(a.shape, lambda i, pt: (0,) * a.ndim)
    return pl.pallas_call(
        functools.partial(_compress_paged_body, n_pages=n_pages),
        grid_spec=pltpu.PrefetchScalarGridSpec(
            num_scalar_prefetch=1,
            grid=(n,),
            in_specs=[pl.BlockSpec(memory_space=pl.ANY), const(perm), const(p2g), const(w1g), const(w2bd)],
            out_specs=pl.BlockSpec((1, n_piece, C_KV_WIDTH), lambda i, pt: (i, 0, 0)),
            scratch_shapes=[pltpu.VMEM((2, n_pages, C_KV_WIDTH, PAGE_SIZE), F32),
                            pltpu.VMEM((C_KV_HEADS, n_piece, CMP_STRIDE * HEAD_DIM), F32),
                            pltpu.SemaphoreType.DMA((2,))]),
        out_shape=jax.ShapeDtypeStruct((n, n_piece, C_KV_WIDTH), F32),
        compiler_params=_params("arbitrary"),
        name="compress_paged",
    )(page_table, pages, perm, p2g, w1g, w2bd)


def _overlap_weights(n_cmp, n_cmp_pad, n_slc, n_slc_pad):
    i = np.arange(n_cmp_pad)[:, None]
    j = np.arange(n_slc_pad)[None, :]
    ov = (np.minimum(i * CMP_STRIDE + CMP_LEN, (j + 1) * SEL_BLOCK) - np.maximum(i * CMP_STRIDE, j * SEL_BLOCK))
    w = np.where((i < n_cmp) & (j < n_slc), np.clip(ov, 0, None) / CMP_LEN, 0.0)
    return jnp.asarray(w, dtype=BF16)


def _select_blocks(p_slc, qpos, n_slc):
    shape = p_slc.shape
    lane = lax.broadcasted_iota(jnp.int32, shape, 1)
    cur = _idiv(qpos, SEL_BLOCK)
    forced = (lane == 0) | (lane == cur) | (lane == cur - 1)
    causal = lane * SEL_BLOCK <= qpos
    score = jnp.where(forced, SEL_FORCE, jnp.where(causal, p_slc, -SEL_FORCE))
    score = jnp.where(lane < n_slc, score, -3.0 * SEL_FORCE)
    lane_f = lane.astype(F32)
    sel = jnp.zeros(shape, F32)
    for _ in range(min(SEL_TOPK, n_slc)):
        m = jnp.max(score, axis=-1, keepdims=True)
        first = jnp.min(jnp.where(score == m, lane_f, 1e9), axis=-1, keepdims=True)
        hit = lane_f == first
        sel = jnp.where(hit, jnp.where(m > -0.5 * SEL_FORCE, 1.0, 0.0), sel)
        score = jnp.where(hit, -4.0 * SEL_FORCE, score)
    return sel


def _select_blocks_t(p_slc, qpos, n_slc):
    n_rows = -(-n_slc // 8) * 8
    shape = (n_rows, p_slc.shape[1])
    blk = lax.broadcasted_iota(jnp.int32, shape, 0)
    cur = _idiv(qpos, SEL_BLOCK)
    forced = (blk == 0) | (blk == cur) | (blk == cur - 1)
    causal = blk * SEL_BLOCK <= qpos
    score = jnp.where(forced, SEL_FORCE, jnp.where(causal, p_slc[:n_rows], -SEL_FORCE))
    score = jnp.where(blk < n_slc, score, -3.0 * SEL_FORCE)
    tiles = [score[8 * v:8 * v + 8] for v in range(n_rows // 8)]
    sub = lax.broadcasted_iota(jnp.int32, (8, shape[1]), 0)
    ranks = [jnp.zeros((8, shape[1]), F32) for _ in tiles]
    for i in range(n_slc):
        row = tiles[i // 8][i % 8:i % 8 + 1]
        for v, tile in enumerate(tiles):
            ge = jnp.where(row >= tile, 1.0, 0.0)
            gt = jnp.where(row > tile, 1.0, 0.0)
            if 8 * v > i:
                ranks[v] = ranks[v] + ge
            elif 8 * v + 7 <= i:
                ranks[v] = ranks[v] + gt
            else:
                ranks[v] = ranks[v] + jnp.where(sub > i - 8 * v, ge, gt)
    rank = jnp.concatenate(ranks, axis=0)
    return jnp.where(rank < min(SEL_TOPK, n_slc), jnp.where(score > -0.5 * SEL_FORCE, 1.0, 0.0), 0.0)


NSA_TILE = 2 * LANES
ONES_ROWS = 16
LOG2E = 1.4426950408889634


def _nsa_prompt_body(q_ref, kc_ref, vct_ref, ks_ref, vst_ref, kw_ref, vwt_ref, g_ref, wovt_ref, o_ref, m_sc, acc_sc,
                     *, n_cmp, n_slc):
    tq = tk = NSA_TILE
    rep = C_HEADS // C_KV_HEADS
    blk = pl.program_id(1)
    q0 = blk * tq
    n_cmp_pad = kc_ref.shape[1]
    sig_t = jax.nn.sigmoid(g_ref[0])
    q_t = [jnp.transpose(q_ref[0, :, c * LANES:(c + 1) * LANES].astype(F32))
           for c in range(C_WIDTH // LANES)]
    qpos = q0 + lax.broadcasted_iota(jnp.int32, (1, tq), 1)
    cmp_i = lax.broadcasted_iota(jnp.int32, (n_cmp_pad, tq), 0)
    cmp_ok = (cmp_i * CMP_STRIDE + (CMP_LEN - 1) <= qpos) & (cmp_i < n_cmp)
    cmp_bias = jnp.where(cmp_ok, 0.0, NEG_INF)
    cmp_keep = jnp.where(cmp_ok, 1.0, 0.0)
    ahead = lax.broadcasted_iota(jnp.int32, (tk, tq), 0) - lax.broadcasted_iota(jnp.int32, (tk, tq), 1)
    diag_bias = jnp.where(ahead <= 0, 0.0, NEG_INF)
    far_bias = jnp.where(ahead >= 0, 0.0, NEG_INF)
    ones_rows = jnp.where(lax.broadcasted_iota(jnp.int32, (ONES_ROWS, tk), 0) == 0, 1.0, 0.0).astype(BF16)

    def flash_update(scores, v_t):
        v_ext = jnp.concatenate([v_t, ones_rows], axis=0)
        for r, s in enumerate(scores):
            m_old = m_sc[r:r + 1, :]
            m_new = jnp.maximum(m_old, jnp.max(s, axis=0, keepdims=True))
            m_sc[r:r + 1, :] = m_new
            pv = _dot(v_ext, jnp.exp2(s - m_new).astype(BF16))
            acc_sc[r] = jnp.exp2(m_old - m_new) * acc_sc[r] + pv

    def flash_reset():
        m_sc[...] = jnp.full(m_sc.shape, NEG_INF, F32)
        acc_sc[...] = jnp.zeros(acc_sc.shape, F32)

    def flash_result(r):
        acc = acc_sc[r]
        return acc[:HEAD_DIM] / acc[HEAD_DIM:HEAD_DIM + 1]

    outs = []
    for g in range(C_KV_HEADS):
        gc = slice(g * HEAD_DIM, (g + 1) * HEAD_DIM)
        heads = [g * rep + r for r in range(rep)]
        q_heads = [q_t[h // 2][(h % 2) * HEAD_DIM:(h % 2 + 1) * HEAD_DIM] for h in heads]
        q_nat = [(qh * (HEAD_DIM ** -0.5)).astype(BF16) for qh in q_heads]
        q_log2 = [(qh * (HEAD_DIM ** -0.5 * LOG2E)).astype(BF16) for qh in q_heads]
        kcb = kc_ref[0, :, gc].astype(BF16)
        vct = vct_ref[0, gc, :].astype(BF16)
        scores = [_dot(kcb, q_nat[r]) + cmp_bias for r in range(rep)]
        probs = []
        for s in scores:
            p = jnp.exp(s - jnp.max(s, axis=0, keepdims=True)) * cmp_keep
            l = jnp.sum(p, axis=0, keepdims=True)
            probs.append(p / jnp.where(l > 0, l, 1.0))
        o_cmp = [_dot(vct, p.astype(BF16)) for p in probs]
        imp = probs[0]
        for p in probs[1:]:
            imp = imp + p
        hi = imp.astype(BF16)
        r1 = imp - hi.astype(F32)
        mid = r1.astype(BF16)
        lo = (r1 - mid.astype(F32)).astype(BF16)
        p_slc = _dot(wovt_ref[...], hi) + _dot(wovt_ref[...], mid) + _dot(wovt_ref[...], lo)
        sel_bias = jnp.where(_select_blocks_t(p_slc, qpos, n_slc) > 0.5, 0.0, NEG_INF).astype(BF16)
        q_sel = [jnp.concatenate([qh, sel_bias], axis=0) for qh in q_log2]
        flash_reset()

        def sel_scores(k0):
            k_tile = ks_ref[0, pl.ds(pl.multiple_of(k0, tk), tk), g * LANES:(g + 1) * LANES]
            return [_dot(k_tile, qs) for qs in q_sel]

        def sel_values(k0):
            return vst_ref[0, gc, pl.ds(pl.multiple_of(k0, tk), tk)]

        def past_pair(i, carry):
            k0 = i * (2 * tk)
            first, second = sel_scores(k0), sel_scores(k0 + tk)
            flash_update(first, sel_values(k0))
            flash_update(second, sel_values(k0 + tk))
            return carry

        lax.fori_loop(0, blk // 2, past_pair, 0)
        k0 = (blk // 2) * (2 * tk)
        first_on_diag = jnp.where(k0 == q0, 1.0, 0.0)
        first = [s + diag_bias * first_on_diag for s in sel_scores(k0)]
        second = [s + (diag_bias * (1.0 - first_on_diag) + NEG_INF * first_on_diag) for s in sel_scores(k0 + tk)]
        flash_update(first, sel_values(k0))
        flash_update(second, sel_values(k0 + tk))
        o_sel = [flash_result(r) for r in range(rep)]
        flash_reset()
        n_back = C_WINDOW // tk
        win = []
        for back in range(n_back + 1):
            k0 = pl.multiple_of(jnp.maximum(q0 - back * tk, 0), tk)
            k_tile = kw_ref[0, pl.ds(k0, tk), gc]
            absent = jnp.where(blk >= back, 0.0, NEG_INF)
            edge = diag_bias if back == 0 else far_bias if back == n_back else None
            bias = absent if edge is None else edge + absent
            win.append(([_dot(k_tile, qh) + bias for qh in q_log2], vwt_ref[0, gc, pl.ds(k0, tk)]))
        for scores, values in win:
            flash_update(scores, values)
        for r, h in enumerate(heads):
            outs.append(sig_t[3 * h:3 * h + 1] * o_cmp[r] + sig_t[3 * h + 1:3 * h + 2] * o_sel[r]
                        + sig_t[3 * h + 2:3 * h + 3] * flash_result(r))
    for c in range(C_WIDTH // LANES):
        o_ref[0, :, c * LANES:(c + 1) * LANES] = jnp.transpose(jnp.concatenate(outs[2 * c:2 * c + 2], axis=0))


def _nsa_prompt(q, kcb, vcb, ks, vs_t, kw, vw_t, gates_t, n_cmp):
    b, t, wq = q.shape
    n_cmp_pad = kcb.shape[1]
    n_slc = -(-t // SEL_BLOCK)
    tile = NSA_TILE
    assert t % (2 * tile) == 0 and n_slc == HEAD_DIM and C_WINDOW % tile == 0
    wov_t = _overlap_weights(n_cmp, n_cmp_pad, n_slc, LANES).T
    vcb_t = jnp.swapaxes(vcb, 1, 2)
    onehot = jnp.asarray(np.arange(t)[:, None] // SEL_BLOCK == np.arange(HEAD_DIM)[None, :], BF16)
    ks_ext = jnp.concatenate([ks.reshape(b, t, C_KV_HEADS, HEAD_DIM),
                              jnp.broadcast_to(onehot[None, :, None, :], (b, t, C_KV_HEADS, HEAD_DIM))],
                             axis=-1).reshape(b, t, C_KV_HEADS * LANES)
    rep = C_HEADS // C_KV_HEADS
    whole = lambda a: pl.BlockSpec((1,) + a.shape[1:], lambda bi, i: (bi, 0, 0))
    return pl.pallas_call(
        functools.partial(_nsa_prompt_body, n_cmp=n_cmp, n_slc=n_slc),
        grid=(b, t // tile),
        in_specs=[pl.BlockSpec((1, tile, wq), lambda bi, i: (bi, i, 0)),
                  whole(kcb), whole(vcb_t), whole(ks_ext), whole(vs_t), whole(kw), whole(vw_t),
                  pl.BlockSpec((1, gates_t.shape[1], tile), lambda bi, i: (bi, 0, i)),
                  pl.BlockSpec(wov_t.shape, lambda bi, i: (0, 0))],
        out_specs=pl.BlockSpec((1, tile, wq), lambda bi, i: (bi, i, 0)),
        out_shape=jax.ShapeDtypeStruct((b, t, wq), F32),
        scratch_shapes=[pltpu.VMEM((rep, tile), F32), pltpu.VMEM((rep, HEAD_DIM + ONES_ROWS, tile), F32)],
        compiler_params=_params("parallel", "arbitrary"),
        name="nsa_prompt",
    )(q, kcb, vcb_t, ks_ext, vs_t, kw, vw_t, gates_t, wov_t)


def _block_diag(q, blk_of_row, n_blk):
    return jnp.concatenate([jnp.where(blk_of_row == b, q, 0.0) for b in range(n_blk)], axis=-1)


def _diag_blocks(r, blk_of_row, n_blk):
    out = jnp.where(blk_of_row == 0, r[:, 0:HEAD_DIM], 0.0)
    for b in range(1, n_blk):
        out = out + jnp.where(blk_of_row == b, r[:, b * HEAD_DIM:(b + 1) * HEAD_DIM], 0.0)
    return out


def _row_ids(n_rows, n_heads, n_blk):
    row = lax.broadcasted_iota(jnp.int32, (n_rows, 1), 0)
    step = _idiv(row, n_heads)
    blk = _idiv(_imod(row, n_heads), n_heads // n_blk)
    return step, blk


def _pattern_weight(dist, patterns):
    w = jnp.zeros(dist.shape, F32)
    for window, dil in patterns:
        w = w + jnp.where(_imod(dist, dil) == 0, jnp.where(dist <= window, 1.0, 0.0), 0.0)
    return jnp.where(dist >= 0, w, 0.0)


def _decode_body(*refs, n_heads, n_blk, patterns, gate_col):
    q_ref, kc_ref, vc_ref, kn_ref, vn_ref = refs[:5]
    g_ref = refs[5] if gate_col is not None else None
    o_ref = refs[-1]
    n_rows = q_ref.shape[1]
    n_cache = kc_ref.shape[2]
    n_new = kn_ref.shape[1]
    step, blk = _row_ids(n_rows, n_heads, n_blk)
    qbd = _block_diag(q_ref[0] * (HEAD_DIM ** -0.5), blk, n_blk)
    s_c = _dot(qbd.astype(BF16), kc_ref[0].astype(BF16))
    w_c = _pattern_weight(n_cache + step - lax.broadcasted_iota(jnp.int32, (n_rows, n_cache), 1), patterns)
    s_c = jnp.where(w_c > 0, s_c, NEG_INF)
    kn = kn_ref[0]
    vn = vn_ref[0]
    s_n, w_n = [], []
    for c in range(n_new):
        s = jnp.sum(qbd * kn[c:c + 1, :], axis=-1, keepdims=True)
        w = _pattern_weight(step - c, patterns)
        s_n.append(jnp.where(w > 0, s, NEG_INF))
        w_n.append(w)
    m = jnp.max(s_c, axis=-1, keepdims=True)
    for s in s_n:
        m = jnp.maximum(m, s)
    p_c = w_c * jnp.exp(s_c - m)
    l = jnp.sum(p_c, axis=-1, keepdims=True)
    r = _dot_nt(p_c.astype(BF16), vc_ref[0].astype(BF16))
    for c in range(n_new):
        p = w_n[c] * jnp.exp(s_n[c] - m)
        l = l + p
        r = r + p * vn[c:c + 1, :]
    o = _diag_blocks(r, blk, n_blk) / l
    if g_ref is not None:
        o = o * jax.nn.sigmoid(g_ref[0])[:, gate_col:gate_col + 1]
    o_ref[0] = o


def _decode_attention(q, kc, vc, kn, vn, *, n_heads, n_blk, patterns, gate=None, gate_col=None):
    n, n_rows, _ = q.shape
    args = [q, kc, vc, kn, vn]
    if gate is not None:
        args.append(gate)
    whole = lambda a: pl.BlockSpec((1,) + a.shape[1:], lambda i: (i, 0, 0))
    return pl.pallas_call(
        functools.partial(_decode_body, n_heads=n_heads, n_blk=n_blk, patterns=patterns,
                          gate_col=gate_col if gate is not None else None),
        grid=(n,),
        in_specs=[whole(a) for a in args],
        out_specs=pl.BlockSpec((1, n_rows, HEAD_DIM), lambda i: (i, 0, 0)),
        out_shape=jax.ShapeDtypeStruct((n, n_rows, HEAD_DIM), F32),
        compiler_params=_params("parallel"),
        name="decode_attention",
    )(*args)


def _nsa_decode_body(pt_ref, q_ref, kc_ref, vc_ref, ksp_ref, vsp_ref, kn_ref, vn_ref, g_ref, wov_ref, o_ref,
                     kbuf, vbuf, expand, ksem, vsem, *, n_pages, n_cmp, n_slc):
    n = pl.program_id(0)
    slot = n % 2
    n_rows = q_ref.shape[1]
    n_new = kn_ref.shape[1]
    n_past = kbuf.shape[2]
    n_slc_pad = wov_ref.shape[1]
    rep = C_HEADS // C_KV_HEADS

    def gather(seq, dst_slot, start):
        for pool_ref, buf, sem in ((ksp_ref, kbuf, ksem), (vsp_ref, vbuf, vsem)):
            for p in range(n_pages):
                page = pt_ref[seq, p] if start else 0
                cp = pltpu.make_async_copy(pool_ref.at[page], buf.at[dst_slot, :, pl.ds(p * PAGE_SIZE, PAGE_SIZE)],
                                           sem.at[dst_slot])
                if start:
                    cp.start()
                else:
                    cp.wait()

    @pl.when(n == 0)
    def _():
        gather(0, 0, True)
        step_cols = 1024
        for c0 in range(0, n_past, step_cols):
            shape = (n_slc_pad, min(step_cols, n_past - c0))
            blk_of_key = _idiv(c0 + lax.broadcasted_iota(jnp.int32, shape, 1), SEL_BLOCK)
            hit = blk_of_key == lax.broadcasted_iota(jnp.int32, shape, 0)
            expand[:, c0:c0 + shape[1]] = jnp.where(hit, 1.0, 0.0).astype(BF16)

    @pl.when(n + 1 < pl.num_programs(0))
    def _():
        gather(n + 1, 1 - slot, True)

    step, blk = _row_ids(n_rows, C_HEADS, C_KV_HEADS)
    qpos = n_past + step
    qbd = _block_diag(q_ref[0] * (HEAD_DIM ** -0.5), blk, C_KV_HEADS)
    qbd_in = qbd.astype(BF16)
    n_cmp_pad = kc_ref.shape[1]
    cmp_i = lax.broadcasted_iota(jnp.int32, (n_rows, n_cmp_pad), 1)
    cmp_ok = (cmp_i * CMP_STRIDE + (CMP_LEN - 1) <= qpos) & (cmp_i < n_cmp)
    s = jnp.where(cmp_ok, _dot_nt(qbd_in, kc_ref[0].astype(BF16)), NEG_INF)
    m = jnp.max(s, axis=-1, keepdims=True)
    p = jnp.where(cmp_ok, jnp.exp(s - m), 0.0)
    l = jnp.sum(p, axis=-1, keepdims=True)
    p = p / jnp.where(l > 0, l, 1.0)
    o_cmp = _diag_blocks(_dot(p.astype(BF16), vc_ref[0].astype(BF16)), blk, C_KV_HEADS)
    grp = (n_rows // rep, rep, n_cmp_pad)
    imp = jnp.broadcast_to(jnp.sum(p.reshape(grp), axis=1, keepdims=True), grp).reshape(n_rows, n_cmp_pad)
    sel_rows = _select_blocks(_split3_dot(imp, wov_ref[...]), qpos, n_slc)
    gather(n, slot, False)
    picked = _dot(sel_rows.astype(BF16), expand[...])
    kpos = lax.broadcasted_iota(jnp.int32, (n_rows, n_past), 1)
    s_p = jnp.where(picked > 0.5, jnp.where(kpos <= qpos, _dot(qbd_in, kbuf[slot].astype(BF16)), NEG_INF), NEG_INF)
    kn = kn_ref[0]
    vn = vn_ref[0]
    s_n = []
    for c in range(n_new):
        blk_c = (n_past + c) // SEL_BLOCK
        ok = (sel_rows[:, blk_c:blk_c + 1] > 0.5) & (n_past + c <= qpos)
        s_n.append(jnp.where(ok, jnp.sum(qbd * kn[c:c + 1, :], axis=-1, keepdims=True), NEG_INF))
    m = jnp.max(s_p, axis=-1, keepdims=True)
    for sc in s_n:
        m = jnp.maximum(m, sc)
    p_p = jnp.exp(s_p - m)
    l = jnp.sum(p_p, axis=-1, keepdims=True)
    r = _dot_nt(p_p.astype(BF16), vbuf[slot].astype(BF16))
    for c in range(n_new):
        pn = jnp.exp(s_n[c] - m)
        l = l + pn
        r = r + pn * vn[c:c + 1, :]
    o_sel = _diag_blocks(r, blk, C_KV_HEADS) / l
    sig = jax.nn.sigmoid(g_ref[0])
    o_ref[0] = sig[:, 0:1] * o_cmp + sig[:, 1:2] * o_sel


def _nsa_decode(q, kcb, vcb, pool_ks, pool_vs, page_table, kn, vn, gate, n_cmp):
    n, n_rows, _ = q.shape
    n_pages = page_table.shape[1]
    n_past = n_pages * PAGE_SIZE
    n_new = kn.shape[1]
    n_slc = -(-(n_past + n_new) // SEL_BLOCK)
    n_slc_pad = -(-n_slc // LANES) * LANES
    wov = _overlap_weights(n_cmp, kcb.shape[1], n_slc, n_slc_pad)
    whole = lambda a: pl.BlockSpec((1,) + a.shape[1:], lambda i, pt: (i, 0, 0))
    pools = [jnp.transpose(a, (0, 2, 3, 1)).reshape(a.shape[0], C_KV_WIDTH, PAGE_SIZE) for a in (pool_ks, pool_vs)]
    return pl.pallas_call(
        functools.partial(_nsa_decode_body, n_pages=n_pages, n_cmp=n_cmp, n_slc=n_slc),
        grid_spec=pltpu.PrefetchScalarGridSpec(
            num_scalar_prefetch=1,
            grid=(n,),
            in_specs=[whole(q), whole(kcb), whole(vcb), pl.BlockSpec(memory_space=pl.ANY),
                      pl.BlockSpec(memory_space=pl.ANY), whole(kn), whole(vn), whole(gate),
                      pl.BlockSpec(wov.shape, lambda i, pt: (0, 0))],
            out_specs=pl.BlockSpec((1, n_rows, HEAD_DIM), lambda i, pt: (i, 0, 0)),
            scratch_shapes=[pltpu.VMEM((2, C_KV_WIDTH, n_past), F32), pltpu.VMEM((2, C_KV_WIDTH, n_past), F32),
                            pltpu.VMEM((n_slc_pad, n_past), BF16),
                            pltpu.SemaphoreType.DMA((2,)), pltpu.SemaphoreType.DMA((2,))]),
        out_shape=jax.ShapeDtypeStruct((n, n_rows, HEAD_DIM), F32),
        compiler_params=_params("arbitrary"),
        name="nsa_decode",
    )(page_table, q, kcb, vcb, *pools, kn, vn, gate, wov)


def _odd_out_body(x_ref, *refs):
    *branch_refs, z_ref, w_ref, gf_ref, y_ref = refs
    o = branch_refs[0][...]
    for ref in branch_refs[1:]:
        o = o + ref[...]
    mixed = (o * _silu(z_ref[...])).astype(BF16)
    h = x_ref[...] + _dot(mixed, w_ref[...])
    y_ref[...] = h * lax.rsqrt(jnp.mean(h * h, axis=-1, keepdims=True) + RMS_EPS) * gf_ref[...]


def _odd_out(x2d, branches, z, w_out, final_gain, tm):
    m, d = x2d.shape
    row = pl.BlockSpec((tm, d), lambda i: (i, 0))
    return pl.pallas_call(
        _odd_out_body,
        grid=(m // tm,),
        in_specs=[row] * (len(branches) + 2) + [pl.BlockSpec(w_out.shape, lambda i: (0, 0)),
                                                 pl.BlockSpec((1, d), lambda i: (0, 0))],
        out_specs=row,
        out_shape=jax.ShapeDtypeStruct((m, d), F32),
        compiler_params=_params("parallel"),
        name="odd_out",
    )(x2d, *branches, z, w_out, final_gain.reshape(1, d))


def _even_outs(prompt):
    a, bk = A_WIDTH, B_HEADS * B_DK
    outs = [(0, a, F32, False), (a, a, F32, False), (2 * a, a, F32, False)]
    if prompt:
        outs += [(a, a, F32, True), (2 * a, a, F32, True)]
    outs += [(3 * a, a, F32, False),
             (4 * a, bk, F32, False), (4 * a + bk, bk, F32, False),
             (4 * a + 2 * bk, B_WIDTH, F32, False), (4 * a + 2 * bk + B_WIDTH, B_WIDTH, F32, False)]
    return outs


def _from_feature_major(u, t0, n_heads):
    b, _, t = u.shape
    return jnp.transpose(u[:, :, t0:].reshape(b, n_heads, HEAD_DIM, t - t0), (0, 3, 1, 2))


def _to_feature_major(u):
    n, length, n_heads, hd = u.shape
    return jnp.transpose(u, (0, 2, 3, 1)).reshape(n, n_heads * hd, length)


def _even_layer_prompt(x, gain, w_in, w_out):
    b, t, d = x.shape
    x2d = x.reshape(b * t, d)
    qa, ka, va, ka_t, va_t, za, qb, kb, vb, zb = _norm_proj(x2d, gain, w_in, _even_outs(True), 512, t)
    seq = lambda u: u.reshape(b, t, u.shape[-1])
    oa = _dilated_attention(seq(qa), seq(ka), seq(va), A_PATTERNS)
    chunk = B_CHUNK if t % B_CHUNK == 0 else t
    s0 = jnp.zeros((b, B_HEADS, B_DK, B_DV), F32)
    ob, s_fin = _retention(seq(qb), seq(kb), seq(vb), s0, jnp.arange(t), chunk=chunk, c_real=chunk)
    y = _even_out(x2d, oa.reshape(b * t, A_WIDTH), za, ob.reshape(b * t, B_WIDTH), zb, w_out, 256)
    keep = min(A_PATTERNS[-1][0], t)
    return (y.reshape(b, t, d), _from_feature_major(ka_t, t - keep, A_HEADS),
            _from_feature_major(va_t, t - keep, A_HEADS), s_fin)


def _even_layer_sample(x, cache_k, cache_v, state, gain, w_in, w_out, past_len):
    n, s_len, d = x.shape
    x2d = x.reshape(n * s_len, d)
    qa, ka32, va32, za, qb, kb, vb, zb = _norm_proj(x2d, gain, w_in, _even_outs(False), n * s_len, n * s_len)
    seq = lambda u: u.reshape(n, s_len, u.shape[-1])
    oa = _decode_attention(qa.reshape(n, s_len * A_HEADS, HEAD_DIM), _to_feature_major(cache_k),
                           _to_feature_major(cache_v), seq(ka32), seq(va32), n_heads=A_HEADS, n_blk=A_HEADS,
                           patterns=A_PATTERNS)
    oa = oa.reshape(n * s_len, A_WIDTH)
    chunk = 16
    pad = lambda u: jnp.pad(seq(u), ((0, 0), (0, chunk - s_len), (0, 0)))
    pos = past_len + jnp.arange(chunk)
    ob, s_new = _retention(pad(qb), pad(kb), pad(vb), state, pos, chunk=chunk, c_real=s_len)
    ob = ob[:, :s_len].reshape(n * s_len, B_WIDTH)
    y = _even_out(x2d, oa, za, ob, zb, w_out, n * s_len)
    heads = lambda u: u.reshape(n, s_len, A_HEADS, HEAD_DIM)
    return y.reshape(n, s_len, d), heads(ka32), heads(va32), s_new


ODD_KV = ("kc", "vc", "ks", "vs", "kw", "vw")
ODD_GATE_COL = C_WIDTH + len(ODD_KV) * C_KV_WIDTH
ODD_Z_COL = ODD_GATE_COL + 3 * C_HEADS


def _odd_outs_prompt():
    outs = [(0, C_WIDTH, BF16, False)]
    for i, name in enumerate(ODD_KV):
        off = C_WIDTH + i * C_KV_WIDTH
        outs.append((off, C_KV_WIDTH, BF16, name in ("vs", "vw")))
        outs.append((off, C_KV_WIDTH, F32, True))
    outs += [(ODD_GATE_COL, 3 * C_HEADS, F32, True), (ODD_Z_COL, C_WIDTH, F32, False)]
    return outs


def _odd_outs_sample():
    outs = [(0, C_WIDTH, BF16, False)]
    outs += [(C_WIDTH + i * C_KV_WIDTH, C_KV_WIDTH, F32, False) for i in range(len(ODD_KV))]
    outs += [(ODD_GATE_COL, 3 * C_HEADS, F32, False), (ODD_Z_COL, C_WIDTH, F32, False)]
    return outs


def _odd_layer_prompt(x, gain, w_in, w_out, cw_k, cw_v, final_gain):
    b, t, d = x.shape
    x2d = x.reshape(b * t, d)
    (q, kc, kc_t, vc, vc_t, ks, ks_t, vs_tb, vs_t, kw, kw_t, vw_tb, vw_t, gl_t, z) = _norm_proj(
        x2d, gain, w_in, _odd_outs_prompt(), 512, t)
    seq = lambda u: u.reshape(b, t, u.shape[-1])
    n_cmp = (t - CMP_LEN) // CMP_STRIDE + 1
    pieces = lambda u: u.reshape(b, t // CMP_STRIDE, CMP_STRIDE * C_KV_WIDTH)
    kcb = _compress(pieces(kc), cw_k)
    vcb = _compress(pieces(vc), cw_v)
    o = _nsa_prompt(seq(q), kcb, vcb, seq(ks), vs_tb, seq(kw), vw_tb, gl_t, n_cmp)
    y = _odd_out(x2d, [o.reshape(b * t, C_WIDTH)], z, w_out, final_gain, 256)
    keep = min(C_WINDOW, t)
    kv = lambda u, t0: _from_feature_major(u, t0, C_KV_HEADS)
    return (y.reshape(b, t, d), kv(kc_t, 0), kv(vc_t, 0), kv(ks_t, 0), kv(vs_t, 0), kv(kw_t, t - keep),
            kv(vw_t, t - keep))


def _odd_layer_sample(x, page_table, pool_kc, pool_vc, pool_ks, pool_vs, buf_kw, buf_vw, gain, w_in, w_out,
                      cw_k, cw_v, final_gain):
    n, s_len, d = x.shape
    x2d = x.reshape(n * s_len, d)
    q, kc, vc, ks32, vs32, kw32, vw32, gl, z = _norm_proj(x2d, gain, w_in, _odd_outs_sample(), n * s_len, n * s_len)
    seq = lambda u: u.reshape(n, s_len, u.shape[-1])
    n_past = page_table.shape[1] * PAGE_SIZE
    n_cmp = (n_past + s_len - CMP_LEN) // CMP_STRIDE + 1
    assert CMP_STRIDE * (n_cmp - 1) + CMP_LEN <= n_past
    kcb = _compress_paged(pool_kc, page_table, cw_k)
    vcb = _compress_paged(pool_vc, page_table, cw_v)
    rows = lambda u: u.astype(F32).reshape(n, s_len * C_HEADS, HEAD_DIM)
    gate = gl.reshape(n, s_len * C_HEADS, 3)
    o_cs = _nsa_decode(rows(q), kcb, vcb, pool_ks, pool_vs, page_table, seq(ks32), seq(vs32), gate, n_cmp)
    o_win = _decode_attention(rows(q), _to_feature_major(buf_kw), _to_feature_major(buf_vw),
                              seq(kw32), seq(vw32), n_heads=C_HEADS, n_blk=C_KV_HEADS,
                              patterns=((C_WINDOW, 1),), gate=gate, gate_col=2)
    y = _odd_out(x2d, [o_cs.reshape(n * s_len, C_WIDTH), o_win.reshape(n * s_len, C_WIDTH)], z, w_out, final_gain,
                 n * s_len)
    kv = lambda u: u.reshape(n, s_len, C_KV_HEADS, HEAD_DIM)
    return y.reshape(n, s_len, d), kv(kc), kv(vc), kv(ks32), kv(vs32), kv(kw32), kv(vw32)


def kernel(x_prompt, x_sample, cache_a_k, cache_a_v, state_ret, cache_c_kcmp, cache_c_vcmp, cache_c_ksel,
           cache_c_vsel, cache_c_kwin, cache_c_vwin, page_table, even_norm, even_w_in, even_w_out, odd_norm,
           odd_w_in, odd_w_out, cmp_pos_k, cmp_w1_k, cmp_w2_k, cmp_pos_v, cmp_w1_v, cmp_w2_v, final_norm):
    assert even_norm.shape[0] == 1 and odd_norm.shape[0] == 1
    past_len = page_table.shape[1] * PAGE_SIZE
    w_in_e = even_w_in[0].T.astype(BF16)
    w_out_e = even_w_out[0].astype(BF16)
    w_in_o = odd_w_in[0].T.astype(BF16)
    w_out_o = odd_w_out[0].astype(BF16)
    cw_k = _compress_weights(cmp_pos_k[0], cmp_w1_k[0], cmp_w2_k[0])
    cw_v = _compress_weights(cmp_pos_v[0], cmp_w1_v[0], cmp_w2_v[0])

    hp, ak_p, av_p, ret_p = _even_layer_prompt(x_prompt, even_norm[0], w_in_e, w_out_e)
    hs, ak_s, av_s, ret_s = _even_layer_sample(x_sample, cache_a_k[0], cache_a_v[0], state_ret[0], even_norm[0],
                                               w_in_e, w_out_e, past_len)
    yp, *rows_p = _odd_layer_prompt(hp, odd_norm[0], w_in_o, w_out_o, cw_k, cw_v, final_norm)
    ys, *rows_s = _odd_layer_sample(hs, page_table, cache_c_kcmp[0], cache_c_vcmp[0], cache_c_ksel[0],
                                    cache_c_vsel[0], cache_c_kwin[0], cache_c_vwin[0], odd_norm[0], w_in_o, w_out_o,
                                    cw_k, cw_v, final_norm)
    lead = lambda u: u[None]
    out = [yp, ys, lead(ak_p), lead(ak_s), lead(av_p), lead(av_s), lead(ret_p), lead(ret_s)]
    for rp, rs in zip(rows_p, rows_s):
        out += [lead(rp), lead(rs)]
    return tuple(out)
```

```python
import functools

import numpy as np
import jax
import jax.numpy as jnp
from jax import lax
from jax.experimental import pallas as pl
from jax.experimental.pallas import tpu as pltpu

F32 = jnp.float32
BF16 = jnp.bfloat16

HEAD_DIM = 64
A_HEADS = 8
A_WIDTH = A_HEADS * HEAD_DIM
A_PATTERNS = ((128, 1), (512, 4), (2048, 16))
B_HEADS = 4
B_DK = 64
B_DV = 128
B_WIDTH = B_HEADS * B_DV
B_CHUNK = 128
ROPE_BASE = 10000.0
GN_EPS = 1e-5
C_HEADS = 16
C_KV_HEADS = 2
C_WIDTH = C_HEADS * HEAD_DIM
C_KV_WIDTH = C_KV_HEADS * HEAD_DIM
CMP_LEN = 32
CMP_STRIDE = 16
CMP_HIDDEN = 256
SEL_BLOCK = 64
SEL_TOPK = 16
SEL_FORCE = 1e9
C_WINDOW = 512
PAGE_SIZE = 128
RMS_EPS = 1e-6
NEG_INF = -1e30

LANES = 128
ATTN_TQ = 128
VMEM_LIMIT = 56 * 1024 * 1024


def _params(*sem):
    return pltpu.CompilerParams(dimension_semantics=sem, vmem_limit_bytes=VMEM_LIMIT)


def _dot(a, b):
    return jnp.dot(a, b, preferred_element_type=F32)


def _dot_nt(a, b):
    return lax.dot_general(a, b, (((1,), (1,)), ((), ())), preferred_element_type=F32)


def _dot_tn(a, b):
    return lax.dot_general(a, b, (((0,), (0,)), ((), ())), preferred_element_type=F32)


def _idiv(x, n):
    assert n & (n - 1) == 0
    return jnp.right_shift(x, n.bit_length() - 1)


def _imod(x, n):
    assert n & (n - 1) == 0
    return jnp.bitwise_and(x, n - 1)


def _split3_dot(a, w):
    hi = a.astype(BF16)
    r1 = a - hi.astype(F32)
    mid = r1.astype(BF16)
    lo = (r1 - mid.astype(F32)).astype(BF16)
    return _dot(hi, w) + _dot(mid, w) + _dot(lo, w)


def _norm_proj_body(x_ref, g_ref, wt_ref, *o_refs, outs, tiles, tm):
    x = x_ref[...]
    y = x * lax.rsqrt(jnp.mean(x * x, axis=-1, keepdims=True) + RMS_EPS) * g_ref[...]
    yb = y.astype(BF16)
    tile = pl.program_id(0) % tiles
    done = {}
    for o_ref, (off, width, _, keep_from) in zip(o_refs, outs):
        w_rows = wt_ref[off:off + width, :]
        if keep_from is None:
            if (off, width) not in done:
                done[(off, width)] = _dot_nt(yb, w_rows)
            o_ref[...] = done[(off, width)].astype(o_ref.dtype)
        elif keep_from == 0:
            if (off, width, 0) not in done:
                done[(off, width, 0)] = _dot_nt(w_rows, yb)
            o_ref[0] = done[(off, width, 0)].astype(o_ref.dtype)
        else:
            @pl.when(tile >= keep_from // tm)
            def _():
                o_ref[0] = _dot_nt(w_rows, yb).astype(o_ref.dtype)


def _norm_proj(x2d, gain, w_t, outs, tm, seq_len):
    m, d = x2d.shape
    n = w_t.shape[0]
    tiles = seq_len // tm
    assert seq_len % tm == 0 and m % seq_len == 0
    out_specs, out_shape = [], []
    for _, width, dt, keep_from in outs:
        if keep_from is None:
            out_specs.append(pl.BlockSpec((tm, width), lambda i: (i, 0)))
            out_shape.append(jax.ShapeDtypeStruct((m, width), dt))
        else:
            assert keep_from % tm == 0
            first = keep_from // tm
            out_specs.append(pl.BlockSpec((1, width, tm),
                                          lambda i, first=first: (i // tiles, 0, jnp.maximum(i % tiles - first, 0))))
            out_shape.append(jax.ShapeDtypeStruct((m // seq_len, width, seq_len - keep_from), dt))
    return pl.pallas_call(
        functools.partial(_norm_proj_body, outs=outs, tiles=tiles, tm=tm),
        grid=(m // tm,),
        in_specs=[pl.BlockSpec((tm, d), lambda i: (i, 0)),
                  pl.BlockSpec((1, d), lambda i: (0, 0)),
                  pl.BlockSpec((n, d), lambda i: (0, 0))],
        out_specs=out_specs,
        out_shape=out_shape,
        compiler_params=_params("arbitrary"),
        name="norm_proj",
    )(x2d, gain.reshape(1, d), w_t)


DIL_GROUP = 4


def _dilated_body(q_ref, k_ref, v_ref, o_ref, o_sc, lse_sc, *, patterns):
    tq = ATTN_TQ
    blk_len = q_ref.shape[1]
    base = pl.program_id(2) * blk_len
    head_of_lane = _idiv(lax.broadcasted_iota(jnp.int32, (tq, LANES), 1), HEAD_DIM)
    own_lanes = [head_of_lane == h for h in range(LANES // HEAD_DIM)]

    def rows(ref, start, size, dil):
        return ref[0, pl.ds(start, size) if dil == 1 else pl.ds(start, size, stride=dil), :]

    for p_idx, (window, dil) in enumerate(patterns):
        lookback = window // dil
        n_prev = -(-lookback // tq)
        span = (n_prev + 1) * tq
        n_u = blk_len // (tq * dil)
        n_sub_blocks = dil * n_u
        assert blk_len % (tq * dil) == 0 and n_sub_blocks % DIL_GROUP == 0

        def trip(i, carry, dil=dil, lookback=lookback, n_prev=n_prev, span=span, p_idx=p_idx):
            work = []
            for j in range(DIL_GROUP):
                idx = i * DIL_GROUP + j
                res = _imod(idx, dil)
                u = _idiv(idx, dil)
                s0 = base // dil + u * tq
                ks = jnp.maximum(s0 - n_prev * tq, 0)
                q_row = res + dil * (u * tq)
                k_row = res + dil * ks
                q2 = rows(q_ref, q_row, tq, dil) * (HEAD_DIM ** -0.5)
                k2 = rows(k_ref, k_row, span, dil).astype(BF16)
                v2 = rows(v_ref, k_row, span, dil).astype(BF16)
                dist = (s0 + lax.broadcasted_iota(jnp.int32, (tq, span), 0)
                        - ks - lax.broadcasted_iota(jnp.int32, (tq, span), 1))
                bias = jnp.where(dist >= 0, jnp.where(dist <= lookback, 0.0, NEG_INF), NEG_INF)
                work.append((q_row, q2, k2, v2, bias))
            scores = [[_dot_nt(jnp.where(own, q2, 0.0).astype(BF16), k2) + bias for own in own_lanes]
                      for _, q2, k2, _, bias in work]
            maxes = [[jnp.max(s, axis=-1, keepdims=True) for s in ss] for ss in scores]
            probs = [[jnp.exp(s - m) for s, m in zip(ss, mm)] for ss, mm in zip(scores, maxes)]
            sums = [[jnp.sum(p, axis=-1, keepdims=True) for p in pp] for pp in probs]
            outs = [[_dot(p.astype(BF16), w[3]) for p in pp] for pp, w in zip(probs, work)]
            for w, oo, mm, ll in zip(work, outs, maxes, sums):
                o2 = oo[0] / ll[0]
                lse2 = jnp.broadcast_to(mm[0] + jnp.log(ll[0]), (tq, LANES))
                for own, o, m, l in list(zip(own_lanes, oo, mm, ll))[1:]:
                    o2 = jnp.where(own, o / l, o2)
                    lse2 = jnp.where(own, m + jnp.log(l), lse2)
                idx = pl.ds(w[0], tq) if dil == 1 else pl.ds(w[0], tq, stride=dil)
                o_sc[p_idx, idx, :] = o2
                lse_sc[p_idx, idx, :] = lse2
            return carry

        lax.fori_loop(0, n_sub_blocks // DIL_GROUP, trip, 0)

    lses = [lse_sc[p] for p in range(len(patterns))]
    top = lses[0]
    for l in lses[1:]:
        top = jnp.maximum(top, l)
    weights = [jnp.exp(l - top) for l in lses]
    num = weights[0] * o_sc[0]
    den = weights[0]
    for p in range(1, len(patterns)):
        num = num + weights[p] * o_sc[p]
        den = den + weights[p]
    o_ref[0] = num / den


def _dilated_attention(q, k, v, patterns):
    b, t, width = q.shape
    blk_len = min(t, max(ATTN_TQ * dil for _, dil in patterns))
    assert t % blk_len == 0
    for window, dil in patterns:
        assert t // dil >= (-(-(window // dil) // ATTN_TQ) + 1) * ATTN_TQ
    return pl.pallas_call(
        functools.partial(_dilated_body, patterns=patterns),
        grid=(b, width // LANES, t // blk_len),
        in_specs=[pl.BlockSpec((1, blk_len, LANES), lambda bi, hp, i: (bi, i, hp)),
                  pl.BlockSpec((1, t, LANES), lambda bi, hp, i: (bi, 0, hp)),
                  pl.BlockSpec((1, t, LANES), lambda bi, hp, i: (bi, 0, hp))],
        out_specs=pl.BlockSpec((1, blk_len, LANES), lambda bi, hp, i: (bi, i, hp)),
        out_shape=jax.ShapeDtypeStruct((b, t, width), F32),
        scratch_shapes=[pltpu.VMEM((len(patterns), blk_len, LANES), F32),
                        pltpu.VMEM((len(patterns), blk_len, LANES), F32)],
        compiler_params=_params("parallel", "parallel", "arbitrary"),
        name="dilated_attention",
    )(q, k, v)


def _retention_body(q_ref, k_ref, v_ref, cos_ref, sin_ref, qd_ref, kd_ref, intra_ref, cd_ref, s0_ref,
                    o_ref, sfin_ref, state):
    c = pl.program_id(1)

    @pl.when(c == 0)
    def _():
        state[...] = s0_ref[0]

    cos = cos_ref[...]
    sin = sin_ref[...]
    width = cos.shape[-1]
    half = B_DK // 2
    lane = _imod(lax.broadcasted_iota(jnp.int32, cos.shape, 1), B_DK)

    def rotary(x):
        fwd = pltpu.roll(x, half, 1)
        bwd = pltpu.roll(x, width - half, 1)
        return x * cos + jnp.where(lane < half, -bwd, fwd) * sin

    q = rotary(q_ref[0])
    k = rotary(k_ref[0]) * (B_DK ** -0.5)
    q_in = q.astype(BF16)
    k_in = k.astype(BF16)
    q_st = (q * qd_ref[...]).astype(BF16)
    k_st = (k * kd_ref[...]).astype(BF16)
    v = v_ref[0].astype(BF16)
    kcs = [slice(h * B_DK, (h + 1) * B_DK) for h in range(B_HEADS)]
    vcs = [slice(h * B_DV, (h + 1) * B_DV) for h in range(B_HEADS)]
    states = [state[h] for h in range(B_HEADS)]
    scores = [_dot_nt(q_in[:, kc], k_in[:, kc]) * intra_ref[h] for h, kc in enumerate(kcs)]
    carried = [_dot(q_st[:, kc], st.astype(BF16)) for kc, st in zip(kcs, states)]
    updates = [_dot_tn(k_st[:, kc], v[:, vc]) for kc, vc in zip(kcs, vcs)]
    inner = [_dot(s.astype(BF16), v[:, vc]) for s, vc in zip(scores, vcs)]
    for h in range(B_HEADS):
        o_ref[0, :, vcs[h]] = inner[h] + carried[h]
        state[h] = states[h] * cd_ref[h] + updates[h]

    @pl.when(c == pl.num_programs(1) - 1)
    def _():
        sfin_ref[0] = state[...]


def _retention(q, k, v, s0, pos, *, chunk, c_real):
    n, t, _ = q.shape
    half = B_DK // 2
    freqs = ROPE_BASE ** (-jnp.arange(half, dtype=F32) / half)
    ang = pos.astype(F32)[:, None] * freqs[None, :]
    cos = jnp.tile(jnp.cos(ang), (1, 2 * B_HEADS))
    sin = jnp.tile(jnp.sin(ang), (1, 2 * B_HEADS))
    log_g = jnp.log1p(-jnp.exp2(-5.0 - jnp.arange(B_HEADS, dtype=F32)))
    i = jnp.arange(chunk, dtype=F32)
    diff = i[:, None] - i[None, :]
    intra = jnp.where(diff >= 0, jnp.exp(log_g[:, None, None] * jnp.maximum(diff, 0.0)), 0.0)
    q_decay = jnp.repeat(jnp.exp(log_g[None, :] * (i[:, None] + 1.0)), B_DK, axis=1)
    k_decay = jnp.repeat(jnp.exp(log_g[None, :] * (c_real - 1.0 - i[:, None])), B_DK, axis=1)
    chunk_decay = jnp.broadcast_to(jnp.exp(log_g * c_real)[:, None, None], (B_HEADS, 1, B_DV))
    wqk = B_HEADS * B_DK
    return pl.pallas_call(
        _retention_body,
        grid=(n, t // chunk),
        in_specs=[pl.BlockSpec((1, chunk, wqk), lambda b, c: (b, c, 0)),
                  pl.BlockSpec((1, chunk, wqk), lambda b, c: (b, c, 0)),
                  pl.BlockSpec((1, chunk, B_WIDTH), lambda b, c: (b, c, 0)),
                  pl.BlockSpec((chunk, wqk), lambda b, c: (c, 0)),
                  pl.BlockSpec((chunk, wqk), lambda b, c: (c, 0)),
                  pl.BlockSpec((chunk, wqk), lambda b, c: (0, 0)),
                  pl.BlockSpec((chunk, wqk), lambda b, c: (0, 0)),
                  pl.BlockSpec((B_HEADS, chunk, chunk), lambda b, c: (0, 0, 0)),
                  pl.BlockSpec((B_HEADS, 1, B_DV), lambda b, c: (0, 0, 0)),
                  pl.BlockSpec((1, B_HEADS, B_DK, B_DV), lambda b, c: (b, 0, 0, 0))],
        out_specs=[pl.BlockSpec((1, chunk, B_WIDTH), lambda b, c: (b, c, 0)),
                   pl.BlockSpec((1, B_HEADS, B_DK, B_DV), lambda b, c: (b, 0, 0, 0))],
        out_shape=[jax.ShapeDtypeStruct((n, t, B_WIDTH), F32),
                   jax.ShapeDtypeStruct((n, B_HEADS, B_DK, B_DV), F32)],
        scratch_shapes=[pltpu.VMEM((B_HEADS, B_DK, B_DV), F32)],
        compiler_params=_params("parallel", "arbitrary"),
        name="retention",
    )(q, k, v, cos, sin, q_decay, k_decay, intra, chunk_decay, s0)


def _silu(z):
    return z * jax.nn.sigmoid(z)


def _even_out_body(x_ref, oa_ref, za_ref, ob_ref, zb_ref, w_ref, y_ref):
    ga = (oa_ref[...] * _silu(za_ref[...])).astype(BF16)
    acc = x_ref[...] + _dot(ga, w_ref[0:A_WIDTH, :])
    ob = ob_ref[...]
    zb = zb_ref[...]
    for h in range(B_HEADS):
        cols = slice(h * B_DV, (h + 1) * B_DV)
        seg = ob[:, cols]
        mu = jnp.mean(seg, axis=-1, keepdims=True)
        cen = seg - mu
        var = jnp.mean(cen * cen, axis=-1, keepdims=True)
        gb = (cen * lax.rsqrt(var + GN_EPS) * _silu(zb[:, cols])).astype(BF16)
        acc = acc + _dot(gb, w_ref[A_WIDTH + h * B_DV:A_WIDTH + (h + 1) * B_DV, :])
    y_ref[...] = acc


def _even_out(x2d, oa, za, ob, zb, w_out, tm):
    m, d = x2d.shape
    row = lambda width: pl.BlockSpec((tm, width), lambda i: (i, 0))
    return pl.pallas_call(
        _even_out_body,
        grid=(m // tm,),
        in_specs=[row(d), row(A_WIDTH), row(A_WIDTH), row(B_WIDTH), row(B_WIDTH),
                  pl.BlockSpec(w_out.shape, lambda i: (0, 0))],
        out_specs=row(d),
        out_shape=jax.ShapeDtypeStruct((m, d), F32),
        compiler_params=_params("parallel"),
        name="even_out",
    )(x2d, oa, za, ob, zb, w_out)


def _compress_math(x, p2_ref, wbig_ref, w2_ref):
    n_piece = x.shape[0]
    a = _dot(x.astype(BF16), wbig_ref[...])
    pc = _dot(p2_ref[...], wbig_ref[...])
    hid = []
    for g in range(C_KV_HEADS):
        lo = slice(g * 2 * CMP_HIDDEN, g * 2 * CMP_HIDDEN + CMP_HIDDEN)
        hi = slice(g * 2 * CMP_HIDDEN + CMP_HIDDEN, (g + 1) * 2 * CMP_HIDDEN)
        nxt = pltpu.roll(a[:, hi], n_piece - 1, 0)
        pos = (pc[0:1, lo] + pc[2:3, lo]) + (pc[1:2, hi] + pc[3:4, hi])
        hid.append(_silu(a[:, lo] + nxt + pos))
    return _dot(jnp.concatenate(hid, axis=-1).astype(BF16), w2_ref[...])


def _compress_body(x_ref, p2_ref, wbig_ref, w2_ref, o_ref):
    o_ref[0] = _compress_math(x_ref[0], p2_ref, wbig_ref, w2_ref)


def _compress_weights(pos_emb, w1, w2):
    per = CMP_LEN // CMP_STRIDE
    w1r = w1.reshape(per, CMP_STRIDE, HEAD_DIM, CMP_HIDDEN)
    eye = jnp.eye(C_KV_HEADS, dtype=w1.dtype)
    wbig = jnp.einsum('jldf,gh->lgdhjf', w1r, eye).reshape(CMP_STRIDE * C_KV_WIDTH, C_KV_HEADS * per * CMP_HIDDEN)
    w2bd = jnp.einsum('fd,gh->gfhd', w2, eye).reshape(C_KV_HEADS * CMP_HIDDEN, C_KV_WIDTH)
    halves = jnp.tile(pos_emb.reshape(per, CMP_STRIDE, 1, HEAD_DIM), (1, 1, C_KV_HEADS, 1)).reshape(per, -1)
    hi = halves.astype(BF16)
    lo = (halves - hi.astype(F32)).astype(BF16)
    p2 = jnp.concatenate([hi, lo, jnp.zeros((16 - 2 * per, halves.shape[1]), BF16)], axis=0)
    w1g = jnp.transpose(w1r, (1, 2, 0, 3)).reshape(CMP_STRIDE * HEAD_DIM, per * CMP_HIDDEN)
    halves_g = pos_emb.reshape(per, CMP_STRIDE * HEAD_DIM)
    hi_g = halves_g.astype(BF16)
    lo_g = (halves_g - hi_g.astype(F32)).astype(BF16)
    p2g = jnp.concatenate([hi_g, lo_g, jnp.zeros((16 - 2 * per, halves_g.shape[1]), BF16)], axis=0)
    return p2, wbig.astype(BF16), w2bd.astype(BF16), p2g, w1g.astype(BF16)


def _compress(pieces, cw):
    n, n_piece, width = pieces.shape
    p2, wbig, w2bd = cw[:3]
    const = lambda a: pl.BlockSpec(a.shape, lambda i: (0,) * a.ndim)
    return pl.pallas_call(
        _compress_body,
        grid=(n,),
        in_specs=[pl.BlockSpec((1, n_piece, width), lambda i: (i, 0, 0)), const(p2), const(wbig), const(w2bd)],
        out_specs=pl.BlockSpec((1, n_piece, C_KV_WIDTH), lambda i: (i, 0, 0)),
        out_shape=jax.ShapeDtypeStruct((n, n_piece, C_KV_WIDTH), F32),
        compiler_params=_params("parallel"),
        name="compress",
    )(pieces, p2, wbig, w2bd)


def _page_dma(pt_ref, pool_ref, buf_ref, sem_ref, n, slot, n_pages, start):
    for p in range(n_pages):
        page = pt_ref[n, p] if start else 0
        cp = pltpu.make_async_copy(pool_ref.at[page], buf_ref.at[slot, p], sem_ref.at[slot])
        if start:
            cp.start()
        else:
            cp.wait()


def _compress_paged_body(pt_ref, pool_ref, perm_ref, p2_ref, w1_ref, w2_ref, o_ref, buf, pieces, sem, *, n_pages):
    n = pl.program_id(0)
    slot = n % 2
    rows = PAGE_SIZE // CMP_STRIDE
    n_piece = n_pages * rows

    @pl.when(n == 0)
    def _():
        _page_dma(pt_ref, pool_ref, buf, sem, 0, 0, n_pages, True)

    @pl.when(n + 1 < pl.num_programs(0))
    def _():
        _page_dma(pt_ref, pool_ref, buf, sem, n + 1, 1 - slot, n_pages, True)

    _page_dma(pt_ref, pool_ref, buf, sem, n, slot, n_pages, False)
    perm = perm_ref[...]

    group = 8
    assert n_pages % group == 0 and CMP_STRIDE % 2 == 0
    low_half = lax.broadcasted_iota(jnp.int32, (rows, LANES), 1) < HEAD_DIM

    def unfold(i, carry):
        pairs = []
        for j in range(0, group, 2):
            two = buf[slot, pl.ds(i * group + j, 2)].reshape(2 * C_KV_WIDTH, PAGE_SIZE)
            pairs.append(_dot_nt(perm, two.astype(BF16)))
        for j in range(group):
            by_row = pairs[j // 2][:, (j % 2) * C_KV_WIDTH:(j % 2 + 1) * C_KV_WIDTH]
            r0 = pl.multiple_of((i * group + j) * rows, rows)
            for l in range(0, CMP_STRIDE, 2):
                even = by_row[l * rows:(l + 1) * rows, :]
                odd = by_row[(l + 1) * rows:(l + 2) * rows, :]
                lanes = slice(l * HEAD_DIM, (l + 2) * HEAD_DIM)
                pieces[0, pl.ds(r0, rows), lanes] = jnp.where(low_half, even, pltpu.roll(odd, HEAD_DIM, 1))
                pieces[1, pl.ds(r0, rows), lanes] = jnp.where(low_half, pltpu.roll(even, HEAD_DIM, 1), odd)
        return carry

    lax.fori_loop(0, n_pages // group, unfold, 0)
    hidden = CMP_HIDDEN
    pc = _dot(p2_ref[...], w1_ref[...])
    pos = (pc[0:1, :hidden] + pc[2:3, :hidden]) + (pc[1:2, hidden:] + pc[3:4, hidden:])
    acts = [_dot(pieces[g].astype(BF16), w1_ref[...]) for g in range(C_KV_HEADS)]
    hid = [_silu(a[:, :hidden] + pltpu.roll(a[:, hidden:], n_piece - 1, 0) + pos) for a in acts]
    o_ref[0] = _dot(jnp.concatenate(hid, axis=-1).astype(BF16), w2_ref[...])


def _compress_paged(pool, page_table, cw):
    n, n_pages = page_table.shape
    rows = PAGE_SIZE // CMP_STRIDE
    n_piece = n_pages * rows
    _, _, w2bd, p2g, w1g = cw
    perm = np.zeros((PAGE_SIZE, PAGE_SIZE), np.float32)
    for l in range(CMP_STRIDE):
        for piece in range(rows):
            perm[l * rows + piece, CMP_STRIDE * piece + l] = 1.0
    perm = jnp.asarray(perm, BF16)
    pages = jnp.transpose(pool, (0, 2, 3, 1)).reshape(pool.shape[0], C_KV_WIDTH, PAGE_SIZE)
    const = lambda a: pl.BlockSpec(a.shape, lambda i, pt: (0,) * a.ndim)
    return pl.pallas_call(
        functools.partial(_compress_paged_body, n_pages=n_pages),
        grid_spec=pltpu.PrefetchScalarGridSpec(
            num_scalar_prefetch=1,
            grid=(n,),
            in_specs=[pl.BlockSpec(memory_space=pl.ANY), const(perm), const(p2g), const(w1g), const(w2bd)],
            out_specs=pl.BlockSpec((1, n_piece, C_KV_WIDTH), lambda i, pt: (i, 0, 0)),
            scratch_shapes=[pltpu.VMEM((2, n_pages, C_KV_WIDTH, PAGE_SIZE), F32),
                            pltpu.VMEM((C_KV_HEADS, n_piece, CMP_STRIDE * HEAD_DIM), F32),
                            pltpu.SemaphoreType.DMA((2,))]),
        out_shape=jax.ShapeDtypeStruct((n, n_piece, C_KV_WIDTH), F32),
        compiler_params=_params("arbitrary"),
        name="compress_paged",
    )(page_table, pages, perm, p2g, w1g, w2bd)


def _overlap_weights(n_cmp, n_cmp_pad, n_slc, n_slc_pad):
    i = np.arange(n_cmp_pad)[:, None]
    j = np.arange(n_slc_pad)[None, :]
    ov = (np.minimum(i * CMP_STRIDE + CMP_LEN, (j + 1) * SEL_BLOCK) - np.maximum(i * CMP_STRIDE, j * SEL_BLOCK))
    w = np.where((i < n_cmp) & (j < n_slc), np.clip(ov, 0, None) / CMP_LEN, 0.0)
    return jnp.asarray(w, dtype=BF16)


def _select_blocks(p_slc, qpos, n_slc):
    shape = p_slc.shape
    lane = lax.broadcasted_iota(jnp.int32, shape, 1)
    cur = _idiv(qpos, SEL_BLOCK)
    forced = (lane == 0) | (lane == cur) | (lane == cur - 1)
    causal = lane * SEL_BLOCK <= qpos
    score = jnp.where(forced, SEL_FORCE, jnp.where(causal, p_slc, -SEL_FORCE))
    score = jnp.where(lane < n_slc, score, -3.0 * SEL_FORCE)
    lane_f = lane.astype(F32)
    sel = jnp.zeros(shape, F32)
    for _ in range(min(SEL_TOPK, n_slc)):
        m = jnp.max(score, axis=-1, keepdims=True)
        first = jnp.min(jnp.where(score == m, lane_f, 1e9), axis=-1, keepdims=True)
        hit = lane_f == first
        sel = jnp.where(hit, jnp.where(m > -0.5 * SEL_FORCE, 1.0, 0.0), sel)
        score = jnp.where(hit, -4.0 * SEL_FORCE, score)
    return sel


def _select_blocks_t(p_slc, qpos, n_slc):
    n_rows = -(-n_slc // 8) * 8
    shape = (n_rows, p_slc.shape[1])
    blk = lax.broadcasted_iota(jnp.int32, shape, 0)
    cur = _idiv(qpos, SEL_BLOCK)
    forced = (blk == 0) | (blk == cur) | (blk == cur - 1)
    causal = blk * SEL_BLOCK <= qpos
    score = jnp.where(forced, SEL_FORCE, jnp.where(causal, p_slc[:n_rows], -SEL_FORCE))
    score = jnp.where(blk < n_slc, score, -3.0 * SEL_FORCE)
    tiles = [score[8 * v:8 * v + 8] for v in range(n_rows // 8)]
    sub = lax.broadcasted_iota(jnp.int32, (8, shape[1]), 0)
    ranks = [jnp.zeros((8, shape[1]), F32) for _ in tiles]
    for i in range(n_slc):
        row = tiles[i // 8][i % 8:i % 8 + 1]
        for v, tile in enumerate(tiles):
            ge = jnp.where(row >= tile, 1.0, 0.0)
            gt = jnp.where(row > tile, 1.0, 0.0)
            if 8 * v > i:
                ranks[v] = ranks[v] + ge
            elif 8 * v + 7 <= i:
                ranks[v] = ranks[v] + gt
            else:
                ranks[v] = ranks[v] + jnp.where(sub > i - 8 * v, ge, gt)
    rank = jnp.concatenate(ranks, axis=0)
    return jnp.where(rank < min(SEL_TOPK, n_slc), jnp.where(score > -0.5 * SEL_FORCE, 1.0, 0.0), 0.0)


NSA_TILE = 2 * LANES
ONES_ROWS = 16
LOG2E = 1.4426950408889634


def _nsa_prompt_body(q_ref, kc_ref, vct_ref, ks_ref, vst_ref, kw_ref, vwt_ref, g_ref, wovt_ref, o_ref, m_sc, acc_sc,
                     *, n_cmp, n_slc):
    tq = tk = NSA_TILE
    rep = C_HEADS // C_KV_HEADS
    blk = pl.program_id(1)
    q0 = blk * tq
    n_cmp_pad = kc_ref.shape[1]
    sig_t = jax.nn.sigmoid(g_ref[0])
    q_t = [jnp.transpose(q_ref[0, :, c * LANES:(c + 1) * LANES].astype(F32))
           for c in range(C_WIDTH // LANES)]
    qpos = q0 + lax.broadcasted_iota(jnp.int32, (1, tq), 1)
    cmp_i = lax.broadcasted_iota(jnp.int32, (n_cmp_pad, tq), 0)
    cmp_ok = (cmp_i * CMP_STRIDE + (CMP_LEN - 1) <= qpos) & (cmp_i < n_cmp)
    cmp_bias = jnp.where(cmp_ok, 0.0, NEG_INF)
    cmp_keep = jnp.where(cmp_ok, 1.0, 0.0)
    ahead = lax.broadcasted_iota(jnp.int32, (tk, tq), 0) - lax.broadcasted_iota(jnp.int32, (tk, tq), 1)
    diag_bias = jnp.where(ahead <= 0, 0.0, NEG_INF)
    far_bias = jnp.where(ahead >= 0, 0.0, NEG_INF)
    ones_rows = jnp.where(lax.broadcasted_iota(jnp.int32, (ONES_ROWS, tk), 0) == 0, 1.0, 0.0).astype(BF16)

    def flash_update(scores, v_t):
        v_ext = jnp.concatenate([v_t, ones_rows], axis=0)
        for r, s in enumerate(scores):
            m_old = m_sc[r:r + 1, :]
            m_new = jnp.maximum(m_old, jnp.max(s, axis=0, keepdims=True))
            m_sc[r:r + 1, :] = m_new
            pv = _dot(v_ext, jnp.exp2(s - m_new).astype(BF16))
            acc_sc[r] = jnp.exp2(m_old - m_new) * acc_sc[r] + pv

    def flash_reset():
        m_sc[...] = jnp.full(m_sc.shape, NEG_INF, F32)
        acc_sc[...] = jnp.zeros(acc_sc.shape, F32)

    def flash_result(r):
        acc = acc_sc[r]
        return acc[:HEAD_DIM] / acc[HEAD_DIM:HEAD_DIM + 1]

    outs = []
    for g in range(C_KV_HEADS):
        gc = slice(g * HEAD_DIM, (g + 1) * HEAD_DIM)
        heads = [g * rep + r for r in range(rep)]
        q_heads = [q_t[h // 2][(h % 2) * HEAD_DIM:(h % 2 + 1) * HEAD_DIM] for h in heads]
        q_nat = [(qh * (HEAD_DIM ** -0.5)).astype(BF16) for qh in q_heads]
        q_log2 = [(qh * (HEAD_DIM ** -0.5 * LOG2E)).astype(BF16) for qh in q_heads]
        kcb = kc_ref[0, :, gc].astype(BF16)
        vct = vct_ref[0, gc, :].astype(BF16)
        scores = [_dot(kcb, q_nat[r]) + cmp_bias for r in range(rep)]
        probs = []
        for s in scores:
            p = jnp.exp(s - jnp.max(s, axis=0, keepdims=True)) * cmp_keep
            l = jnp.sum(p, axis=0, keepdims=True)
            probs.append(p / jnp.where(l > 0, l, 1.0))
        o_cmp = [_dot(vct, p.astype(BF16)) for p in probs]
        imp = probs[0]
        for p in probs[1:]:
            imp = imp + p
        hi = imp.astype(BF16)
        r1 = imp - hi.astype(F32)
        mid = r1.astype(BF16)
        lo = (r1 - mid.astype(F32)).astype(BF16)
        p_slc = _dot(wovt_ref[...], hi) + _dot(wovt_ref[...], mid) + _dot(wovt_ref[...], lo)
        sel_bias = jnp.where(_select_blocks_t(p_slc, qpos, n_slc) > 0.5, 0.0, NEG_INF).astype(BF16)
        q_sel = [jnp.concatenate([qh, sel_bias], axis=0) for qh in q_log2]
        flash_reset()

        def sel_scores(k0):
            k_tile = ks_ref[0, pl.ds(pl.multiple_of(k0, tk), tk), g * LANES:(g + 1) * LANES]
            return [_dot(k_tile, qs) for qs in q_sel]

        def sel_values(k0):
            return vst_ref[0, gc, pl.ds(pl.multiple_of(k0, tk), tk)]

        def past_pair(i, carry):
            k0 = i * (2 * tk)
            first, second = sel_scores(k0), sel_scores(k0 + tk)
            flash_update(first, sel_values(k0))
            flash_update(second, sel_values(k0 + tk))
            return carry

        lax.fori_loop(0, blk // 2, past_pair, 0)
        k0 = (blk // 2) * (2 * tk)
        first_on_diag = jnp.where(k0 == q0, 1.0, 0.0)
        first = [s + diag_bias * first_on_diag for s in sel_scores(k0)]
        second = [s + (diag_bias * (1.0 - first_on_diag) + NEG_INF * first_on_diag) for s in sel_scores(k0 + tk)]
        flash_update(first, sel_values(k0))
        flash_update(second, sel_values(k0 + tk))
        o_sel = [flash_result(r) for r in range(rep)]
        flash_reset()
        n_back = C_WINDOW // tk
        win = []
        for back in range(n_back + 1):
            k0 = pl.multiple_of(jnp.maximum(q0 - back * tk, 0), tk)
            k_tile = kw_ref[0, pl.ds(k0, tk), gc]
            absent = jnp.where(blk >= back, 0.0, NEG_INF)
            edge = diag_bias if back == 0 else far_bias if back == n_back else None
            bias = absent if edge is None else edge + absent
            win.append(([_dot(k_tile, qh) + bias for qh in q_log2], vwt_ref[0, gc, pl.ds(k0, tk)]))
        for scores, values in win:
            flash_update(scores, values)
        for r, h in enumerate(heads):
            outs.append(sig_t[3 * h:3 * h + 1] * o_cmp[r] + sig_t[3 * h + 1:3 * h + 2] * o_sel[r]
                        + sig_t[3 * h + 2:3 * h + 3] * flash_result(r))
    for c in range(C_WIDTH // LANES):
        o_ref[0, :, c * LANES:(c + 1) * LANES] = jnp.transpose(jnp.concatenate(outs[2 * c:2 * c + 2], axis=0))


def _nsa_prompt(q, kcb, vcb, ks, vs_t, kw, vw_t, gates_t, n_cmp):
    b, t, wq = q.shape
    n_cmp_pad = kcb.shape[1]
    n_slc = -(-t // SEL_BLOCK)
    tile = NSA_TILE
    assert t % (2 * tile) == 0 and n_slc == HEAD_DIM and C_WINDOW % tile == 0
    wov_t = _overlap_weights(n_cmp, n_cmp_pad, n_slc, LANES).T
    vcb_t = jnp.swapaxes(vcb, 1, 2)
    onehot = jnp.asarray(np.arange(t)[:, None] // SEL_BLOCK == np.arange(HEAD_DIM)[None, :], BF16)
    ks_ext = jnp.concatenate([ks.reshape(b, t, C_KV_HEADS, HEAD_DIM),
                              jnp.broadcast_to(onehot[None, :, None, :], (b, t, C_KV_HEADS, HEAD_DIM))],
                             axis=-1).reshape(b, t, C_KV_HEADS * LANES)
    rep = C_HEADS // C_KV_HEADS
    whole = lambda a: pl.BlockSpec((1,) + a.shape[1:], lambda bi, i: (bi, 0, 0))
    return pl.pallas_call(
        functools.partial(_nsa_prompt_body, n_cmp=n_cmp, n_slc=n_slc),
        grid=(b, t // tile),
        in_specs=[pl.BlockSpec((1, tile, wq), lambda bi, i: (bi, i, 0)),
                  whole(kcb), whole(vcb_t), whole(ks_ext), whole(vs_t), whole(kw), whole(vw_t),
                  pl.BlockSpec((1, gates_t.shape[1], tile), lambda bi, i: (bi, 0, i)),
                  pl.BlockSpec(wov_t.shape, lambda bi, i: (0, 0))],
        out_specs=pl.BlockSpec((1, tile, wq), lambda bi, i: (bi, i, 0)),
        out_shape=jax.ShapeDtypeStruct((b, t, wq), F32),
        scratch_shapes=[pltpu.VMEM((rep, tile), F32), pltpu.VMEM((rep, HEAD_DIM + ONES_ROWS, tile), F32)],
        compiler_params=_params("parallel", "arbitrary"),
        name="nsa_prompt",
    )(q, kcb, vcb_t, ks_ext, vs_t, kw, vw_t, gates_t, wov_t)


def _block_diag(q, blk_of_row, n_blk):
    return jnp.concatenate([jnp.where(blk_of_row == b, q, 0.0) for b in range(n_blk)], axis=-1)


def _diag_blocks(r, blk_of_row, n_blk):
    out = jnp.where(blk_of_row == 0, r[:, 0:HEAD_DIM], 0.0)
    for b in range(1, n_blk):
        out = out + jnp.where(blk_of_row == b, r[:, b * HEAD_DIM:(b + 1) * HEAD_DIM], 0.0)
    return out


def _row_ids(n_rows, n_heads, n_blk):
    row = lax.broadcasted_iota(jnp.int32, (n_rows, 1), 0)
    step = _idiv(row, n_heads)
    blk = _idiv(_imod(row, n_heads), n_heads // n_blk)
    return step, blk


def _pattern_weight(dist, patterns):
    w = jnp.zeros(dist.shape, F32)
    for window, dil in patterns:
        w = w + jnp.where(_imod(dist, dil) == 0, jnp.where(dist <= window, 1.0, 0.0), 0.0)
    return jnp.where(dist >= 0, w, 0.0)


def _decode_body(*refs, n_heads, n_blk, patterns, gate_col):
    q_ref, kc_ref, vc_ref, kn_ref, vn_ref = refs[:5]
    g_ref = refs[5] if gate_col is not None else None
    o_ref = refs[-1]
    n_rows = q_ref.shape[1]
    n_cache = kc_ref.shape[2]
    n_new = kn_ref.shape[1]
    step, blk = _row_ids(n_rows, n_heads, n_blk)
    qbd = _block_diag(q_ref[0] * (HEAD_DIM ** -0.5), blk, n_blk)
    s_c = _dot(qbd.astype(BF16), kc_ref[0].astype(BF16))
    w_c = _pattern_weight(n_cache + step - lax.broadcasted_iota(jnp.int32, (n_rows, n_cache), 1), patterns)
    s_c = jnp.where(w_c > 0, s_c, NEG_INF)
    kn = kn_ref[0]
    vn = vn_ref[0]
    s_n, w_n = [], []
    for c in range(n_new):
        s = jnp.sum(qbd * kn[c:c + 1, :], axis=-1, keepdims=True)
        w = _pattern_weight(step - c, patterns)
        s_n.append(jnp.where(w > 0, s, NEG_INF))
        w_n.append(w)
    m = jnp.max(s_c, axis=-1, keepdims=True)
    for s in s_n:
        m = jnp.maximum(m, s)
    p_c = w_c * jnp.exp(s_c - m)
    l = jnp.sum(p_c, axis=-1, keepdims=True)
    r = _dot_nt(p_c.astype(BF16), vc_ref[0].astype(BF16))
    for c in range(n_new):
        p = w_n[c] * jnp.exp(s_n[c] - m)
        l = l + p
        r = r + p * vn[c:c + 1, :]
    o = _diag_blocks(r, blk, n_blk) / l
    if g_ref is not None:
        o = o * jax.nn.sigmoid(g_ref[0])[:, gate_col:gate_col + 1]
    o_ref[0] = o


def _decode_attention(q, kc, vc, kn, vn, *, n_heads, n_blk, patterns, gate=None, gate_col=None):
    n, n_rows, _ = q.shape
    args = [q, kc, vc, kn, vn]
    if gate is not None:
        args.append(gate)
    whole = lambda a: pl.BlockSpec((1,) + a.shape[1:], lambda i: (i, 0, 0))
    return pl.pallas_call(
        functools.partial(_decode_body, n_heads=n_heads, n_blk=n_blk, patterns=patterns,
                          gate_col=gate_col if gate is not None else None),
        grid=(n,),
        in_specs=[whole(a) for a in args],
        out_specs=pl.BlockSpec((1, n_rows, HEAD_DIM), lambda i: (i, 0, 0)),
        out_shape=jax.ShapeDtypeStruct((n, n_rows, HEAD_DIM), F32),
        compiler_params=_params("parallel"),
        name="decode_attention",
    )(*args)


def _nsa_decode_body(pt_ref, q_ref, kc_ref, vc_ref, ksp_ref, vsp_ref, kn_ref, vn_ref, g_ref, wov_ref, o_ref,
                     kbuf, vbuf, expand, ksem, vsem, *, n_pages, n_cmp, n_slc):
    n = pl.program_id(0)
    slot = n % 2
    n_rows = q_ref.shape[1]
    n_new = kn_ref.shape[1]
    n_past = kbuf.shape[2]
    n_slc_pad = wov_ref.shape[1]
    rep = C_HEADS // C_KV_HEADS

    def gather(seq, dst_slot, start):
        for pool_ref, buf, sem in ((ksp_ref, kbuf, ksem), (vsp_ref, vbuf, vsem)):
            for p in range(n_pages):
                page = pt_ref[seq, p] if start else 0
                cp = pltpu.make_async_copy(pool_ref.at[page], buf.at[dst_slot, :, pl.ds(p * PAGE_SIZE, PAGE_SIZE)],
                                           sem.at[dst_slot])
                if start:
                    cp.start()
                else:
                    cp.wait()

    @pl.when(n == 0)
    def _():
        gather(0, 0, True)
        step_cols = 1024
        for c0 in range(0, n_past, step_cols):
            shape = (n_slc_pad, min(step_cols, n_past - c0))
            blk_of_key = _idiv(c0 + lax.broadcasted_iota(jnp.int32, shape, 1), SEL_BLOCK)
            hit = blk_of_key == lax.broadcasted_iota(jnp.int32, shape, 0)
            expand[:, c0:c0 + shape[1]] = jnp.where(hit, 1.0, 0.0).astype(BF16)

    @pl.when(n + 1 < pl.num_programs(0))
    def _():
        gather(n + 1, 1 - slot, True)

    step, blk = _row_ids(n_rows, C_HEADS, C_KV_HEADS)
    qpos = n_past + step
    qbd = _block_diag(q_ref[0] * (HEAD_DIM ** -0.5), blk, C_KV_HEADS)
    qbd_in = qbd.astype(BF16)
    n_cmp_pad = kc_ref.shape[1]
    cmp_i = lax.broadcasted_iota(jnp.int32, (n_rows, n_cmp_pad), 1)
    cmp_ok = (cmp_i * CMP_STRIDE + (CMP_LEN - 1) <= qpos) & (cmp_i < n_cmp)
    s = jnp.where(cmp_ok, _dot_nt(qbd_in, kc_ref[0].astype(BF16)), NEG_INF)
    m = jnp.max(s, axis=-1, keepdims=True)
    p = jnp.where(cmp_ok, jnp.exp(s - m), 0.0)
    l = jnp.sum(p, axis=-1, keepdims=True)
    p = p / jnp.where(l > 0, l, 1.0)
    o_cmp = _diag_blocks(_dot(p.astype(BF16), vc_ref[0].astype(BF16)), blk, C_KV_HEADS)
    grp = (n_rows // rep, rep, n_cmp_pad)
    imp = jnp.broadcast_to(jnp.sum(p.reshape(grp), axis=1, keepdims=True), grp).reshape(n_rows, n_cmp_pad)
    sel_rows = _select_blocks(_split3_dot(imp, wov_ref[...]), qpos, n_slc)
    gather(n, slot, False)
    picked = _dot(sel_rows.astype(BF16), expand[...])
    kpos = lax.broadcasted_iota(jnp.int32, (n_rows, n_past), 1)
    s_p = jnp.where(picked > 0.5, jnp.where(kpos <= qpos, _dot(qbd_in, kbuf[slot].astype(BF16)), NEG_INF), NEG_INF)
    kn = kn_ref[0]
    vn = vn_ref[0]
    s_n = []
    for c in range(n_new):
        blk_c = (n_past + c) // SEL_BLOCK
        ok = (sel_rows[:, blk_c:blk_c + 1] > 0.5) & (n_past + c <= qpos)
        s_n.append(jnp.where(ok, jnp.sum(qbd * kn[c:c + 1, :], axis=-1, keepdims=True), NEG_INF))
    m = jnp.max(s_p, axis=-1, keepdims=True)
    for sc in s_n:
        m = jnp.maximum(m, sc)
    p_p = jnp.exp(s_p - m)
    l = jnp.sum(p_p, axis=-1, keepdims=True)
    r = _dot_nt(p_p.astype(BF16), vbuf[slot].astype(BF16))
    for c in range(n_new):
        pn = jnp.exp(s_n[c] - m)
        l = l + pn
        r = r + pn * vn[c:c + 1, :]
    o_sel = _diag_blocks(r, blk, C_KV_HEADS) / l
    sig = jax.nn.sigmoid(g_ref[0])
    o_ref[0] = sig[:, 0:1] * o_cmp + sig[:, 1:2] * o_sel


def _nsa_decode(q, kcb, vcb, pool_ks, pool_vs, page_table, kn, vn, gate, n_cmp):
    n, n_rows, _ = q.shape
    n_pages = page_table.shape[1]
    n_past = n_pages * PAGE_SIZE
    n_new = kn.shape[1]
    n_slc = -(-(n_past + n_new) // SEL_BLOCK)
    n_slc_pad = -(-n_slc // LANES) * LANES
    wov = _overlap_weights(n_cmp, kcb.shape[1], n_slc, n_slc_pad)
    whole = lambda a: pl.BlockSpec((1,) + a.shape[1:], lambda i, pt: (i, 0, 0))
    pools = [jnp.transpose(a, (0, 2, 3, 1)).reshape(a.shape[0], C_KV_WIDTH, PAGE_SIZE) for a in (pool_ks, pool_vs)]
    return pl.pallas_call(
        functools.partial(_nsa_decode_body, n_pages=n_pages, n_cmp=n_cmp, n_slc=n_slc),
        grid_spec=pltpu.PrefetchScalarGridSpec(
            num_scalar_prefetch=1,
            grid=(n,),
            in_specs=[whole(q), whole(kcb), whole(vcb), pl.BlockSpec(memory_space=pl.ANY),
                      pl.BlockSpec(memory_space=pl.ANY), whole(kn), whole(vn), whole(gate),
                      pl.BlockSpec(wov.shape, lambda i, pt: (0, 0))],
            out_specs=pl.BlockSpec((1, n_rows, HEAD_DIM), lambda i, pt: (i, 0, 0)),
            scratch_shapes=[pltpu.VMEM((2, C_KV_WIDTH, n_past), F32), pltpu.VMEM((2, C_KV_WIDTH, n_past), F32),
                            pltpu.VMEM((n_slc_pad, n_past), BF16),
                            pltpu.SemaphoreType.DMA((2,)), pltpu.SemaphoreType.DMA((2,))]),
        out_shape=jax.ShapeDtypeStruct((n, n_rows, HEAD_DIM), F32),
        compiler_params=_params("arbitrary"),
        name="nsa_decode",
    )(page_table, q, kcb, vcb, *pools, kn, vn, gate, wov)


def _odd_out_body(x_ref, *refs):
    *branch_refs, z_ref, w_ref, gf_ref, y_ref = refs
    o = branch_refs[0][...]
    for ref in branch_refs[1:]:
        o = o + ref[...]
    mixed = (o * _silu(z_ref[...])).astype(BF16)
    h = x_ref[...] + _dot(mixed, w_ref[...])
    y_ref[...] = h * lax.rsqrt(jnp.mean(h * h, axis=-1, keepdims=True) + RMS_EPS) * gf_ref[...]


def _odd_out(x2d, branches, z, w_out, final_gain, tm):
    m, d = x2d.shape
    row = pl.BlockSpec((tm, d), lambda i: (i, 0))
    return pl.pallas_call(
        _odd_out_body,
        grid=(m // tm,),
        in_specs=[row] * (len(branches) + 2) + [pl.BlockSpec(w_out.shape, lambda i: (0, 0)),
                                                 pl.BlockSpec((1, d), lambda i: (0, 0))],
        out_specs=row,
        out_shape=jax.ShapeDtypeStruct((m, d), F32),
        compiler_params=_params("parallel"),
        name="odd_out",
    )(x2d, *branches, z, w_out, final_gain.reshape(1, d))


def _even_outs(cache_from):
    a, bk = A_WIDTH, B_HEADS * B_DK
    outs = [(0, a, F32, None), (a, a, F32, None), (2 * a, a, F32, None)]
    if cache_from is not None:
        outs += [(a, a, F32, cache_from), (2 * a, a, F32, cache_from)]
    outs += [(3 * a, a, F32, None),
             (4 * a, bk, F32, None), (4 * a + bk, bk, F32, None),
             (4 * a + 2 * bk, B_WIDTH, F32, None), (4 * a + 2 * bk + B_WIDTH, B_WIDTH, F32, None)]
    return outs


def _from_feature_major(u, n_heads):
    b, _, length = u.shape
    return jnp.transpose(u.reshape(b, n_heads, HEAD_DIM, length), (0, 3, 1, 2))


def _to_feature_major(u):
    n, length, n_heads, hd = u.shape
    return jnp.transpose(u, (0, 2, 3, 1)).reshape(n, n_heads * hd, length)


def _even_layer_prompt(x, gain, w_in, w_out):
    b, t, d = x.shape
    x2d = x.reshape(b * t, d)
    keep = min(A_PATTERNS[-1][0], t)
    qa, ka, va, ka_t, va_t, za, qb, kb, vb, zb = _norm_proj(x2d, gain, w_in, _even_outs(t - keep), 512, t)
    seq = lambda u: u.reshape(b, t, u.shape[-1])
    oa = _dilated_attention(seq(qa), seq(ka), seq(va), A_PATTERNS)
    chunk = B_CHUNK if t % B_CHUNK == 0 else t
    s0 = jnp.zeros((b, B_HEADS, B_DK, B_DV), F32)
    ob, s_fin = _retention(seq(qb), seq(kb), seq(vb), s0, jnp.arange(t), chunk=chunk, c_real=chunk)
    y = _even_out(x2d, oa.reshape(b * t, A_WIDTH), za, ob.reshape(b * t, B_WIDTH), zb, w_out, 256)
    return y.reshape(b, t, d), _from_feature_major(ka_t, A_HEADS), _from_feature_major(va_t, A_HEADS), s_fin


def _even_layer_sample(x, cache_k, cache_v, state, gain, w_in, w_out, past_len):
    n, s_len, d = x.shape
    x2d = x.reshape(n * s_len, d)
    qa, ka32, va32, za, qb, kb, vb, zb = _norm_proj(x2d, gain, w_in, _even_outs(None), n * s_len, n * s_len)
    seq = lambda u: u.reshape(n, s_len, u.shape[-1])
    oa = _decode_attention(qa.reshape(n, s_len * A_HEADS, HEAD_DIM), _to_feature_major(cache_k),
                           _to_feature_major(cache_v), seq(ka32), seq(va32), n_heads=A_HEADS, n_blk=A_HEADS,
                           patterns=A_PATTERNS)
    oa = oa.reshape(n * s_len, A_WIDTH)
    chunk = 16
    pad = lambda u: jnp.pad(seq(u), ((0, 0), (0, chunk - s_len), (0, 0)))
    pos = past_len + jnp.arange(chunk)
    ob, s_new = _retention(pad(qb), pad(kb), pad(vb), state, pos, chunk=chunk, c_real=s_len)
    ob = ob[:, :s_len].reshape(n * s_len, B_WIDTH)
    y = _even_out(x2d, oa, za, ob, zb, w_out, n * s_len)
    heads = lambda u: u.reshape(n, s_len, A_HEADS, HEAD_DIM)
    return y.reshape(n, s_len, d), heads(ka32), heads(va32), s_new


ODD_KV = ("kc", "vc", "ks", "vs", "kw", "vw")
ODD_GATE_COL = C_WIDTH + len(ODD_KV) * C_KV_WIDTH
ODD_Z_COL = ODD_GATE_COL + 3 * C_HEADS


def _odd_outs_prompt(win_from):
    outs = [(0, C_WIDTH, BF16, None)]
    for i, name in enumerate(ODD_KV):
        off = C_WIDTH + i * C_KV_WIDTH
        outs.append((off, C_KV_WIDTH, BF16, 0 if name in ("vs", "vw") else None))
        outs.append((off, C_KV_WIDTH, F32, win_from if name in ("kw", "vw") else 0))
    outs += [(ODD_GATE_COL, 3 * C_HEADS, F32, 0), (ODD_Z_COL, C_WIDTH, F32, None)]
    return outs


def _odd_outs_sample():
    outs = [(0, C_WIDTH, BF16, None)]
    outs += [(C_WIDTH + i * C_KV_WIDTH, C_KV_WIDTH, F32, None) for i in range(len(ODD_KV))]
    outs += [(ODD_GATE_COL, 3 * C_HEADS, F32, None), (ODD_Z_COL, C_WIDTH, F32, None)]
    return outs


def _odd_layer_prompt(x, gain, w_in, w_out, cw_k, cw_v, final_gain):
    b, t, d = x.shape
    x2d = x.reshape(b * t, d)
    keep = min(C_WINDOW, t)
    (q, kc, kc_t, vc, vc_t, ks, ks_t, vs_tb, vs_t, kw, kw_t, vw_tb, vw_t, gl_t, z) = _norm_proj(
        x2d, gain, w_in, _odd_outs_prompt(t - keep), 512, t)
    seq = lambda u: u.reshape(b, t, u.shape[-1])
    n_cmp = (t - CMP_LEN) // CMP_STRIDE + 1
    pieces = lambda u: u.reshape(b, t // CMP_STRIDE, CMP_STRIDE * C_KV_WIDTH)
    kcb = _compress(pieces(kc), cw_k)
    vcb = _compress(pieces(vc), cw_v)
    o = _nsa_prompt(seq(q), kcb, vcb, seq(ks), vs_tb, seq(kw), vw_tb, gl_t, n_cmp)
    y = _odd_out(x2d, [o.reshape(b * t, C_WIDTH)], z, w_out, final_gain, 256)
    kv = lambda u: _from_feature_major(u, C_KV_HEADS)
    return y.reshape(b, t, d), kv(kc_t), kv(vc_t), kv(ks_t), kv(vs_t), kv(kw_t), kv(vw_t)


def _odd_layer_sample(x, page_table, pool_kc, pool_vc, pool_ks, pool_vs, buf_kw, buf_vw, gain, w_in, w_out,
                      cw_k, cw_v, final_gain):
    n, s_len, d = x.shape
    x2d = x.reshape(n * s_len, d)
    q, kc, vc, ks32, vs32, kw32, vw32, gl, z = _norm_proj(x2d, gain, w_in, _odd_outs_sample(), n * s_len, n * s_len)
    seq = lambda u: u.reshape(n, s_len, u.shape[-1])
    n_past = page_table.shape[1] * PAGE_SIZE
    n_cmp = (n_past + s_len - CMP_LEN) // CMP_STRIDE + 1
    assert CMP_STRIDE * (n_cmp - 1) + CMP_LEN <= n_past
    kcb = _compress_paged(pool_kc, page_table, cw_k)
    vcb = _compress_paged(pool_vc, page_table, cw_v)
    rows = lambda u: u.astype(F32).reshape(n, s_len * C_HEADS, HEAD_DIM)
    gate = gl.reshape(n, s_len * C_HEADS, 3)
    o_cs = _nsa_decode(rows(q), kcb, vcb, pool_ks, pool_vs, page_table, seq(ks32), seq(vs32), gate, n_cmp)
    o_win = _decode_attention(rows(q), _to_feature_major(buf_kw), _to_feature_major(buf_vw),
                              seq(kw32), seq(vw32), n_heads=C_HEADS, n_blk=C_KV_HEADS,
                              patterns=((C_WINDOW, 1),), gate=gate, gate_col=2)
    y = _odd_out(x2d, [o_cs.reshape(n * s_len, C_WIDTH), o_win.reshape(n * s_len, C_WIDTH)], z, w_out, final_gain,
                 n * s_len)
    kv = lambda u: u.reshape(n, s_len, C_KV_HEADS, HEAD_DIM)
    return y.reshape(n, s_len, d), kv(kc), kv(vc), kv(ks32), kv(vs32), kv(kw32), kv(vw32)


def kernel(x_prompt, x_sample, cache_a_k, cache_a_v, state_ret, cache_c_kcmp, cache_c_vcmp, cache_c_ksel,
           cache_c_vsel, cache_c_kwin, cache_c_vwin, page_table, even_norm, even_w_in, even_w_out, odd_norm,
           odd_w_in, odd_w_out, cmp_pos_k, cmp_w1_k, cmp_w2_k, cmp_pos_v, cmp_w1_v, cmp_w2_v, final_norm):
    assert even_norm.shape[0] == 1 and odd_norm.shape[0] == 1
    past_len = page_table.shape[1] * PAGE_SIZE
    w_in_e = even_w_in[0].T.astype(BF16)
    w_out_e = even_w_out[0].astype(BF16)
    w_in_o = odd_w_in[0].T.astype(BF16)
    w_out_o = odd_w_out[0].astype(BF16)
    cw_k = _compress_weights(cmp_pos_k[0], cmp_w1_k[0], cmp_w2_k[0])
    cw_v = _compress_weights(cmp_pos_v[0], cmp_w1_v[0], cmp_w2_v[0])

    hp, ak_p, av_p, ret_p = _even_layer_prompt(x_prompt, even_norm[0], w_in_e, w_out_e)
    hs, ak_s, av_s, ret_s = _even_layer_sample(x_sample, cache_a_k[0], cache_a_v[0], state_ret[0], even_norm[0],
                                               w_in_e, w_out_e, past_len)
    yp, *rows_p = _odd_layer_prompt(hp, odd_norm[0], w_in_o, w_out_o, cw_k, cw_v, final_norm)
    ys, *rows_s = _odd_layer_sample(hs, page_table, cache_c_kcmp[0], cache_c_vcmp[0], cache_c_ksel[0],
                                    cache_c_vsel[0], cache_c_kwin[0], cache_c_vwin[0], odd_norm[0], w_in_o, w_out_o,
                                    cw_k, cw_v, final_norm)
    lead = lambda u: u[None]
    out = [yp, ys, lead(ak_p), lead(ak_s), lead(av_p), lead(av_s), lead(ret_p), lead(ret_s)]
    for rp, rs in zip(rows_p, rows_s):
        out += [lead(rp), lead(rs)]
    return tuple(out)
```

```python
import functools

import numpy as np
import jax
import jax.numpy as jnp
from jax import lax
from jax.experimental import pallas as pl
from jax.experimental.pallas import tpu as pltpu

F32 = jnp.float32
BF16 = jnp.bfloat16

HEAD_DIM = 64
A_HEADS = 8
A_WIDTH = A_HEADS * HEAD_DIM
A_PATTERNS = ((128, 1), (512, 4), (2048, 16))
B_HEADS = 4
B_DK = 64
B_DV = 128
B_WIDTH = B_HEADS * B_DV
B_CHUNK = 128
ROPE_BASE = 10000.0
GN_EPS = 1e-5
C_HEADS = 16
C_KV_HEADS = 2
C_WIDTH = C_HEADS * HEAD_DIM
C_KV_WIDTH = C_KV_HEADS * HEAD_DIM
CMP_LEN = 32
CMP_STRIDE = 16
CMP_HIDDEN = 256
SEL_BLOCK = 64
SEL_TOPK = 16
SEL_FORCE = 1e9
C_WINDOW = 512
PAGE_SIZE = 128
RMS_EPS = 1e-6
NEG_INF = -1e30

LANES = 128
ATTN_TQ = 128
VMEM_LIMIT = 56 * 1024 * 1024


def _params(*sem):
    return pltpu.CompilerParams(dimension_semantics=sem, vmem_limit_bytes=VMEM_LIMIT)


def _dot(a, b):
    return jnp.dot(a, b, preferred_element_type=F32)


def _dot_nt(a, b):
    return lax.dot_general(a, b, (((1,), (1,)), ((), ())), preferred_element_type=F32)


def _dot_tn(a, b):
    return lax.dot_general(a, b, (((0,), (0,)), ((), ())), preferred_element_type=F32)


def _idiv(x, n):
    assert n & (n - 1) == 0
    return jnp.right_shift(x, n.bit_length() - 1)


def _imod(x, n):
    assert n & (n - 1) == 0
    return jnp.bitwise_and(x, n - 1)


def _split3_dot(a, w):
    hi = a.astype(BF16)
    r1 = a - hi.astype(F32)
    mid = r1.astype(BF16)
    lo = (r1 - mid.astype(F32)).astype(BF16)
    return _dot(hi, w) + _dot(mid, w) + _dot(lo, w)


BLOCK_TAGGED = "block_tagged"


def _norm_proj_body(x_ref, g_ref, wt_ref, *o_refs, outs, tiles, tm):
    x = x_ref[...]
    y = x * lax.rsqrt(jnp.mean(x * x, axis=-1, keepdims=True) + RMS_EPS) * g_ref[...]
    yb = y.astype(BF16)
    tile = pl.program_id(0) % tiles
    done = {}
    for o_ref, (off, width, _, keep_from) in zip(o_refs, outs):
        w_rows = wt_ref[off:off + width, :]
        if keep_from is None or keep_from == BLOCK_TAGGED:
            if (off, width) not in done:
                done[(off, width)] = _dot_nt(yb, w_rows)
            res = done[(off, width)]
            if keep_from is None:
                o_ref[...] = res.astype(o_ref.dtype)
            else:
                lane = lax.broadcasted_iota(jnp.int32, res.shape, 1)
                pos = tile * tm + lax.broadcasted_iota(jnp.int32, res.shape, 0)
                tag = jnp.where(_idiv(pos, SEL_BLOCK) == _imod(lane, HEAD_DIM), 1.0, 0.0)
                low = lane < HEAD_DIM
                o_ref[:, 0:LANES] = jnp.where(low, res, tag).astype(o_ref.dtype)
                o_ref[:, LANES:2 * LANES] = jnp.where(low, pltpu.roll(res, HEAD_DIM, 1), tag).astype(o_ref.dtype)
        elif keep_from == 0:
            if (off, width, 0) not in done:
                done[(off, width, 0)] = _dot_nt(w_rows, yb)
            o_ref[0] = done[(off, width, 0)].astype(o_ref.dtype)
        else:
            @pl.when(tile >= keep_from // tm)
            def _():
                o_ref[0] = _dot_nt(w_rows, yb).astype(o_ref.dtype)


def _norm_proj(x2d, gain, w_t, outs, tm, seq_len):
    m, d = x2d.shape
    n = w_t.shape[0]
    tiles = seq_len // tm
    assert seq_len % tm == 0 and m % seq_len == 0
    out_specs, out_shape = [], []
    for _, width, dt, keep_from in outs:
        if keep_from is None or keep_from == BLOCK_TAGGED:
            if keep_from == BLOCK_TAGGED:
                assert width == C_KV_WIDTH and seq_len <= SEL_BLOCK * HEAD_DIM
                width = 2 * LANES
            out_specs.append(pl.BlockSpec((tm, width), lambda i: (i, 0)))
            out_shape.append(jax.ShapeDtypeStruct((m, width), dt))
        else:
            assert keep_from % tm == 0
            first = keep_from // tm
            out_specs.append(pl.BlockSpec((1, width, tm),
                                          lambda i, first=first: (i // tiles, 0, jnp.maximum(i % tiles - first, 0))))
            out_shape.append(jax.ShapeDtypeStruct((m // seq_len, width, seq_len - keep_from), dt))
    return pl.pallas_call(
        functools.partial(_norm_proj_body, outs=outs, tiles=tiles, tm=tm),
        grid=(m // tm,),
        in_specs=[pl.BlockSpec((tm, d), lambda i: (i, 0)),
                  pl.BlockSpec((1, d), lambda i: (0, 0)),
                  pl.BlockSpec((n, d), lambda i: (0, 0))],
        out_specs=out_specs,
        out_shape=out_shape,
        compiler_params=_params("arbitrary"),
        name="norm_proj",
    )(x2d, gain.reshape(1, d), w_t)


DIL_GROUP = 4


def _dilated_body(q_ref, k_ref, v_ref, o_ref, o_sc, lse_sc, *, patterns):
    tq = ATTN_TQ
    blk_len = q_ref.shape[1]
    base = pl.program_id(2) * blk_len
    head_of_lane = _idiv(lax.broadcasted_iota(jnp.int32, (tq, LANES), 1), HEAD_DIM)
    own_lanes = [head_of_lane == h for h in range(LANES // HEAD_DIM)]

    def rows(ref, start, size, dil):
        return ref[0, pl.ds(start, size) if dil == 1 else pl.ds(start, size, stride=dil), :]

    for p_idx, (window, dil) in enumerate(patterns):
        lookback = window // dil
        n_prev = -(-lookback // tq)
        span = (n_prev + 1) * tq
        n_u = blk_len // (tq * dil)
        n_sub_blocks = dil * n_u
        assert blk_len % (tq * dil) == 0 and n_sub_blocks % DIL_GROUP == 0

        def trip(i, carry, dil=dil, lookback=lookback, n_prev=n_prev, span=span, p_idx=p_idx):
            work = []
            for j in range(DIL_GROUP):
                idx = i * DIL_GROUP + j
                res = _imod(idx, dil)
                u = _idiv(idx, dil)
                s0 = base // dil + u * tq
                ks = jnp.maximum(s0 - n_prev * tq, 0)
                q_row = res + dil * (u * tq)
                k_row = res + dil * ks
                q2 = rows(q_ref, q_row, tq, dil) * (HEAD_DIM ** -0.5)
                k2 = rows(k_ref, k_row, span, dil).astype(BF16)
                v2 = rows(v_ref, k_row, span, dil).astype(BF16)
                dist = (s0 + lax.broadcasted_iota(jnp.int32, (tq, span), 0)
                        - ks - lax.broadcasted_iota(jnp.int32, (tq, span), 1))
                bias = jnp.where(dist >= 0, jnp.where(dist <= lookback, 0.0, NEG_INF), NEG_INF)
                work.append((q_row, q2, k2, v2, bias))
            scores = [[_dot_nt(jnp.where(own, q2, 0.0).astype(BF16), k2) + bias for own in own_lanes]
                      for _, q2, k2, _, bias in work]
            maxes = [[jnp.max(s, axis=-1, keepdims=True) for s in ss] for ss in scores]
            probs = [[jnp.exp(s - m) for s, m in zip(ss, mm)] for ss, mm in zip(scores, maxes)]
            sums = [[jnp.sum(p, axis=-1, keepdims=True) for p in pp] for pp in probs]
            outs = [[_dot(p.astype(BF16), w[3]) for p in pp] for pp, w in zip(probs, work)]
            for w, oo, mm, ll in zip(work, outs, maxes, sums):
                o2 = oo[0] / ll[0]
                lse2 = jnp.broadcast_to(mm[0] + jnp.log(ll[0]), (tq, LANES))
                for own, o, m, l in list(zip(own_lanes, oo, mm, ll))[1:]:
                    o2 = jnp.where(own, o / l, o2)
                    lse2 = jnp.where(own, m + jnp.log(l), lse2)
                idx = pl.ds(w[0], tq) if dil == 1 else pl.ds(w[0], tq, stride=dil)
                o_sc[p_idx, idx, :] = o2
                lse_sc[p_idx, idx, :] = lse2
            return carry

        lax.fori_loop(0, n_sub_blocks // DIL_GROUP, trip, 0)

    lses = [lse_sc[p] for p in range(len(patterns))]
    top = lses[0]
    for l in lses[1:]:
        top = jnp.maximum(top, l)
    weights = [jnp.exp(l - top) for l in lses]
    num = weights[0] * o_sc[0]
    den = weights[0]
    for p in range(1, len(patterns)):
        num = num + weights[p] * o_sc[p]
        den = den + weights[p]
    o_ref[0] = (num / den).astype(o_ref.dtype)


def _dilated_attention(q, k, v, patterns):
    b, t, width = q.shape
    blk_len = min(t, max(ATTN_TQ * dil for _, dil in patterns))
    assert t % blk_len == 0
    for window, dil in patterns:
        assert t // dil >= (-(-(window // dil) // ATTN_TQ) + 1) * ATTN_TQ
    return pl.pallas_call(
        functools.partial(_dilated_body, patterns=patterns),
        grid=(b, width // LANES, t // blk_len),
        in_specs=[pl.BlockSpec((1, blk_len, LANES), lambda bi, hp, i: (bi, i, hp)),
                  pl.BlockSpec((1, t, LANES), lambda bi, hp, i: (bi, 0, hp)),
                  pl.BlockSpec((1, t, LANES), lambda bi, hp, i: (bi, 0, hp))],
        out_specs=pl.BlockSpec((1, blk_len, LANES), lambda bi, hp, i: (bi, i, hp)),
        out_shape=jax.ShapeDtypeStruct((b, t, width), BF16),
        scratch_shapes=[pltpu.VMEM((len(patterns), blk_len, LANES), F32),
                        pltpu.VMEM((len(patterns), blk_len, LANES), F32)],
        compiler_params=_params("parallel", "parallel", "arbitrary"),
        name="dilated_attention",
    )(q, k, v)


def _retention_body(q_ref, k_ref, v_ref, cos_ref, sin_ref, qd_ref, kd_ref, intra_ref, cd_ref, s0_ref,
                    o_ref, sfin_ref, state):
    c = pl.program_id(1)

    @pl.when(c == 0)
    def _():
        state[...] = s0_ref[0]

    cos = cos_ref[...]
    sin = sin_ref[...]
    width = cos.shape[-1]
    half = B_DK // 2
    lane = _imod(lax.broadcasted_iota(jnp.int32, cos.shape, 1), B_DK)

    def rotary(x):
        fwd = pltpu.roll(x, half, 1)
        bwd = pltpu.roll(x, width - half, 1)
        return x * cos + jnp.where(lane < half, -bwd, fwd) * sin

    q = rotary(q_ref[0])
    k = rotary(k_ref[0]) * (B_DK ** -0.5)
    q_in = q.astype(BF16)
    k_in = k.astype(BF16)
    q_st = (q * qd_ref[...]).astype(BF16)
    k_st = (k * kd_ref[...]).astype(BF16)
    v = v_ref[0].astype(BF16)
    kcs = [slice(h * B_DK, (h + 1) * B_DK) for h in range(B_HEADS)]
    vcs = [slice(h * B_DV, (h + 1) * B_DV) for h in range(B_HEADS)]
    states = [state[h] for h in range(B_HEADS)]
    scores = [_dot_nt(q_in[:, kc], k_in[:, kc]) * intra_ref[h] for h, kc in enumerate(kcs)]
    carried = [_dot(q_st[:, kc], st.astype(BF16)) for kc, st in zip(kcs, states)]
    updates = [_dot_tn(k_st[:, kc], v[:, vc]) for kc, vc in zip(kcs, vcs)]
    inner = [_dot(s.astype(BF16), v[:, vc]) for s, vc in zip(scores, vcs)]
    for h in range(B_HEADS):
        o_ref[0, :, vcs[h]] = (inner[h] + carried[h]).astype(o_ref.dtype)
        state[h] = states[h] * cd_ref[h] + updates[h]

    @pl.when(c == pl.num_programs(1) - 1)
    def _():
        sfin_ref[0] = state[...]


def _retention(q, k, v, s0, pos, *, chunk, c_real):
    n, t, _ = q.shape
    half = B_DK // 2
    freqs = ROPE_BASE ** (-jnp.arange(half, dtype=F32) / half)
    ang = pos.astype(F32)[:, None] * freqs[None, :]
    cos = jnp.tile(jnp.cos(ang), (1, 2 * B_HEADS))
    sin = jnp.tile(jnp.sin(ang), (1, 2 * B_HEADS))
    log_g = jnp.log1p(-jnp.exp2(-5.0 - jnp.arange(B_HEADS, dtype=F32)))
    i = jnp.arange(chunk, dtype=F32)
    diff = i[:, None] - i[None, :]
    intra = jnp.where(diff >= 0, jnp.exp(log_g[:, None, None] * jnp.maximum(diff, 0.0)), 0.0)
    q_decay = jnp.repeat(jnp.exp(log_g[None, :] * (i[:, None] + 1.0)), B_DK, axis=1)
    k_decay = jnp.repeat(jnp.exp(log_g[None, :] * (c_real - 1.0 - i[:, None])), B_DK, axis=1)
    chunk_decay = jnp.broadcast_to(jnp.exp(log_g * c_real)[:, None, None], (B_HEADS, 1, B_DV))
    wqk = B_HEADS * B_DK
    return pl.pallas_call(
        _retention_body,
        grid=(n, t // chunk),
        in_specs=[pl.BlockSpec((1, chunk, wqk), lambda b, c: (b, c, 0)),
                  pl.BlockSpec((1, chunk, wqk), lambda b, c: (b, c, 0)),
                  pl.BlockSpec((1, chunk, B_WIDTH), lambda b, c: (b, c, 0)),
                  pl.BlockSpec((chunk, wqk), lambda b, c: (c, 0)),
                  pl.BlockSpec((chunk, wqk), lambda b, c: (c, 0)),
                  pl.BlockSpec((chunk, wqk), lambda b, c: (0, 0)),
                  pl.BlockSpec((chunk, wqk), lambda b, c: (0, 0)),
                  pl.BlockSpec((B_HEADS, chunk, chunk), lambda b, c: (0, 0, 0)),
                  pl.BlockSpec((B_HEADS, 1, B_DV), lambda b, c: (0, 0, 0)),
                  pl.BlockSpec((1, B_HEADS, B_DK, B_DV), lambda b, c: (b, 0, 0, 0))],
        out_specs=[pl.BlockSpec((1, chunk, B_WIDTH), lambda b, c: (b, c, 0)),
                   pl.BlockSpec((1, B_HEADS, B_DK, B_DV), lambda b, c: (b, 0, 0, 0))],
        out_shape=[jax.ShapeDtypeStruct((n, t, B_WIDTH), BF16),
                   jax.ShapeDtypeStruct((n, B_HEADS, B_DK, B_DV), F32)],
        scratch_shapes=[pltpu.VMEM((B_HEADS, B_DK, B_DV), F32)],
        compiler_params=_params("parallel", "arbitrary"),
        name="retention",
    )(q, k, v, cos, sin, q_decay, k_decay, intra, chunk_decay, s0)


def _silu(z):
    return z * jax.nn.sigmoid(z)


def _even_out_body(x_ref, oa_ref, za_ref, ob_ref, zb_ref, w_ref, y_ref):
    ga = (oa_ref[...].astype(F32) * _silu(za_ref[...].astype(F32))).astype(BF16)
    acc = x_ref[...] + _dot(ga, w_ref[0:A_WIDTH, :])
    ob = ob_ref[...].astype(F32)
    zb = zb_ref[...].astype(F32)
    for h in range(B_HEADS):
        cols = slice(h * B_DV, (h + 1) * B_DV)
        seg = ob[:, cols]
        mu = jnp.mean(seg, axis=-1, keepdims=True)
        cen = seg - mu
        var = jnp.mean(cen * cen, axis=-1, keepdims=True)
        gb = (cen * lax.rsqrt(var + GN_EPS) * _silu(zb[:, cols])).astype(BF16)
        acc = acc + _dot(gb, w_ref[A_WIDTH + h * B_DV:A_WIDTH + (h + 1) * B_DV, :])
    y_ref[...] = acc


def _even_out(x2d, oa, za, ob, zb, w_out, tm):
    m, d = x2d.shape
    row = lambda width: pl.BlockSpec((tm, width), lambda i: (i, 0))
    return pl.pallas_call(
        _even_out_body,
        grid=(m // tm,),
        in_specs=[row(d), row(A_WIDTH), row(A_WIDTH), row(B_WIDTH), row(B_WIDTH),
                  pl.BlockSpec(w_out.shape, lambda i: (0, 0))],
        out_specs=row(d),
        out_shape=jax.ShapeDtypeStruct((m, d), F32),
        compiler_params=_params("parallel"),
        name="even_out",
    )(x2d, oa, za, ob, zb, w_out)


def _compress_math(x, p2_ref, wbig_ref, w2_ref):
    n_piece = x.shape[0]
    a = _dot(x.astype(BF16), wbig_ref[...])
    pc = _dot(p2_ref[...], wbig_ref[...])
    hid = []
    for g in range(C_KV_HEADS):
        lo = slice(g * 2 * CMP_HIDDEN, g * 2 * CMP_HIDDEN + CMP_HIDDEN)
        hi = slice(g * 2 * CMP_HIDDEN + CMP_HIDDEN, (g + 1) * 2 * CMP_HIDDEN)
        nxt = pltpu.roll(a[:, hi], n_piece - 1, 0)
        pos = (pc[0:1, lo] + pc[2:3, lo]) + (pc[1:2, hi] + pc[3:4, hi])
        hid.append(_silu(a[:, lo] + nxt + pos))
    return _dot(jnp.concatenate(hid, axis=-1).astype(BF16), w2_ref[...])


def _compress_body(x_ref, p2_ref, wbig_ref, w2_ref, o_ref):
    o_ref[0] = _compress_math(x_ref[0], p2_ref, wbig_ref, w2_ref)


def _compress_weights(pos_emb, w1, w2):
    per = CMP_LEN // CMP_STRIDE
    w1r = w1.reshape(per, CMP_STRIDE, HEAD_DIM, CMP_HIDDEN)
    eye = jnp.eye(C_KV_HEADS, dtype=w1.dtype)
    wbig = jnp.einsum('jldf,gh->lgdhjf', w1r, eye).reshape(CMP_STRIDE * C_KV_WIDTH, C_KV_HEADS * per * CMP_HIDDEN)
    w2bd = jnp.einsum('fd,gh->gfhd', w2, eye).reshape(C_KV_HEADS * CMP_HIDDEN, C_KV_WIDTH)
    halves = jnp.tile(pos_emb.reshape(per, CMP_STRIDE, 1, HEAD_DIM), (1, 1, C_KV_HEADS, 1)).reshape(per, -1)
    hi = halves.astype(BF16)
    lo = (halves - hi.astype(F32)).astype(BF16)
    p2 = jnp.concatenate([hi, lo, jnp.zeros((16 - 2 * per, halves.shape[1]), BF16)], axis=0)
    w1g = jnp.transpose(w1r, (1, 2, 0, 3)).reshape(CMP_STRIDE * HEAD_DIM, per * CMP_HIDDEN)
    halves_g = pos_emb.reshape(per, CMP_STRIDE * HEAD_DIM)
    hi_g = halves_g.astype(BF16)
    lo_g = (halves_g - hi_g.astype(F32)).astype(BF16)
    p2g = jnp.concatenate([hi_g, lo_g, jnp.zeros((16 - 2 * per, halves_g.shape[1]), BF16)], axis=0)
    return p2, wbig.astype(BF16), w2bd.astype(BF16), p2g, w1g.astype(BF16)


def _compress(pieces, cw):
    n, n_piece, width = pieces.shape
    p2, wbig, w2bd = cw[:3]
    const = lambda a: pl.BlockSpec(a.shape, lambda i: (0,) * a.ndim)
    return pl.pallas_call(
        _compress_body,
        grid=(n,),
        in_specs=[pl.BlockSpec((1, n_piece, width), lambda i: (i, 0, 0)), const(p2), const(wbig), const(w2bd)],
        out_specs=pl.BlockSpec((1, n_piece, C_KV_WIDTH), lambda i: (i, 0, 0)),
        out_shape=jax.ShapeDtypeStruct((n, n_piece, C_KV_WIDTH), F32),
        compiler_params=_params("parallel"),
        name="compress",
    )(pieces, p2, wbig, w2bd)


def _page_dma(pt_ref, pool_ref, buf_ref, sem_ref, n, slot, n_pages, start):
    for p in range(n_pages):
        page = pt_ref[n, p] if start else 0
        cp = pltpu.make_async_copy(pool_ref.at[page], buf_ref.at[slot, p], sem_ref.at[slot])
        if start:
            cp.start()
        else:
            cp.wait()


def _compress_paged_body(pt_ref, pool_ref, perm_ref, p2_ref, w1_ref, w2_ref, o_ref, buf, pieces, sem, *, n_pages):
    n = pl.program_id(0)
    slot = n % 2
    rows = PAGE_SIZE // CMP_STRIDE
    n_piece = n_pages * rows

    @pl.when(n == 0)
    def _():
        _page_dma(pt_ref, pool_ref, buf, sem, 0, 0, n_pages, True)

    @pl.when(n + 1 < pl.num_programs(0))
    def _():
        _page_dma(pt_ref, pool_ref, buf, sem, n + 1, 1 - slot, n_pages, True)

    _page_dma(pt_ref, pool_ref, buf, sem, n, slot, n_pages, False)
    perm = perm_ref[...]

    group = 16
    assert n_pages % group == 0 and CMP_STRIDE % 2 == 0
    low_half = lax.broadcasted_iota(jnp.int32, (rows, LANES), 1) < HEAD_DIM

    def unfold(i, carry):
        pairs = []
        for j in range(0, group, 2):
            two = buf[slot, pl.ds(i * group + j, 2)].reshape(2 * C_KV_WIDTH, PAGE_SIZE)
            pairs.append(_dot_nt(perm, two.astype(BF16)))
        for j in range(group):
            by_row = pairs[j // 2][:, (j % 2) * C_KV_WIDTH:(j % 2 + 1) * C_KV_WIDTH]
            r0 = pl.multiple_of((i * group + j) * rows, rows)
            for l in range(0, CMP_STRIDE, 2):
                even = by_row[l * rows:(l + 1) * rows, :]
                odd = by_row[(l + 1) * rows:(l + 2) * rows, :]
                lanes = slice(l * HEAD_DIM, (l + 2) * HEAD_DIM)
                swapped = pltpu.roll(jnp.where(low_half, odd, even), HEAD_DIM, 1)
                pieces[0, pl.ds(r0, rows), lanes] = jnp.where(low_half, even, swapped)
                pieces[1, pl.ds(r0, rows), lanes] = jnp.where(low_half, swapped, odd)
        return carry

    lax.fori_loop(0, n_pages // group, unfold, 0)
    hidden = CMP_HIDDEN
    pc = _dot(p2_ref[...], w1_ref[...])
    pos = (pc[0:1, :hidden] + pc[2:3, :hidden]) + (pc[1:2, hidden:] + pc[3:4, hidden:])
    acts = [_dot(pieces[g].astype(BF16), w1_ref[...]) for g in range(C_KV_HEADS)]
    hid = [_silu(a[:, :hidden] + pltpu.roll(a[:, hidden:], n_piece - 1, 0) + pos) for a in acts]
    o_ref[0] = _dot(jnp.concatenate(hid, axis=-1).astype(BF16), w2_ref[...])


def _compress_paged(pool, page_table, cw):
    n, n_pages = page_table.shape
    rows = PAGE_SIZE // CMP_STRIDE
    n_piece = n_pages * rows
    _, _, w2bd, p2g, w1g = cw
    perm = np.zeros((PAGE_SIZE, PAGE_SIZE), np.float32)
    for l in range(CMP_STRIDE):
        for piece in range(rows):
            perm[l * rows + piece, CMP_STRIDE * piece + l] = 1.0
    perm = jnp.asarray(perm, BF16)
    pages = jnp.transpose(pool, (0, 2, 3, 1)).reshape(pool.shape[0], C_KV_WIDTH, PAGE_SIZE)
    const = lambda a: pl.BlockSpec(a.shape, lambda i, pt: (0,) * a.ndim)
    return pl.pallas_call(
        functools.partial(_compress_paged_body, n_pages=n_pages),
        grid_spec=pltpu.PrefetchScalarGridSpec(
            num_scalar_prefetch=1,
            grid=(n,),
            in_specs=[pl.BlockSpec(memory_space=pl.ANY), const(perm), const(p2g), const(w1g), const(w2bd)],
            out_specs=pl.BlockSpec((1, n_piece, C_KV_WIDTH), lambda i, pt: (i, 0, 0)),
            scratch_shapes=[pltpu.VMEM((2, n_pages, C_KV_WIDTH, PAGE_SIZE), F32),
                            pltpu.VMEM((C_KV_HEADS, n_piece, CMP_STRIDE * HEAD_DIM), F32),
                            pltpu.SemaphoreType.DMA((2,))]),
        out_shape=jax.ShapeDtypeStruct((n, n_piece, C_KV_WIDTH), F32),
        compiler_params=_params("arbitrary"),
        name="compress_paged",
    )(page_table, pages, perm, p2g, w1g, w2bd)


def _overlap_weights(n_cmp, n_cmp_pad, n_slc, n_slc_pad):
    i = np.arange(n_cmp_pad)[:, None]
    j = np.arange(n_slc_pad)[None, :]
    ov = (np.minimum(i * CMP_STRIDE + CMP_LEN, (j + 1) * SEL_BLOCK) - np.maximum(i * CMP_STRIDE, j * SEL_BLOCK))
    w = np.where((i < n_cmp) & (j < n_slc), np.clip(ov, 0, None) / CMP_LEN, 0.0)
    return jnp.asarray(w, dtype=BF16)


def _select_blocks(p_slc, qpos, n_slc):
    shape = p_slc.shape
    lane = lax.broadcasted_iota(jnp.int32, shape, 1)
    cur = _idiv(qpos, SEL_BLOCK)
    forced = (lane == 0) | (lane == cur) | (lane == cur - 1)
    causal = lane * SEL_BLOCK <= qpos
    score = jnp.where(forced, SEL_FORCE, jnp.where(causal, p_slc, -SEL_FORCE))
    score = jnp.where(lane < n_slc, score, -3.0 * SEL_FORCE)
    lane_f = lane.astype(F32)
    sel = jnp.zeros(shape, F32)
    for _ in range(min(SEL_TOPK, n_slc)):
        m = jnp.max(score, axis=-1, keepdims=True)
        first = jnp.min(jnp.where(score == m, lane_f, 1e9), axis=-1, keepdims=True)
        hit = lane_f == first
        sel = jnp.where(hit, jnp.where(m > -0.5 * SEL_FORCE, 1.0, 0.0), sel)
        score = jnp.where(hit, -4.0 * SEL_FORCE, score)
    return sel


def _select_blocks_t(p_slc, qpos, n_slc):
    n_rows = -(-n_slc // 8) * 8
    shape = (n_rows, p_slc.shape[1])
    blk = lax.broadcasted_iota(jnp.int32, shape, 0)
    cur = _idiv(qpos, SEL_BLOCK)
    forced = (blk == 0) | (blk == cur) | (blk == cur - 1)
    causal = blk * SEL_BLOCK <= qpos
    score = jnp.where(forced, SEL_FORCE, jnp.where(causal, p_slc[:n_rows], -SEL_FORCE))
    score = jnp.where(blk < n_slc, score, -3.0 * SEL_FORCE)
    tiles = [score[8 * v:8 * v + 8] for v in range(n_rows // 8)]
    sub = lax.broadcasted_iota(jnp.int32, (8, shape[1]), 0)
    ranks = [jnp.zeros((8, shape[1]), F32) for _ in tiles]
    for i in range(n_slc):
        row = tiles[i // 8][i % 8:i % 8 + 1]
        for v, tile in enumerate(tiles):
            ge = jnp.where(row >= tile, 1.0, 0.0)
            gt = jnp.where(row > tile, 1.0, 0.0)
            if 8 * v > i:
                ranks[v] = ranks[v] + ge
            elif 8 * v + 7 <= i:
                ranks[v] = ranks[v] + gt
            else:
                ranks[v] = ranks[v] + jnp.where(sub > i - 8 * v, ge, gt)
    rank = jnp.concatenate(ranks, axis=0)
    return jnp.where(rank < min(SEL_TOPK, n_slc), jnp.where(score > -0.5 * SEL_FORCE, 1.0, 0.0), 0.0)


NSA_TILE = 2 * LANES
ONES_ROWS = 16
LOG2E = 1.4426950408889634


def _nsa_prompt_body(q_ref, kc_ref, vct_ref, ks_ref, vst_ref, kw_ref, vwt_ref, g_ref, wovt_ref, o_ref, m_sc, acc_sc,
                     *, n_cmp, n_slc):
    tq = tk = NSA_TILE
    rep = C_HEADS // C_KV_HEADS
    blk = pl.program_id(1)
    q0 = blk * tq
    n_cmp_pad = kc_ref.shape[1]
    sig_t = jax.nn.sigmoid(g_ref[0])
    q_t = [jnp.transpose(q_ref[0, :, c * LANES:(c + 1) * LANES].astype(F32))
           for c in range(C_WIDTH // LANES)]
    qpos = q0 + lax.broadcasted_iota(jnp.int32, (1, tq), 1)
    cmp_i = lax.broadcasted_iota(jnp.int32, (n_cmp_pad, tq), 0)
    cmp_ok = (cmp_i * CMP_STRIDE + (CMP_LEN - 1) <= qpos) & (cmp_i < n_cmp)
    cmp_bias = jnp.where(cmp_ok, 0.0, NEG_INF)
    cmp_keep = jnp.where(cmp_ok, 1.0, 0.0)
    ahead = lax.broadcasted_iota(jnp.int32, (tk, tq), 0) - lax.broadcasted_iota(jnp.int32, (tk, tq), 1)
    diag_bias = jnp.where(ahead <= 0, 0.0, NEG_INF)
    far_bias = jnp.where(ahead >= 0, 0.0, NEG_INF)
    ones_rows = jnp.where(lax.broadcasted_iota(jnp.int32, (ONES_ROWS, tk), 0) == 0, 1.0, 0.0).astype(BF16)

    def flash_update(scores, v_t):
        v_ext = jnp.concatenate([v_t, ones_rows], axis=0)
        for r, s in enumerate(scores):
            m_old = m_sc[r:r + 1, :]
            m_new = jnp.maximum(m_old, jnp.max(s, axis=0, keepdims=True))
            m_sc[r:r + 1, :] = m_new
            pv = _dot(v_ext, jnp.exp2(s - m_new).astype(BF16))
            acc_sc[r] = jnp.exp2(m_old - m_new) * acc_sc[r] + pv

    def flash_reset():
        m_sc[...] = jnp.full(m_sc.shape, NEG_INF, F32)
        acc_sc[...] = jnp.zeros(acc_sc.shape, F32)

    def flash_result(r):
        acc = acc_sc[r]
        return acc[:HEAD_DIM] / acc[HEAD_DIM:HEAD_DIM + 1]

    outs = []
    for g in range(C_KV_HEADS):
        gc = slice(g * HEAD_DIM, (g + 1) * HEAD_DIM)
        heads = [g * rep + r for r in range(rep)]
        q_heads = [q_t[h // 2][(h % 2) * HEAD_DIM:(h % 2 + 1) * HEAD_DIM] for h in heads]
        q_nat = [(qh * (HEAD_DIM ** -0.5)).astype(BF16) for qh in q_heads]
        q_log2 = [(qh * (HEAD_DIM ** -0.5 * LOG2E)).astype(BF16) for qh in q_heads]
        kcb = kc_ref[0, :, gc].astype(BF16)
        vct = vct_ref[0, gc, :].astype(BF16)
        scores = [_dot(kcb, q_nat[r]) + cmp_bias for r in range(rep)]
        probs = []
        for s in scores:
            p = jnp.exp(s - jnp.max(s, axis=0, keepdims=True)) * cmp_keep
            l = jnp.sum(p, axis=0, keepdims=True)
            probs.append(p / jnp.where(l > 0, l, 1.0))
        o_cmp = [_dot(vct, p.astype(BF16)) for p in probs]
        imp = probs[0]
        for p in probs[1:]:
            imp = imp + p
        hi = imp.astype(BF16)
        r1 = imp - hi.astype(F32)
        mid = r1.astype(BF16)
        lo = (r1 - mid.astype(F32)).astype(BF16)
        p_slc = _dot(wovt_ref[...], hi) + _dot(wovt_ref[...], mid) + _dot(wovt_ref[...], lo)
        sel_bias = jnp.where(_select_blocks_t(p_slc, qpos, n_slc) > 0.5, 0.0, NEG_INF).astype(BF16)
        q_sel = [jnp.concatenate([qh, sel_bias], axis=0) for qh in q_log2]
        flash_reset()

        def sel_scores(k0):
            k_tile = ks_ref[0, pl.ds(pl.multiple_of(k0, tk), tk), g * LANES:(g + 1) * LANES]
            return [_dot(k_tile, qs) for qs in q_sel]

        def sel_values(k0):
            return vst_ref[0, gc, pl.ds(pl.multiple_of(k0, tk), tk)]

        def past_pair(i, carry):
            k0 = i * (2 * tk)
            first, second = sel_scores(k0), sel_scores(k0 + tk)
            flash_update(first, sel_values(k0))
            flash_update(second, sel_values(k0 + tk))
            return carry

        lax.fori_loop(0, blk // 2, past_pair, 0)
        k0 = (blk // 2) * (2 * tk)
        first_on_diag = jnp.where(k0 == q0, 1.0, 0.0)
        first = [s + diag_bias * first_on_diag for s in sel_scores(k0)]
        second = [s + (diag_bias * (1.0 - first_on_diag) + NEG_INF * first_on_diag) for s in sel_scores(k0 + tk)]
        flash_update(first, sel_values(k0))
        flash_update(second, sel_values(k0 + tk))
        o_sel = [flash_result(r) for r in range(rep)]
        flash_reset()
        n_back = C_WINDOW // tk
        win = []
        for back in range(n_back + 1):
            k0 = pl.multiple_of(jnp.maximum(q0 - back * tk, 0), tk)
            k_tile = kw_ref[0, pl.ds(k0, tk), gc]
            absent = jnp.where(blk >= back, 0.0, NEG_INF)
            edge = diag_bias if back == 0 else far_bias if back == n_back else None
            bias = absent if edge is None else edge + absent
            win.append(([_dot(k_tile, qh) + bias for qh in q_log2], vwt_ref[0, gc, pl.ds(k0, tk)]))
        for scores, values in win:
            flash_update(scores, values)
        for r, h in enumerate(heads):
            outs.append(sig_t[3 * h:3 * h + 1] * o_cmp[r] + sig_t[3 * h + 1:3 * h + 2] * o_sel[r]
                        + sig_t[3 * h + 2:3 * h + 3] * flash_result(r))
    for c in range(C_WIDTH // LANES):
        pair = jnp.transpose(jnp.concatenate(outs[2 * c:2 * c + 2], axis=0))
        o_ref[0, :, c * LANES:(c + 1) * LANES] = pair.astype(o_ref.dtype)


def _nsa_prompt(q, kcb, vcb, ks_ext, vs_t, kw, vw_t, gates_t, n_cmp):
    b, t, wq = q.shape
    n_cmp_pad = kcb.shape[1]
    n_slc = -(-t // SEL_BLOCK)
    tile = NSA_TILE
    assert t % (2 * tile) == 0 and n_slc == HEAD_DIM and C_WINDOW % tile == 0
    wov_t = _overlap_weights(n_cmp, n_cmp_pad, n_slc, LANES).T
    vcb_t = jnp.swapaxes(vcb, 1, 2)
    rep = C_HEADS // C_KV_HEADS
    whole = lambda a: pl.BlockSpec((1,) + a.shape[1:], lambda bi, i: (bi, 0, 0))
    return pl.pallas_call(
        functools.partial(_nsa_prompt_body, n_cmp=n_cmp, n_slc=n_slc),
        grid=(b, t // tile),
        in_specs=[pl.BlockSpec((1, tile, wq), lambda bi, i: (bi, i, 0)),
                  whole(kcb), whole(vcb_t), whole(ks_ext), whole(vs_t), whole(kw), whole(vw_t),
                  pl.BlockSpec((1, gates_t.shape[1], tile), lambda bi, i: (bi, 0, i)),
                  pl.BlockSpec(wov_t.shape, lambda bi, i: (0, 0))],
        out_specs=pl.BlockSpec((1, tile, wq), lambda bi, i: (bi, i, 0)),
        out_shape=jax.ShapeDtypeStruct((b, t, wq), BF16),
        scratch_shapes=[pltpu.VMEM((rep, tile), F32), pltpu.VMEM((rep, HEAD_DIM + ONES_ROWS, tile), F32)],
        compiler_params=_params("parallel", "arbitrary"),
        name="nsa_prompt",
    )(q, kcb, vcb_t, ks_ext, vs_t, kw, vw_t, gates_t, wov_t)


def _block_diag(q, blk_of_row, n_blk):
    return jnp.concatenate([jnp.where(blk_of_row == b, q, 0.0) for b in range(n_blk)], axis=-1)


def _diag_blocks(r, blk_of_row, n_blk):
    out = jnp.where(blk_of_row == 0, r[:, 0:HEAD_DIM], 0.0)
    for b in range(1, n_blk):
        out = out + jnp.where(blk_of_row == b, r[:, b * HEAD_DIM:(b + 1) * HEAD_DIM], 0.0)
    return out


def _row_ids(n_rows, n_heads, n_blk):
    row = lax.broadcasted_iota(jnp.int32, (n_rows, 1), 0)
    step = _idiv(row, n_heads)
    blk = _idiv(_imod(row, n_heads), n_heads // n_blk)
    return step, blk


def _pattern_weight(dist, patterns):
    w = jnp.zeros(dist.shape, F32)
    for window, dil in patterns:
        w = w + jnp.where(_imod(dist, dil) == 0, jnp.where(dist <= window, 1.0, 0.0), 0.0)
    return jnp.where(dist >= 0, w, 0.0)


def _decode_body(*refs, n_heads, n_blk, patterns, gate_col):
    q_ref, kc_ref, vc_ref, kn_ref, vn_ref = refs[:5]
    g_ref = refs[5] if gate_col is not None else None
    o_ref = refs[-1]
    n_rows = q_ref.shape[1]
    n_cache = kc_ref.shape[2]
    n_new = kn_ref.shape[1]
    step, blk = _row_ids(n_rows, n_heads, n_blk)
    qbd = _block_diag(q_ref[0] * (HEAD_DIM ** -0.5), blk, n_blk)
    s_c = _dot(qbd.astype(BF16), kc_ref[0].astype(BF16))
    w_c = _pattern_weight(n_cache + step - lax.broadcasted_iota(jnp.int32, (n_rows, n_cache), 1), patterns)
    s_c = jnp.where(w_c > 0, s_c, NEG_INF)
    kn = kn_ref[0]
    vn = vn_ref[0]
    s_n, w_n = [], []
    for c in range(n_new):
        s = jnp.sum(qbd * kn[c:c + 1, :], axis=-1, keepdims=True)
        w = _pattern_weight(step - c, patterns)
        s_n.append(jnp.where(w > 0, s, NEG_INF))
        w_n.append(w)
    m = jnp.max(s_c, axis=-1, keepdims=True)
    for s in s_n:
        m = jnp.maximum(m, s)
    p_c = w_c * jnp.exp(s_c - m)
    l = jnp.sum(p_c, axis=-1, keepdims=True)
    r = _dot_nt(p_c.astype(BF16), vc_ref[0].astype(BF16))
    for c in range(n_new):
        p = w_n[c] * jnp.exp(s_n[c] - m)
        l = l + p
        r = r + p * vn[c:c + 1, :]
    o = _diag_blocks(r, blk, n_blk) / l
    if g_ref is not None:
        o = o * jax.nn.sigmoid(g_ref[0])[:, gate_col:gate_col + 1]
    o_ref[0] = o


def _decode_attention(q, kc, vc, kn, vn, *, n_heads, n_blk, patterns, gate=None, gate_col=None):
    n, n_rows, _ = q.shape
    args = [q, kc, vc, kn, vn]
    if gate is not None:
        args.append(gate)
    whole = lambda a: pl.BlockSpec((1,) + a.shape[1:], lambda i: (i, 0, 0))
    return pl.pallas_call(
        functools.partial(_decode_body, n_heads=n_heads, n_blk=n_blk, patterns=patterns,
                          gate_col=gate_col if gate is not None else None),
        grid=(n,),
        in_specs=[whole(a) for a in args],
        out_specs=pl.BlockSpec((1, n_rows, HEAD_DIM), lambda i: (i, 0, 0)),
        out_shape=jax.ShapeDtypeStruct((n, n_rows, HEAD_DIM), F32),
        compiler_params=_params("parallel"),
        name="decode_attention",
    )(*args)


def _nsa_decode_body(pt_ref, q_ref, kc_ref, vc_ref, ksp_ref, vsp_ref, kn_ref, vn_ref, g_ref, wov_ref, o_ref,
                     kbuf, vbuf, expand, ksem, vsem, *, n_pages, n_cmp, n_slc):
    n = pl.program_id(0)
    slot = n % 2
    n_rows = q_ref.shape[1]
    n_new = kn_ref.shape[1]
    n_past = kbuf.shape[2]
    n_slc_pad = wov_ref.shape[1]
    rep = C_HEADS // C_KV_HEADS

    def gather(seq, dst_slot, start):
        for pool_ref, buf, sem in ((ksp_ref, kbuf, ksem), (vsp_ref, vbuf, vsem)):
            for p in range(n_pages):
                page = pt_ref[seq, p] if start else 0
                cp = pltpu.make_async_copy(pool_ref.at[page], buf.at[dst_slot, :, pl.ds(p * PAGE_SIZE, PAGE_SIZE)],
                                           sem.at[dst_slot])
                if start:
                    cp.start()
                else:
                    cp.wait()

    @pl.when(n == 0)
    def _():
        gather(0, 0, True)
        step_cols = 1024
        for c0 in range(0, n_past, step_cols):
            shape = (n_slc_pad, min(step_cols, n_past - c0))
            blk_of_key = _idiv(c0 + lax.broadcasted_iota(jnp.int32, shape, 1), SEL_BLOCK)
            hit = blk_of_key == lax.broadcasted_iota(jnp.int32, shape, 0)
            expand[:, c0:c0 + shape[1]] = jnp.where(hit, 1.0, 0.0).astype(BF16)

    @pl.when(n + 1 < pl.num_programs(0))
    def _():
        gather(n + 1, 1 - slot, True)

    step, blk = _row_ids(n_rows, C_HEADS, C_KV_HEADS)
    qpos = n_past + step
    qbd = _block_diag(q_ref[0] * (HEAD_DIM ** -0.5), blk, C_KV_HEADS)
    qbd_in = qbd.astype(BF16)
    n_cmp_pad = kc_ref.shape[1]
    cmp_i = lax.broadcasted_iota(jnp.int32, (n_rows, n_cmp_pad), 1)
    cmp_ok = (cmp_i * CMP_STRIDE + (CMP_LEN - 1) <= qpos) & (cmp_i < n_cmp)
    s = jnp.where(cmp_ok, _dot_nt(qbd_in, kc_ref[0].astype(BF16)), NEG_INF)
    m = jnp.max(s, axis=-1, keepdims=True)
    p = jnp.where(cmp_ok, jnp.exp(s - m), 0.0)
    l = jnp.sum(p, axis=-1, keepdims=True)
    p = p / jnp.where(l > 0, l, 1.0)
    o_cmp = _diag_blocks(_dot(p.astype(BF16), vc_ref[0].astype(BF16)), blk, C_KV_HEADS)
    grp = (n_rows // rep, rep, n_cmp_pad)
    imp = jnp.broadcast_to(jnp.sum(p.reshape(grp), axis=1, keepdims=True), grp).reshape(n_rows, n_cmp_pad)
    sel_rows = _select_blocks(_split3_dot(imp, wov_ref[...]), qpos, n_slc)
    gather(n, slot, False)
    picked = _dot(sel_rows.astype(BF16), expand[...])
    kpos = lax.broadcasted_iota(jnp.int32, (n_rows, n_past), 1)
    s_p = jnp.where(picked > 0.5, jnp.where(kpos <= qpos, _dot(qbd_in, kbuf[slot].astype(BF16)), NEG_INF), NEG_INF)
    kn = kn_ref[0]
    vn = vn_ref[0]
    s_n = []
    for c in range(n_new):
        blk_c = (n_past + c) // SEL_BLOCK
        ok = (sel_rows[:, blk_c:blk_c + 1] > 0.5) & (n_past + c <= qpos)
        s_n.append(jnp.where(ok, jnp.sum(qbd * kn[c:c + 1, :], axis=-1, keepdims=True), NEG_INF))
    m = jnp.max(s_p, axis=-1, keepdims=True)
    for sc in s_n:
        m = jnp.maximum(m, sc)
    p_p = jnp.exp(s_p - m)
    l = jnp.sum(p_p, axis=-1, keepdims=True)
    r = _dot_nt(p_p.astype(BF16), vbuf[slot].astype(BF16))
    for c in range(n_new):
        pn = jnp.exp(s_n[c] - m)
        l = l + pn
        r = r + pn * vn[c:c + 1, :]
    o_sel = _diag_blocks(r, blk, C_KV_HEADS) / l
    sig = jax.nn.sigmoid(g_ref[0])
    o_ref[0] = sig[:, 0:1] * o_cmp + sig[:, 1:2] * o_sel


def _nsa_decode(q, kcb, vcb, pool_ks, pool_vs, page_table, kn, vn, gate, n_cmp):
    n, n_rows, _ = q.shape
    n_pages = page_table.shape[1]
    n_past = n_pages * PAGE_SIZE
    n_new = kn.shape[1]
    n_slc = -(-(n_past + n_new) // SEL_BLOCK)
    n_slc_pad = -(-n_slc // LANES) * LANES
    wov = _overlap_weights(n_cmp, kcb.shape[1], n_slc, n_slc_pad)
    whole = lambda a: pl.BlockSpec((1,) + a.shape[1:], lambda i, pt: (i, 0, 0))
    pools = [jnp.transpose(a, (0, 2, 3, 1)).reshape(a.shape[0], C_KV_WIDTH, PAGE_SIZE) for a in (pool_ks, pool_vs)]
    return pl.pallas_call(
        functools.partial(_nsa_decode_body, n_pages=n_pages, n_cmp=n_cmp, n_slc=n_slc),
        grid_spec=pltpu.PrefetchScalarGridSpec(
            num_scalar_prefetch=1,
            grid=(n,),
            in_specs=[whole(q), whole(kcb), whole(vcb), pl.BlockSpec(memory_space=pl.ANY),
                      pl.BlockSpec(memory_space=pl.ANY), whole(kn), whole(vn), whole(gate),
                      pl.BlockSpec(wov.shape, lambda i, pt: (0, 0))],
            out_specs=pl.BlockSpec((1, n_rows, HEAD_DIM), lambda i, pt: (i, 0, 0)),
            scratch_shapes=[pltpu.VMEM((2, C_KV_WIDTH, n_past), F32), pltpu.VMEM((2, C_KV_WIDTH, n_past), F32),
                            pltpu.VMEM((n_slc_pad, n_past), BF16),
                            pltpu.SemaphoreType.DMA((2,)), pltpu.SemaphoreType.DMA((2,))]),
        out_shape=jax.ShapeDtypeStruct((n, n_rows, HEAD_DIM), F32),
        compiler_params=_params("arbitrary"),
        name="nsa_decode",
    )(page_table, q, kcb, vcb, *pools, kn, vn, gate, wov)


def _odd_out_body(x_ref, *refs):
    *branch_refs, z_ref, w_ref, gf_ref, y_ref = refs
    o = branch_refs[0][...].astype(F32)
    for ref in branch_refs[1:]:
        o = o + ref[...].astype(F32)
    mixed = (o * _silu(z_ref[...].astype(F32))).astype(BF16)
    h = x_ref[...] + _dot(mixed, w_ref[...])
    y_ref[...] = h * lax.rsqrt(jnp.mean(h * h, axis=-1, keepdims=True) + RMS_EPS) * gf_ref[...]


def _odd_out(x2d, branches, z, w_out, final_gain, tm):
    m, d = x2d.shape
    row = pl.BlockSpec((tm, d), lambda i: (i, 0))
    return pl.pallas_call(
        _odd_out_body,
        grid=(m // tm,),
        in_specs=[row] * (len(branches) + 2) + [pl.BlockSpec(w_out.shape, lambda i: (0, 0)),
                                                 pl.BlockSpec((1, d), lambda i: (0, 0))],
        out_specs=row,
        out_shape=jax.ShapeDtypeStruct((m, d), F32),
        compiler_params=_params("parallel"),
        name="odd_out",
    )(x2d, *branches, z, w_out, final_gain.reshape(1, d))


def _even_outs(cache_from):
    a, bk = A_WIDTH, B_HEADS * B_DK
    outs = [(0, a, F32, None), (a, a, F32, None), (2 * a, a, F32, None)]
    if cache_from is not None:
        outs += [(a, a, F32, cache_from), (2 * a, a, F32, cache_from)]
    outs += [(3 * a, a, BF16, None),
             (4 * a, bk, F32, None), (4 * a + bk, bk, F32, None),
             (4 * a + 2 * bk, B_WIDTH, F32, None), (4 * a + 2 * bk + B_WIDTH, B_WIDTH, BF16, None)]
    return outs


def _from_feature_major(u, n_heads):
    b, _, length = u.shape
    return jnp.transpose(u.reshape(b, n_heads, HEAD_DIM, length), (0, 3, 1, 2))


def _to_feature_major(u):
    n, length, n_heads, hd = u.shape
    return jnp.transpose(u, (0, 2, 3, 1)).reshape(n, n_heads * hd, length)


def _even_layer_prompt(x, gain, w_in, w_out):
    b, t, d = x.shape
    x2d = x.reshape(b * t, d)
    keep = min(A_PATTERNS[-1][0], t)
    qa, ka, va, ka_t, va_t, za, qb, kb, vb, zb = _norm_proj(x2d, gain, w_in, _even_outs(t - keep), 512, t)
    seq = lambda u: u.reshape(b, t, u.shape[-1])
    oa = _dilated_attention(seq(qa), seq(ka), seq(va), A_PATTERNS)
    chunk = B_CHUNK if t % B_CHUNK == 0 else t
    s0 = jnp.zeros((b, B_HEADS, B_DK, B_DV), F32)
    ob, s_fin = _retention(seq(qb), seq(kb), seq(vb), s0, jnp.arange(t), chunk=chunk, c_real=chunk)
    y = _even_out(x2d, oa.reshape(b * t, A_WIDTH), za, ob.reshape(b * t, B_WIDTH), zb, w_out, 256)
    return y.reshape(b, t, d), _from_feature_major(ka_t, A_HEADS), _from_feature_major(va_t, A_HEADS), s_fin


def _even_layer_sample(x, cache_k, cache_v, state, gain, w_in, w_out, past_len):
    n, s_len, d = x.shape
    x2d = x.reshape(n * s_len, d)
    qa, ka32, va32, za, qb, kb, vb, zb = _norm_proj(x2d, gain, w_in, _even_outs(None), n * s_len, n * s_len)
    seq = lambda u: u.reshape(n, s_len, u.shape[-1])
    oa = _decode_attention(qa.reshape(n, s_len * A_HEADS, HEAD_DIM), _to_feature_major(cache_k),
                           _to_feature_major(cache_v), seq(ka32), seq(va32), n_heads=A_HEADS, n_blk=A_HEADS,
                           patterns=A_PATTERNS)
    oa = oa.reshape(n * s_len, A_WIDTH)
    chunk = 16
    pad = lambda u: jnp.pad(seq(u), ((0, 0), (0, chunk - s_len), (0, 0)))
    pos = past_len + jnp.arange(chunk)
    ob, s_new = _retention(pad(qb), pad(kb), pad(vb), state, pos, chunk=chunk, c_real=s_len)
    ob = ob[:, :s_len].reshape(n * s_len, B_WIDTH)
    y = _even_out(x2d, oa, za, ob, zb, w_out, n * s_len)
    heads = lambda u: u.reshape(n, s_len, A_HEADS, HEAD_DIM)
    return y.reshape(n, s_len, d), heads(ka32), heads(va32), s_new


ODD_KV = ("kc", "vc", "ks", "vs", "kw", "vw")
ODD_GATE_COL = C_WIDTH + len(ODD_KV) * C_KV_WIDTH
ODD_Z_COL = ODD_GATE_COL + 3 * C_HEADS


def _odd_outs_prompt(win_from):
    outs = [(0, C_WIDTH, BF16, None)]
    for i, name in enumerate(ODD_KV):
        off = C_WIDTH + i * C_KV_WIDTH
        outs.append((off, C_KV_WIDTH, BF16, 0 if name in ("vs", "vw") else BLOCK_TAGGED if name == "ks" else None))
        outs.append((off, C_KV_WIDTH, F32, win_from if name in ("kw", "vw") else 0))
    outs += [(ODD_GATE_COL, 3 * C_HEADS, F32, 0), (ODD_Z_COL, C_WIDTH, BF16, None)]
    return outs


def _odd_outs_sample():
    outs = [(0, C_WIDTH, BF16, None)]
    outs += [(C_WIDTH + i * C_KV_WIDTH, C_KV_WIDTH, F32, None) for i in range(len(ODD_KV))]
    outs += [(ODD_GATE_COL, 3 * C_HEADS, F32, None), (ODD_Z_COL, C_WIDTH, BF16, None)]
    return outs


def _odd_layer_prompt(x, gain, w_in, w_out, cw_k, cw_v, final_gain):
    b, t, d = x.shape
    x2d = x.reshape(b * t, d)
    keep = min(C_WINDOW, t)
    (q, kc, kc_t, vc, vc_t, ks, ks_t, vs_tb, vs_t, kw, kw_t, vw_tb, vw_t, gl_t, z) = _norm_proj(
        x2d, gain, w_in, _odd_outs_prompt(t - keep), 512, t)
    seq = lambda u: u.reshape(b, t, u.shape[-1])
    n_cmp = (t - CMP_LEN) // CMP_STRIDE + 1
    pieces = lambda u: u.reshape(b, t // CMP_STRIDE, CMP_STRIDE * C_KV_WIDTH)
    kcb = _compress(pieces(kc), cw_k)
    vcb = _compress(pieces(vc), cw_v)
    o = _nsa_prompt(seq(q), kcb, vcb, seq(ks), vs_tb, seq(kw), vw_tb, gl_t, n_cmp)
    y = _odd_out(x2d, [o.reshape(b * t, C_WIDTH)], z, w_out, final_gain, 256)
    kv = lambda u: _from_feature_major(u, C_KV_HEADS)
    return y.reshape(b, t, d), kv(kc_t), kv(vc_t), kv(ks_t), kv(vs_t), kv(kw_t), kv(vw_t)


def _odd_layer_sample(x, page_table, pool_kc, pool_vc, pool_ks, pool_vs, buf_kw, buf_vw, gain, w_in, w_out,
                      cw_k, cw_v, final_gain):
    n, s_len, d = x.shape
    x2d = x.reshape(n * s_len, d)
    q, kc, vc, ks32, vs32, kw32, vw32, gl, z = _norm_proj(x2d, gain, w_in, _odd_outs_sample(), n * s_len, n * s_len)
    seq = lambda u: u.reshape(n, s_len, u.shape[-1])
    n_past = page_table.shape[1] * PAGE_SIZE
    n_cmp = (n_past + s_len - CMP_LEN) // CMP_STRIDE + 1
    assert CMP_STRIDE * (n_cmp - 1) + CMP_LEN <= n_past
    kcb = _compress_paged(pool_kc, page_table, cw_k)
    vcb = _compress_paged(pool_vc, page_table, cw_v)
    rows = lambda u: u.astype(F32).reshape(n, s_len * C_HEADS, HEAD_DIM)
    gate = gl.reshape(n, s_len * C_HEADS, 3)
    o_cs = _nsa_decode(rows(q), kcb, vcb, pool_ks, pool_vs, page_table, seq(ks32), seq(vs32), gate, n_cmp)
    o_win = _decode_attention(rows(q), _to_feature_major(buf_kw), _to_feature_major(buf_vw),
                              seq(kw32), seq(vw32), n_heads=C_HEADS, n_blk=C_KV_HEADS,
                              patterns=((C_WINDOW, 1),), gate=gate, gate_col=2)
    y = _odd_out(x2d, [o_cs.reshape(n * s_len, C_WIDTH), o_win.reshape(n * s_len, C_WIDTH)], z, w_out, final_gain,
                 n * s_len)
    kv = lambda u: u.reshape(n, s_len, C_KV_HEADS, HEAD_DIM)
    return y.reshape(n, s_len, d), kv(kc), kv(vc), kv(ks32), kv(vs32), kv(kw32), kv(vw32)


def kernel(x_prompt, x_sample, cache_a_k, cache_a_v, state_ret, cache_c_kcmp, cache_c_vcmp, cache_c_ksel,
           cache_c_vsel, cache_c_kwin, cache_c_vwin, page_table, even_norm, even_w_in, even_w_out, odd_norm,
           odd_w_in, odd_w_out, cmp_pos_k, cmp_w1_k, cmp_w2_k, cmp_pos_v, cmp_w1_v, cmp_w2_v, final_norm):
    assert even_norm.shape[0] == 1 and odd_norm.shape[0] == 1
    past_len = page_table.shape[1] * PAGE_SIZE
    w_in_e = even_w_in[0].T.astype(BF16)
    w_out_e = even_w_out[0].astype(BF16)
    w_in_o = odd_w_in[0].T.astype(BF16)
    w_out_o = odd_w_out[0].astype(BF16)
    cw_k = _compress_weights(cmp_pos_k[0], cmp_w1_k[0], cmp_w2_k[0])
    cw_v = _compress_weights(cmp_pos_v[0], cmp_w1_v[0], cmp_w2_v[0])

    hp, ak_p, av_p, ret_p = _even_layer_prompt(x_prompt, even_norm[0], w_in_e, w_out_e)
    hs, ak_s, av_s, ret_s = _even_layer_sample(x_sample, cache_a_k[0], cache_a_v[0], state_ret[0], even_norm[0],
                                               w_in_e, w_out_e, past_len)
    yp, *rows_p = _odd_layer_prompt(hp, odd_norm[0], w_in_o, w_out_o, cw_k, cw_v, final_norm)
    ys, *rows_s = _odd_layer_sample(hs, page_table, cache_c_kcmp[0], cache_c_vcmp[0], cache_c_ksel[0],
                                    cache_c_vsel[0], cache_c_kwin[0], cache_c_vwin[0], odd_norm[0], w_in_o, w_out_o,
                                    cw_k, cw_v, final_norm)
    lead = lambda u: u[None]
    out = [yp, ys, lead(ak_p), lead(ak_s), lead(av_p), lead(av_s), lead(ret_p), lead(ret_s)]
    for rp, rs in zip(rows_p, rows_s):
        out += [lead(rp), lead(rs)]
    return tuple(out)
```

```python
import functools

import numpy as np
import jax
import jax.numpy as jnp
from jax import lax
from jax.experimental import pallas as pl
from jax.experimental.pallas import tpu as pltpu

F32 = jnp.float32
BF16 = jnp.bfloat16

HEAD_DIM = 64
A_HEADS = 8
A_WIDTH = A_HEADS * HEAD_DIM
A_PATTERNS = ((128, 1), (512, 4), (2048, 16))
B_HEADS = 4
B_DK = 64
B_DV = 128
B_WIDTH = B_HEADS * B_DV
B_CHUNK = 128
ROPE_BASE = 10000.0
GN_EPS = 1e-5
C_HEADS = 16
C_KV_HEADS = 2
C_WIDTH = C_HEADS * HEAD_DIM
C_KV_WIDTH = C_KV_HEADS * HEAD_DIM
CMP_LEN = 32
CMP_STRIDE = 16
CMP_HIDDEN = 256
SEL_BLOCK = 64
SEL_TOPK = 16
SEL_FORCE = 1e9
C_WINDOW = 512
PAGE_SIZE = 128
RMS_EPS = 1e-6
NEG_INF = -1e30

LANES = 128
ATTN_TQ = 128
VMEM_LIMIT = 56 * 1024 * 1024


def _params(*sem):
    return pltpu.CompilerParams(dimension_semantics=sem, vmem_limit_bytes=VMEM_LIMIT)


def _dot(a, b):
    return jnp.dot(a, b, preferred_element_type=F32)


def _dot_nt(a, b):
    return lax.dot_general(a, b, (((1,), (1,)), ((), ())), preferred_element_type=F32)


def _dot_tn(a, b):
    return lax.dot_general(a, b, (((0,), (0,)), ((), ())), preferred_element_type=F32)


def _idiv(x, n):
    assert n & (n - 1) == 0
    return jnp.right_shift(x, n.bit_length() - 1)


def _imod(x, n):
    assert n & (n - 1) == 0
    return jnp.bitwise_and(x, n - 1)


def _split3_dot(a, w):
    hi = a.astype(BF16)
    r1 = a - hi.astype(F32)
    mid = r1.astype(BF16)
    lo = (r1 - mid.astype(F32)).astype(BF16)
    return _dot(hi, w) + _dot(mid, w) + _dot(lo, w)


BLOCK_TAGGED = "block_tagged"


def _norm_proj_body(x_ref, g_ref, wt_ref, *o_refs, outs, tiles, tm):
    x = x_ref[...]
    y = x * lax.rsqrt(jnp.mean(x * x, axis=-1, keepdims=True) + RMS_EPS) * g_ref[...]
    yb = y.astype(BF16)
    tile = pl.program_id(0) % tiles
    done = {}
    for o_ref, (off, width, _, keep_from) in zip(o_refs, outs):
        w_rows = wt_ref[off:off + width, :]
        if keep_from is None or keep_from == BLOCK_TAGGED:
            if (off, width) not in done:
                done[(off, width)] = _dot_nt(yb, w_rows)
            res = done[(off, width)]
            if keep_from is None:
                o_ref[...] = res.astype(o_ref.dtype)
            else:
                lane = lax.broadcasted_iota(jnp.int32, res.shape, 1)
                pos = tile * tm + lax.broadcasted_iota(jnp.int32, res.shape, 0)
                tag = jnp.where(_idiv(pos, SEL_BLOCK) == _imod(lane, HEAD_DIM), 1.0, 0.0)
                low = lane < HEAD_DIM
                o_ref[:, 0:LANES] = jnp.where(low, res, tag).astype(o_ref.dtype)
                o_ref[:, LANES:2 * LANES] = jnp.where(low, pltpu.roll(res, HEAD_DIM, 1), tag).astype(o_ref.dtype)
        elif keep_from == 0:
            if (off, width, 0) not in done:
                done[(off, width, 0)] = _dot_nt(w_rows, yb)
            o_ref[0] = done[(off, width, 0)].astype(o_ref.dtype)
        else:
            @pl.when(tile >= keep_from // tm)
            def _():
                o_ref[0] = _dot_nt(w_rows, yb).astype(o_ref.dtype)


def _norm_proj(x2d, gain, w_t, outs, tm, seq_len):
    m, d = x2d.shape
    n = w_t.shape[0]
    tiles = seq_len // tm
    assert seq_len % tm == 0 and m % seq_len == 0
    out_specs, out_shape = [], []
    for _, width, dt, keep_from in outs:
        if keep_from is None or keep_from == BLOCK_TAGGED:
            if keep_from == BLOCK_TAGGED:
                assert width == C_KV_WIDTH and seq_len <= SEL_BLOCK * HEAD_DIM
                width = 2 * LANES
            out_specs.append(pl.BlockSpec((tm, width), lambda i: (i, 0)))
            out_shape.append(jax.ShapeDtypeStruct((m, width), dt))
        else:
            assert keep_from % tm == 0
            first = keep_from // tm
            out_specs.append(pl.BlockSpec((1, width, tm),
                                          lambda i, first=first: (i // tiles, 0, jnp.maximum(i % tiles - first, 0))))
            out_shape.append(jax.ShapeDtypeStruct((m // seq_len, width, seq_len - keep_from), dt))
    return pl.pallas_call(
        functools.partial(_norm_proj_body, outs=outs, tiles=tiles, tm=tm),
        grid=(m // tm,),
        in_specs=[pl.BlockSpec((tm, d), lambda i: (i, 0)),
                  pl.BlockSpec((1, d), lambda i: (0, 0)),
                  pl.BlockSpec((n, d), lambda i: (0, 0))],
        out_specs=out_specs,
        out_shape=out_shape,
        compiler_params=_params("arbitrary"),
        name="norm_proj",
    )(x2d, gain.reshape(1, d), w_t)


DIL_GROUP = 4


def _dilated_body(q_ref, k_ref, v_ref, o_ref, o_sc, lse_sc, *, patterns):
    tq = ATTN_TQ
    blk_len = q_ref.shape[1]
    base = pl.program_id(2) * blk_len
    head_of_lane = _idiv(lax.broadcasted_iota(jnp.int32, (tq, LANES), 1), HEAD_DIM)
    own_lanes = [head_of_lane == h for h in range(LANES // HEAD_DIM)]

    def rows(ref, start, size, dil):
        return ref[0, pl.ds(start, size) if dil == 1 else pl.ds(start, size, stride=dil), :]

    for p_idx, (window, dil) in enumerate(patterns):
        lookback = window // dil
        n_prev = -(-lookback // tq)
        span = (n_prev + 1) * tq
        n_u = blk_len // (tq * dil)
        n_sub_blocks = dil * n_u
        assert blk_len % (tq * dil) == 0 and n_sub_blocks % DIL_GROUP == 0

        def trip(i, carry, dil=dil, lookback=lookback, n_prev=n_prev, span=span, p_idx=p_idx):
            work = []
            for j in range(DIL_GROUP):
                idx = i * DIL_GROUP + j
                res = _imod(idx, dil)
                u = _idiv(idx, dil)
                s0 = base // dil + u * tq
                ks = jnp.maximum(s0 - n_prev * tq, 0)
                q_row = res + dil * (u * tq)
                k_row = res + dil * ks
                q2 = rows(q_ref, q_row, tq, dil) * (HEAD_DIM ** -0.5)
                k2 = rows(k_ref, k_row, span, dil).astype(BF16)
                v2 = rows(v_ref, k_row, span, dil).astype(BF16)
                dist = (s0 + lax.broadcasted_iota(jnp.int32, (tq, span), 0)
                        - ks - lax.broadcasted_iota(jnp.int32, (tq, span), 1))
                bias = jnp.where(dist >= 0, jnp.where(dist <= lookback, 0.0, NEG_INF), NEG_INF)
                work.append((q_row, q2, k2, v2, bias))
            scores = [[_dot_nt(jnp.where(own, q2, 0.0).astype(BF16), k2) + bias for own in own_lanes]
                      for _, q2, k2, _, bias in work]
            maxes = [[jnp.max(s, axis=-1, keepdims=True) for s in ss] for ss in scores]
            probs = [[jnp.exp(s - m) for s, m in zip(ss, mm)] for ss, mm in zip(scores, maxes)]
            sums = [[jnp.sum(p, axis=-1, keepdims=True) for p in pp] for pp in probs]
            outs = [[_dot(p.astype(BF16), w[3]) for p in pp] for pp, w in zip(probs, work)]
            for w, oo, mm, ll in zip(work, outs, maxes, sums):
                o2 = oo[0] * (1.0 / ll[0])
                lse2 = jnp.broadcast_to(mm[0] + jnp.log(ll[0]), (tq, LANES))
                for own, o, m, l in list(zip(own_lanes, oo, mm, ll))[1:]:
                    o2 = jnp.where(own, o * (1.0 / l), o2)
                    lse2 = jnp.where(own, m + jnp.log(l), lse2)
                idx = pl.ds(w[0], tq) if dil == 1 else pl.ds(w[0], tq, stride=dil)
                o_sc[p_idx, idx, :] = o2
                lse_sc[p_idx, idx, :] = lse2
            return carry

        lax.fori_loop(0, n_sub_blocks // DIL_GROUP, trip, 0)

    lses = [lse_sc[p] for p in range(len(patterns))]
    top = lses[0]
    for l in lses[1:]:
        top = jnp.maximum(top, l)
    weights = [jnp.exp(l - top) for l in lses]
    num = weights[0] * o_sc[0]
    den = weights[0]
    for p in range(1, len(patterns)):
        num = num + weights[p] * o_sc[p]
        den = den + weights[p]
    o_ref[0] = (num / den).astype(o_ref.dtype)


def _dilated_attention(q, k, v, patterns):
    b, t, width = q.shape
    blk_len = min(t, max(ATTN_TQ * dil for _, dil in patterns))
    assert t % blk_len == 0
    for window, dil in patterns:
        assert t // dil >= (-(-(window // dil) // ATTN_TQ) + 1) * ATTN_TQ
    return pl.pallas_call(
        functools.partial(_dilated_body, patterns=patterns),
        grid=(b, width // LANES, t // blk_len),
        in_specs=[pl.BlockSpec((1, blk_len, LANES), lambda bi, hp, i: (bi, i, hp)),
                  pl.BlockSpec((1, t, LANES), lambda bi, hp, i: (bi, 0, hp)),
                  pl.BlockSpec((1, t, LANES), lambda bi, hp, i: (bi, 0, hp))],
        out_specs=pl.BlockSpec((1, blk_len, LANES), lambda bi, hp, i: (bi, i, hp)),
        out_shape=jax.ShapeDtypeStruct((b, t, width), BF16),
        scratch_shapes=[pltpu.VMEM((len(patterns), blk_len, LANES), F32),
                        pltpu.VMEM((len(patterns), blk_len, LANES), F32)],
        compiler_params=_params("parallel", "parallel", "arbitrary"),
        name="dilated_attention",
    )(q, k, v)


def _retention_body(q_ref, k_ref, v_ref, cos_ref, sin_ref, qd_ref, kd_ref, intra_ref, cd_ref, s0_ref,
                    o_ref, sfin_ref, state):
    c = pl.program_id(1)

    @pl.when(c == 0)
    def _():
        state[...] = s0_ref[0]

    cos = cos_ref[...]
    sin = sin_ref[...]
    width = cos.shape[-1]
    half = B_DK // 2
    lane = _imod(lax.broadcasted_iota(jnp.int32, cos.shape, 1), B_DK)

    def rotary(x):
        fwd = pltpu.roll(x, half, 1)
        bwd = pltpu.roll(x, width - half, 1)
        return x * cos + jnp.where(lane < half, -bwd, fwd) * sin

    q = rotary(q_ref[0])
    k = rotary(k_ref[0]) * (B_DK ** -0.5)
    q_in = q.astype(BF16)
    k_in = k.astype(BF16)
    q_st = (q * qd_ref[...]).astype(BF16)
    k_st = (k * kd_ref[...]).astype(BF16)
    v = v_ref[0].astype(BF16)
    kcs = [slice(h * B_DK, (h + 1) * B_DK) for h in range(B_HEADS)]
    vcs = [slice(h * B_DV, (h + 1) * B_DV) for h in range(B_HEADS)]
    states = [state[h] for h in range(B_HEADS)]
    scores = [_dot_nt(q_in[:, kc], k_in[:, kc]) * intra_ref[h] for h, kc in enumerate(kcs)]
    carried = [_dot(q_st[:, kc], st.astype(BF16)) for kc, st in zip(kcs, states)]
    updates = [_dot_tn(k_st[:, kc], v[:, vc]) for kc, vc in zip(kcs, vcs)]
    inner = [_dot(s.astype(BF16), v[:, vc]) for s, vc in zip(scores, vcs)]
    for h in range(B_HEADS):
        o_ref[0, :, vcs[h]] = (inner[h] + carried[h]).astype(o_ref.dtype)
        state[h] = states[h] * cd_ref[h] + updates[h]

    @pl.when(c == pl.num_programs(1) - 1)
    def _():
        sfin_ref[0] = state[...]


def _retention(q, k, v, s0, pos, *, chunk, c_real):
    n, t, _ = q.shape
    half = B_DK // 2
    freqs = ROPE_BASE ** (-jnp.arange(half, dtype=F32) / half)
    ang = pos.astype(F32)[:, None] * freqs[None, :]
    cos = jnp.tile(jnp.cos(ang), (1, 2 * B_HEADS))
    sin = jnp.tile(jnp.sin(ang), (1, 2 * B_HEADS))
    log_g = jnp.log1p(-jnp.exp2(-5.0 - jnp.arange(B_HEADS, dtype=F32)))
    i = jnp.arange(chunk, dtype=F32)
    diff = i[:, None] - i[None, :]
    intra = jnp.where(diff >= 0, jnp.exp(log_g[:, None, None] * jnp.maximum(diff, 0.0)), 0.0)
    q_decay = jnp.repeat(jnp.exp(log_g[None, :] * (i[:, None] + 1.0)), B_DK, axis=1)
    k_decay = jnp.repeat(jnp.exp(log_g[None, :] * (c_real - 1.0 - i[:, None])), B_DK, axis=1)
    chunk_decay = jnp.broadcast_to(jnp.exp(log_g * c_real)[:, None, None], (B_HEADS, 1, B_DV))
    wqk = B_HEADS * B_DK
    return pl.pallas_call(
        _retention_body,
        grid=(n, t // chunk),
        in_specs=[pl.BlockSpec((1, chunk, wqk), lambda b, c: (b, c, 0)),
                  pl.BlockSpec((1, chunk, wqk), lambda b, c: (b, c, 0)),
                  pl.BlockSpec((1, chunk, B_WIDTH), lambda b, c: (b, c, 0)),
                  pl.BlockSpec((chunk, wqk), lambda b, c: (c, 0)),
                  pl.BlockSpec((chunk, wqk), lambda b, c: (c, 0)),
                  pl.BlockSpec((chunk, wqk), lambda b, c: (0, 0)),
                  pl.BlockSpec((chunk, wqk), lambda b, c: (0, 0)),
                  pl.BlockSpec((B_HEADS, chunk, chunk), lambda b, c: (0, 0, 0)),
                  pl.BlockSpec((B_HEADS, 1, B_DV), lambda b, c: (0, 0, 0)),
                  pl.BlockSpec((1, B_HEADS, B_DK, B_DV), lambda b, c: (b, 0, 0, 0))],
        out_specs=[pl.BlockSpec((1, chunk, B_WIDTH), lambda b, c: (b, c, 0)),
                   pl.BlockSpec((1, B_HEADS, B_DK, B_DV), lambda b, c: (b, 0, 0, 0))],
        out_shape=[jax.ShapeDtypeStruct((n, t, B_WIDTH), BF16),
                   jax.ShapeDtypeStruct((n, B_HEADS, B_DK, B_DV), F32)],
        scratch_shapes=[pltpu.VMEM((B_HEADS, B_DK, B_DV), F32)],
        compiler_params=_params("parallel", "arbitrary"),
        name="retention",
    )(q, k, v, cos, sin, q_decay, k_decay, intra, chunk_decay, s0)


def _silu(z):
    return z * jax.nn.sigmoid(z)


def _even_out_body(x_ref, oa_ref, za_ref, ob_ref, zb_ref, w_ref, y_ref):
    ga = (oa_ref[...].astype(F32) * _silu(za_ref[...].astype(F32))).astype(BF16)
    acc = x_ref[...] + _dot(ga, w_ref[0:A_WIDTH, :])
    ob = ob_ref[...].astype(F32)
    zb = zb_ref[...].astype(F32)
    for h in range(B_HEADS):
        cols = slice(h * B_DV, (h + 1) * B_DV)
        seg = ob[:, cols]
        mu = jnp.mean(seg, axis=-1, keepdims=True)
        cen = seg - mu
        var = jnp.mean(cen * cen, axis=-1, keepdims=True)
        gb = (cen * lax.rsqrt(var + GN_EPS) * _silu(zb[:, cols])).astype(BF16)
        acc = acc + _dot(gb, w_ref[A_WIDTH + h * B_DV:A_WIDTH + (h + 1) * B_DV, :])
    y_ref[...] = acc


def _even_out(x2d, oa, za, ob, zb, w_out, tm):
    m, d = x2d.shape
    row = lambda width: pl.BlockSpec((tm, width), lambda i: (i, 0))
    return pl.pallas_call(
        _even_out_body,
        grid=(m // tm,),
        in_specs=[row(d), row(A_WIDTH), row(A_WIDTH), row(B_WIDTH), row(B_WIDTH),
                  pl.BlockSpec(w_out.shape, lambda i: (0, 0))],
        out_specs=row(d),
        out_shape=jax.ShapeDtypeStruct((m, d), F32),
        compiler_params=_params("parallel"),
        name="even_out",
    )(x2d, oa, za, ob, zb, w_out)


def _compress_math(x, p2_ref, wbig_ref, w2_ref):
    n_piece = x.shape[0]
    a = _dot(x.astype(BF16), wbig_ref[...])
    pc = _dot(p2_ref[...], wbig_ref[...])
    hid = []
    for g in range(C_KV_HEADS):
        lo = slice(g * 2 * CMP_HIDDEN, g * 2 * CMP_HIDDEN + CMP_HIDDEN)
        hi = slice(g * 2 * CMP_HIDDEN + CMP_HIDDEN, (g + 1) * 2 * CMP_HIDDEN)
        nxt = pltpu.roll(a[:, hi], n_piece - 1, 0)
        pos = (pc[0:1, lo] + pc[2:3, lo]) + (pc[1:2, hi] + pc[3:4, hi])
        hid.append(_silu(a[:, lo] + nxt + pos))
    return _dot(jnp.concatenate(hid, axis=-1).astype(BF16), w2_ref[...])


def _compress_body(x_ref, p2_ref, wbig_ref, w2_ref, o_ref):
    o_ref[0] = _compress_math(x_ref[0], p2_ref, wbig_ref, w2_ref)


def _compress_weights(pos_emb, w1, w2):
    per = CMP_LEN // CMP_STRIDE
    w1r = w1.reshape(per, CMP_STRIDE, HEAD_DIM, CMP_HIDDEN)
    eye = jnp.eye(C_KV_HEADS, dtype=w1.dtype)
    wbig = jnp.einsum('jldf,gh->lgdhjf', w1r, eye).reshape(CMP_STRIDE * C_KV_WIDTH, C_KV_HEADS * per * CMP_HIDDEN)
    w2bd = jnp.einsum('fd,gh->gfhd', w2, eye).reshape(C_KV_HEADS * CMP_HIDDEN, C_KV_WIDTH)
    halves = jnp.tile(pos_emb.reshape(per, CMP_STRIDE, 1, HEAD_DIM), (1, 1, C_KV_HEADS, 1)).reshape(per, -1)
    hi = halves.astype(BF16)
    lo = (halves - hi.astype(F32)).astype(BF16)
    p2 = jnp.concatenate([hi, lo, jnp.zeros((16 - 2 * per, halves.shape[1]), BF16)], axis=0)
    w1g = jnp.transpose(w1r, (1, 2, 0, 3)).reshape(CMP_STRIDE * HEAD_DIM, per * CMP_HIDDEN)
    halves_g = pos_emb.reshape(per, CMP_STRIDE * HEAD_DIM)
    hi_g = halves_g.astype(BF16)
    lo_g = (halves_g - hi_g.astype(F32)).astype(BF16)
    p2g = jnp.concatenate([hi_g, lo_g, jnp.zeros((16 - 2 * per, halves_g.shape[1]), BF16)], axis=0)
    return p2, wbig.astype(BF16), w2bd.astype(BF16), p2g, w1g.astype(BF16)


def _compress(pieces, cw):
    n, n_piece, width = pieces.shape
    p2, wbig, w2bd = cw[:3]
    const = lambda a: pl.BlockSpec(a.shape, lambda i: (0,) * a.ndim)
    return pl.pallas_call(
        _compress_body,
        grid=(n,),
        in_specs=[pl.BlockSpec((1, n_piece, width), lambda i: (i, 0, 0)), const(p2), const(wbig), const(w2bd)],
        out_specs=pl.BlockSpec((1, n_piece, C_KV_WIDTH), lambda i: (i, 0, 0)),
        out_shape=jax.ShapeDtypeStruct((n, n_piece, C_KV_WIDTH), F32),
        compiler_params=_params("parallel"),
        name="compress",
    )(pieces, p2, wbig, w2bd)


def _page_dma(pt_ref, pool_ref, buf_ref, sem_ref, n, slot, n_pages, start):
    for p in range(n_pages):
        page = pt_ref[n, p] if start else 0
        cp = pltpu.make_async_copy(pool_ref.at[page], buf_ref.at[slot, p], sem_ref.at[slot])
        if start:
            cp.start()
        else:
            cp.wait()


def _compress_paged_body(pt_ref, pool_ref, perm_ref, p2_ref, w1_ref, w2_ref, o_ref, buf, pieces, sem, *, n_pages):
    n = pl.program_id(0)
    slot = n % 2
    rows = PAGE_SIZE // CMP_STRIDE
    n_piece = n_pages * rows

    @pl.when(n == 0)
    def _():
        _page_dma(pt_ref, pool_ref, buf, sem, 0, 0, n_pages, True)

    @pl.when(n + 1 < pl.num_programs(0))
    def _():
        _page_dma(pt_ref, pool_ref, buf, sem, n + 1, 1 - slot, n_pages, True)

    _page_dma(pt_ref, pool_ref, buf, sem, n, slot, n_pages, False)
    perm = perm_ref[...]

    group = 16
    assert n_pages % group == 0 and CMP_STRIDE % 2 == 0
    low_half = lax.broadcasted_iota(jnp.int32, (rows, LANES), 1) < HEAD_DIM

    def unfold(i, carry):
        pairs = []
        for j in range(0, group, 2):
            two = buf[slot, pl.ds(i * group + j, 2)].reshape(2 * C_KV_WIDTH, PAGE_SIZE)
            pairs.append(_dot_nt(perm, two.astype(BF16)))
        for j in range(group):
            by_row = pairs[j // 2][:, (j % 2) * C_KV_WIDTH:(j % 2 + 1) * C_KV_WIDTH]
            r0 = pl.multiple_of((i * group + j) * rows, rows)
            for l in range(0, CMP_STRIDE, 2):
                even = by_row[l * rows:(l + 1) * rows, :]
                odd = by_row[(l + 1) * rows:(l + 2) * rows, :]
                lanes = slice(l * HEAD_DIM, (l + 2) * HEAD_DIM)
                swapped = pltpu.roll(jnp.where(low_half, odd, even), HEAD_DIM, 1)
                pieces[0, pl.ds(r0, rows), lanes] = jnp.where(low_half, even, swapped)
                pieces[1, pl.ds(r0, rows), lanes] = jnp.where(low_half, swapped, odd)
        return carry

    lax.fori_loop(0, n_pages // group, unfold, 0)
    hidden = CMP_HIDDEN
    pc = _dot(p2_ref[...], w1_ref[...])
    pos = (pc[0:1, :hidden] + pc[2:3, :hidden]) + (pc[1:2, hidden:] + pc[3:4, hidden:])
    acts = [_dot(pieces[g].astype(BF16), w1_ref[...]) for g in range(C_KV_HEADS)]
    hid = [_silu(a[:, :hidden] + pltpu.roll(a[:, hidden:], n_piece - 1, 0) + pos) for a in acts]
    o_ref[0] = _dot(jnp.concatenate(hid, axis=-1).astype(BF16), w2_ref[...])


def _compress_paged(pool, page_table, cw):
    n, n_pages = page_table.shape
    rows = PAGE_SIZE // CMP_STRIDE
    n_piece = n_pages * rows
    _, _, w2bd, p2g, w1g = cw
    perm = np.zeros((PAGE_SIZE, PAGE_SIZE), np.float32)
    for l in range(CMP_STRIDE):
        for piece in range(rows):
            perm[l * rows + piece, CMP_STRIDE * piece + l] = 1.0
    perm = jnp.asarray(perm, BF16)
    pages = jnp.transpose(pool, (0, 2, 3, 1)).reshape(pool.shape[0], C_KV_WIDTH, PAGE_SIZE)
    const = lambda a: pl.BlockSpec(a.shape, lambda i, pt: (0,) * a.ndim)
    return pl.pallas_call(
        functools.partial(_compress_paged_body, n_pages=n_pages),
        grid_spec=pltpu.PrefetchScalarGridSpec(
            num_scalar_prefetch=1,
            grid=(n,),
            in_specs=[pl.BlockSpec(memory_space=pl.ANY), const(perm), const(p2g), const(w1g), const(w2bd)],
            out_specs=pl.BlockSpec((1, n_piece, C_KV_WIDTH), lambda i, pt: (i, 0, 0)),
            scratch_shapes=[pltpu.VMEM((2, n_pages, C_KV_WIDTH, PAGE_SIZE), F32),
                            pltpu.VMEM((C_KV_HEADS, n_piece, CMP_STRIDE * HEAD_DIM), F32),
                            pltpu.SemaphoreType.DMA((2,))]),
        out_shape=jax.ShapeDtypeStruct((n, n_piece, C_KV_WIDTH), F32),
        compiler_params=_params("arbitrary"),
        name="compress_paged",
    )(page_table, pages, perm, p2g, w1g, w2bd)


def _overlap_weights(n_cmp, n_cmp_pad, n_slc, n_slc_pad):
    i = np.arange(n_cmp_pad)[:, None]
    j = np.arange(n_slc_pad)[None, :]
    ov = (np.minimum(i * CMP_STRIDE + CMP_LEN, (j + 1) * SEL_BLOCK) - np.maximum(i * CMP_STRIDE, j * SEL_BLOCK))
    w = np.where((i < n_cmp) & (j < n_slc), np.clip(ov, 0, None) / CMP_LEN, 0.0)
    return jnp.asarray(w, dtype=BF16)


def _select_blocks(p_slc, qpos, n_slc):
    shape = p_slc.shape
    lane = lax.broadcasted_iota(jnp.int32, shape, 1)
    cur = _idiv(qpos, SEL_BLOCK)
    forced = (lane == 0) | (lane == cur) | (lane == cur - 1)
    causal = lane * SEL_BLOCK <= qpos
    score = jnp.where(forced, SEL_FORCE, jnp.where(causal, p_slc, -SEL_FORCE))
    score = jnp.where(lane < n_slc, score, -3.0 * SEL_FORCE)
    lane_f = lane.astype(F32)
    sel = jnp.zeros(shape, F32)
    for _ in range(min(SEL_TOPK, n_slc)):
        m = jnp.max(score, axis=-1, keepdims=True)
        first = jnp.min(jnp.where(score == m, lane_f, 1e9), axis=-1, keepdims=True)
        hit = lane_f == first
        sel = jnp.where(hit, jnp.where(m > -0.5 * SEL_FORCE, 1.0, 0.0), sel)
        score = jnp.where(hit, -4.0 * SEL_FORCE, score)
    return sel


def _select_blocks_t(p_slc, qpos, n_slc):
    n_rows = -(-n_slc // 8) * 8
    shape = (n_rows, p_slc.shape[1])
    blk = lax.broadcasted_iota(jnp.int32, shape, 0)
    cur = _idiv(qpos, SEL_BLOCK)
    forced = (blk == 0) | (blk == cur) | (blk == cur - 1)
    causal = blk * SEL_BLOCK <= qpos
    score = jnp.where(forced, SEL_FORCE, jnp.where(causal, p_slc[:n_rows], -SEL_FORCE))
    score = jnp.where(blk < n_slc, score, -3.0 * SEL_FORCE)
    tiles = [score[8 * v:8 * v + 8] for v in range(n_rows // 8)]
    sub = lax.broadcasted_iota(jnp.int32, (8, shape[1]), 0)
    ranks = [jnp.zeros((8, shape[1]), F32) for _ in tiles]
    for i in range(n_slc):
        row = tiles[i // 8][i % 8:i % 8 + 1]
        for v, tile in enumerate(tiles):
            ge = jnp.where(row >= tile, 1.0, 0.0)
            gt = jnp.where(row > tile, 1.0, 0.0)
            if 8 * v > i:
                ranks[v] = ranks[v] + ge
            elif 8 * v + 7 <= i:
                ranks[v] = ranks[v] + gt
            else:
                ranks[v] = ranks[v] + jnp.where(sub > i - 8 * v, ge, gt)
    rank = jnp.concatenate(ranks, axis=0)
    return jnp.where(rank < min(SEL_TOPK, n_slc), jnp.where(score > -0.5 * SEL_FORCE, 1.0, 0.0), 0.0)


NSA_TILE = 2 * LANES
ONES_ROWS = 16
LOG2E = 1.4426950408889634


def _nsa_prompt_body(q_ref, kc_ref, vct_ref, ks_ref, vst_ref, kw_ref, vwt_ref, g_ref, wovt_ref, o_ref, m_sc, acc_sc,
                     *, n_cmp, n_slc):
    tq = tk = NSA_TILE
    rep = C_HEADS // C_KV_HEADS
    blk = pl.program_id(1)
    q0 = blk * tq
    n_cmp_pad = kc_ref.shape[1]
    sig_t = jax.nn.sigmoid(g_ref[0])
    q_t = [jnp.transpose(q_ref[0, :, c * LANES:(c + 1) * LANES].astype(F32))
           for c in range(C_WIDTH // LANES)]
    qpos = q0 + lax.broadcasted_iota(jnp.int32, (1, tq), 1)
    cmp_i = lax.broadcasted_iota(jnp.int32, (n_cmp_pad, tq), 0)
    cmp_ok = (cmp_i * CMP_STRIDE + (CMP_LEN - 1) <= qpos) & (cmp_i < n_cmp)
    cmp_bias = jnp.where(cmp_ok, 0.0, NEG_INF)
    any_block = jnp.where(qpos >= CMP_LEN - 1, 1.0, 0.0)
    ahead = lax.broadcasted_iota(jnp.int32, (tk, tq), 0) - lax.broadcasted_iota(jnp.int32, (tk, tq), 1)
    diag_bias = jnp.where(ahead <= 0, 0.0, NEG_INF)
    far_bias = jnp.where(ahead >= 0, 0.0, NEG_INF)
    ones_rows = jnp.where(lax.broadcasted_iota(jnp.int32, (ONES_ROWS, tk), 0) == 0, 1.0, 0.0).astype(BF16)

    def flash_update(scores, v_t):
        v_ext = jnp.concatenate([v_t, ones_rows], axis=0)
        for r, s in enumerate(scores):
            m_old = m_sc[r:r + 1, :]
            m_new = jnp.maximum(m_old, jnp.max(s, axis=0, keepdims=True))
            m_sc[r:r + 1, :] = m_new
            pv = _dot(v_ext, jnp.exp2(s - m_new).astype(BF16))
            acc_sc[r] = jnp.exp2(m_old - m_new) * acc_sc[r] + pv

    def flash_reset():
        m_sc[...] = jnp.full(m_sc.shape, NEG_INF, F32)
        acc_sc[...] = jnp.zeros(acc_sc.shape, F32)

    def flash_result(r):
        acc = acc_sc[r]
        return acc[:HEAD_DIM] * (1.0 / acc[HEAD_DIM:HEAD_DIM + 1])

    outs = []
    for g in range(C_KV_HEADS):
        gc = slice(g * HEAD_DIM, (g + 1) * HEAD_DIM)
        heads = [g * rep + r for r in range(rep)]
        q_heads = [q_t[h // 2][(h % 2) * HEAD_DIM:(h % 2 + 1) * HEAD_DIM] for h in heads]
        q_nat = [(qh * (HEAD_DIM ** -0.5)).astype(BF16) for qh in q_heads]
        q_log2 = [(qh * (HEAD_DIM ** -0.5 * LOG2E)).astype(BF16) for qh in q_heads]
        kcb = kc_ref[0, :, gc].astype(BF16)
        vct = vct_ref[0, gc, :].astype(BF16)
        scores = [_dot(kcb, q_nat[r]) + cmp_bias for r in range(rep)]
        probs = []
        imp = None
        for s in scores:
            p = jnp.exp(s - jnp.max(s, axis=0, keepdims=True))
            p = p * (any_block / jnp.sum(p, axis=0, keepdims=True))
            imp = p if imp is None else imp + p
            probs.append(p.astype(BF16))
        o_cmp = [_dot(vct, p) for p in probs]
        hi = imp.astype(BF16)
        r1 = imp - hi.astype(F32)
        mid = r1.astype(BF16)
        lo = (r1 - mid.astype(F32)).astype(BF16)
        p_slc = _dot(wovt_ref[...], hi) + _dot(wovt_ref[...], mid) + _dot(wovt_ref[...], lo)
        sel_bias = jnp.where(_select_blocks_t(p_slc, qpos, n_slc) > 0.5, 0.0, NEG_INF).astype(BF16)
        q_sel = [jnp.concatenate([qh, sel_bias], axis=0) for qh in q_log2]
        flash_reset()

        def sel_scores(k0):
            k_tile = ks_ref[0, pl.ds(pl.multiple_of(k0, tk), tk), g * LANES:(g + 1) * LANES]
            return [_dot(k_tile, qs) for qs in q_sel]

        def sel_values(k0):
            return vst_ref[0, gc, pl.ds(pl.multiple_of(k0, tk), tk)]

        def past_pair(i, carry):
            k0 = i * (2 * tk)
            first, second = sel_scores(k0), sel_scores(k0 + tk)
            flash_update(first, sel_values(k0))
            flash_update(second, sel_values(k0 + tk))
            return carry

        lax.fori_loop(0, blk // 2, past_pair, 0)
        k0 = (blk // 2) * (2 * tk)

        @pl.when(k0 == q0)
        def _():
            flash_update([s + diag_bias for s in sel_scores(q0)], sel_values(q0))

        @pl.when(k0 != q0)
        def _():
            first, second = sel_scores(k0), [s + diag_bias for s in sel_scores(q0)]
            flash_update(first, sel_values(k0))
            flash_update(second, sel_values(q0))

        o_sel = [flash_result(r) for r in range(rep)]
        flash_reset()
        n_back = C_WINDOW // tk
        win = []
        for back in range(n_back + 1):
            k0 = pl.multiple_of(jnp.maximum(q0 - back * tk, 0), tk)
            k_tile = kw_ref[0, pl.ds(k0, tk), gc]
            absent = jnp.where(blk >= back, 0.0, NEG_INF)
            edge = diag_bias if back == 0 else far_bias if back == n_back else None
            bias = absent if edge is None else edge + absent
            win.append(([_dot(k_tile, qh) + bias for qh in q_log2], vwt_ref[0, gc, pl.ds(k0, tk)]))
        for scores, values in win:
            flash_update(scores, values)
        for r, h in enumerate(heads):
            outs.append(sig_t[3 * h:3 * h + 1] * o_cmp[r] + sig_t[3 * h + 1:3 * h + 2] * o_sel[r]
                        + sig_t[3 * h + 2:3 * h + 3] * flash_result(r))
    for c in range(C_WIDTH // LANES):
        pair = jnp.transpose(jnp.concatenate(outs[2 * c:2 * c + 2], axis=0))
        o_ref[0, :, c * LANES:(c + 1) * LANES] = pair.astype(o_ref.dtype)


def _nsa_prompt(q, kcb, vcb, ks_ext, vs_t, kw, vw_t, gates_t, n_cmp):
    b, t, wq = q.shape
    n_cmp_pad = kcb.shape[1]
    n_slc = -(-t // SEL_BLOCK)
    tile = NSA_TILE
    assert t % (2 * tile) == 0 and n_slc == HEAD_DIM and C_WINDOW % tile == 0
    wov_t = _overlap_weights(n_cmp, n_cmp_pad, n_slc, LANES).T
    vcb_t = jnp.swapaxes(vcb, 1, 2)
    rep = C_HEADS // C_KV_HEADS
    whole = lambda a: pl.BlockSpec((1,) + a.shape[1:], lambda bi, i: (bi, 0, 0))
    return pl.pallas_call(
        functools.partial(_nsa_prompt_body, n_cmp=n_cmp, n_slc=n_slc),
        grid=(b, t // tile),
        in_specs=[pl.BlockSpec((1, tile, wq), lambda bi, i: (bi, i, 0)),
                  whole(kcb), whole(vcb_t), whole(ks_ext), whole(vs_t), whole(kw), whole(vw_t),
                  pl.BlockSpec((1, gates_t.shape[1], tile), lambda bi, i: (bi, 0, i)),
                  pl.BlockSpec(wov_t.shape, lambda bi, i: (0, 0))],
        out_specs=pl.BlockSpec((1, tile, wq), lambda bi, i: (bi, i, 0)),
        out_shape=jax.ShapeDtypeStruct((b, t, wq), BF16),
        scratch_shapes=[pltpu.VMEM((rep, tile), F32), pltpu.VMEM((rep, HEAD_DIM + ONES_ROWS, tile), F32)],
        compiler_params=_params("parallel", "arbitrary"),
        name="nsa_prompt",
    )(q, kcb, vcb_t, ks_ext, vs_t, kw, vw_t, gates_t, wov_t)


def _block_diag(q, blk_of_row, n_blk):
    return jnp.concatenate([jnp.where(blk_of_row == b, q, 0.0) for b in range(n_blk)], axis=-1)


def _diag_blocks(r, blk_of_row, n_blk):
    out = jnp.where(blk_of_row == 0, r[:, 0:HEAD_DIM], 0.0)
    for b in range(1, n_blk):
        out = out + jnp.where(blk_of_row == b, r[:, b * HEAD_DIM:(b + 1) * HEAD_DIM], 0.0)
    return out


def _row_ids(n_rows, n_heads, n_blk):
    row = lax.broadcasted_iota(jnp.int32, (n_rows, 1), 0)
    step = _idiv(row, n_heads)
    blk = _idiv(_imod(row, n_heads), n_heads // n_blk)
    return step, blk


def _pattern_weight(dist, patterns):
    w = jnp.zeros(dist.shape, F32)
    for window, dil in patterns:
        w = w + jnp.where(_imod(dist, dil) == 0, jnp.where(dist <= window, 1.0, 0.0), 0.0)
    return jnp.where(dist >= 0, w, 0.0)


def _decode_body(*refs, n_heads, n_blk, patterns, gate_col):
    q_ref, kc_ref, vc_ref, kn_ref, vn_ref = refs[:5]
    g_ref = refs[5] if gate_col is not None else None
    o_ref = refs[-1]
    n_rows = q_ref.shape[1]
    n_cache = kc_ref.shape[2]
    n_new = kn_ref.shape[1]
    step, blk = _row_ids(n_rows, n_heads, n_blk)
    qbd = _block_diag(q_ref[0] * (HEAD_DIM ** -0.5), blk, n_blk)
    s_c = _dot(qbd.astype(BF16), kc_ref[0].astype(BF16))
    w_c = _pattern_weight(n_cache + step - lax.broadcasted_iota(jnp.int32, (n_rows, n_cache), 1), patterns)
    s_c = jnp.where(w_c > 0, s_c, NEG_INF)
    kn = kn_ref[0]
    vn = vn_ref[0]
    s_n, w_n = [], []
    for c in range(n_new):
        s = jnp.sum(qbd * kn[c:c + 1, :], axis=-1, keepdims=True)
        w = _pattern_weight(step - c, patterns)
        s_n.append(jnp.where(w > 0, s, NEG_INF))
        w_n.append(w)
    m = jnp.max(s_c, axis=-1, keepdims=True)
    for s in s_n:
        m = jnp.maximum(m, s)
    p_c = w_c * jnp.exp(s_c - m)
    l = jnp.sum(p_c, axis=-1, keepdims=True)
    r = _dot_nt(p_c.astype(BF16), vc_ref[0].astype(BF16))
    for c in range(n_new):
        p = w_n[c] * jnp.exp(s_n[c] - m)
        l = l + p
        r = r + p * vn[c:c + 1, :]
    o = _diag_blocks(r, blk, n_blk) / l
    if g_ref is not None:
        o = o * jax.nn.sigmoid(g_ref[0])[:, gate_col:gate_col + 1]
    o_ref[0] = o


def _decode_attention(q, kc, vc, kn, vn, *, n_heads, n_blk, patterns, gate=None, gate_col=None):
    n, n_rows, _ = q.shape
    args = [q, kc, vc, kn, vn]
    if gate is not None:
        args.append(gate)
    whole = lambda a: pl.BlockSpec((1,) + a.shape[1:], lambda i: (i, 0, 0))
    return pl.pallas_call(
        functools.partial(_decode_body, n_heads=n_heads, n_blk=n_blk, patterns=patterns,
                          gate_col=gate_col if gate is not None else None),
        grid=(n,),
        in_specs=[whole(a) for a in args],
        out_specs=pl.BlockSpec((1, n_rows, HEAD_DIM), lambda i: (i, 0, 0)),
        out_shape=jax.ShapeDtypeStruct((n, n_rows, HEAD_DIM), F32),
        compiler_params=_params("parallel"),
        name="decode_attention",
    )(*args)


def _nsa_decode_body(pt_ref, q_ref, kc_ref, vc_ref, ksp_ref, vsp_ref, kn_ref, vn_ref, g_ref, wov_ref, o_ref,
                     kbuf, vbuf, expand, ksem, vsem, *, n_pages, n_cmp, n_slc):
    n = pl.program_id(0)
    slot = n % 2
    n_rows = q_ref.shape[1]
    n_new = kn_ref.shape[1]
    n_past = kbuf.shape[2]
    n_slc_pad = wov_ref.shape[1]
    rep = C_HEADS // C_KV_HEADS

    def gather(seq, dst_slot, start):
        for pool_ref, buf, sem in ((ksp_ref, kbuf, ksem), (vsp_ref, vbuf, vsem)):
            for p in range(n_pages):
                page = pt_ref[seq, p] if start else 0
                cp = pltpu.make_async_copy(pool_ref.at[page], buf.at[dst_slot, :, pl.ds(p * PAGE_SIZE, PAGE_SIZE)],
                                           sem.at[dst_slot])
                if start:
                    cp.start()
                else:
                    cp.wait()

    @pl.when(n == 0)
    def _():
        gather(0, 0, True)
        step_cols = 1024
        for c0 in range(0, n_past, step_cols):
            shape = (n_slc_pad, min(step_cols, n_past - c0))
            blk_of_key = _idiv(c0 + lax.broadcasted_iota(jnp.int32, shape, 1), SEL_BLOCK)
            hit = blk_of_key == lax.broadcasted_iota(jnp.int32, shape, 0)
            expand[:, c0:c0 + shape[1]] = jnp.where(hit, 1.0, 0.0).astype(BF16)

    @pl.when(n + 1 < pl.num_programs(0))
    def _():
        gather(n + 1, 1 - slot, True)

    step, blk = _row_ids(n_rows, C_HEADS, C_KV_HEADS)
    qpos = n_past + step
    qbd = _block_diag(q_ref[0] * (HEAD_DIM ** -0.5), blk, C_KV_HEADS)
    qbd_in = qbd.astype(BF16)
    n_cmp_pad = kc_ref.shape[1]
    cmp_i = lax.broadcasted_iota(jnp.int32, (n_rows, n_cmp_pad), 1)
    cmp_ok = (cmp_i * CMP_STRIDE + (CMP_LEN - 1) <= qpos) & (cmp_i < n_cmp)
    s = jnp.where(cmp_ok, _dot_nt(qbd_in, kc_ref[0].astype(BF16)), NEG_INF)
    m = jnp.max(s, axis=-1, keepdims=True)
    p = jnp.where(cmp_ok, jnp.exp(s - m), 0.0)
    l = jnp.sum(p, axis=-1, keepdims=True)
    p = p / jnp.where(l > 0, l, 1.0)
    o_cmp = _diag_blocks(_dot(p.astype(BF16), vc_ref[0].astype(BF16)), blk, C_KV_HEADS)
    grp = (n_rows // rep, rep, n_cmp_pad)
    imp = jnp.broadcast_to(jnp.sum(p.reshape(grp), axis=1, keepdims=True), grp).reshape(n_rows, n_cmp_pad)
    sel_rows = _select_blocks(_split3_dot(imp, wov_ref[...]), qpos, n_slc)
    gather(n, slot, False)
    picked = _dot(sel_rows.astype(BF16), expand[...])
    kpos = lax.broadcasted_iota(jnp.int32, (n_rows, n_past), 1)
    s_p = jnp.where(picked > 0.5, jnp.where(kpos <= qpos, _dot(qbd_in, kbuf[slot].astype(BF16)), NEG_INF), NEG_INF)
    kn = kn_ref[0]
    vn = vn_ref[0]
    s_n = []
    for c in range(n_new):
        blk_c = (n_past + c) // SEL_BLOCK
        ok = (sel_rows[:, blk_c:blk_c + 1] > 0.5) & (n_past + c <= qpos)
        s_n.append(jnp.where(ok, jnp.sum(qbd * kn[c:c + 1, :], axis=-1, keepdims=True), NEG_INF))
    m = jnp.max(s_p, axis=-1, keepdims=True)
    for sc in s_n:
        m = jnp.maximum(m, sc)
    p_p = jnp.exp(s_p - m)
    l = jnp.sum(p_p, axis=-1, keepdims=True)
    r = _dot_nt(p_p.astype(BF16), vbuf[slot].astype(BF16))
    for c in range(n_new):
        pn = jnp.exp(s_n[c] - m)
        l = l + pn
        r = r + pn * vn[c:c + 1, :]
    o_sel = _diag_blocks(r, blk, C_KV_HEADS) / l
    sig = jax.nn.sigmoid(g_ref[0])
    o_ref[0] = sig[:, 0:1] * o_cmp + sig[:, 1:2] * o_sel


def _nsa_decode(q, kcb, vcb, pool_ks, pool_vs, page_table, kn, vn, gate, n_cmp):
    n, n_rows, _ = q.shape
    n_pages = page_table.shape[1]
    n_past = n_pages * PAGE_SIZE
    n_new = kn.shape[1]
    n_slc = -(-(n_past + n_new) // SEL_BLOCK)
    n_slc_pad = -(-n_slc // LANES) * LANES
    wov = _overlap_weights(n_cmp, kcb.shape[1], n_slc, n_slc_pad)
    whole = lambda a: pl.BlockSpec((1,) + a.shape[1:], lambda i, pt: (i, 0, 0))
    pools = [jnp.transpose(a, (0, 2, 3, 1)).reshape(a.shape[0], C_KV_WIDTH, PAGE_SIZE) for a in (pool_ks, pool_vs)]
    return pl.pallas_call(
        functools.partial(_nsa_decode_body, n_pages=n_pages, n_cmp=n_cmp, n_slc=n_slc),
        grid_spec=pltpu.PrefetchScalarGridSpec(
            num_scalar_prefetch=1,
            grid=(n,),
            in_specs=[whole(q), whole(kcb), whole(vcb), pl.BlockSpec(memory_space=pl.ANY),
                      pl.BlockSpec(memory_space=pl.ANY), whole(kn), whole(vn), whole(gate),
                      pl.BlockSpec(wov.shape, lambda i, pt: (0, 0))],
            out_specs=pl.BlockSpec((1, n_rows, HEAD_DIM), lambda i, pt: (i, 0, 0)),
            scratch_shapes=[pltpu.VMEM((2, C_KV_WIDTH, n_past), F32), pltpu.VMEM((2, C_KV_WIDTH, n_past), F32),
                            pltpu.VMEM((n_slc_pad, n_past), BF16),
                            pltpu.SemaphoreType.DMA((2,)), pltpu.SemaphoreType.DMA((2,))]),
        out_shape=jax.ShapeDtypeStruct((n, n_rows, HEAD_DIM), F32),
        compiler_params=_params("arbitrary"),
        name="nsa_decode",
    )(page_table, q, kcb, vcb, *pools, kn, vn, gate, wov)


def _odd_out_body(x_ref, *refs):
    *branch_refs, z_ref, w_ref, gf_ref, y_ref = refs
    o = branch_refs[0][...].astype(F32)
    for ref in branch_refs[1:]:
        o = o + ref[...].astype(F32)
    mixed = (o * _silu(z_ref[...].astype(F32))).astype(BF16)
    h = x_ref[...] + _dot(mixed, w_ref[...])
    y_ref[...] = h * lax.rsqrt(jnp.mean(h * h, axis=-1, keepdims=True) + RMS_EPS) * gf_ref[...]


def _odd_out(x2d, branches, z, w_out, final_gain, tm):
    m, d = x2d.shape
    row = pl.BlockSpec((tm, d), lambda i: (i, 0))
    return pl.pallas_call(
        _odd_out_body,
        grid=(m // tm,),
        in_specs=[row] * (len(branches) + 2) + [pl.BlockSpec(w_out.shape, lambda i: (0, 0)),
                                                 pl.BlockSpec((1, d), lambda i: (0, 0))],
        out_specs=row,
        out_shape=jax.ShapeDtypeStruct((m, d), F32),
        compiler_params=_params("parallel"),
        name="odd_out",
    )(x2d, *branches, z, w_out, final_gain.reshape(1, d))


def _even_outs(cache_from):
    a, bk = A_WIDTH, B_HEADS * B_DK
    outs = [(0, a, F32, None), (a, a, F32, None), (2 * a, a, F32, None)]
    if cache_from is not None:
        outs += [(a, a, F32, cache_from), (2 * a, a, F32, cache_from)]
    outs += [(3 * a, a, BF16, None),
             (4 * a, bk, F32, None), (4 * a + bk, bk, F32, None),
             (4 * a + 2 * bk, B_WIDTH, F32, None), (4 * a + 2 * bk + B_WIDTH, B_WIDTH, BF16, None)]
    return outs


def _from_feature_major(u, n_heads):
    b, _, length = u.shape
    return jnp.transpose(u.reshape(b, n_heads, HEAD_DIM, length), (0, 3, 1, 2))


def _to_feature_major(u):
    n, length, n_heads, hd = u.shape
    return jnp.transpose(u, (0, 2, 3, 1)).reshape(n, n_heads * hd, length)


def _even_layer_prompt(x, gain, w_in, w_out):
    b, t, d = x.shape
    x2d = x.reshape(b * t, d)
    keep = min(A_PATTERNS[-1][0], t)
    qa, ka, va, ka_t, va_t, za, qb, kb, vb, zb = _norm_proj(x2d, gain, w_in, _even_outs(t - keep), 512, t)
    seq = lambda u: u.reshape(b, t, u.shape[-1])
    oa = _dilated_attention(seq(qa), seq(ka), seq(va), A_PATTERNS)
    chunk = B_CHUNK if t % B_CHUNK == 0 else t
    s0 = jnp.zeros((b, B_HEADS, B_DK, B_DV), F32)
    ob, s_fin = _retention(seq(qb), seq(kb), seq(vb), s0, jnp.arange(t), chunk=chunk, c_real=chunk)
    y = _even_out(x2d, oa.reshape(b * t, A_WIDTH), za, ob.reshape(b * t, B_WIDTH), zb, w_out, 256)
    return y.reshape(b, t, d), _from_feature_major(ka_t, A_HEADS), _from_feature_major(va_t, A_HEADS), s_fin


def _even_layer_sample(x, cache_k, cache_v, state, gain, w_in, w_out, past_len):
    n, s_len, d = x.shape
    x2d = x.reshape(n * s_len, d)
    qa, ka32, va32, za, qb, kb, vb, zb = _norm_proj(x2d, gain, w_in, _even_outs(None), n * s_len, n * s_len)
    seq = lambda u: u.reshape(n, s_len, u.shape[-1])
    oa = _decode_attention(qa.reshape(n, s_len * A_HEADS, HEAD_DIM), _to_feature_major(cache_k),
                           _to_feature_major(cache_v), seq(ka32), seq(va32), n_heads=A_HEADS, n_blk=A_HEADS,
                           patterns=A_PATTERNS)
    oa = oa.reshape(n * s_len, A_WIDTH)
    chunk = 16
    pad = lambda u: jnp.pad(seq(u), ((0, 0), (0, chunk - s_len), (0, 0)))
    pos = past_len + jnp.arange(chunk)
    ob, s_new = _retention(pad(qb), pad(kb), pad(vb), state, pos, chunk=chunk, c_real=s_len)
    ob = ob[:, :s_len].reshape(n * s_len, B_WIDTH)
    y = _even_out(x2d, oa, za, ob, zb, w_out, n * s_len)
    heads = lambda u: u.reshape(n, s_len, A_HEADS, HEAD_DIM)
    return y.reshape(n, s_len, d), heads(ka32), heads(va32), s_new


ODD_KV = ("kc", "vc", "ks", "vs", "kw", "vw")
ODD_GATE_COL = C_WIDTH + len(ODD_KV) * C_KV_WIDTH
ODD_Z_COL = ODD_GATE_COL + 3 * C_HEADS


def _odd_outs_prompt(win_from):
    outs = [(0, C_WIDTH, BF16, None)]
    for i, name in enumerate(ODD_KV):
        off = C_WIDTH + i * C_KV_WIDTH
        outs.append((off, C_KV_WIDTH, BF16, 0 if name in ("vs", "vw") else BLOCK_TAGGED if name == "ks" else None))
        outs.append((off, C_KV_WIDTH, F32, win_from if name in ("kw", "vw") else 0))
    outs += [(ODD_GATE_COL, 3 * C_HEADS, F32, 0), (ODD_Z_COL, C_WIDTH, BF16, None)]
    return outs


def _odd_outs_sample():
    outs = [(0, C_WIDTH, BF16, None)]
    outs += [(C_WIDTH + i * C_KV_WIDTH, C_KV_WIDTH, F32, None) for i in range(len(ODD_KV))]
    outs += [(ODD_GATE_COL, 3 * C_HEADS, F32, None), (ODD_Z_COL, C_WIDTH, BF16, None)]
    return outs


def _odd_layer_prompt(x, gain, w_in, w_out, cw_k, cw_v, final_gain):
    b, t, d = x.shape
    x2d = x.reshape(b * t, d)
    keep = min(C_WINDOW, t)
    (q, kc, kc_t, vc, vc_t, ks, ks_t, vs_tb, vs_t, kw, kw_t, vw_tb, vw_t, gl_t, z) = _norm_proj(
        x2d, gain, w_in, _odd_outs_prompt(t - keep), 512, t)
    seq = lambda u: u.reshape(b, t, u.shape[-1])
    n_cmp = (t - CMP_LEN) // CMP_STRIDE + 1
    pieces = lambda u: u.reshape(b, t // CMP_STRIDE, CMP_STRIDE * C_KV_WIDTH)
    kcb = _compress(pieces(kc), cw_k)
    vcb = _compress(pieces(vc), cw_v)
    o = _nsa_prompt(seq(q), kcb, vcb, seq(ks), vs_tb, seq(kw), vw_tb, gl_t, n_cmp)
    y = _odd_out(x2d, [o.reshape(b * t, C_WIDTH)], z, w_out, final_gain, 256)
    kv = lambda u: _from_feature_major(u, C_KV_HEADS)
    return y.reshape(b, t, d), kv(kc_t), kv(vc_t), kv(ks_t), kv(vs_t), kv(kw_t), kv(vw_t)


def _odd_layer_sample(x, page_table, pool_kc, pool_vc, pool_ks, pool_vs, buf_kw, buf_vw, gain, w_in, w_out,
                      cw_k, cw_v, final_gain):
    n, s_len, d = x.shape
    x2d = x.reshape(n * s_len, d)
    q, kc, vc, ks32, vs32, kw32, vw32, gl, z = _norm_proj(x2d, gain, w_in, _odd_outs_sample(), n * s_len, n * s_len)
    seq = lambda u: u.reshape(n, s_len, u.shape[-1])
    n_past = page_table.shape[1] * PAGE_SIZE
    n_cmp = (n_past + s_len - CMP_LEN) // CMP_STRIDE + 1
    assert CMP_STRIDE * (n_cmp - 1) + CMP_LEN <= n_past
    kcb = _compress_paged(pool_kc, page_table, cw_k)
    vcb = _compress_paged(pool_vc, page_table, cw_v)
    rows = lambda u: u.astype(F32).reshape(n, s_len * C_HEADS, HEAD_DIM)
    gate = gl.reshape(n, s_len * C_HEADS, 3)
    o_cs = _nsa_decode(rows(q), kcb, vcb, pool_ks, pool_vs, page_table, seq(ks32), seq(vs32), gate, n_cmp)
    o_win = _decode_attention(rows(q), _to_feature_major(buf_kw), _to_feature_major(buf_vw),
                              seq(kw32), seq(vw32), n_heads=C_HEADS, n_blk=C_KV_HEADS,
                              patterns=((C_WINDOW, 1),), gate=gate, gate_col=2)
    y = _odd_out(x2d, [o_cs.reshape(n * s_len, C_WIDTH), o_win.reshape(n * s_len, C_WIDTH)], z, w_out, final_gain,
                 n * s_len)
    kv = lambda u: u.reshape(n, s_len, C_KV_HEADS, HEAD_DIM)
    return y.reshape(n, s_len, d), kv(kc), kv(vc), kv(ks32), kv(vs32), kv(kw32), kv(vw32)


def kernel(x_prompt, x_sample, cache_a_k, cache_a_v, state_ret, cache_c_kcmp, cache_c_vcmp, cache_c_ksel,
           cache_c_vsel, cache_c_kwin, cache_c_vwin, page_table, even_norm, even_w_in, even_w_out, odd_norm,
           odd_w_in, odd_w_out, cmp_pos_k, cmp_w1_k, cmp_w2_k, cmp_pos_v, cmp_w1_v, cmp_w2_v, final_norm):
    assert even_norm.shape[0] == 1 and odd_norm.shape[0] == 1
    past_len = page_table.shape[1] * PAGE_SIZE
    w_in_e = even_w_in[0].T.astype(BF16)
    w_out_e = even_w_out[0].astype(BF16)
    w_in_o = odd_w_in[0].T.astype(BF16)
    w_out_o = odd_w_out[0].astype(BF16)
    cw_k = _compress_weights(cmp_pos_k[0], cmp_w1_k[0], cmp_w2_k[0])
    cw_v = _compress_weights(cmp_pos_v[0], cmp_w1_v[0], cmp_w2_v[0])

    hp, ak_p, av_p, ret_p = _even_layer_prompt(x_prompt, even_norm[0], w_in_e, w_out_e)
    hs, ak_s, av_s, ret_s = _even_layer_sample(x_sample, cache_a_k[0], cache_a_v[0], state_ret[0], even_norm[0],
                                               w_in_e, w_out_e, past_len)
    yp, *rows_p = _odd_layer_prompt(hp, odd_norm[0], w_in_o, w_out_o, cw_k, cw_v, final_norm)
    ys, *rows_s = _odd_layer_sample(hs, page_table, cache_c_kcmp[0], cache_c_vcmp[0], cache_c_ksel[0],
                                    cache_c_vsel[0], cache_c_kwin[0], cache_c_vwin[0], odd_norm[0], w_in_o, w_out_o,
                                    cw_k, cw_v, final_norm)
    lead = lambda u: u[None]
    out = [yp, ys, lead(ak_p), lead(ak_s), lead(av_p), lead(av_s), lead(ret_p), lead(ret_s)]
    for rp, rs in zip(rows_p, rows_s):
        out += [lead(rp), lead(rs)]
    return tuple(out)
```

```python
import functools

import numpy as np
import jax
import jax.numpy as jnp
from jax import lax
from jax.experimental import pallas as pl
from jax.experimental.pallas import tpu as pltpu

F32 = jnp.float32
BF16 = jnp.bfloat16

HEAD_DIM = 64
A_HEADS = 8
A_WIDTH = A_HEADS * HEAD_DIM
A_PATTERNS = ((128, 1), (512, 4), (2048, 16))
B_HEADS = 4
B_DK = 64
B_DV = 128
B_WIDTH = B_HEADS * B_DV
RET_CHUNK = 256
ROPE_BASE = 10000.0
GN_EPS = 1e-5
C_HEADS = 16
C_KV_HEADS = 2
C_WIDTH = C_HEADS * HEAD_DIM
C_KV_WIDTH = C_KV_HEADS * HEAD_DIM
CMP_LEN = 32
CMP_STRIDE = 16
CMP_HIDDEN = 256
SEL_BLOCK = 64
SEL_TOPK = 16
SEL_FORCE = 1e9
C_WINDOW = 512
PAGE_SIZE = 128
RMS_EPS = 1e-6
NEG_INF = -1e30

LANES = 128
ATTN_TQ = 128
PROJ_ROWS = 512
OUT_ROWS = 512
VMEM_LIMIT = 56 * 1024 * 1024


def _params(*sem):
    return pltpu.CompilerParams(dimension_semantics=sem, vmem_limit_bytes=VMEM_LIMIT)


def _dot(a, b):
    return jnp.dot(a, b, preferred_element_type=F32)


def _dot_nt(a, b):
    return lax.dot_general(a, b, (((1,), (1,)), ((), ())), preferred_element_type=F32)


def _dot_tn(a, b):
    return lax.dot_general(a, b, (((0,), (0,)), ((), ())), preferred_element_type=F32)


def _idiv(x, n):
    assert n & (n - 1) == 0
    return jnp.right_shift(x, n.bit_length() - 1)


def _imod(x, n):
    assert n & (n - 1) == 0
    return jnp.bitwise_and(x, n - 1)


def _split3_dot(a, w):
    hi = a.astype(BF16)
    r1 = a - hi.astype(F32)
    mid = r1.astype(BF16)
    lo = (r1 - mid.astype(F32)).astype(BF16)
    return _dot(hi, w) + _dot(mid, w) + _dot(lo, w)


BLOCK_TAGGED = "block_tagged"


def _norm_proj_body(x_ref, g_ref, wt_ref, *o_refs, outs, tiles, tm):
    x = x_ref[...]
    y = x * lax.rsqrt(jnp.mean(x * x, axis=-1, keepdims=True) + RMS_EPS) * g_ref[...]
    yb = y.astype(BF16)
    tile = pl.program_id(0) % tiles
    done = {}
    for o_ref, (off, width, _, keep_from) in zip(o_refs, outs):
        w_rows = wt_ref[off:off + width, :]
        if keep_from is None or keep_from == BLOCK_TAGGED:
            if (off, width) not in done:
                done[(off, width)] = _dot_nt(yb, w_rows)
            res = done[(off, width)]
            if keep_from is None:
                o_ref[...] = res.astype(o_ref.dtype)
            else:
                lane = lax.broadcasted_iota(jnp.int32, res.shape, 1)
                pos = tile * tm + lax.broadcasted_iota(jnp.int32, res.shape, 0)
                tag = jnp.where(_idiv(pos, SEL_BLOCK) == _imod(lane, HEAD_DIM), 1.0, 0.0)
                low = lane < HEAD_DIM
                o_ref[:, 0:LANES] = jnp.where(low, res, tag).astype(o_ref.dtype)
                o_ref[:, LANES:2 * LANES] = jnp.where(low, pltpu.roll(res, HEAD_DIM, 1), tag).astype(o_ref.dtype)
        elif keep_from == 0:
            if (off, width, 0) not in done:
                done[(off, width, 0)] = _dot_nt(w_rows, yb)
            o_ref[0] = done[(off, width, 0)].astype(o_ref.dtype)
        else:
            @pl.when(tile >= keep_from // tm)
            def _():
                o_ref[0] = _dot_nt(w_rows, yb).astype(o_ref.dtype)


def _norm_proj(x2d, gain, w_t, outs, tm, seq_len):
    m, d = x2d.shape
    n = w_t.shape[0]
    tiles = seq_len // tm
    assert seq_len % tm == 0 and m % seq_len == 0
    out_specs, out_shape = [], []
    for _, width, dt, keep_from in outs:
        if keep_from is None or keep_from == BLOCK_TAGGED:
            if keep_from == BLOCK_TAGGED:
                assert width == C_KV_WIDTH and seq_len <= SEL_BLOCK * HEAD_DIM
                width = 2 * LANES
            out_specs.append(pl.BlockSpec((tm, width), lambda i: (i, 0)))
            out_shape.append(jax.ShapeDtypeStruct((m, width), dt))
        else:
            assert keep_from % tm == 0
            first = keep_from // tm
            out_specs.append(pl.BlockSpec((1, width, tm),
                                          lambda i, first=first: (i // tiles, 0, jnp.maximum(i % tiles - first, 0))))
            out_shape.append(jax.ShapeDtypeStruct((m // seq_len, width, seq_len - keep_from), dt))
    return pl.pallas_call(
        functools.partial(_norm_proj_body, outs=outs, tiles=tiles, tm=tm),
        grid=(m // tm,),
        in_specs=[pl.BlockSpec((tm, d), lambda i: (i, 0)),
                  pl.BlockSpec((1, d), lambda i: (0, 0)),
                  pl.BlockSpec((n, d), lambda i: (0, 0))],
        out_specs=out_specs,
        out_shape=out_shape,
        compiler_params=_params("arbitrary"),
        name="norm_proj",
    )(x2d, gain.reshape(1, d), w_t)


DIL_GROUP = 4


def _dilated_body(q_ref, k_ref, v_ref, o_ref, o_sc, lse_sc, *, patterns):
    tq = ATTN_TQ
    blk_len = q_ref.shape[1]
    base = pl.program_id(2) * blk_len
    head_of_lane = _idiv(lax.broadcasted_iota(jnp.int32, (tq, LANES), 1), HEAD_DIM)
    own_lanes = [head_of_lane == h for h in range(LANES // HEAD_DIM)]

    def rows(ref, start, size, dil):
        return ref[0, pl.ds(start, size) if dil == 1 else pl.ds(start, size, stride=dil), :]

    for p_idx, (window, dil) in enumerate(patterns):
        lookback = window // dil
        n_prev = -(-lookback // tq)
        span = (n_prev + 1) * tq
        n_u = blk_len // (tq * dil)
        n_sub_blocks = dil * n_u
        assert blk_len % (tq * dil) == 0 and n_sub_blocks % DIL_GROUP == 0
        offs = lax.broadcasted_iota(jnp.int32, (tq, span), 0) - lax.broadcasted_iota(jnp.int32, (tq, span), 1)
        biases = []
        for lead in range(n_prev + 1):
            dist = offs + lead * tq
            biases.append(jnp.where(dist >= 0, jnp.where(dist <= lookback, 0.0, NEG_INF), NEG_INF))

        def trip(i, carry, dil=dil, n_prev=n_prev, span=span, p_idx=p_idx, biases=biases):
            work = []
            for j in range(DIL_GROUP):
                idx = i * DIL_GROUP + j
                res = _imod(idx, dil)
                u = _idiv(idx, dil)
                s0 = base // dil + u * tq
                ks = jnp.maximum(s0 - n_prev * tq, 0)
                q_row = res + dil * (u * tq)
                k_row = res + dil * ks
                q2 = rows(q_ref, q_row, tq, dil) * (HEAD_DIM ** -0.5 * LOG2E)
                k2 = rows(k_ref, k_row, span, dil).astype(BF16)
                v2 = rows(v_ref, k_row, span, dil).astype(BF16)
                bias = biases[n_prev]
                for lead in range(n_prev):
                    bias = jnp.where(s0 - ks == lead * tq, biases[lead], bias)
                work.append((q_row, q2, k2, v2, bias))
            scores = [[_dot_nt(jnp.where(own, q2, 0.0).astype(BF16), k2) + bias for own in own_lanes]
                      for _, q2, k2, _, bias in work]
            maxes = [[jnp.max(s, axis=-1, keepdims=True) for s in ss] for ss in scores]
            probs = [[jnp.exp2(s - m) for s, m in zip(ss, mm)] for ss, mm in zip(scores, maxes)]
            sums = [[jnp.sum(p, axis=-1, keepdims=True) for p in pp] for pp in probs]
            outs = [[_dot(p.astype(BF16), w[3]) for p in pp] for pp, w in zip(probs, work)]
            for w, oo, mm, ll in zip(work, outs, maxes, sums):
                o2 = oo[0] * (1.0 / ll[0])
                lse2 = jnp.broadcast_to(mm[0] * LN2 + jnp.log(ll[0]), (tq, LANES))
                for own, o, m, l in list(zip(own_lanes, oo, mm, ll))[1:]:
                    o2 = jnp.where(own, o * (1.0 / l), o2)
                    lse2 = jnp.where(own, m * LN2 + jnp.log(l), lse2)
                idx = pl.ds(w[0], tq) if dil == 1 else pl.ds(w[0], tq, stride=dil)
                o_sc[p_idx, idx, :] = o2
                lse_sc[p_idx, idx, :] = lse2
            return carry

        lax.fori_loop(0, n_sub_blocks // DIL_GROUP, trip, 0)

    lses = [lse_sc[p] for p in range(len(patterns))]
    top = lses[0]
    for l in lses[1:]:
        top = jnp.maximum(top, l)
    weights = [jnp.exp(l - top) for l in lses]
    num = weights[0] * o_sc[0]
    den = weights[0]
    for p in range(1, len(patterns)):
        num = num + weights[p] * o_sc[p]
        den = den + weights[p]
    o_ref[0] = (num / den).astype(o_ref.dtype)


def _dilated_attention(q, k, v, patterns):
    b, t, width = q.shape
    blk_len = min(t, max(ATTN_TQ * dil for _, dil in patterns))
    assert t % blk_len == 0
    for window, dil in patterns:
        assert t // dil >= (-(-(window // dil) // ATTN_TQ) + 1) * ATTN_TQ
    return pl.pallas_call(
        functools.partial(_dilated_body, patterns=patterns),
        grid=(b, width // LANES, t // blk_len),
        in_specs=[pl.BlockSpec((1, blk_len, LANES), lambda bi, hp, i: (bi, i, hp)),
                  pl.BlockSpec((1, t, LANES), lambda bi, hp, i: (bi, 0, hp)),
                  pl.BlockSpec((1, t, LANES), lambda bi, hp, i: (bi, 0, hp))],
        out_specs=pl.BlockSpec((1, blk_len, LANES), lambda bi, hp, i: (bi, i, hp)),
        out_shape=jax.ShapeDtypeStruct((b, t, width), BF16),
        scratch_shapes=[pltpu.VMEM((len(patterns), blk_len, LANES), F32),
                        pltpu.VMEM((len(patterns), blk_len, LANES), F32)],
        compiler_params=_params("parallel", "parallel", "arbitrary"),
        name="dilated_attention",
    )(q, k, v)


def _retention_body(q_ref, k_ref, v_ref, cos_ref, sin_ref, qd_ref, kd_ref, intra_ref, cd_ref, s0_ref,
                    o_ref, sfin_ref, state):
    c = pl.program_id(1)

    @pl.when(c == 0)
    def _():
        state[...] = s0_ref[0]

    cos = cos_ref[...]
    sin = sin_ref[...]
    width = cos.shape[-1]
    half = B_DK // 2
    lane = _imod(lax.broadcasted_iota(jnp.int32, cos.shape, 1), B_DK)

    def rotary(x):
        fwd = pltpu.roll(x, half, 1)
        bwd = pltpu.roll(x, width - half, 1)
        return x * cos + jnp.where(lane < half, -bwd, fwd) * sin

    q = rotary(q_ref[0])
    k = rotary(k_ref[0]) * (B_DK ** -0.5)
    q_in = q.astype(BF16)
    k_in = k.astype(BF16)
    q_st = (q * qd_ref[...]).astype(BF16)
    k_st = (k * kd_ref[...]).astype(BF16)
    v = v_ref[0].astype(BF16)
    kcs = [slice(h * B_DK, (h + 1) * B_DK) for h in range(B_HEADS)]
    vcs = [slice(h * B_DV, (h + 1) * B_DV) for h in range(B_HEADS)]
    states = [state[h] for h in range(B_HEADS)]
    scores = [_dot_nt(q_in[:, kc], k_in[:, kc]) * intra_ref[h] for h, kc in enumerate(kcs)]
    carried = [_dot(q_st[:, kc], st.astype(BF16)) for kc, st in zip(kcs, states)]
    updates = [_dot_tn(k_st[:, kc], v[:, vc]) for kc, vc in zip(kcs, vcs)]
    inner = [_dot(s.astype(BF16), v[:, vc]) for s, vc in zip(scores, vcs)]
    for h in range(B_HEADS):
        o_ref[0, :, vcs[h]] = (inner[h] + carried[h]).astype(o_ref.dtype)
        state[h] = states[h] * cd_ref[h] + updates[h]

    @pl.when(c == pl.num_programs(1) - 1)
    def _():
        sfin_ref[0] = state[...]


def _retention(q, k, v, s0, pos, *, chunk, c_real):
    n, t, _ = q.shape
    half = B_DK // 2
    freqs = ROPE_BASE ** (-jnp.arange(half, dtype=F32) / half)
    ang = pos.astype(F32)[:, None] * freqs[None, :]
    cos = jnp.tile(jnp.cos(ang), (1, 2 * B_HEADS))
    sin = jnp.tile(jnp.sin(ang), (1, 2 * B_HEADS))
    log_g = jnp.log1p(-jnp.exp2(-5.0 - jnp.arange(B_HEADS, dtype=F32)))
    i = jnp.arange(chunk, dtype=F32)
    diff = i[:, None] - i[None, :]
    intra = jnp.where(diff >= 0, jnp.exp(log_g[:, None, None] * jnp.maximum(diff, 0.0)), 0.0)
    q_decay = jnp.repeat(jnp.exp(log_g[None, :] * (i[:, None] + 1.0)), B_DK, axis=1)
    k_decay = jnp.repeat(jnp.exp(log_g[None, :] * (c_real - 1.0 - i[:, None])), B_DK, axis=1)
    chunk_decay = jnp.broadcast_to(jnp.exp(log_g * c_real)[:, None, None], (B_HEADS, 1, B_DV))
    wqk = B_HEADS * B_DK
    return pl.pallas_call(
        _retention_body,
        grid=(n, t // chunk),
        in_specs=[pl.BlockSpec((1, chunk, wqk), lambda b, c: (b, c, 0)),
                  pl.BlockSpec((1, chunk, wqk), lambda b, c: (b, c, 0)),
                  pl.BlockSpec((1, chunk, B_WIDTH), lambda b, c: (b, c, 0)),
                  pl.BlockSpec((chunk, wqk), lambda b, c: (c, 0)),
                  pl.BlockSpec((chunk, wqk), lambda b, c: (c, 0)),
                  pl.BlockSpec((chunk, wqk), lambda b, c: (0, 0)),
                  pl.BlockSpec((chunk, wqk), lambda b, c: (0, 0)),
                  pl.BlockSpec((B_HEADS, chunk, chunk), lambda b, c: (0, 0, 0)),
                  pl.BlockSpec((B_HEADS, 1, B_DV), lambda b, c: (0, 0, 0)),
                  pl.BlockSpec((1, B_HEADS, B_DK, B_DV), lambda b, c: (b, 0, 0, 0))],
        out_specs=[pl.BlockSpec((1, chunk, B_WIDTH), lambda b, c: (b, c, 0)),
                   pl.BlockSpec((1, B_HEADS, B_DK, B_DV), lambda b, c: (b, 0, 0, 0))],
        out_shape=[jax.ShapeDtypeStruct((n, t, B_WIDTH), BF16),
                   jax.ShapeDtypeStruct((n, B_HEADS, B_DK, B_DV), F32)],
        scratch_shapes=[pltpu.VMEM((B_HEADS, B_DK, B_DV), F32)],
        compiler_params=_params("parallel", "arbitrary"),
        name="retention",
    )(q, k, v, cos, sin, q_decay, k_decay, intra, chunk_decay, s0)


def _silu(z):
    return z * jax.nn.sigmoid(z)


def _even_out_body(x_ref, oa_ref, za_ref, ob_ref, zb_ref, w_ref, y_ref):
    ga = (oa_ref[...].astype(F32) * _silu(za_ref[...].astype(F32))).astype(BF16)
    acc = x_ref[...] + _dot(ga, w_ref[0:A_WIDTH, :])
    ob = ob_ref[...].astype(F32)
    zb = zb_ref[...].astype(F32)
    for h in range(B_HEADS):
        cols = slice(h * B_DV, (h + 1) * B_DV)
        seg = ob[:, cols]
        mu = jnp.mean(seg, axis=-1, keepdims=True)
        cen = seg - mu
        var = jnp.mean(cen * cen, axis=-1, keepdims=True)
        gb = (cen * lax.rsqrt(var + GN_EPS) * _silu(zb[:, cols])).astype(BF16)
        acc = acc + _dot(gb, w_ref[A_WIDTH + h * B_DV:A_WIDTH + (h + 1) * B_DV, :])
    y_ref[...] = acc


def _even_out(x2d, oa, za, ob, zb, w_out, tm):
    m, d = x2d.shape
    row = lambda width: pl.BlockSpec((tm, width), lambda i: (i, 0))
    return pl.pallas_call(
        _even_out_body,
        grid=(m // tm,),
        in_specs=[row(d), row(A_WIDTH), row(A_WIDTH), row(B_WIDTH), row(B_WIDTH),
                  pl.BlockSpec(w_out.shape, lambda i: (0, 0))],
        out_specs=row(d),
        out_shape=jax.ShapeDtypeStruct((m, d), F32),
        compiler_params=_params("parallel"),
        name="even_out",
    )(x2d, oa, za, ob, zb, w_out)


def _compress_math(x, p2_ref, wbig_ref, w2_ref):
    n_piece = x.shape[0]
    a = _dot(x.astype(BF16), wbig_ref[...])
    pc = _dot(p2_ref[...], wbig_ref[...])
    hid = []
    for g in range(C_KV_HEADS):
        lo = slice(g * 2 * CMP_HIDDEN, g * 2 * CMP_HIDDEN + CMP_HIDDEN)
        hi = slice(g * 2 * CMP_HIDDEN + CMP_HIDDEN, (g + 1) * 2 * CMP_HIDDEN)
        nxt = pltpu.roll(a[:, hi], n_piece - 1, 0)
        pos = (pc[0:1, lo] + pc[2:3, lo]) + (pc[1:2, hi] + pc[3:4, hi])
        hid.append(_silu(a[:, lo] + nxt + pos))
    return _dot(jnp.concatenate(hid, axis=-1).astype(BF16), w2_ref[...])


def _compress_body(x_ref, p2_ref, wbig_ref, w2_ref, o_ref):
    o_ref[0] = _compress_math(x_ref[0], p2_ref, wbig_ref, w2_ref)


def _compress_weights(pos_emb, w1, w2):
    per = CMP_LEN // CMP_STRIDE
    w1r = w1.reshape(per, CMP_STRIDE, HEAD_DIM, CMP_HIDDEN)
    eye = jnp.eye(C_KV_HEADS, dtype=w1.dtype)
    wbig = jnp.einsum('jldf,gh->lgdhjf', w1r, eye).reshape(CMP_STRIDE * C_KV_WIDTH, C_KV_HEADS * per * CMP_HIDDEN)
    w2bd = jnp.einsum('fd,gh->gfhd', w2, eye).reshape(C_KV_HEADS * CMP_HIDDEN, C_KV_WIDTH)
    halves = jnp.tile(pos_emb.reshape(per, CMP_STRIDE, 1, HEAD_DIM), (1, 1, C_KV_HEADS, 1)).reshape(per, -1)
    hi = halves.astype(BF16)
    lo = (halves - hi.astype(F32)).astype(BF16)
    p2 = jnp.concatenate([hi, lo, jnp.zeros((16 - 2 * per, halves.shape[1]), BF16)], axis=0)
    w1g = jnp.transpose(w1r, (1, 2, 0, 3)).reshape(CMP_STRIDE * HEAD_DIM, per * CMP_HIDDEN)
    halves_g = pos_emb.reshape(per, CMP_STRIDE * HEAD_DIM)
    hi_g = halves_g.astype(BF16)
    lo_g = (halves_g - hi_g.astype(F32)).astype(BF16)
    p2g = jnp.concatenate([hi_g, lo_g, jnp.zeros((16 - 2 * per, halves_g.shape[1]), BF16)], axis=0)
    return p2, wbig.astype(BF16), w2bd.astype(BF16), p2g, w1g.astype(BF16)


def _compress(pieces, cw):
    n, n_piece, width = pieces.shape
    p2, wbig, w2bd = cw[:3]
    const = lambda a: pl.BlockSpec(a.shape, lambda i: (0,) * a.ndim)
    return pl.pallas_call(
        _compress_body,
        grid=(n,),
        in_specs=[pl.BlockSpec((1, n_piece, width), lambda i: (i, 0, 0)), const(p2), const(wbig), const(w2bd)],
        out_specs=pl.BlockSpec((1, n_piece, C_KV_WIDTH), lambda i: (i, 0, 0)),
        out_shape=jax.ShapeDtypeStruct((n, n_piece, C_KV_WIDTH), F32),
        compiler_params=_params("parallel"),
        name="compress",
    )(pieces, p2, wbig, w2bd)


def _page_dma(pt_ref, pool_ref, buf_ref, sem_ref, n, slot, n_pages, start):
    for p in range(n_pages):
        page = pt_ref[n, p] if start else 0
        cp = pltpu.make_async_copy(pool_ref.at[page], buf_ref.at[slot, p], sem_ref.at[slot])
        if start:
            cp.start()
        else:
            cp.wait()


def _compress_paged_body(pt_ref, pool_ref, perm_ref, p2_ref, w1_ref, w2_ref, o_ref, buf, pieces, sem, *, n_pages):
    n = pl.program_id(0)
    slot = n % 2
    rows = PAGE_SIZE // CMP_STRIDE
    n_piece = n_pages * rows

    @pl.when(n == 0)
    def _():
        _page_dma(pt_ref, pool_ref, buf, sem, 0, 0, n_pages, True)

    @pl.when(n + 1 < pl.num_programs(0))
    def _():
        _page_dma(pt_ref, pool_ref, buf, sem, n + 1, 1 - slot, n_pages, True)

    _page_dma(pt_ref, pool_ref, buf, sem, n, slot, n_pages, False)
    perm = perm_ref[...]

    group = 16
    assert n_pages % group == 0 and CMP_STRIDE % 2 == 0
    low_half = lax.broadcasted_iota(jnp.int32, (rows, LANES), 1) < HEAD_DIM

    def unfold(i, carry):
        pairs = []
        for j in range(0, group, 2):
            two = buf[slot, pl.ds(i * group + j, 2)].reshape(2 * C_KV_WIDTH, PAGE_SIZE)
            pairs.append(_dot_nt(perm, two.astype(BF16)))
        for j in range(group):
            by_row = pairs[j // 2][:, (j % 2) * C_KV_WIDTH:(j % 2 + 1) * C_KV_WIDTH]
            r0 = pl.multiple_of((i * group + j) * rows, rows)
            for l in range(0, CMP_STRIDE, 2):
                even = by_row[l * rows:(l + 1) * rows, :]
                odd = by_row[(l + 1) * rows:(l + 2) * rows, :]
                lanes = slice(l * HEAD_DIM, (l + 2) * HEAD_DIM)
                swapped = pltpu.roll(jnp.where(low_half, odd, even), HEAD_DIM, 1)
                pieces[0, pl.ds(r0, rows), lanes] = jnp.where(low_half, even, swapped)
                pieces[1, pl.ds(r0, rows), lanes] = jnp.where(low_half, swapped, odd)
        return carry

    lax.fori_loop(0, n_pages // group, unfold, 0)
    hidden = CMP_HIDDEN
    pc = _dot(p2_ref[...], w1_ref[...])
    pos = (pc[0:1, :hidden] + pc[2:3, :hidden]) + (pc[1:2, hidden:] + pc[3:4, hidden:])
    acts = [_dot(pieces[g].astype(BF16), w1_ref[...]) for g in range(C_KV_HEADS)]
    hid = [_silu(a[:, :hidden] + pltpu.roll(a[:, hidden:], n_piece - 1, 0) + pos) for a in acts]
    o_ref[0] = _dot(jnp.concatenate(hid, axis=-1).astype(BF16), w2_ref[...])


def _compress_paged(pool, page_table, cw):
    n, n_pages = page_table.shape
    rows = PAGE_SIZE // CMP_STRIDE
    n_piece = n_pages * rows
    _, _, w2bd, p2g, w1g = cw
    perm = np.zeros((PAGE_SIZE, PAGE_SIZE), np.float32)
    for l in range(CMP_STRIDE):
        for piece in range(rows):
            perm[l * rows + piece, CMP_STRIDE * piece + l] = 1.0
    perm = jnp.asarray(perm, BF16)
    pages = jnp.transpose(pool, (0, 2, 3, 1)).reshape(pool.shape[0], C_KV_WIDTH, PAGE_SIZE)
    const = lambda a: pl.BlockSpec(a.shape, lambda i, pt: (0,) * a.ndim)
    return pl.pallas_call(
        functools.partial(_compress_paged_body, n_pages=n_pages),
        grid_spec=pltpu.PrefetchScalarGridSpec(
            num_scalar_prefetch=1,
            grid=(n,),
            in_specs=[pl.BlockSpec(memory_space=pl.ANY), const(perm), const(p2g), const(w1g), const(w2bd)],
            out_specs=pl.BlockSpec((1, n_piece, C_KV_WIDTH), lambda i, pt: (i, 0, 0)),
            scratch_shapes=[pltpu.VMEM((2, n_pages, C_KV_WIDTH, PAGE_SIZE), F32),
                            pltpu.VMEM((C_KV_HEADS, n_piece, CMP_STRIDE * HEAD_DIM), F32),
                            pltpu.SemaphoreType.DMA((2,))]),
        out_shape=jax.ShapeDtypeStruct((n, n_piece, C_KV_WIDTH), F32),
        compiler_params=_params("arbitrary"),
        name="compress_paged",
    )(page_table, pages, perm, p2g, w1g, w2bd)


def _overlap_weights(n_cmp, n_cmp_pad, n_slc, n_slc_pad):
    i = np.arange(n_cmp_pad)[:, None]
    j = np.arange(n_slc_pad)[None, :]
    ov = (np.minimum(i * CMP_STRIDE + CMP_LEN, (j + 1) * SEL_BLOCK) - np.maximum(i * CMP_STRIDE, j * SEL_BLOCK))
    w = np.where((i < n_cmp) & (j < n_slc), np.clip(ov, 0, None) / CMP_LEN, 0.0)
    return jnp.asarray(w, dtype=BF16)


def _select_blocks(p_slc, qpos, n_slc):
    shape = p_slc.shape
    lane = lax.broadcasted_iota(jnp.int32, shape, 1)
    cur = _idiv(qpos, SEL_BLOCK)
    forced = (lane == 0) | (lane == cur) | (lane == cur - 1)
    causal = lane * SEL_BLOCK <= qpos
    score = jnp.where(forced, SEL_FORCE, jnp.where(causal, p_slc, -SEL_FORCE))
    score = jnp.where(lane < n_slc, score, -3.0 * SEL_FORCE)
    lane_f = lane.astype(F32)
    sel = jnp.zeros(shape, F32)
    for _ in range(min(SEL_TOPK, n_slc)):
        m = jnp.max(score, axis=-1, keepdims=True)
        first = jnp.min(jnp.where(score == m, lane_f, 1e9), axis=-1, keepdims=True)
        hit = lane_f == first
        sel = jnp.where(hit, jnp.where(m > -0.5 * SEL_FORCE, 1.0, 0.0), sel)
        score = jnp.where(hit, -4.0 * SEL_FORCE, score)
    return sel


def _select_blocks_t(p_slc, qpos, n_slc):
    n_rows = -(-n_slc // 8) * 8
    shape = (n_rows, p_slc.shape[1])
    blk = lax.broadcasted_iota(jnp.int32, shape, 0)
    cur = _idiv(qpos, SEL_BLOCK)
    forced = (blk == 0) | (blk == cur) | (blk == cur - 1)
    causal = blk * SEL_BLOCK <= qpos
    score = jnp.where(forced, SEL_FORCE, jnp.where(causal, p_slc[:n_rows], -SEL_FORCE))
    score = jnp.where(blk < n_slc, score, -3.0 * SEL_FORCE)
    tiles = [score[8 * v:8 * v + 8] for v in range(n_rows // 8)]
    sub = lax.broadcasted_iota(jnp.int32, (8, shape[1]), 0)
    ranks = [jnp.zeros((8, shape[1]), F32) for _ in tiles]
    for i in range(n_slc):
        row = tiles[i // 8][i % 8:i % 8 + 1]
        for v, tile in enumerate(tiles):
            ge = jnp.where(row >= tile, 1.0, 0.0)
            gt = jnp.where(row > tile, 1.0, 0.0)
            if 8 * v > i:
                ranks[v] = ranks[v] + ge
            elif 8 * v + 7 <= i:
                ranks[v] = ranks[v] + gt
            else:
                ranks[v] = ranks[v] + jnp.where(sub > i - 8 * v, ge, gt)
    rank = jnp.concatenate(ranks, axis=0)
    return jnp.where(rank < min(SEL_TOPK, n_slc), jnp.where(score > -0.5 * SEL_FORCE, 1.0, 0.0), 0.0)


NSA_TILE = 2 * LANES
ONES_ROWS = 16
LOG2E = 1.4426950408889634
LN2 = 0.6931471805599453


def _nsa_prompt_body(q_ref, kc_ref, vct_ref, ks_ref, vst_ref, kw_ref, vwt_ref, g_ref, wovt_ref, o_ref, m_sc, acc_sc,
                     *, n_cmp, n_slc):
    tq = tk = NSA_TILE
    rep = C_HEADS // C_KV_HEADS
    blk = pl.program_id(1)
    q0 = blk * tq
    n_cmp_pad = kc_ref.shape[1]
    sig_t = jax.nn.sigmoid(g_ref[0])
    q_t = [jnp.transpose(q_ref[0, :, c * LANES:(c + 1) * LANES].astype(F32))
           for c in range(C_WIDTH // LANES)]
    qpos = q0 + lax.broadcasted_iota(jnp.int32, (1, tq), 1)
    cmp_i = lax.broadcasted_iota(jnp.int32, (n_cmp_pad, tq), 0)
    cmp_ok = (cmp_i * CMP_STRIDE + (CMP_LEN - 1) <= qpos) & (cmp_i < n_cmp)
    cmp_bias = jnp.where(cmp_ok, 0.0, NEG_INF)
    any_block = jnp.where(qpos >= CMP_LEN - 1, 1.0, 0.0)
    ahead = lax.broadcasted_iota(jnp.int32, (tk, tq), 0) - lax.broadcasted_iota(jnp.int32, (tk, tq), 1)
    diag_bias = jnp.where(ahead <= 0, 0.0, NEG_INF)
    far_bias = jnp.where(ahead >= 0, 0.0, NEG_INF)
    ones_rows = jnp.where(lax.broadcasted_iota(jnp.int32, (ONES_ROWS, tk), 0) == 0, 1.0, 0.0).astype(BF16)

    def flash_update(scores, v_t):
        v_ext = jnp.concatenate([v_t, ones_rows], axis=0)
        for r, s in enumerate(scores):
            m_old = m_sc[r:r + 1, :]
            m_new = jnp.maximum(m_old, jnp.max(s, axis=0, keepdims=True))
            m_sc[r:r + 1, :] = m_new
            pv = _dot(v_ext, jnp.exp2(s - m_new).astype(BF16))
            acc_sc[r] = jnp.exp2(m_old - m_new) * acc_sc[r] + pv

    def flash_reset():
        m_sc[...] = jnp.full(m_sc.shape, NEG_INF, F32)
        acc_sc[...] = jnp.zeros(acc_sc.shape, F32)

    def flash_result(r):
        acc = acc_sc[r]
        return acc[:HEAD_DIM] * (1.0 / acc[HEAD_DIM:HEAD_DIM + 1])

    outs = []
    for g in range(C_KV_HEADS):
        gc = slice(g * HEAD_DIM, (g + 1) * HEAD_DIM)
        heads = [g * rep + r for r in range(rep)]
        q_heads = [q_t[h // 2][(h % 2) * HEAD_DIM:(h % 2 + 1) * HEAD_DIM] for h in heads]
        q_nat = [(qh * (HEAD_DIM ** -0.5)).astype(BF16) for qh in q_heads]
        q_log2 = [(qh * (HEAD_DIM ** -0.5 * LOG2E)).astype(BF16) for qh in q_heads]
        kcb = kc_ref[0, :, gc].astype(BF16)
        vct = vct_ref[0, gc, :].astype(BF16)
        scores = [_dot(kcb, q_nat[r]) + cmp_bias for r in range(rep)]
        probs = []
        imp = None
        for s in scores:
            p = jnp.exp(s - jnp.max(s, axis=0, keepdims=True))
            p = p * (any_block / jnp.sum(p, axis=0, keepdims=True))
            imp = p if imp is None else imp + p
            probs.append(p.astype(BF16))
        o_cmp = [_dot(vct, p) for p in probs]
        hi = imp.astype(BF16)
        r1 = imp - hi.astype(F32)
        mid = r1.astype(BF16)
        lo = (r1 - mid.astype(F32)).astype(BF16)
        p_slc = _dot(wovt_ref[...], hi) + _dot(wovt_ref[...], mid) + _dot(wovt_ref[...], lo)
        sel_bias = jnp.where(_select_blocks_t(p_slc, qpos, n_slc) > 0.5, 0.0, NEG_INF).astype(BF16)
        q_sel = [jnp.concatenate([qh, sel_bias], axis=0) for qh in q_log2]
        flash_reset()

        def sel_scores(k0):
            k_tile = ks_ref[0, pl.ds(pl.multiple_of(k0, tk), tk), g * LANES:(g + 1) * LANES]
            return [_dot(k_tile, qs) for qs in q_sel]

        def sel_values(k0):
            return vst_ref[0, gc, pl.ds(pl.multiple_of(k0, tk), tk)]

        def past_pair(i, carry):
            k0 = i * (2 * tk)
            first, second = sel_scores(k0), sel_scores(k0 + tk)
            flash_update(first, sel_values(k0))
            flash_update(second, sel_values(k0 + tk))
            return carry

        lax.fori_loop(0, blk // 2, past_pair, 0)
        k0 = (blk // 2) * (2 * tk)

        @pl.when(k0 == q0)
        def _():
            flash_update([s + diag_bias for s in sel_scores(q0)], sel_values(q0))

        @pl.when(k0 != q0)
        def _():
            first, second = sel_scores(k0), [s + diag_bias for s in sel_scores(q0)]
            flash_update(first, sel_values(k0))
            flash_update(second, sel_values(q0))

        o_sel = [flash_result(r) for r in range(rep)]
        flash_reset()
        n_back = C_WINDOW // tk
        win = []
        for back in range(n_back + 1):
            k0 = pl.multiple_of(jnp.maximum(q0 - back * tk, 0), tk)
            k_tile = kw_ref[0, pl.ds(k0, tk), gc]
            absent = jnp.where(blk >= back, 0.0, NEG_INF)
            edge = diag_bias if back == 0 else far_bias if back == n_back else None
            bias = absent if edge is None else edge + absent
            win.append(([_dot(k_tile, qh) + bias for qh in q_log2], vwt_ref[0, gc, pl.ds(k0, tk)]))
        for scores, values in win:
            flash_update(scores, values)
        for r, h in enumerate(heads):
            outs.append(sig_t[3 * h:3 * h + 1] * o_cmp[r] + sig_t[3 * h + 1:3 * h + 2] * o_sel[r]
                        + sig_t[3 * h + 2:3 * h + 3] * flash_result(r))
    for c in range(C_WIDTH // LANES):
        pair = jnp.transpose(jnp.concatenate(outs[2 * c:2 * c + 2], axis=0))
        o_ref[0, :, c * LANES:(c + 1) * LANES] = pair.astype(o_ref.dtype)


def _nsa_prompt(q, kcb, vcb, ks_ext, vs_t, kw, vw_t, gates_t, n_cmp):
    b, t, wq = q.shape
    n_cmp_pad = kcb.shape[1]
    n_slc = -(-t // SEL_BLOCK)
    tile = NSA_TILE
    assert t % (2 * tile) == 0 and n_slc == HEAD_DIM and C_WINDOW % tile == 0
    wov_t = _overlap_weights(n_cmp, n_cmp_pad, n_slc, LANES).T
    vcb_t = jnp.swapaxes(vcb, 1, 2)
    rep = C_HEADS // C_KV_HEADS
    whole = lambda a: pl.BlockSpec((1,) + a.shape[1:], lambda bi, i: (bi, 0, 0))
    return pl.pallas_call(
        functools.partial(_nsa_prompt_body, n_cmp=n_cmp, n_slc=n_slc),
        grid=(b, t // tile),
        in_specs=[pl.BlockSpec((1, tile, wq), lambda bi, i: (bi, i, 0)),
                  whole(kcb), whole(vcb_t), whole(ks_ext), whole(vs_t), whole(kw), whole(vw_t),
                  pl.BlockSpec((1, gates_t.shape[1], tile), lambda bi, i: (bi, 0, i)),
                  pl.BlockSpec(wov_t.shape, lambda bi, i: (0, 0))],
        out_specs=pl.BlockSpec((1, tile, wq), lambda bi, i: (bi, i, 0)),
        out_shape=jax.ShapeDtypeStruct((b, t, wq), BF16),
        scratch_shapes=[pltpu.VMEM((rep, tile), F32), pltpu.VMEM((rep, HEAD_DIM + ONES_ROWS, tile), F32)],
        compiler_params=_params("parallel", "arbitrary"),
        name="nsa_prompt",
    )(q, kcb, vcb_t, ks_ext, vs_t, kw, vw_t, gates_t, wov_t)


def _block_diag(q, blk_of_row, n_blk):
    return jnp.concatenate([jnp.where(blk_of_row == b, q, 0.0) for b in range(n_blk)], axis=-1)


def _diag_blocks(r, blk_of_row, n_blk):
    out = jnp.where(blk_of_row == 0, r[:, 0:HEAD_DIM], 0.0)
    for b in range(1, n_blk):
        out = out + jnp.where(blk_of_row == b, r[:, b * HEAD_DIM:(b + 1) * HEAD_DIM], 0.0)
    return out


def _row_ids(n_rows, n_heads, n_blk):
    row = lax.broadcasted_iota(jnp.int32, (n_rows, 1), 0)
    step = _idiv(row, n_heads)
    blk = _idiv(_imod(row, n_heads), n_heads // n_blk)
    return step, blk


def _pattern_weight(dist, patterns):
    w = jnp.zeros(dist.shape, F32)
    for window, dil in patterns:
        w = w + jnp.where(_imod(dist, dil) == 0, jnp.where(dist <= window, 1.0, 0.0), 0.0)
    return jnp.where(dist >= 0, w, 0.0)


def _decode_body(*refs, n_heads, n_blk, patterns, gate_col):
    q_ref, kc_ref, vc_ref, kn_ref, vn_ref = refs[:5]
    g_ref = refs[5] if gate_col is not None else None
    o_ref = refs[-1]
    n_rows = q_ref.shape[1]
    n_cache = kc_ref.shape[2]
    n_new = kn_ref.shape[1]
    step, blk = _row_ids(n_rows, n_heads, n_blk)
    qbd = _block_diag(q_ref[0] * (HEAD_DIM ** -0.5), blk, n_blk)
    s_c = _dot(qbd.astype(BF16), kc_ref[0].astype(BF16))
    w_c = _pattern_weight(n_cache + step - lax.broadcasted_iota(jnp.int32, (n_rows, n_cache), 1), patterns)
    s_c = jnp.where(w_c > 0, s_c, NEG_INF)
    kn = kn_ref[0]
    vn = vn_ref[0]
    s_n, w_n = [], []
    for c in range(n_new):
        s = jnp.sum(qbd * kn[c:c + 1, :], axis=-1, keepdims=True)
        w = _pattern_weight(step - c, patterns)
        s_n.append(jnp.where(w > 0, s, NEG_INF))
        w_n.append(w)
    m = jnp.max(s_c, axis=-1, keepdims=True)
    for s in s_n:
        m = jnp.maximum(m, s)
    p_c = w_c * jnp.exp(s_c - m)
    l = jnp.sum(p_c, axis=-1, keepdims=True)
    r = _dot_nt(p_c.astype(BF16), vc_ref[0].astype(BF16))
    for c in range(n_new):
        p = w_n[c] * jnp.exp(s_n[c] - m)
        l = l + p
        r = r + p * vn[c:c + 1, :]
    o = _diag_blocks(r, blk, n_blk) / l
    if g_ref is not None:
        o = o * jax.nn.sigmoid(g_ref[0])[:, gate_col:gate_col + 1]
    o_ref[0] = o


def _decode_attention(q, kc, vc, kn, vn, *, n_heads, n_blk, patterns, gate=None, gate_col=None):
    n, n_rows, _ = q.shape
    args = [q, kc, vc, kn, vn]
    if gate is not None:
        args.append(gate)
    whole = lambda a: pl.BlockSpec((1,) + a.shape[1:], lambda i: (i, 0, 0))
    return pl.pallas_call(
        functools.partial(_decode_body, n_heads=n_heads, n_blk=n_blk, patterns=patterns,
                          gate_col=gate_col if gate is not None else None),
        grid=(n,),
        in_specs=[whole(a) for a in args],
        out_specs=pl.BlockSpec((1, n_rows, HEAD_DIM), lambda i: (i, 0, 0)),
        out_shape=jax.ShapeDtypeStruct((n, n_rows, HEAD_DIM), F32),
        compiler_params=_params("parallel"),
        name="decode_attention",
    )(*args)


def _nsa_decode_body(pt_ref, q_ref, kc_ref, vc_ref, ksp_ref, vsp_ref, kn_ref, vn_ref, g_ref, wov_ref, o_ref,
                     kbuf, vbuf, expand, ksem, vsem, *, n_pages, n_cmp, n_slc):
    n = pl.program_id(0)
    slot = n % 2
    n_rows = q_ref.shape[1]
    n_new = kn_ref.shape[1]
    n_past = kbuf.shape[2]
    n_slc_pad = wov_ref.shape[1]
    rep = C_HEADS // C_KV_HEADS

    def gather(seq, dst_slot, start):
        for pool_ref, buf, sem in ((ksp_ref, kbuf, ksem), (vsp_ref, vbuf, vsem)):
            for p in range(n_pages):
                page = pt_ref[seq, p] if start else 0
                cp = pltpu.make_async_copy(pool_ref.at[page], buf.at[dst_slot, :, pl.ds(p * PAGE_SIZE, PAGE_SIZE)],
                                           sem.at[dst_slot])
                if start:
                    cp.start()
                else:
                    cp.wait()

    @pl.when(n == 0)
    def _():
        gather(0, 0, True)
        step_cols = 1024
        for c0 in range(0, n_past, step_cols):
            shape = (n_slc_pad, min(step_cols, n_past - c0))
            blk_of_key = _idiv(c0 + lax.broadcasted_iota(jnp.int32, shape, 1), SEL_BLOCK)
            hit = blk_of_key == lax.broadcasted_iota(jnp.int32, shape, 0)
            expand[:, c0:c0 + shape[1]] = jnp.where(hit, 1.0, 0.0).astype(BF16)

    @pl.when(n + 1 < pl.num_programs(0))
    def _():
        gather(n + 1, 1 - slot, True)

    step, blk = _row_ids(n_rows, C_HEADS, C_KV_HEADS)
    qpos = n_past + step
    qbd = _block_diag(q_ref[0] * (HEAD_DIM ** -0.5), blk, C_KV_HEADS)
    qbd_in = qbd.astype(BF16)
    n_cmp_pad = kc_ref.shape[1]
    cmp_i = lax.broadcasted_iota(jnp.int32, (n_rows, n_cmp_pad), 1)
    cmp_ok = (cmp_i * CMP_STRIDE + (CMP_LEN - 1) <= qpos) & (cmp_i < n_cmp)
    s = jnp.where(cmp_ok, _dot_nt(qbd_in, kc_ref[0].astype(BF16)), NEG_INF)
    m = jnp.max(s, axis=-1, keepdims=True)
    p = jnp.where(cmp_ok, jnp.exp(s - m), 0.0)
    l = jnp.sum(p, axis=-1, keepdims=True)
    p = p / jnp.where(l > 0, l, 1.0)
    o_cmp = _diag_blocks(_dot(p.astype(BF16), vc_ref[0].astype(BF16)), blk, C_KV_HEADS)
    grp = (n_rows // rep, rep, n_cmp_pad)
    imp = jnp.broadcast_to(jnp.sum(p.reshape(grp), axis=1, keepdims=True), grp).reshape(n_rows, n_cmp_pad)
    sel_rows = _select_blocks(_split3_dot(imp, wov_ref[...]), qpos, n_slc)
    gather(n, slot, False)
    blk_bias = jnp.where(sel_rows > 0.5, 0.0, NEG_INF).astype(BF16)
    s_p = _dot(qbd_in, kbuf[slot].astype(BF16)) + _dot(blk_bias, expand[...])
    kn = kn_ref[0]
    vn = vn_ref[0]
    s_n = []
    for c in range(n_new):
        blk_c = (n_past + c) // SEL_BLOCK
        ok = (sel_rows[:, blk_c:blk_c + 1] > 0.5) & (n_past + c <= qpos)
        s_n.append(jnp.where(ok, jnp.sum(qbd * kn[c:c + 1, :], axis=-1, keepdims=True), NEG_INF))
    m = jnp.max(s_p, axis=-1, keepdims=True)
    for sc in s_n:
        m = jnp.maximum(m, sc)
    p_p = jnp.exp(s_p - m)
    l = jnp.sum(p_p, axis=-1, keepdims=True)
    r = _dot_nt(p_p.astype(BF16), vbuf[slot].astype(BF16))
    for c in range(n_new):
        pn = jnp.exp(s_n[c] - m)
        l = l + pn
        r = r + pn * vn[c:c + 1, :]
    o_sel = _diag_blocks(r, blk, C_KV_HEADS) / l
    sig = jax.nn.sigmoid(g_ref[0])
    o_ref[0] = sig[:, 0:1] * o_cmp + sig[:, 1:2] * o_sel


def _nsa_decode(q, kcb, vcb, pool_ks, pool_vs, page_table, kn, vn, gate, n_cmp):
    n, n_rows, _ = q.shape
    n_pages = page_table.shape[1]
    n_past = n_pages * PAGE_SIZE
    n_new = kn.shape[1]
    n_slc = -(-(n_past + n_new) // SEL_BLOCK)
    n_slc_pad = -(-n_slc // LANES) * LANES
    wov = _overlap_weights(n_cmp, kcb.shape[1], n_slc, n_slc_pad)
    whole = lambda a: pl.BlockSpec((1,) + a.shape[1:], lambda i, pt: (i, 0, 0))
    pools = [jnp.transpose(a, (0, 2, 3, 1)).reshape(a.shape[0], C_KV_WIDTH, PAGE_SIZE) for a in (pool_ks, pool_vs)]
    return pl.pallas_call(
        functools.partial(_nsa_decode_body, n_pages=n_pages, n_cmp=n_cmp, n_slc=n_slc),
        grid_spec=pltpu.PrefetchScalarGridSpec(
            num_scalar_prefetch=1,
            grid=(n,),
            in_specs=[whole(q), whole(kcb), whole(vcb), pl.BlockSpec(memory_space=pl.ANY),
                      pl.BlockSpec(memory_space=pl.ANY), whole(kn), whole(vn), whole(gate),
                      pl.BlockSpec(wov.shape, lambda i, pt: (0, 0))],
            out_specs=pl.BlockSpec((1, n_rows, HEAD_DIM), lambda i, pt: (i, 0, 0)),
            scratch_shapes=[pltpu.VMEM((2, C_KV_WIDTH, n_past), F32), pltpu.VMEM((2, C_KV_WIDTH, n_past), F32),
                            pltpu.VMEM((n_slc_pad, n_past), BF16),
                            pltpu.SemaphoreType.DMA((2,)), pltpu.SemaphoreType.DMA((2,))]),
        out_shape=jax.ShapeDtypeStruct((n, n_rows, HEAD_DIM), F32),
        compiler_params=_params("arbitrary"),
        name="nsa_decode",
    )(page_table, q, kcb, vcb, *pools, kn, vn, gate, wov)


def _odd_out_body(x_ref, *refs):
    *branch_refs, z_ref, w_ref, gf_ref, y_ref = refs
    o = branch_refs[0][...].astype(F32)
    for ref in branch_refs[1:]:
        o = o + ref[...].astype(F32)
    mixed = (o * _silu(z_ref[...].astype(F32))).astype(BF16)
    h = x_ref[...] + _dot(mixed, w_ref[...])
    y_ref[...] = h * lax.rsqrt(jnp.mean(h * h, axis=-1, keepdims=True) + RMS_EPS) * gf_ref[...]


def _odd_out(x2d, branches, z, w_out, final_gain, tm):
    m, d = x2d.shape
    row = pl.BlockSpec((tm, d), lambda i: (i, 0))
    return pl.pallas_call(
        _odd_out_body,
        grid=(m // tm,),
        in_specs=[row] * (len(branches) + 2) + [pl.BlockSpec(w_out.shape, lambda i: (0, 0)),
                                                 pl.BlockSpec((1, d), lambda i: (0, 0))],
        out_specs=row,
        out_shape=jax.ShapeDtypeStruct((m, d), F32),
        compiler_params=_params("parallel"),
        name="odd_out",
    )(x2d, *branches, z, w_out, final_gain.reshape(1, d))


def _even_outs(cache_from):
    a, bk = A_WIDTH, B_HEADS * B_DK
    outs = [(0, a, F32, None), (a, a, F32, None), (2 * a, a, F32, None)]
    if cache_from is not None:
        outs += [(a, a, F32, cache_from), (2 * a, a, F32, cache_from)]
    outs += [(3 * a, a, BF16, None),
             (4 * a, bk, F32, None), (4 * a + bk, bk, F32, None),
             (4 * a + 2 * bk, B_WIDTH, F32, None), (4 * a + 2 * bk + B_WIDTH, B_WIDTH, BF16, None)]
    return outs


def _from_feature_major(u, n_heads):
    b, _, length = u.shape
    return jnp.transpose(u.reshape(b, n_heads, HEAD_DIM, length), (0, 3, 1, 2))


def _to_feature_major(u):
    n, length, n_heads, hd = u.shape
    return jnp.transpose(u, (0, 2, 3, 1)).reshape(n, n_heads * hd, length)


def _even_layer_prompt(x, gain, w_in, w_out):
    b, t, d = x.shape
    x2d = x.reshape(b * t, d)
    keep = min(A_PATTERNS[-1][0], t)
    qa, ka, va, ka_t, va_t, za, qb, kb, vb, zb = _norm_proj(x2d, gain, w_in, _even_outs(t - keep), PROJ_ROWS, t)
    seq = lambda u: u.reshape(b, t, u.shape[-1])
    oa = _dilated_attention(seq(qa), seq(ka), seq(va), A_PATTERNS)
    chunk = RET_CHUNK if t % RET_CHUNK == 0 else t
    s0 = jnp.zeros((b, B_HEADS, B_DK, B_DV), F32)
    ob, s_fin = _retention(seq(qb), seq(kb), seq(vb), s0, jnp.arange(t), chunk=chunk, c_real=chunk)
    y = _even_out(x2d, oa.reshape(b * t, A_WIDTH), za, ob.reshape(b * t, B_WIDTH), zb, w_out, OUT_ROWS)
    return y.reshape(b, t, d), _from_feature_major(ka_t, A_HEADS), _from_feature_major(va_t, A_HEADS), s_fin


def _even_layer_sample(x, cache_k, cache_v, state, gain, w_in, w_out, past_len):
    n, s_len, d = x.shape
    x2d = x.reshape(n * s_len, d)
    qa, ka32, va32, za, qb, kb, vb, zb = _norm_proj(x2d, gain, w_in, _even_outs(None), n * s_len, n * s_len)
    seq = lambda u: u.reshape(n, s_len, u.shape[-1])
    oa = _decode_attention(qa.reshape(n, s_len * A_HEADS, HEAD_DIM), _to_feature_major(cache_k),
                           _to_feature_major(cache_v), seq(ka32), seq(va32), n_heads=A_HEADS, n_blk=A_HEADS,
                           patterns=A_PATTERNS)
    oa = oa.reshape(n * s_len, A_WIDTH)
    chunk = 16
    pad = lambda u: jnp.pad(seq(u), ((0, 0), (0, chunk - s_len), (0, 0)))
    pos = past_len + jnp.arange(chunk)
    ob, s_new = _retention(pad(qb), pad(kb), pad(vb), state, pos, chunk=chunk, c_real=s_len)
    ob = ob[:, :s_len].reshape(n * s_len, B_WIDTH)
    y = _even_out(x2d, oa, za, ob, zb, w_out, n * s_len)
    heads = lambda u: u.reshape(n, s_len, A_HEADS, HEAD_DIM)
    return y.reshape(n, s_len, d), heads(ka32), heads(va32), s_new


ODD_KV = ("kc", "vc", "ks", "vs", "kw", "vw")
ODD_GATE_COL = C_WIDTH + len(ODD_KV) * C_KV_WIDTH
ODD_Z_COL = ODD_GATE_COL + 3 * C_HEADS


def _odd_outs_prompt(win_from):
    outs = [(0, C_WIDTH, BF16, None)]
    for i, name in enumerate(ODD_KV):
        off = C_WIDTH + i * C_KV_WIDTH
        outs.append((off, C_KV_WIDTH, BF16, 0 if name in ("vs", "vw") else BLOCK_TAGGED if name == "ks" else None))
        outs.append((off, C_KV_WIDTH, F32, win_from if name in ("kw", "vw") else 0))
    outs += [(ODD_GATE_COL, 3 * C_HEADS, F32, 0), (ODD_Z_COL, C_WIDTH, BF16, None)]
    return outs


def _odd_outs_sample():
    outs = [(0, C_WIDTH, BF16, None)]
    outs += [(C_WIDTH + i * C_KV_WIDTH, C_KV_WIDTH, F32, None) for i in range(len(ODD_KV))]
    outs += [(ODD_GATE_COL, 3 * C_HEADS, F32, None), (ODD_Z_COL, C_WIDTH, BF16, None)]
    return outs


def _odd_layer_prompt(x, gain, w_in, w_out, cw_k, cw_v, final_gain):
    b, t, d = x.shape
    x2d = x.reshape(b * t, d)
    keep = min(C_WINDOW, t)
    (q, kc, kc_t, vc, vc_t, ks, ks_t, vs_tb, vs_t, kw, kw_t, vw_tb, vw_t, gl_t, z) = _norm_proj(
        x2d, gain, w_in, _odd_outs_prompt(t - keep), PROJ_ROWS, t)
    seq = lambda u: u.reshape(b, t, u.shape[-1])
    n_cmp = (t - CMP_LEN) // CMP_STRIDE + 1
    pieces = lambda u: u.reshape(b, t // CMP_STRIDE, CMP_STRIDE * C_KV_WIDTH)
    kcb = _compress(pieces(kc), cw_k)
    vcb = _compress(pieces(vc), cw_v)
    o = _nsa_prompt(seq(q), kcb, vcb, seq(ks), vs_tb, seq(kw), vw_tb, gl_t, n_cmp)
    y = _odd_out(x2d, [o.reshape(b * t, C_WIDTH)], z, w_out, final_gain, OUT_ROWS)
    kv = lambda u: _from_feature_major(u, C_KV_HEADS)
    return y.reshape(b, t, d), kv(kc_t), kv(vc_t), kv(ks_t), kv(vs_t), kv(kw_t), kv(vw_t)


def _odd_layer_sample(x, page_table, pool_kc, pool_vc, pool_ks, pool_vs, buf_kw, buf_vw, gain, w_in, w_out,
                      cw_k, cw_v, final_gain):
    n, s_len, d = x.shape
    x2d = x.reshape(n * s_len, d)
    q, kc, vc, ks32, vs32, kw32, vw32, gl, z = _norm_proj(x2d, gain, w_in, _odd_outs_sample(), n * s_len, n * s_len)
    seq = lambda u: u.reshape(n, s_len, u.shape[-1])
    n_past = page_table.shape[1] * PAGE_SIZE
    n_cmp = (n_past + s_len - CMP_LEN) // CMP_STRIDE + 1
    assert CMP_STRIDE * (n_cmp - 1) + CMP_LEN <= n_past
    kcb = _compress_paged(pool_kc, page_table, cw_k)
    vcb = _compress_paged(pool_vc, page_table, cw_v)
    rows = lambda u: u.astype(F32).reshape(n, s_len * C_HEADS, HEAD_DIM)
    gate = gl.reshape(n, s_len * C_HEADS, 3)
    o_cs = _nsa_decode(rows(q), kcb, vcb, pool_ks, pool_vs, page_table, seq(ks32), seq(vs32), gate, n_cmp)
    o_win = _decode_attention(rows(q), _to_feature_major(buf_kw), _to_feature_major(buf_vw),
                              seq(kw32), seq(vw32), n_heads=C_HEADS, n_blk=C_KV_HEADS,
                              patterns=((C_WINDOW, 1),), gate=gate, gate_col=2)
    y = _odd_out(x2d, [o_cs.reshape(n * s_len, C_WIDTH), o_win.reshape(n * s_len, C_WIDTH)], z, w_out, final_gain,
                 n * s_len)
    kv = lambda u: u.reshape(n, s_len, C_KV_HEADS, HEAD_DIM)
    return y.reshape(n, s_len, d), kv(kc), kv(vc), kv(ks32), kv(vs32), kv(kw32), kv(vw32)


def kernel(x_prompt, x_sample, cache_a_k, cache_a_v, state_ret, cache_c_kcmp, cache_c_vcmp, cache_c_ksel,
           cache_c_vsel, cache_c_kwin, cache_c_vwin, page_table, even_norm, even_w_in, even_w_out, odd_norm,
           odd_w_in, odd_w_out, cmp_pos_k, cmp_w1_k, cmp_w2_k, cmp_pos_v, cmp_w1_v, cmp_w2_v, final_norm):
    assert even_norm.shape[0] == 1 and odd_norm.shape[0] == 1
    past_len = page_table.shape[1] * PAGE_SIZE
    w_in_e = even_w_in[0].T.astype(BF16)
    w_out_e = even_w_out[0].astype(BF16)
    w_in_o = odd_w_in[0].T.astype(BF16)
    w_out_o = odd_w_out[0].astype(BF16)
    cw_k = _compress_weights(cmp_pos_k[0], cmp_w1_k[0], cmp_w2_k[0])
    cw_v = _compress_weights(cmp_pos_v[0], cmp_w1_v[0], cmp_w2_v[0])

    hp, ak_p, av_p, ret_p = _even_layer_prompt(x_prompt, even_norm[0], w_in_e, w_out_e)
    hs, ak_s, av_s, ret_s = _even_layer_sample(x_sample, cache_a_k[0], cache_a_v[0], state_ret[0], even_norm[0],
                                               w_in_e, w_out_e, past_len)
    yp, *rows_p = _odd_layer_prompt(hp, odd_norm[0], w_in_o, w_out_o, cw_k, cw_v, final_norm)
    ys, *rows_s = _odd_layer_sample(hs, page_table, cache_c_kcmp[0], cache_c_vcmp[0], cache_c_ksel[0],
                                    cache_c_vsel[0], cache_c_kwin[0], cache_c_vwin[0], odd_norm[0], w_in_o, w_out_o,
                                    cw_k, cw_v, final_norm)
    lead = lambda u: u[None]
    out = [yp, ys, lead(ak_p), lead(ak_s), lead(av_p), lead(av_s), lead(ret_p), lead(ret_s)]
    for rp, rs in zip(rows_p, rows_s):
        out += [lead(rp), lead(rs)]
    return tuple(out)
```

```python
import functools

import numpy as np
import jax
import jax.numpy as jnp
from jax import lax
from jax.experimental import pallas as pl
from jax.experimental.pallas import tpu as pltpu

F32 = jnp.float32
BF16 = jnp.bfloat16

HEAD_DIM = 64
A_HEADS = 8
A_WIDTH = A_HEADS * HEAD_DIM
A_PATTERNS = ((128, 1), (512, 4), (2048, 16))
B_HEADS = 4
B_DK = 64
B_DV = 128
B_WIDTH = B_HEADS * B_DV
RET_CHUNK = 256
ROPE_BASE = 10000.0
GN_EPS = 1e-5
C_HEADS = 16
C_KV_HEADS = 2
C_WIDTH = C_HEADS * HEAD_DIM
C_KV_WIDTH = C_KV_HEADS * HEAD_DIM
CMP_LEN = 32
CMP_STRIDE = 16
CMP_HIDDEN = 256
SEL_BLOCK = 64
SEL_TOPK = 16
SEL_FORCE = 1e9
C_WINDOW = 512
PAGE_SIZE = 128
RMS_EPS = 1e-6
NEG_INF = -1e30

LANES = 128
ATTN_TQ = 128
PROJ_ROWS = 512
OUT_ROWS = 1024
VMEM_LIMIT = 56 * 1024 * 1024


def _params(*sem):
    return pltpu.CompilerParams(dimension_semantics=sem, vmem_limit_bytes=VMEM_LIMIT)


def _dot(a, b):
    return jnp.dot(a, b, preferred_element_type=F32)


def _dot_nt(a, b):
    return lax.dot_general(a, b, (((1,), (1,)), ((), ())), preferred_element_type=F32)


def _dot_tn(a, b):
    return lax.dot_general(a, b, (((0,), (0,)), ((), ())), preferred_element_type=F32)


def _idiv(x, n):
    assert n & (n - 1) == 0
    return jnp.right_shift(x, n.bit_length() - 1)


def _imod(x, n):
    assert n & (n - 1) == 0
    return jnp.bitwise_and(x, n - 1)


def _split3_dot(a, w):
    hi = a.astype(BF16)
    r1 = a - hi.astype(F32)
    mid = r1.astype(BF16)
    lo = (r1 - mid.astype(F32)).astype(BF16)
    return _dot(hi, w) + _dot(mid, w) + _dot(lo, w)


BLOCK_TAGGED = "block_tagged"


def _norm_proj_body(x_ref, g_ref, wt_ref, *o_refs, outs, tiles, tm):
    x = x_ref[...]
    y = x * lax.rsqrt(jnp.mean(x * x, axis=-1, keepdims=True) + RMS_EPS) * g_ref[...]
    yb = y.astype(BF16)
    tile = pl.program_id(0) % tiles
    done = {}
    for o_ref, (off, width, _, keep_from) in zip(o_refs, outs):
        w_rows = wt_ref[off:off + width, :]
        if keep_from is None or keep_from == BLOCK_TAGGED:
            if (off, width) not in done:
                done[(off, width)] = _dot_nt(yb, w_rows)
            res = done[(off, width)]
            if keep_from is None:
                o_ref[...] = res.astype(o_ref.dtype)
            else:
                lane = lax.broadcasted_iota(jnp.int32, res.shape, 1)
                pos = tile * tm + lax.broadcasted_iota(jnp.int32, res.shape, 0)
                tag = jnp.where(_idiv(pos, SEL_BLOCK) == _imod(lane, HEAD_DIM), 1.0, 0.0)
                low = lane < HEAD_DIM
                o_ref[:, 0:LANES] = jnp.where(low, res, tag).astype(o_ref.dtype)
                o_ref[:, LANES:2 * LANES] = jnp.where(low, pltpu.roll(res, HEAD_DIM, 1), tag).astype(o_ref.dtype)
        elif keep_from == 0:
            if (off, width, 0) not in done:
                done[(off, width, 0)] = _dot_nt(w_rows, yb)
            o_ref[0] = done[(off, width, 0)].astype(o_ref.dtype)
        else:
            @pl.when(tile >= keep_from // tm)
            def _():
                o_ref[0] = _dot_nt(w_rows, yb).astype(o_ref.dtype)


def _norm_proj(x2d, gain, w_t, outs, tm, seq_len):
    m, d = x2d.shape
    n = w_t.shape[0]
    tiles = seq_len // tm
    assert seq_len % tm == 0 and m % seq_len == 0
    out_specs, out_shape = [], []
    for _, width, dt, keep_from in outs:
        if keep_from is None or keep_from == BLOCK_TAGGED:
            if keep_from == BLOCK_TAGGED:
                assert width == C_KV_WIDTH and seq_len <= SEL_BLOCK * HEAD_DIM
                width = 2 * LANES
            out_specs.append(pl.BlockSpec((tm, width), lambda i: (i, 0)))
            out_shape.append(jax.ShapeDtypeStruct((m, width), dt))
        else:
            assert keep_from % tm == 0
            first = keep_from // tm
            out_specs.append(pl.BlockSpec((1, width, tm),
                                          lambda i, first=first: (i // tiles, 0, jnp.maximum(i % tiles - first, 0))))
            out_shape.append(jax.ShapeDtypeStruct((m // seq_len, width, seq_len - keep_from), dt))
    return pl.pallas_call(
        functools.partial(_norm_proj_body, outs=outs, tiles=tiles, tm=tm),
        grid=(m // tm,),
        in_specs=[pl.BlockSpec((tm, d), lambda i: (i, 0)),
                  pl.BlockSpec((1, d), lambda i: (0, 0)),
                  pl.BlockSpec((n, d), lambda i: (0, 0))],
        out_specs=out_specs,
        out_shape=out_shape,
        compiler_params=_params("arbitrary"),
        name="norm_proj",
    )(x2d, gain.reshape(1, d), w_t)


DIL_GROUP = 4


def _dilated_body(q_ref, k_ref, v_ref, o_ref, o_sc, lse_sc, *, patterns):
    tq = ATTN_TQ
    blk_len = q_ref.shape[1]
    base = pl.program_id(2) * blk_len
    head_of_lane = _idiv(lax.broadcasted_iota(jnp.int32, (tq, LANES), 1), HEAD_DIM)
    own_lanes = [head_of_lane == h for h in range(LANES // HEAD_DIM)]

    def rows(ref, start, size, dil):
        return ref[0, pl.ds(start, size) if dil == 1 else pl.ds(start, size, stride=dil), :]

    for p_idx, (window, dil) in enumerate(patterns):
        lookback = window // dil
        n_prev = -(-lookback // tq)
        span = (n_prev + 1) * tq
        n_u = blk_len // (tq * dil)
        n_sub_blocks = dil * n_u
        assert blk_len % (tq * dil) == 0 and n_sub_blocks % DIL_GROUP == 0
        offs = lax.broadcasted_iota(jnp.int32, (tq, span), 0) - lax.broadcasted_iota(jnp.int32, (tq, span), 1)
        biases = []
        for lead in range(n_prev + 1):
            dist = offs + lead * tq
            biases.append(jnp.where(dist >= 0, jnp.where(dist <= lookback, 0.0, NEG_INF), NEG_INF))

        def trip(i, carry, dil=dil, n_prev=n_prev, span=span, p_idx=p_idx, biases=biases):
            work = []
            for j in range(DIL_GROUP):
                idx = i * DIL_GROUP + j
                res = _imod(idx, dil)
                u = _idiv(idx, dil)
                s0 = base // dil + u * tq
                ks = jnp.maximum(s0 - n_prev * tq, 0)
                q_row = res + dil * (u * tq)
                k_row = res + dil * ks
                q2 = rows(q_ref, q_row, tq, dil) * (HEAD_DIM ** -0.5 * LOG2E)
                k2 = rows(k_ref, k_row, span, dil).astype(BF16)
                v2 = rows(v_ref, k_row, span, dil).astype(BF16)
                bias = biases[n_prev]
                for lead in range(n_prev):
                    bias = jnp.where(s0 - ks == lead * tq, biases[lead], bias)
                work.append((q_row, q2, k2, v2, bias))
            scores = [[_dot_nt(jnp.where(own, q2, 0.0).astype(BF16), k2) + bias for own in own_lanes]
                      for _, q2, k2, _, bias in work]
            maxes = [[jnp.max(s, axis=-1, keepdims=True) for s in ss] for ss in scores]
            probs = [[jnp.exp2(s - m) for s, m in zip(ss, mm)] for ss, mm in zip(scores, maxes)]
            sums = [[jnp.sum(p, axis=-1, keepdims=True) for p in pp] for pp in probs]
            outs = [[_dot(p.astype(BF16), w[3]) for p in pp] for pp, w in zip(probs, work)]
            for w, oo, mm, ll in zip(work, outs, maxes, sums):
                o2 = oo[0] * (1.0 / ll[0])
                lse2 = jnp.broadcast_to(mm[0] * LN2 + jnp.log(ll[0]), (tq, LANES))
                for own, o, m, l in list(zip(own_lanes, oo, mm, ll))[1:]:
                    o2 = jnp.where(own, o * (1.0 / l), o2)
                    lse2 = jnp.where(own, m * LN2 + jnp.log(l), lse2)
                idx = pl.ds(w[0], tq) if dil == 1 else pl.ds(w[0], tq, stride=dil)
                o_sc[p_idx, idx, :] = o2
                lse_sc[p_idx, idx, :] = lse2
            return carry

        lax.fori_loop(0, n_sub_blocks // DIL_GROUP, trip, 0)

    lses = [lse_sc[p] for p in range(len(patterns))]
    top = lses[0]
    for l in lses[1:]:
        top = jnp.maximum(top, l)
    weights = [jnp.exp(l - top) for l in lses]
    num = weights[0] * o_sc[0]
    den = weights[0]
    for p in range(1, len(patterns)):
        num = num + weights[p] * o_sc[p]
        den = den + weights[p]
    o_ref[0] = (num / den).astype(o_ref.dtype)


def _dilated_attention(q, k, v, patterns):
    b, t, width = q.shape
    blk_len = min(t, max(ATTN_TQ * dil for _, dil in patterns))
    assert t % blk_len == 0
    for window, dil in patterns:
        assert t // dil >= (-(-(window // dil) // ATTN_TQ) + 1) * ATTN_TQ
    return pl.pallas_call(
        functools.partial(_dilated_body, patterns=patterns),
        grid=(b, width // LANES, t // blk_len),
        in_specs=[pl.BlockSpec((1, blk_len, LANES), lambda bi, hp, i: (bi, i, hp)),
                  pl.BlockSpec((1, t, LANES), lambda bi, hp, i: (bi, 0, hp)),
                  pl.BlockSpec((1, t, LANES), lambda bi, hp, i: (bi, 0, hp))],
        out_specs=pl.BlockSpec((1, blk_len, LANES), lambda bi, hp, i: (bi, i, hp)),
        out_shape=jax.ShapeDtypeStruct((b, t, width), BF16),
        scratch_shapes=[pltpu.VMEM((len(patterns), blk_len, LANES), F32),
                        pltpu.VMEM((len(patterns), blk_len, LANES), F32)],
        compiler_params=_params("parallel", "parallel", "arbitrary"),
        name="dilated_attention",
    )(q, k, v)


def _retention_body(q_ref, k_ref, v_ref, cos_ref, sin_ref, qd_ref, kd_ref, intra_ref, cd_ref, s0_ref,
                    o_ref, sfin_ref, state):
    c = pl.program_id(1)

    @pl.when(c == 0)
    def _():
        state[...] = s0_ref[0]

    cos = cos_ref[...]
    sin = sin_ref[...]
    width = cos.shape[-1]
    half = B_DK // 2
    lane = _imod(lax.broadcasted_iota(jnp.int32, cos.shape, 1), B_DK)

    def rotary(x):
        fwd = pltpu.roll(x, half, 1)
        bwd = pltpu.roll(x, width - half, 1)
        return x * cos + jnp.where(lane < half, -bwd, fwd) * sin

    q = rotary(q_ref[0])
    k = rotary(k_ref[0]) * (B_DK ** -0.5)
    q_in = q.astype(BF16)
    k_in = k.astype(BF16)
    q_st = (q * qd_ref[...]).astype(BF16)
    k_st = (k * kd_ref[...]).astype(BF16)
    v = v_ref[0].astype(BF16)
    kcs = [slice(h * B_DK, (h + 1) * B_DK) for h in range(B_HEADS)]
    vcs = [slice(h * B_DV, (h + 1) * B_DV) for h in range(B_HEADS)]
    states = [state[h] for h in range(B_HEADS)]
    scores = [_dot_nt(q_in[:, kc], k_in[:, kc]) * intra_ref[h] for h, kc in enumerate(kcs)]
    carried = [_dot(q_st[:, kc], st.astype(BF16)) for kc, st in zip(kcs, states)]
    updates = [_dot_tn(k_st[:, kc], v[:, vc]) for kc, vc in zip(kcs, vcs)]
    inner = [_dot(s.astype(BF16), v[:, vc]) for s, vc in zip(scores, vcs)]
    for h in range(B_HEADS):
        o_ref[0, :, vcs[h]] = (inner[h] + carried[h]).astype(o_ref.dtype)
        state[h] = states[h] * cd_ref[h] + updates[h]

    @pl.when(c == pl.num_programs(1) - 1)
    def _():
        sfin_ref[0] = state[...]


def _retention(q, k, v, s0, pos, *, chunk, c_real):
    n, t, _ = q.shape
    half = B_DK // 2
    freqs = ROPE_BASE ** (-jnp.arange(half, dtype=F32) / half)
    ang = pos.astype(F32)[:, None] * freqs[None, :]
    cos = jnp.tile(jnp.cos(ang), (1, 2 * B_HEADS))
    sin = jnp.tile(jnp.sin(ang), (1, 2 * B_HEADS))
    log_g = jnp.log1p(-jnp.exp2(-5.0 - jnp.arange(B_HEADS, dtype=F32)))
    i = jnp.arange(chunk, dtype=F32)
    diff = i[:, None] - i[None, :]
    intra = jnp.where(diff >= 0, jnp.exp(log_g[:, None, None] * jnp.maximum(diff, 0.0)), 0.0)
    q_decay = jnp.repeat(jnp.exp(log_g[None, :] * (i[:, None] + 1.0)), B_DK, axis=1)
    k_decay = jnp.repeat(jnp.exp(log_g[None, :] * (c_real - 1.0 - i[:, None])), B_DK, axis=1)
    chunk_decay = jnp.broadcast_to(jnp.exp(log_g * c_real)[:, None, None], (B_HEADS, 1, B_DV))
    wqk = B_HEADS * B_DK
    return pl.pallas_call(
        _retention_body,
        grid=(n, t // chunk),
        in_specs=[pl.BlockSpec((1, chunk, wqk), lambda b, c: (b, c, 0)),
                  pl.BlockSpec((1, chunk, wqk), lambda b, c: (b, c, 0)),
                  pl.BlockSpec((1, chunk, B_WIDTH), lambda b, c: (b, c, 0)),
                  pl.BlockSpec((chunk, wqk), lambda b, c: (c, 0)),
                  pl.BlockSpec((chunk, wqk), lambda b, c: (c, 0)),
                  pl.BlockSpec((chunk, wqk), lambda b, c: (0, 0)),
                  pl.BlockSpec((chunk, wqk), lambda b, c: (0, 0)),
                  pl.BlockSpec((B_HEADS, chunk, chunk), lambda b, c: (0, 0, 0)),
                  pl.BlockSpec((B_HEADS, 1, B_DV), lambda b, c: (0, 0, 0)),
                  pl.BlockSpec((1, B_HEADS, B_DK, B_DV), lambda b, c: (b, 0, 0, 0))],
        out_specs=[pl.BlockSpec((1, chunk, B_WIDTH), lambda b, c: (b, c, 0)),
                   pl.BlockSpec((1, B_HEADS, B_DK, B_DV), lambda b, c: (b, 0, 0, 0))],
        out_shape=[jax.ShapeDtypeStruct((n, t, B_WIDTH), BF16),
                   jax.ShapeDtypeStruct((n, B_HEADS, B_DK, B_DV), F32)],
        scratch_shapes=[pltpu.VMEM((B_HEADS, B_DK, B_DV), F32)],
        compiler_params=_params("parallel", "arbitrary"),
        name="retention",
    )(q, k, v, cos, sin, q_decay, k_decay, intra, chunk_decay, s0)


def _silu(z):
    return z * jax.nn.sigmoid(z)


def _even_out_body(x_ref, oa_ref, za_ref, ob_ref, zb_ref, w_ref, y_ref):
    ga = (oa_ref[...].astype(F32) * _silu(za_ref[...].astype(F32))).astype(BF16)
    acc = x_ref[...] + _dot(ga, w_ref[0:A_WIDTH, :])
    ob = ob_ref[...].astype(F32)
    zb = zb_ref[...].astype(F32)
    for h in range(B_HEADS):
        cols = slice(h * B_DV, (h + 1) * B_DV)
        seg = ob[:, cols]
        mu = jnp.mean(seg, axis=-1, keepdims=True)
        cen = seg - mu
        var = jnp.mean(cen * cen, axis=-1, keepdims=True)
        gb = (cen * lax.rsqrt(var + GN_EPS) * _silu(zb[:, cols])).astype(BF16)
        acc = acc + _dot(gb, w_ref[A_WIDTH + h * B_DV:A_WIDTH + (h + 1) * B_DV, :])
    y_ref[...] = acc


def _even_out(x2d, oa, za, ob, zb, w_out, tm):
    m, d = x2d.shape
    row = lambda width: pl.BlockSpec((tm, width), lambda i: (i, 0))
    return pl.pallas_call(
        _even_out_body,
        grid=(m // tm,),
        in_specs=[row(d), row(A_WIDTH), row(A_WIDTH), row(B_WIDTH), row(B_WIDTH),
                  pl.BlockSpec(w_out.shape, lambda i: (0, 0))],
        out_specs=row(d),
        out_shape=jax.ShapeDtypeStruct((m, d), F32),
        compiler_params=_params("parallel"),
        name="even_out",
    )(x2d, oa, za, ob, zb, w_out)


def _compress_mlp(pieces, p2_ref, w1_ref, w2_ref):
    n_piece = pieces[0].shape[0]
    hidden = CMP_HIDDEN
    pc = _dot(p2_ref[...], w1_ref[...])
    pos = (pc[0:1, :hidden] + pc[2:3, :hidden]) + (pc[1:2, hidden:] + pc[3:4, hidden:])
    acts = [_dot(x.astype(BF16), w1_ref[...]) for x in pieces]
    hid = [_silu(a[:, :hidden] + pltpu.roll(a[:, hidden:], n_piece - 1, 0) + pos) for a in acts]
    return _dot(jnp.concatenate(hid, axis=-1).astype(BF16), w2_ref[...])


def _split_heads(even, odd, low_half):
    swapped = pltpu.roll(jnp.where(low_half, odd, even), HEAD_DIM, 1)
    return jnp.where(low_half, even, swapped), jnp.where(low_half, swapped, odd)


def _compress_body(x_ref, p2_ref, w1_ref, w2_ref, o_ref):
    x = x_ref[0].astype(F32)
    low_half = lax.broadcasted_iota(jnp.int32, (x.shape[0], LANES), 1) < HEAD_DIM
    slabs = [_split_heads(x[:, l * LANES:(l + 1) * LANES], x[:, (l + 1) * LANES:(l + 2) * LANES], low_half)
             for l in range(0, CMP_STRIDE, 2)]
    pieces = [jnp.concatenate([s[g] for s in slabs], axis=-1) for g in range(C_KV_HEADS)]
    o_ref[0] = _compress_mlp(pieces, p2_ref, w1_ref, w2_ref)


def _compress_weights(pos_emb, w1, w2):
    per = CMP_LEN // CMP_STRIDE
    w1g = jnp.transpose(w1.reshape(per, CMP_STRIDE, HEAD_DIM, CMP_HIDDEN), (1, 2, 0, 3))
    w1g = w1g.reshape(CMP_STRIDE * HEAD_DIM, per * CMP_HIDDEN)
    eye = jnp.eye(C_KV_HEADS, dtype=w2.dtype)
    w2bd = jnp.einsum('fd,gh->gfhd', w2, eye).reshape(C_KV_HEADS * CMP_HIDDEN, C_KV_WIDTH)
    halves = pos_emb.reshape(per, CMP_STRIDE * HEAD_DIM)
    hi = halves.astype(BF16)
    lo = (halves - hi.astype(F32)).astype(BF16)
    p2 = jnp.concatenate([hi, lo, jnp.zeros((16 - 2 * per, halves.shape[1]), BF16)], axis=0)
    return p2, w1g.astype(BF16), w2bd.astype(BF16)


def _compress(pieces, cw):
    n, n_piece, width = pieces.shape
    p2, w1g, w2bd = cw
    const = lambda a: pl.BlockSpec(a.shape, lambda i: (0,) * a.ndim)
    return pl.pallas_call(
        _compress_body,
        grid=(n,),
        in_specs=[pl.BlockSpec((1, n_piece, width), lambda i: (i, 0, 0)), const(p2), const(w1g), const(w2bd)],
        out_specs=pl.BlockSpec((1, n_piece, C_KV_WIDTH), lambda i: (i, 0, 0)),
        out_shape=jax.ShapeDtypeStruct((n, n_piece, C_KV_WIDTH), F32),
        compiler_params=_params("parallel"),
        name="compress",
    )(pieces, p2, w1g, w2bd)


def _page_dma(pt_ref, pool_ref, buf_ref, sem_ref, n, slot, n_pages, start):
    for p in range(n_pages):
        page = pt_ref[n, p] if start else 0
        cp = pltpu.make_async_copy(pool_ref.at[page], buf_ref.at[slot, p], sem_ref.at[slot])
        if start:
            cp.start()
        else:
            cp.wait()


def _compress_paged_body(pt_ref, pool_ref, perm_ref, p2_ref, w1_ref, w2_ref, o_ref, buf, pieces, sem, *, n_pages):
    n = pl.program_id(0)
    slot = n % 2
    rows = PAGE_SIZE // CMP_STRIDE

    @pl.when(n == 0)
    def _():
        _page_dma(pt_ref, pool_ref, buf, sem, 0, 0, n_pages, True)

    @pl.when(n + 1 < pl.num_programs(0))
    def _():
        _page_dma(pt_ref, pool_ref, buf, sem, n + 1, 1 - slot, n_pages, True)

    _page_dma(pt_ref, pool_ref, buf, sem, n, slot, n_pages, False)
    perm = perm_ref[...]

    group = 16
    assert n_pages % group == 0 and CMP_STRIDE % 2 == 0
    low_half = lax.broadcasted_iota(jnp.int32, (rows, LANES), 1) < HEAD_DIM

    def unfold(i, carry):
        pairs = []
        for j in range(0, group, 2):
            two = buf[slot, pl.ds(i * group + j, 2)].reshape(2 * C_KV_WIDTH, PAGE_SIZE)
            pairs.append(_dot_nt(perm, two.astype(BF16)))
        for j in range(group):
            by_row = pairs[j // 2][:, (j % 2) * C_KV_WIDTH:(j % 2 + 1) * C_KV_WIDTH]
            r0 = pl.multiple_of((i * group + j) * rows, rows)
            for l in range(0, CMP_STRIDE, 2):
                even = by_row[l * rows:(l + 1) * rows, :]
                odd = by_row[(l + 1) * rows:(l + 2) * rows, :]
                lanes = slice(l * HEAD_DIM, (l + 2) * HEAD_DIM)
                pieces[0, pl.ds(r0, rows), lanes], pieces[1, pl.ds(r0, rows), lanes] = _split_heads(even, odd, low_half)
        return carry

    lax.fori_loop(0, n_pages // group, unfold, 0)
    o_ref[0] = _compress_mlp([pieces[g] for g in range(C_KV_HEADS)], p2_ref, w1_ref, w2_ref)


def _compress_paged(pool, page_table, cw):
    n, n_pages = page_table.shape
    rows = PAGE_SIZE // CMP_STRIDE
    n_piece = n_pages * rows
    p2g, w1g, w2bd = cw
    perm = np.zeros((PAGE_SIZE, PAGE_SIZE), np.float32)
    for l in range(CMP_STRIDE):
        for piece in range(rows):
            perm[l * rows + piece, CMP_STRIDE * piece + l] = 1.0
    perm = jnp.asarray(perm, BF16)
    pages = jnp.transpose(pool, (0, 2, 3, 1)).reshape(pool.shape[0], C_KV_WIDTH, PAGE_SIZE)
    const = lambda a: pl.BlockSpec(a.shape, lambda i, pt: (0,) * a.ndim)
    return pl.pallas_call(
        functools.partial(_compress_paged_body, n_pages=n_pages),
        grid_spec=pltpu.PrefetchScalarGridSpec(
            num_scalar_prefetch=1,
            grid=(n,),
            in_specs=[pl.BlockSpec(memory_space=pl.ANY), const(perm), const(p2g), const(w1g), const(w2bd)],
            out_specs=pl.BlockSpec((1, n_piece, C_KV_WIDTH), lambda i, pt: (i, 0, 0)),
            scratch_shapes=[pltpu.VMEM((2, n_pages, C_KV_WIDTH, PAGE_SIZE), F32),
                            pltpu.VMEM((C_KV_HEADS, n_piece, CMP_STRIDE * HEAD_DIM), F32),
                            pltpu.SemaphoreType.DMA((2,))]),
        out_shape=jax.ShapeDtypeStruct((n, n_piece, C_KV_WIDTH), F32),
        compiler_params=_params("arbitrary"),
        name="compress_paged",
    )(page_table, pages, perm, p2g, w1g, w2bd)


def _overlap_weights(n_cmp, n_cmp_pad, n_slc, n_slc_pad):
    i = np.arange(n_cmp_pad)[:, None]
    j = np.arange(n_slc_pad)[None, :]
    ov = (np.minimum(i * CMP_STRIDE + CMP_LEN, (j + 1) * SEL_BLOCK) - np.maximum(i * CMP_STRIDE, j * SEL_BLOCK))
    w = np.where((i < n_cmp) & (j < n_slc), np.clip(ov, 0, None) / CMP_LEN, 0.0)
    return jnp.asarray(w, dtype=BF16)


def _select_blocks(p_slc, qpos, n_slc):
    shape = p_slc.shape
    lane = lax.broadcasted_iota(jnp.int32, shape, 1)
    cur = _idiv(qpos, SEL_BLOCK)
    forced = (lane == 0) | (lane == cur) | (lane == cur - 1)
    causal = lane * SEL_BLOCK <= qpos
    score = jnp.where(forced, SEL_FORCE, jnp.where(causal, p_slc, -SEL_FORCE))
    score = jnp.where(lane < n_slc, score, -3.0 * SEL_FORCE)
    lane_f = lane.astype(F32)
    sel = jnp.zeros(shape, F32)
    for _ in range(min(SEL_TOPK, n_slc)):
        m = jnp.max(score, axis=-1, keepdims=True)
        first = jnp.min(jnp.where(score == m, lane_f, 1e9), axis=-1, keepdims=True)
        hit = lane_f == first
        sel = jnp.where(hit, jnp.where(m > -0.5 * SEL_FORCE, 1.0, 0.0), sel)
        score = jnp.where(hit, -4.0 * SEL_FORCE, score)
    return sel


def _select_blocks_t(p_slc, qpos, n_slc):
    n_rows = -(-n_slc // 8) * 8
    shape = (n_rows, p_slc.shape[1])
    blk = lax.broadcasted_iota(jnp.int32, shape, 0)
    cur = _idiv(qpos, SEL_BLOCK)
    forced = (blk == 0) | (blk == cur) | (blk == cur - 1)
    causal = blk * SEL_BLOCK <= qpos
    score = jnp.where(forced, SEL_FORCE, jnp.where(causal, p_slc[:n_rows], -SEL_FORCE))
    score = jnp.where(blk < n_slc, score, -3.0 * SEL_FORCE)
    tiles = [score[8 * v:8 * v + 8] for v in range(n_rows // 8)]
    sub = lax.broadcasted_iota(jnp.int32, (8, shape[1]), 0)
    ranks = [jnp.zeros((8, shape[1]), F32) for _ in tiles]
    for i in range(n_slc):
        row = tiles[i // 8][i % 8:i % 8 + 1]
        for v, tile in enumerate(tiles):
            ge = jnp.where(row >= tile, 1.0, 0.0)
            gt = jnp.where(row > tile, 1.0, 0.0)
            if 8 * v > i:
                ranks[v] = ranks[v] + ge
            elif 8 * v + 7 <= i:
                ranks[v] = ranks[v] + gt
            else:
                ranks[v] = ranks[v] + jnp.where(sub > i - 8 * v, ge, gt)
    rank = jnp.concatenate(ranks, axis=0)
    return jnp.where(rank < min(SEL_TOPK, n_slc), jnp.where(score > -0.5 * SEL_FORCE, 1.0, 0.0), 0.0)


NSA_TILE = 2 * LANES
ONES_ROWS = 16
LOG2E = 1.4426950408889634
LN2 = 0.6931471805599453


def _nsa_prompt_body(q_ref, kc_ref, vct_ref, ks_ref, vst_ref, kw_ref, vwt_ref, g_ref, wovt_ref, o_ref, m_sc, acc_sc,
                     *, n_cmp, n_slc):
    tq = tk = NSA_TILE
    rep = C_HEADS // C_KV_HEADS
    blk = pl.program_id(1)
    q0 = blk * tq
    n_cmp_pad = kc_ref.shape[1]
    sig_t = jax.nn.sigmoid(g_ref[0])
    q_t = [jnp.transpose(q_ref[0, :, c * LANES:(c + 1) * LANES].astype(F32))
           for c in range(C_WIDTH // LANES)]
    qpos = q0 + lax.broadcasted_iota(jnp.int32, (1, tq), 1)
    cmp_i = lax.broadcasted_iota(jnp.int32, (n_cmp_pad, tq), 0)
    cmp_ok = (cmp_i * CMP_STRIDE + (CMP_LEN - 1) <= qpos) & (cmp_i < n_cmp)
    cmp_bias = jnp.where(cmp_ok, 0.0, NEG_INF)
    any_block = jnp.where(qpos >= CMP_LEN - 1, 1.0, 0.0)
    ahead = lax.broadcasted_iota(jnp.int32, (tk, tq), 0) - lax.broadcasted_iota(jnp.int32, (tk, tq), 1)
    diag_bias = jnp.where(ahead <= 0, 0.0, NEG_INF)
    far_bias = jnp.where(ahead >= 0, 0.0, NEG_INF)
    ones_rows = jnp.where(lax.broadcasted_iota(jnp.int32, (ONES_ROWS, tk), 0) == 0, 1.0, 0.0).astype(BF16)

    def flash_update(scores, v_t):
        v_ext = jnp.concatenate([v_t, ones_rows], axis=0)
        for r, s in enumerate(scores):
            m_old = m_sc[r:r + 1, :]
            m_new = jnp.maximum(m_old, jnp.max(s, axis=0, keepdims=True))
            m_sc[r:r + 1, :] = m_new
            pv = _dot(v_ext, jnp.exp2(s - m_new).astype(BF16))
            acc_sc[r] = jnp.exp2(m_old - m_new) * acc_sc[r] + pv

    def flash_reset():
        m_sc[...] = jnp.full(m_sc.shape, NEG_INF, F32)
        acc_sc[...] = jnp.zeros(acc_sc.shape, F32)

    def flash_result(r):
        acc = acc_sc[r]
        return acc[:HEAD_DIM] * (1.0 / acc[HEAD_DIM:HEAD_DIM + 1])

    outs = []
    for g in range(C_KV_HEADS):
        gc = slice(g * HEAD_DIM, (g + 1) * HEAD_DIM)
        heads = [g * rep + r for r in range(rep)]
        q_heads = [q_t[h // 2][(h % 2) * HEAD_DIM:(h % 2 + 1) * HEAD_DIM] for h in heads]
        q_nat = [(qh * (HEAD_DIM ** -0.5)).astype(BF16) for qh in q_heads]
        q_log2 = [(qh * (HEAD_DIM ** -0.5 * LOG2E)).astype(BF16) for qh in q_heads]
        kcb = kc_ref[0, :, gc].astype(BF16)
        vct = vct_ref[0, gc, :].astype(BF16)
        scores = [_dot(kcb, q_nat[r]) + cmp_bias for r in range(rep)]
        probs = []
        imp = None
        for s in scores:
            p = jnp.exp(s - jnp.max(s, axis=0, keepdims=True))
            p = p * (any_block / jnp.sum(p, axis=0, keepdims=True))
            imp = p if imp is None else imp + p
            probs.append(p.astype(BF16))
        o_cmp = [_dot(vct, p) for p in probs]
        hi = imp.astype(BF16)
        r1 = imp - hi.astype(F32)
        mid = r1.astype(BF16)
        lo = (r1 - mid.astype(F32)).astype(BF16)
        p_slc = _dot(wovt_ref[...], hi) + _dot(wovt_ref[...], mid) + _dot(wovt_ref[...], lo)
        sel_bias = jnp.where(_select_blocks_t(p_slc, qpos, n_slc) > 0.5, 0.0, NEG_INF).astype(BF16)
        q_sel = [jnp.concatenate([qh, sel_bias], axis=0) for qh in q_log2]
        flash_reset()

        def sel_scores(k0):
            k_tile = ks_ref[0, pl.ds(pl.multiple_of(k0, tk), tk), g * LANES:(g + 1) * LANES]
            return [_dot(k_tile, qs) for qs in q_sel]

        def sel_values(k0):
            return vst_ref[0, gc, pl.ds(pl.multiple_of(k0, tk), tk)]

        def past_pair(i, carry):
            k0 = i * (2 * tk)
            first, second = sel_scores(k0), sel_scores(k0 + tk)
            flash_update(first, sel_values(k0))
            flash_update(second, sel_values(k0 + tk))
            return carry

        lax.fori_loop(0, blk // 2, past_pair, 0)
        k0 = (blk // 2) * (2 * tk)

        @pl.when(k0 == q0)
        def _():
            flash_update([s + diag_bias for s in sel_scores(q0)], sel_values(q0))

        @pl.when(k0 != q0)
        def _():
            first, second = sel_scores(k0), [s + diag_bias for s in sel_scores(q0)]
            flash_update(first, sel_values(k0))
            flash_update(second, sel_values(q0))

        o_sel = [flash_result(r) for r in range(rep)]
        flash_reset()
        n_back = C_WINDOW // tk
        win = []
        for back in range(n_back + 1):
            k0 = pl.multiple_of(jnp.maximum(q0 - back * tk, 0), tk)
            k_tile = kw_ref[0, pl.ds(k0, tk), gc]
            absent = jnp.where(blk >= back, 0.0, NEG_INF)
            edge = diag_bias if back == 0 else far_bias if back == n_back else None
            bias = absent if edge is None else edge + absent
            win.append(([_dot(k_tile, qh) + bias for qh in q_log2], vwt_ref[0, gc, pl.ds(k0, tk)]))
        for scores, values in win:
            flash_update(scores, values)
        for r, h in enumerate(heads):
            outs.append(sig_t[3 * h:3 * h + 1] * o_cmp[r] + sig_t[3 * h + 1:3 * h + 2] * o_sel[r]
                        + sig_t[3 * h + 2:3 * h + 3] * flash_result(r))
    for c in range(C_WIDTH // LANES):
        pair = jnp.transpose(jnp.concatenate(outs[2 * c:2 * c + 2], axis=0))
        o_ref[0, :, c * LANES:(c + 1) * LANES] = pair.astype(o_ref.dtype)


def _nsa_prompt(q, kcb, vcb, ks_ext, vs_t, kw, vw_t, gates_t, n_cmp):
    b, t, wq = q.shape
    n_cmp_pad = kcb.shape[1]
    n_slc = -(-t // SEL_BLOCK)
    tile = NSA_TILE
    assert t % (2 * tile) == 0 and n_slc == HEAD_DIM and C_WINDOW % tile == 0
    wov_t = _overlap_weights(n_cmp, n_cmp_pad, n_slc, LANES).T
    vcb_t = jnp.swapaxes(vcb, 1, 2)
    rep = C_HEADS // C_KV_HEADS
    whole = lambda a: pl.BlockSpec((1,) + a.shape[1:], lambda bi, i: (bi, 0, 0))
    return pl.pallas_call(
        functools.partial(_nsa_prompt_body, n_cmp=n_cmp, n_slc=n_slc),
        grid=(b, t // tile),
        in_specs=[pl.BlockSpec((1, tile, wq), lambda bi, i: (bi, i, 0)),
                  whole(kcb), whole(vcb_t), whole(ks_ext), whole(vs_t), whole(kw), whole(vw_t),
                  pl.BlockSpec((1, gates_t.shape[1], tile), lambda bi, i: (bi, 0, i)),
                  pl.BlockSpec(wov_t.shape, lambda bi, i: (0, 0))],
        out_specs=pl.BlockSpec((1, tile, wq), lambda bi, i: (bi, i, 0)),
        out_shape=jax.ShapeDtypeStruct((b, t, wq), BF16),
        scratch_shapes=[pltpu.VMEM((rep, tile), F32), pltpu.VMEM((rep, HEAD_DIM + ONES_ROWS, tile), F32)],
        compiler_params=_params("parallel", "arbitrary"),
        name="nsa_prompt",
    )(q, kcb, vcb_t, ks_ext, vs_t, kw, vw_t, gates_t, wov_t)


def _block_diag(q, blk_of_row, n_blk):
    return jnp.concatenate([jnp.where(blk_of_row == b, q, 0.0) for b in range(n_blk)], axis=-1)


def _diag_blocks(r, blk_of_row, n_blk):
    out = jnp.where(blk_of_row == 0, r[:, 0:HEAD_DIM], 0.0)
    for b in range(1, n_blk):
        out = out + jnp.where(blk_of_row == b, r[:, b * HEAD_DIM:(b + 1) * HEAD_DIM], 0.0)
    return out


def _row_ids(n_rows, n_heads, n_blk):
    row = lax.broadcasted_iota(jnp.int32, (n_rows, 1), 0)
    step = _idiv(row, n_heads)
    blk = _idiv(_imod(row, n_heads), n_heads // n_blk)
    return step, blk


def _pattern_weight(dist, patterns):
    w = jnp.zeros(dist.shape, F32)
    for window, dil in patterns:
        w = w + jnp.where(_imod(dist, dil) == 0, jnp.where(dist <= window, 1.0, 0.0), 0.0)
    return jnp.where(dist >= 0, w, 0.0)


def _decode_body(*refs, n_heads, n_blk, patterns, gate_col):
    q_ref, kc_ref, vc_ref, kn_ref, vn_ref = refs[:5]
    g_ref = refs[5] if gate_col is not None else None
    o_ref = refs[-1]
    n_rows = q_ref.shape[1]
    n_cache = kc_ref.shape[2]
    n_new = kn_ref.shape[1]
    step, blk = _row_ids(n_rows, n_heads, n_blk)
    qbd = _block_diag(q_ref[0] * (HEAD_DIM ** -0.5), blk, n_blk)
    s_c = _dot(qbd.astype(BF16), kc_ref[0].astype(BF16))
    w_c = _pattern_weight(n_cache + step - lax.broadcasted_iota(jnp.int32, (n_rows, n_cache), 1), patterns)
    s_c = jnp.where(w_c > 0, s_c, NEG_INF)
    kn = kn_ref[0]
    vn = vn_ref[0]
    s_n, w_n = [], []
    for c in range(n_new):
        s = jnp.sum(qbd * kn[c:c + 1, :], axis=-1, keepdims=True)
        w = _pattern_weight(step - c, patterns)
        s_n.append(jnp.where(w > 0, s, NEG_INF))
        w_n.append(w)
    m = jnp.max(s_c, axis=-1, keepdims=True)
    for s in s_n:
        m = jnp.maximum(m, s)
    p_c = w_c * jnp.exp(s_c - m)
    l = jnp.sum(p_c, axis=-1, keepdims=True)
    r = _dot_nt(p_c.astype(BF16), vc_ref[0].astype(BF16))
    for c in range(n_new):
        p = w_n[c] * jnp.exp(s_n[c] - m)
        l = l + p
        r = r + p * vn[c:c + 1, :]
    o = _diag_blocks(r, blk, n_blk) / l
    if g_ref is not None:
        o = o * jax.nn.sigmoid(g_ref[0])[:, gate_col:gate_col + 1]
    o_ref[0] = o


def _decode_attention(q, kc, vc, kn, vn, *, n_heads, n_blk, patterns, gate=None, gate_col=None):
    n, n_rows, _ = q.shape
    args = [q, kc, vc, kn, vn]
    if gate is not None:
        args.append(gate)
    whole = lambda a: pl.BlockSpec((1,) + a.shape[1:], lambda i: (i, 0, 0))
    return pl.pallas_call(
        functools.partial(_decode_body, n_heads=n_heads, n_blk=n_blk, patterns=patterns,
                          gate_col=gate_col if gate is not None else None),
        grid=(n,),
        in_specs=[whole(a) for a in args],
        out_specs=pl.BlockSpec((1, n_rows, HEAD_DIM), lambda i: (i, 0, 0)),
        out_shape=jax.ShapeDtypeStruct((n, n_rows, HEAD_DIM), F32),
        compiler_params=_params("parallel"),
        name="decode_attention",
    )(*args)


def _nsa_decode_body(pt_ref, q_ref, kc_ref, vc_ref, ksp_ref, vsp_ref, kn_ref, vn_ref, g_ref, wov_ref, o_ref,
                     kbuf, vbuf, expand, ksem, vsem, *, n_pages, n_cmp, n_slc):
    n = pl.program_id(0)
    slot = n % 2
    n_rows = q_ref.shape[1]
    n_new = kn_ref.shape[1]
    n_past = kbuf.shape[2]
    n_slc_pad = wov_ref.shape[1]
    rep = C_HEADS // C_KV_HEADS

    def gather(seq, dst_slot, start):
        for pool_ref, buf, sem in ((ksp_ref, kbuf, ksem), (vsp_ref, vbuf, vsem)):
            for p in range(n_pages):
                page = pt_ref[seq, p] if start else 0
                cp = pltpu.make_async_copy(pool_ref.at[page], buf.at[dst_slot, :, pl.ds(p * PAGE_SIZE, PAGE_SIZE)],
                                           sem.at[dst_slot])
                if start:
                    cp.start()
                else:
                    cp.wait()

    @pl.when(n == 0)
    def _():
        gather(0, 0, True)
        step_cols = 1024
        for c0 in range(0, n_past, step_cols):
            shape = (n_slc_pad, min(step_cols, n_past - c0))
            blk_of_key = _idiv(c0 + lax.broadcasted_iota(jnp.int32, shape, 1), SEL_BLOCK)
            hit = blk_of_key == lax.broadcasted_iota(jnp.int32, shape, 0)
            expand[:, c0:c0 + shape[1]] = jnp.where(hit, 1.0, 0.0).astype(BF16)

    @pl.when(n + 1 < pl.num_programs(0))
    def _():
        gather(n + 1, 1 - slot, True)

    step, blk = _row_ids(n_rows, C_HEADS, C_KV_HEADS)
    qpos = n_past + step
    qbd = _block_diag(q_ref[0] * (HEAD_DIM ** -0.5), blk, C_KV_HEADS)
    qbd_in = qbd.astype(BF16)
    n_cmp_pad = kc_ref.shape[1]
    cmp_i = lax.broadcasted_iota(jnp.int32, (n_rows, n_cmp_pad), 1)
    cmp_ok = (cmp_i * CMP_STRIDE + (CMP_LEN - 1) <= qpos) & (cmp_i < n_cmp)
    s = jnp.where(cmp_ok, _dot_nt(qbd_in, kc_ref[0].astype(BF16)), NEG_INF)
    m = jnp.max(s, axis=-1, keepdims=True)
    p = jnp.where(cmp_ok, jnp.exp(s - m), 0.0)
    l = jnp.sum(p, axis=-1, keepdims=True)
    p = p / jnp.where(l > 0, l, 1.0)
    o_cmp = _diag_blocks(_dot(p.astype(BF16), vc_ref[0].astype(BF16)), blk, C_KV_HEADS)
    grp = (n_rows // rep, rep, n_cmp_pad)
    imp = jnp.broadcast_to(jnp.sum(p.reshape(grp), axis=1, keepdims=True), grp).reshape(n_rows, n_cmp_pad)
    sel_rows = _select_blocks(_split3_dot(imp, wov_ref[...]), qpos, n_slc)
    gather(n, slot, False)
    blk_bias = jnp.where(sel_rows > 0.5, 0.0, NEG_INF).astype(BF16)
    s_p = _dot(qbd_in, kbuf[slot].astype(BF16)) + _dot(blk_bias, expand[...])
    kn = kn_ref[0]
    vn = vn_ref[0]
    s_n = []
    for c in range(n_new):
        blk_c = (n_past + c) // SEL_BLOCK
        ok = (sel_rows[:, blk_c:blk_c + 1] > 0.5) & (n_past + c <= qpos)
        s_n.append(jnp.where(ok, jnp.sum(qbd * kn[c:c + 1, :], axis=-1, keepdims=True), NEG_INF))
    m = jnp.max(s_p, axis=-1, keepdims=True)
    for sc in s_n:
        m = jnp.maximum(m, sc)
    p_p = jnp.exp(s_p - m)
    l = jnp.sum(p_p, axis=-1, keepdims=True)
    r = _dot_nt(p_p.astype(BF16), vbuf[slot].astype(BF16))
    for c in range(n_new):
        pn = jnp.exp(s_n[c] - m)
        l = l + pn
        r = r + pn * vn[c:c + 1, :]
    o_sel = _diag_blocks(r, blk, C_KV_HEADS) / l
    sig = jax.nn.sigmoid(g_ref[0])
    o_ref[0] = sig[:, 0:1] * o_cmp + sig[:, 1:2] * o_sel


def _nsa_decode(q, kcb, vcb, pool_ks, pool_vs, page_table, kn, vn, gate, n_cmp):
    n, n_rows, _ = q.shape
    n_pages = page_table.shape[1]
    n_past = n_pages * PAGE_SIZE
    n_new = kn.shape[1]
    n_slc = -(-(n_past + n_new) // SEL_BLOCK)
    n_slc_pad = -(-n_slc // LANES) * LANES
    wov = _overlap_weights(n_cmp, kcb.shape[1], n_slc, n_slc_pad)
    whole = lambda a: pl.BlockSpec((1,) + a.shape[1:], lambda i, pt: (i, 0, 0))
    pools = [jnp.transpose(a, (0, 2, 3, 1)).reshape(a.shape[0], C_KV_WIDTH, PAGE_SIZE) for a in (pool_ks, pool_vs)]
    return pl.pallas_call(
        functools.partial(_nsa_decode_body, n_pages=n_pages, n_cmp=n_cmp, n_slc=n_slc),
        grid_spec=pltpu.PrefetchScalarGridSpec(
            num_scalar_prefetch=1,
            grid=(n,),
            in_specs=[whole(q), whole(kcb), whole(vcb), pl.BlockSpec(memory_space=pl.ANY),
                      pl.BlockSpec(memory_space=pl.ANY), whole(kn), whole(vn), whole(gate),
                      pl.BlockSpec(wov.shape, lambda i, pt: (0, 0))],
            out_specs=pl.BlockSpec((1, n_rows, HEAD_DIM), lambda i, pt: (i, 0, 0)),
            scratch_shapes=[pltpu.VMEM((2, C_KV_WIDTH, n_past), F32), pltpu.VMEM((2, C_KV_WIDTH, n_past), F32),
                            pltpu.VMEM((n_slc_pad, n_past), BF16),
                            pltpu.SemaphoreType.DMA((2,)), pltpu.SemaphoreType.DMA((2,))]),
        out_shape=jax.ShapeDtypeStruct((n, n_rows, HEAD_DIM), F32),
        compiler_params=_params("arbitrary"),
        name="nsa_decode",
    )(page_table, q, kcb, vcb, *pools, kn, vn, gate, wov)


def _odd_out_body(x_ref, *refs):
    *branch_refs, z_ref, w_ref, gf_ref, y_ref = refs
    o = branch_refs[0][...].astype(F32)
    for ref in branch_refs[1:]:
        o = o + ref[...].astype(F32)
    mixed = (o * _silu(z_ref[...].astype(F32))).astype(BF16)
    h = x_ref[...] + _dot(mixed, w_ref[...])
    y_ref[...] = h * lax.rsqrt(jnp.mean(h * h, axis=-1, keepdims=True) + RMS_EPS) * gf_ref[...]


def _odd_out(x2d, branches, z, w_out, final_gain, tm):
    m, d = x2d.shape
    row = pl.BlockSpec((tm, d), lambda i: (i, 0))
    return pl.pallas_call(
        _odd_out_body,
        grid=(m // tm,),
        in_specs=[row] * (len(branches) + 2) + [pl.BlockSpec(w_out.shape, lambda i: (0, 0)),
                                                 pl.BlockSpec((1, d), lambda i: (0, 0))],
        out_specs=row,
        out_shape=jax.ShapeDtypeStruct((m, d), F32),
        compiler_params=_params("parallel"),
        name="odd_out",
    )(x2d, *branches, z, w_out, final_gain.reshape(1, d))


def _even_outs(cache_from):
    a, bk = A_WIDTH, B_HEADS * B_DK
    outs = [(0, a, F32, None), (a, a, F32, None), (2 * a, a, F32, None)]
    if cache_from is not None:
        outs += [(a, a, F32, cache_from), (2 * a, a, F32, cache_from)]
    outs += [(3 * a, a, BF16, None),
             (4 * a, bk, F32, None), (4 * a + bk, bk, F32, None),
             (4 * a + 2 * bk, B_WIDTH, F32, None), (4 * a + 2 * bk + B_WIDTH, B_WIDTH, BF16, None)]
    return outs


def _from_feature_major(u, n_heads):
    b, _, length = u.shape
    return jnp.transpose(u.reshape(b, n_heads, HEAD_DIM, length), (0, 3, 1, 2))


def _to_feature_major(u):
    n, length, n_heads, hd = u.shape
    return jnp.transpose(u, (0, 2, 3, 1)).reshape(n, n_heads * hd, length)


def _even_layer_prompt(x, gain, w_in, w_out):
    b, t, d = x.shape
    x2d = x.reshape(b * t, d)
    keep = min(A_PATTERNS[-1][0], t)
    qa, ka, va, ka_t, va_t, za, qb, kb, vb, zb = _norm_proj(x2d, gain, w_in, _even_outs(t - keep), PROJ_ROWS, t)
    seq = lambda u: u.reshape(b, t, u.shape[-1])
    oa = _dilated_attention(seq(qa), seq(ka), seq(va), A_PATTERNS)
    chunk = RET_CHUNK if t % RET_CHUNK == 0 else t
    s0 = jnp.zeros((b, B_HEADS, B_DK, B_DV), F32)
    ob, s_fin = _retention(seq(qb), seq(kb), seq(vb), s0, jnp.arange(t), chunk=chunk, c_real=chunk)
    y = _even_out(x2d, oa.reshape(b * t, A_WIDTH), za, ob.reshape(b * t, B_WIDTH), zb, w_out, OUT_ROWS)
    return y.reshape(b, t, d), _from_feature_major(ka_t, A_HEADS), _from_feature_major(va_t, A_HEADS), s_fin


def _even_layer_sample(x, cache_k, cache_v, state, gain, w_in, w_out, past_len):
    n, s_len, d = x.shape
    x2d = x.reshape(n * s_len, d)
    qa, ka32, va32, za, qb, kb, vb, zb = _norm_proj(x2d, gain, w_in, _even_outs(None), n * s_len, n * s_len)
    seq = lambda u: u.reshape(n, s_len, u.shape[-1])
    oa = _decode_attention(qa.reshape(n, s_len * A_HEADS, HEAD_DIM), _to_feature_major(cache_k),
                           _to_feature_major(cache_v), seq(ka32), seq(va32), n_heads=A_HEADS, n_blk=A_HEADS,
                           patterns=A_PATTERNS)
    oa = oa.reshape(n * s_len, A_WIDTH)
    chunk = 16
    pad = lambda u: jnp.pad(seq(u), ((0, 0), (0, chunk - s_len), (0, 0)))
    pos = past_len + jnp.arange(chunk)
    ob, s_new = _retention(pad(qb), pad(kb), pad(vb), state, pos, chunk=chunk, c_real=s_len)
    ob = ob[:, :s_len].reshape(n * s_len, B_WIDTH)
    y = _even_out(x2d, oa, za, ob, zb, w_out, n * s_len)
    heads = lambda u: u.reshape(n, s_len, A_HEADS, HEAD_DIM)
    return y.reshape(n, s_len, d), heads(ka32), heads(va32), s_new


ODD_KV = ("kc", "vc", "ks", "vs", "kw", "vw")
ODD_GATE_COL = C_WIDTH + len(ODD_KV) * C_KV_WIDTH
ODD_Z_COL = ODD_GATE_COL + 3 * C_HEADS


def _odd_outs_prompt(win_from):
    outs = [(0, C_WIDTH, BF16, None)]
    for i, name in enumerate(ODD_KV):
        off = C_WIDTH + i * C_KV_WIDTH
        outs.append((off, C_KV_WIDTH, BF16, 0 if name in ("vs", "vw") else BLOCK_TAGGED if name == "ks" else None))
        outs.append((off, C_KV_WIDTH, F32, win_from if name in ("kw", "vw") else 0))
    outs += [(ODD_GATE_COL, 3 * C_HEADS, F32, 0), (ODD_Z_COL, C_WIDTH, BF16, None)]
    return outs


def _odd_outs_sample():
    outs = [(0, C_WIDTH, BF16, None)]
    outs += [(C_WIDTH + i * C_KV_WIDTH, C_KV_WIDTH, F32, None) for i in range(len(ODD_KV))]
    outs += [(ODD_GATE_COL, 3 * C_HEADS, F32, None), (ODD_Z_COL, C_WIDTH, BF16, None)]
    return outs


def _odd_layer_prompt(x, gain, w_in, w_out, cw_k, cw_v, final_gain):
    b, t, d = x.shape
    x2d = x.reshape(b * t, d)
    keep = min(C_WINDOW, t)
    (q, kc, kc_t, vc, vc_t, ks, ks_t, vs_tb, vs_t, kw, kw_t, vw_tb, vw_t, gl_t, z) = _norm_proj(
        x2d, gain, w_in, _odd_outs_prompt(t - keep), PROJ_ROWS, t)
    seq = lambda u: u.reshape(b, t, u.shape[-1])
    n_cmp = (t - CMP_LEN) // CMP_STRIDE + 1
    pieces = lambda u: u.reshape(b, t // CMP_STRIDE, CMP_STRIDE * C_KV_WIDTH)
    kcb = _compress(pieces(kc), cw_k)
    vcb = _compress(pieces(vc), cw_v)
    o = _nsa_prompt(seq(q), kcb, vcb, seq(ks), vs_tb, seq(kw), vw_tb, gl_t, n_cmp)
    y = _odd_out(x2d, [o.reshape(b * t, C_WIDTH)], z, w_out, final_gain, OUT_ROWS)
    kv = lambda u: _from_feature_major(u, C_KV_HEADS)
    return y.reshape(b, t, d), kv(kc_t), kv(vc_t), kv(ks_t), kv(vs_t), kv(kw_t), kv(vw_t)


def _odd_layer_sample(x, page_table, pool_kc, pool_vc, pool_ks, pool_vs, buf_kw, buf_vw, gain, w_in, w_out,
                      cw_k, cw_v, final_gain):
    n, s_len, d = x.shape
    x2d = x.reshape(n * s_len, d)
    q, kc, vc, ks32, vs32, kw32, vw32, gl, z = _norm_proj(x2d, gain, w_in, _odd_outs_sample(), n * s_len, n * s_len)
    seq = lambda u: u.reshape(n, s_len, u.shape[-1])
    n_past = page_table.shape[1] * PAGE_SIZE
    n_cmp = (n_past + s_len - CMP_LEN) // CMP_STRIDE + 1
    assert CMP_STRIDE * (n_cmp - 1) + CMP_LEN <= n_past
    kcb = _compress_paged(pool_kc, page_table, cw_k)
    vcb = _compress_paged(pool_vc, page_table, cw_v)
    rows = lambda u: u.astype(F32).reshape(n, s_len * C_HEADS, HEAD_DIM)
    gate = gl.reshape(n, s_len * C_HEADS, 3)
    o_cs = _nsa_decode(rows(q), kcb, vcb, pool_ks, pool_vs, page_table, seq(ks32), seq(vs32), gate, n_cmp)
    o_win = _decode_attention(rows(q), _to_feature_major(buf_kw), _to_feature_major(buf_vw),
                              seq(kw32), seq(vw32), n_heads=C_HEADS, n_blk=C_KV_HEADS,
                              patterns=((C_WINDOW, 1),), gate=gate, gate_col=2)
    y = _odd_out(x2d, [o_cs.reshape(n * s_len, C_WIDTH), o_win.reshape(n * s_len, C_WIDTH)], z, w_out, final_gain,
                 n * s_len)
    kv = lambda u: u.reshape(n, s_len, C_KV_HEADS, HEAD_DIM)
    return y.reshape(n, s_len, d), kv(kc), kv(vc), kv(ks32), kv(vs32), kv(kw32), kv(vw32)


def kernel(x_prompt, x_sample, cache_a_k, cache_a_v, state_ret, cache_c_kcmp, cache_c_vcmp, cache_c_ksel,
           cache_c_vsel, cache_c_kwin, cache_c_vwin, page_table, even_norm, even_w_in, even_w_out, odd_norm,
           odd_w_in, odd_w_out, cmp_pos_k, cmp_w1_k, cmp_w2_k, cmp_pos_v, cmp_w1_v, cmp_w2_v, final_norm):
    assert even_norm.shape[0] == 1 and odd_norm.shape[0] == 1
    past_len = page_table.shape[1] * PAGE_SIZE
    w_in_e = even_w_in[0].T.astype(BF16)
    w_out_e = even_w_out[0].astype(BF16)
    w_in_o = odd_w_in[0].T.astype(BF16)
    w_out_o = odd_w_out[0].astype(BF16)
    cw_k = _compress_weights(cmp_pos_k[0], cmp_w1_k[0], cmp_w2_k[0])
    cw_v = _compress_weights(cmp_pos_v[0], cmp_w1_v[0], cmp_w2_v[0])

    hp, ak_p, av_p, ret_p = _even_layer_prompt(x_prompt, even_norm[0], w_in_e, w_out_e)
    hs, ak_s, av_s, ret_s = _even_layer_sample(x_sample, cache_a_k[0], cache_a_v[0], state_ret[0], even_norm[0],
                                               w_in_e, w_out_e, past_len)
    yp, *rows_p = _odd_layer_prompt(hp, odd_norm[0], w_in_o, w_out_o, cw_k, cw_v, final_norm)
    ys, *rows_s = _odd_layer_sample(hs, page_table, cache_c_kcmp[0], cache_c_vcmp[0], cache_c_ksel[0],
                                    cache_c_vsel[0], cache_c_kwin[0], cache_c_vwin[0], odd_norm[0], w_in_o, w_out_o,
                                    cw_k, cw_v, final_norm)
    lead = lambda u: u[None]
    out = [yp, ys, lead(ak_p), lead(ak_s), lead(av_p), lead(av_s), lead(ret_p), lead(ret_s)]
    for rp, rs in zip(rows_p, rows_s):
        out += [lead(rp), lead(rs)]
    return tuple(out)
```

```python
import functools

import numpy as np
import jax
import jax.numpy as jnp
from jax import lax
from jax.experimental import pallas as pl
from jax.experimental.pallas import tpu as pltpu

F32 = jnp.float32
BF16 = jnp.bfloat16

HEAD_DIM = 64
A_HEADS = 8
A_WIDTH = A_HEADS * HEAD_DIM
A_PATTERNS = ((128, 1), (512, 4), (2048, 16))
B_HEADS = 4
B_DK = 64
B_DV = 128
B_WIDTH = B_HEADS * B_DV
RET_CHUNK = 256
ROPE_BASE = 10000.0
GN_EPS = 1e-5
C_HEADS = 16
C_KV_HEADS = 2
C_WIDTH = C_HEADS * HEAD_DIM
C_KV_WIDTH = C_KV_HEADS * HEAD_DIM
CMP_LEN = 32
CMP_STRIDE = 16
CMP_HIDDEN = 256
SEL_BLOCK = 64
SEL_TOPK = 16
SEL_FORCE = 1e9
C_WINDOW = 512
PAGE_SIZE = 128
RMS_EPS = 1e-6
NEG_INF = -1e30

LANES = 128
ATTN_TQ = 128
PROJ_ROWS = 512
OUT_ROWS = 1024
VMEM_LIMIT = 56 * 1024 * 1024


def _params(*sem):
    return pltpu.CompilerParams(dimension_semantics=sem, vmem_limit_bytes=VMEM_LIMIT)


def _dot(a, b):
    return jnp.dot(a, b, preferred_element_type=F32)


def _dot_nt(a, b):
    return lax.dot_general(a, b, (((1,), (1,)), ((), ())), preferred_element_type=F32)


def _dot_tn(a, b):
    return lax.dot_general(a, b, (((0,), (0,)), ((), ())), preferred_element_type=F32)


def _idiv(x, n):
    assert n & (n - 1) == 0
    return jnp.right_shift(x, n.bit_length() - 1)


def _imod(x, n):
    assert n & (n - 1) == 0
    return jnp.bitwise_and(x, n - 1)


def _split3_dot(a, w):
    hi = a.astype(BF16)
    r1 = a - hi.astype(F32)
    mid = r1.astype(BF16)
    lo = (r1 - mid.astype(F32)).astype(BF16)
    return _dot(hi, w) + _dot(mid, w) + _dot(lo, w)


BLOCK_TAGGED = "block_tagged"


def _norm_proj_body(x_ref, g_ref, wt_ref, *o_refs, outs, tiles, tm):
    x = x_ref[...]
    y = x * lax.rsqrt(jnp.mean(x * x, axis=-1, keepdims=True) + RMS_EPS) * g_ref[...]
    yb = y.astype(BF16)
    tile = pl.program_id(0) % tiles
    done = {}
    for o_ref, (off, width, _, keep_from) in zip(o_refs, outs):
        w_rows = wt_ref[off:off + width, :]
        if keep_from is None or keep_from == BLOCK_TAGGED:
            if (off, width) not in done:
                done[(off, width)] = _dot_nt(yb, w_rows)
            res = done[(off, width)]
            if keep_from is None:
                o_ref[...] = res.astype(o_ref.dtype)
            else:
                lane = lax.broadcasted_iota(jnp.int32, res.shape, 1)
                pos = tile * tm + lax.broadcasted_iota(jnp.int32, res.shape, 0)
                tag = jnp.where(_idiv(pos, SEL_BLOCK) == _imod(lane, HEAD_DIM), 1.0, 0.0)
                low = lane < HEAD_DIM
                o_ref[:, 0:LANES] = jnp.where(low, res, tag).astype(o_ref.dtype)
                o_ref[:, LANES:2 * LANES] = jnp.where(low, pltpu.roll(res, HEAD_DIM, 1), tag).astype(o_ref.dtype)
        elif keep_from == 0:
            if (off, width, 0) not in done:
                done[(off, width, 0)] = _dot_nt(w_rows, yb)
            o_ref[0] = done[(off, width, 0)].astype(o_ref.dtype)
        else:
            @pl.when(tile >= keep_from // tm)
            def _():
                o_ref[0] = _dot_nt(w_rows, yb).astype(o_ref.dtype)


def _norm_proj(x2d, gain, w_t, outs, tm, seq_len):
    m, d = x2d.shape
    n = w_t.shape[0]
    tiles = seq_len // tm
    assert seq_len % tm == 0 and m % seq_len == 0
    out_specs, out_shape = [], []
    for _, width, dt, keep_from in outs:
        if keep_from is None or keep_from == BLOCK_TAGGED:
            if keep_from == BLOCK_TAGGED:
                assert width == C_KV_WIDTH and seq_len <= SEL_BLOCK * HEAD_DIM
                width = 2 * LANES
            out_specs.append(pl.BlockSpec((tm, width), lambda i: (i, 0)))
            out_shape.append(jax.ShapeDtypeStruct((m, width), dt))
        else:
            assert keep_from % tm == 0
            first = keep_from // tm
            out_specs.append(pl.BlockSpec((1, width, tm),
                                          lambda i, first=first: (i // tiles, 0, jnp.maximum(i % tiles - first, 0))))
            out_shape.append(jax.ShapeDtypeStruct((m // seq_len, width, seq_len - keep_from), dt))
    return pl.pallas_call(
        functools.partial(_norm_proj_body, outs=outs, tiles=tiles, tm=tm),
        grid=(m // tm,),
        in_specs=[pl.BlockSpec((tm, d), lambda i: (i, 0)),
                  pl.BlockSpec((1, d), lambda i: (0, 0)),
                  pl.BlockSpec((n, d), lambda i: (0, 0))],
        out_specs=out_specs,
        out_shape=out_shape,
        compiler_params=_params("arbitrary"),
        name="norm_proj",
    )(x2d, gain.reshape(1, d), w_t)


DIL_GROUP = 4


DIL_SPLIT = 4


def _dilated_body(q_ref, k_ref, v_ref, o_ref, o_sc, lse_sc, *split_refs, patterns):
    tq = ATTN_TQ
    blk_len = q_ref.shape[1]
    base = pl.program_id(2) * blk_len
    head_of_lane = _idiv(lax.broadcasted_iota(jnp.int32, (tq, LANES), 1), HEAD_DIM)
    own_lanes = [head_of_lane == h for h in range(LANES // HEAD_DIM)]

    q_split = k_split = v_split = None
    if split_refs:
        q_split, k_split, v_split = split_refs
        for r in range(DIL_SPLIT):
            q_split[r] = q_ref[0, pl.ds(r, blk_len // DIL_SPLIT, stride=DIL_SPLIT), :]

        @pl.when(pl.program_id(2) == 0)
        def _():
            n_rows = k_ref.shape[1] // DIL_SPLIT
            for r in range(DIL_SPLIT):
                k_split[r] = k_ref[0, pl.ds(r, n_rows, stride=DIL_SPLIT), :]
                v_split[r] = v_ref[0, pl.ds(r, n_rows, stride=DIL_SPLIT), :]

    def rows(ref, split, start, size, dil):
        if dil == 1:
            return ref[0, pl.ds(start, size), :]
        if dil <= DIL_SPLIT:
            return ref[0, pl.ds(start, size, stride=dil), :]
        return split[_imod(start, DIL_SPLIT), pl.ds(_idiv(start, DIL_SPLIT), size, stride=dil // DIL_SPLIT), :]

    for p_idx, (window, dil) in enumerate(patterns):
        lookback = window // dil
        n_prev = -(-lookback // tq)
        span = (n_prev + 1) * tq
        n_u = blk_len // (tq * dil)
        n_sub_blocks = dil * n_u
        assert blk_len % (tq * dil) == 0 and n_sub_blocks % DIL_GROUP == 0
        offs = lax.broadcasted_iota(jnp.int32, (tq, span), 0) - lax.broadcasted_iota(jnp.int32, (tq, span), 1)
        biases = []
        for lead in range(n_prev + 1):
            dist = offs + lead * tq
            biases.append(jnp.where(dist >= 0, jnp.where(dist <= lookback, 0.0, NEG_INF), NEG_INF))

        def trip(i, carry, dil=dil, n_prev=n_prev, span=span, p_idx=p_idx, biases=biases):
            work = []
            for j in range(DIL_GROUP):
                idx = i * DIL_GROUP + j
                res = _imod(idx, dil)
                u = _idiv(idx, dil)
                s0 = base // dil + u * tq
                ks = jnp.maximum(s0 - n_prev * tq, 0)
                q_row = res + dil * (u * tq)
                k_row = res + dil * ks
                q2 = rows(q_ref, q_split, q_row, tq, dil) * (HEAD_DIM ** -0.5 * LOG2E)
                k2 = rows(k_ref, k_split, k_row, span, dil).astype(BF16)
                v2 = rows(v_ref, v_split, k_row, span, dil).astype(BF16)
                bias = biases[n_prev]
                for lead in range(n_prev):
                    bias = jnp.where(s0 - ks == lead * tq, biases[lead], bias)
                work.append((q_row, q2, k2, v2, bias))
            scores = [[_dot_nt(jnp.where(own, q2, 0.0).astype(BF16), k2) + bias for own in own_lanes]
                      for _, q2, k2, _, bias in work]
            maxes = [[jnp.max(s, axis=-1, keepdims=True) for s in ss] for ss in scores]
            probs = [[jnp.exp2(s - m) for s, m in zip(ss, mm)] for ss, mm in zip(scores, maxes)]
            sums = [[jnp.sum(p, axis=-1, keepdims=True) for p in pp] for pp in probs]
            outs = [[_dot(p.astype(BF16), w[3]) for p in pp] for pp, w in zip(probs, work)]
            for w, oo, mm, ll in zip(work, outs, maxes, sums):
                o2 = oo[0] * (1.0 / ll[0])
                lse2 = jnp.broadcast_to(mm[0] * LN2 + jnp.log(ll[0]), (tq, LANES))
                for own, o, m, l in list(zip(own_lanes, oo, mm, ll))[1:]:
                    o2 = jnp.where(own, o * (1.0 / l), o2)
                    lse2 = jnp.where(own, m * LN2 + jnp.log(l), lse2)
                idx = pl.ds(w[0], tq) if dil == 1 else pl.ds(w[0], tq, stride=dil)
                o_sc[p_idx, idx, :] = o2
                lse_sc[p_idx, idx, :] = lse2
            return carry

        lax.fori_loop(0, n_sub_blocks // DIL_GROUP, trip, 0)

    lses = [lse_sc[p] for p in range(len(patterns))]
    top = lses[0]
    for l in lses[1:]:
        top = jnp.maximum(top, l)
    weights = [jnp.exp(l - top) for l in lses]
    num = weights[0] * o_sc[0]
    den = weights[0]
    for p in range(1, len(patterns)):
        num = num + weights[p] * o_sc[p]
        den = den + weights[p]
    o_ref[0] = (num / den).astype(o_ref.dtype)


def _dilated_attention(q, k, v, patterns):
    b, t, width = q.shape
    blk_len = min(t, max(ATTN_TQ * dil for _, dil in patterns))
    assert t % blk_len == 0
    for window, dil in patterns:
        assert t // dil >= (-(-(window // dil) // ATTN_TQ) + 1) * ATTN_TQ
    scratch = [pltpu.VMEM((len(patterns), blk_len, LANES), F32), pltpu.VMEM((len(patterns), blk_len, LANES), F32)]
    if any(dil > DIL_SPLIT for _, dil in patterns):
        assert all(dil % DIL_SPLIT == 0 for _, dil in patterns if dil > DIL_SPLIT)
        scratch += [pltpu.VMEM((DIL_SPLIT, blk_len // DIL_SPLIT, LANES), F32),
                    pltpu.VMEM((DIL_SPLIT, t // DIL_SPLIT, LANES), F32),
                    pltpu.VMEM((DIL_SPLIT, t // DIL_SPLIT, LANES), F32)]
    return pl.pallas_call(
        functools.partial(_dilated_body, patterns=patterns),
        grid=(b, width // LANES, t // blk_len),
        in_specs=[pl.BlockSpec((1, blk_len, LANES), lambda bi, hp, i: (bi, i, hp)),
                  pl.BlockSpec((1, t, LANES), lambda bi, hp, i: (bi, 0, hp)),
                  pl.BlockSpec((1, t, LANES), lambda bi, hp, i: (bi, 0, hp))],
        out_specs=pl.BlockSpec((1, blk_len, LANES), lambda bi, hp, i: (bi, i, hp)),
        out_shape=jax.ShapeDtypeStruct((b, t, width), BF16),
        scratch_shapes=scratch,
        compiler_params=_params("parallel", "parallel", "arbitrary"),
        name="dilated_attention",
    )(q, k, v)


def _retention_body(q_ref, k_ref, v_ref, cos_ref, sin_ref, qd_ref, kd_ref, intra_ref, cd_ref, s0_ref,
                    o_ref, sfin_ref, state):
    c = pl.program_id(1)

    @pl.when(c == 0)
    def _():
        state[...] = s0_ref[0]

    cos = cos_ref[...]
    sin = sin_ref[...]
    width = cos.shape[-1]
    half = B_DK // 2
    lane = _imod(lax.broadcasted_iota(jnp.int32, cos.shape, 1), B_DK)

    def rotary(x):
        fwd = pltpu.roll(x, half, 1)
        bwd = pltpu.roll(x, width - half, 1)
        return x * cos + jnp.where(lane < half, -bwd, fwd) * sin

    q = rotary(q_ref[0])
    k = rotary(k_ref[0]) * (B_DK ** -0.5)
    q_in = q.astype(BF16)
    k_in = k.astype(BF16)
    q_st = (q * qd_ref[...]).astype(BF16)
    k_st = (k * kd_ref[...]).astype(BF16)
    v = v_ref[0].astype(BF16)
    kcs = [slice(h * B_DK, (h + 1) * B_DK) for h in range(B_HEADS)]
    vcs = [slice(h * B_DV, (h + 1) * B_DV) for h in range(B_HEADS)]
    states = [state[h] for h in range(B_HEADS)]
    scores = [_dot_nt(q_in[:, kc], k_in[:, kc]) * intra_ref[h] for h, kc in enumerate(kcs)]
    carried = [_dot(q_st[:, kc], st.astype(BF16)) for kc, st in zip(kcs, states)]
    updates = [_dot_tn(k_st[:, kc], v[:, vc]) for kc, vc in zip(kcs, vcs)]
    inner = [_dot(s.astype(BF16), v[:, vc]) for s, vc in zip(scores, vcs)]
    for h in range(B_HEADS):
        o_ref[0, :, vcs[h]] = (inner[h] + carried[h]).astype(o_ref.dtype)
        state[h] = states[h] * cd_ref[h] + updates[h]

    @pl.when(c == pl.num_programs(1) - 1)
    def _():
        sfin_ref[0] = state[...]


def _retention(q, k, v, s0, pos, *, chunk, c_real):
    n, t, _ = q.shape
    half = B_DK // 2
    freqs = ROPE_BASE ** (-jnp.arange(half, dtype=F32) / half)
    ang = pos.astype(F32)[:, None] * freqs[None, :]
    cos = jnp.tile(jnp.cos(ang), (1, 2 * B_HEADS))
    sin = jnp.tile(jnp.sin(ang), (1, 2 * B_HEADS))
    log_g = jnp.log1p(-jnp.exp2(-5.0 - jnp.arange(B_HEADS, dtype=F32)))
    i = jnp.arange(chunk, dtype=F32)
    diff = i[:, None] - i[None, :]
    intra = jnp.where(diff >= 0, jnp.exp(log_g[:, None, None] * jnp.maximum(diff, 0.0)), 0.0)
    q_decay = jnp.repeat(jnp.exp(log_g[None, :] * (i[:, None] + 1.0)), B_DK, axis=1)
    k_decay = jnp.repeat(jnp.exp(log_g[None, :] * (c_real - 1.0 - i[:, None])), B_DK, axis=1)
    chunk_decay = jnp.broadcast_to(jnp.exp(log_g * c_real)[:, None, None], (B_HEADS, 1, B_DV))
    wqk = B_HEADS * B_DK
    return pl.pallas_call(
        _retention_body,
        grid=(n, t // chunk),
        in_specs=[pl.BlockSpec((1, chunk, wqk), lambda b, c: (b, c, 0)),
                  pl.BlockSpec((1, chunk, wqk), lambda b, c: (b, c, 0)),
                  pl.BlockSpec((1, chunk, B_WIDTH), lambda b, c: (b, c, 0)),
                  pl.BlockSpec((chunk, wqk), lambda b, c: (c, 0)),
                  pl.BlockSpec((chunk, wqk), lambda b, c: (c, 0)),
                  pl.BlockSpec((chunk, wqk), lambda b, c: (0, 0)),
                  pl.BlockSpec((chunk, wqk), lambda b, c: (0, 0)),
                  pl.BlockSpec((B_HEADS, chunk, chunk), lambda b, c: (0, 0, 0)),
                  pl.BlockSpec((B_HEADS, 1, B_DV), lambda b, c: (0, 0, 0)),
                  pl.BlockSpec((1, B_HEADS, B_DK, B_DV), lambda b, c: (b, 0, 0, 0))],
        out_specs=[pl.BlockSpec((1, chunk, B_WIDTH), lambda b, c: (b, c, 0)),
                   pl.BlockSpec((1, B_HEADS, B_DK, B_DV), lambda b, c: (b, 0, 0, 0))],
        out_shape=[jax.ShapeDtypeStruct((n, t, B_WIDTH), BF16),
                   jax.ShapeDtypeStruct((n, B_HEADS, B_DK, B_DV), F32)],
        scratch_shapes=[pltpu.VMEM((B_HEADS, B_DK, B_DV), F32)],
        compiler_params=_params("parallel", "arbitrary"),
        name="retention",
    )(q, k, v, cos, sin, q_decay, k_decay, intra, chunk_decay, s0)


def _silu(z):
    return z * jax.nn.sigmoid(z)


def _even_out_body(x_ref, oa_ref, za_ref, ob_ref, zb_ref, w_ref, y_ref):
    ga = (oa_ref[...].astype(F32) * _silu(za_ref[...].astype(F32))).astype(BF16)
    acc = x_ref[...] + _dot(ga, w_ref[0:A_WIDTH, :])
    ob = ob_ref[...].astype(F32)
    zb = zb_ref[...].astype(F32)
    for h in range(B_HEADS):
        cols = slice(h * B_DV, (h + 1) * B_DV)
        seg = ob[:, cols]
        mu = jnp.mean(seg, axis=-1, keepdims=True)
        cen = seg - mu
        var = jnp.mean(cen * cen, axis=-1, keepdims=True)
        gb = (cen * lax.rsqrt(var + GN_EPS) * _silu(zb[:, cols])).astype(BF16)
        acc = acc + _dot(gb, w_ref[A_WIDTH + h * B_DV:A_WIDTH + (h + 1) * B_DV, :])
    y_ref[...] = acc


def _even_out(x2d, oa, za, ob, zb, w_out, tm):
    m, d = x2d.shape
    row = lambda width: pl.BlockSpec((tm, width), lambda i: (i, 0))
    return pl.pallas_call(
        _even_out_body,
        grid=(m // tm,),
        in_specs=[row(d), row(A_WIDTH), row(A_WIDTH), row(B_WIDTH), row(B_WIDTH),
                  pl.BlockSpec(w_out.shape, lambda i: (0, 0))],
        out_specs=row(d),
        out_shape=jax.ShapeDtypeStruct((m, d), F32),
        compiler_params=_params("parallel"),
        name="even_out",
    )(x2d, oa, za, ob, zb, w_out)


def _compress_mlp(pieces, p2_ref, w1_ref, w2_ref):
    n_piece = pieces[0].shape[0]
    hidden = CMP_HIDDEN
    pc = _dot(p2_ref[...], w1_ref[...])
    pos = (pc[0:1, :hidden] + pc[2:3, :hidden]) + (pc[1:2, hidden:] + pc[3:4, hidden:])
    acts = [_dot(x.astype(BF16), w1_ref[...]) for x in pieces]
    hid = [_silu(a[:, :hidden] + pltpu.roll(a[:, hidden:], n_piece - 1, 0) + pos) for a in acts]
    return _dot(jnp.concatenate(hid, axis=-1).astype(BF16), w2_ref[...])


def _split_heads(even, odd, low_half):
    swapped = pltpu.roll(jnp.where(low_half, odd, even), HEAD_DIM, 1)
    return jnp.where(low_half, even, swapped), jnp.where(low_half, swapped, odd)


def _compress_body(x_ref, p2_ref, w1_ref, w2_ref, o_ref):
    x = x_ref[0].astype(F32)
    low_half = lax.broadcasted_iota(jnp.int32, (x.shape[0], LANES), 1) < HEAD_DIM
    slabs = [_split_heads(x[:, l * LANES:(l + 1) * LANES], x[:, (l + 1) * LANES:(l + 2) * LANES], low_half)
             for l in range(0, CMP_STRIDE, 2)]
    pieces = [jnp.concatenate([s[g] for s in slabs], axis=-1) for g in range(C_KV_HEADS)]
    o_ref[0] = _compress_mlp(pieces, p2_ref, w1_ref, w2_ref)


def _compress_weights(pos_emb, w1, w2):
    per = CMP_LEN // CMP_STRIDE
    w1g = jnp.transpose(w1.reshape(per, CMP_STRIDE, HEAD_DIM, CMP_HIDDEN), (1, 2, 0, 3))
    w1g = w1g.reshape(CMP_STRIDE * HEAD_DIM, per * CMP_HIDDEN)
    eye = jnp.eye(C_KV_HEADS, dtype=w2.dtype)
    w2bd = jnp.einsum('fd,gh->gfhd', w2, eye).reshape(C_KV_HEADS * CMP_HIDDEN, C_KV_WIDTH)
    halves = pos_emb.reshape(per, CMP_STRIDE * HEAD_DIM)
    hi = halves.astype(BF16)
    lo = (halves - hi.astype(F32)).astype(BF16)
    p2 = jnp.concatenate([hi, lo, jnp.zeros((16 - 2 * per, halves.shape[1]), BF16)], axis=0)
    return p2, w1g.astype(BF16), w2bd.astype(BF16)


def _compress(pieces, cw):
    n, n_piece, width = pieces.shape
    p2, w1g, w2bd = cw
    const = lambda a: pl.BlockSpec(a.shape, lambda i: (0,) * a.ndim)
    return pl.pallas_call(
        _compress_body,
        grid=(n,),
        in_specs=[pl.BlockSpec((1, n_piece, width), lambda i: (i, 0, 0)), const(p2), const(w1g), const(w2bd)],
        out_specs=pl.BlockSpec((1, n_piece, C_KV_WIDTH), lambda i: (i, 0, 0)),
        out_shape=jax.ShapeDtypeStruct((n, n_piece, C_KV_WIDTH), F32),
        compiler_params=_params("parallel"),
        name="compress",
    )(pieces, p2, w1g, w2bd)


def _page_dma(pt_ref, pool_ref, buf_ref, sem_ref, n, slot, n_pages, start):
    for p in range(n_pages):
        page = pt_ref[n, p] if start else 0
        cp = pltpu.make_async_copy(pool_ref.at[page], buf_ref.at[slot, p], sem_ref.at[slot])
        if start:
            cp.start()
        else:
            cp.wait()


def _compress_paged_body(pt_ref, pool_ref, perm_ref, p2_ref, w1_ref, w2_ref, o_ref, buf, pieces, sem, *, n_pages):
    n = pl.program_id(0)
    slot = n % 2
    rows = PAGE_SIZE // CMP_STRIDE

    @pl.when(n == 0)
    def _():
        _page_dma(pt_ref, pool_ref, buf, sem, 0, 0, n_pages, True)

    @pl.when(n + 1 < pl.num_programs(0))
    def _():
        _page_dma(pt_ref, pool_ref, buf, sem, n + 1, 1 - slot, n_pages, True)

    _page_dma(pt_ref, pool_ref, buf, sem, n, slot, n_pages, False)
    perm = perm_ref[...]

    group = 16
    assert n_pages % group == 0 and CMP_STRIDE % 2 == 0
    low_half = lax.broadcasted_iota(jnp.int32, (rows, LANES), 1) < HEAD_DIM

    def unfold(i, carry):
        pairs = []
        for j in range(0, group, 2):
            two = buf[slot, pl.ds(i * group + j, 2)].reshape(2 * C_KV_WIDTH, PAGE_SIZE)
            pairs.append(_dot_nt(perm, two.astype(BF16)))
        for j in range(group):
            by_row = pairs[j // 2][:, (j % 2) * C_KV_WIDTH:(j % 2 + 1) * C_KV_WIDTH]
            r0 = pl.multiple_of((i * group + j) * rows, rows)
            for l in range(0, CMP_STRIDE, 2):
                even = by_row[l * rows:(l + 1) * rows, :]
                odd = by_row[(l + 1) * rows:(l + 2) * rows, :]
                lanes = slice(l * HEAD_DIM, (l + 2) * HEAD_DIM)
                pieces[0, pl.ds(r0, rows), lanes], pieces[1, pl.ds(r0, rows), lanes] = _split_heads(even, odd, low_half)
        return carry

    lax.fori_loop(0, n_pages // group, unfold, 0)
    o_ref[0] = _compress_mlp([pieces[g] for g in range(C_KV_HEADS)], p2_ref, w1_ref, w2_ref)


def _compress_paged(pool, page_table, cw):
    n, n_pages = page_table.shape
    rows = PAGE_SIZE // CMP_STRIDE
    n_piece = n_pages * rows
    p2g, w1g, w2bd = cw
    perm = np.zeros((PAGE_SIZE, PAGE_SIZE), np.float32)
    for l in range(CMP_STRIDE):
        for piece in range(rows):
            perm[l * rows + piece, CMP_STRIDE * piece + l] = 1.0
    perm = jnp.asarray(perm, BF16)
    pages = jnp.transpose(pool, (0, 2, 3, 1)).reshape(pool.shape[0], C_KV_WIDTH, PAGE_SIZE)
    const = lambda a: pl.BlockSpec(a.shape, lambda i, pt: (0,) * a.ndim)
    return pl.pallas_call(
        functools.partial(_compress_paged_body, n_pages=n_pages),
        grid_spec=pltpu.PrefetchScalarGridSpec(
            num_scalar_prefetch=1,
            grid=(n,),
            in_specs=[pl.BlockSpec(memory_space=pl.ANY), const(perm), const(p2g), const(w1g), const(w2bd)],
            out_specs=pl.BlockSpec((1, n_piece, C_KV_WIDTH), lambda i, pt: (i, 0, 0)),
            scratch_shapes=[pltpu.VMEM((2, n_pages, C_KV_WIDTH, PAGE_SIZE), F32),
                            pltpu.VMEM((C_KV_HEADS, n_piece, CMP_STRIDE * HEAD_DIM), F32),
                            pltpu.SemaphoreType.DMA((2,))]),
        out_shape=jax.ShapeDtypeStruct((n, n_piece, C_KV_WIDTH), F32),
        compiler_params=_params("arbitrary"),
        name="compress_paged",
    )(page_table, pages, perm, p2g, w1g, w2bd)


def _overlap_weights(n_cmp, n_cmp_pad, n_slc, n_slc_pad):
    i = np.arange(n_cmp_pad)[:, None]
    j = np.arange(n_slc_pad)[None, :]
    ov = (np.minimum(i * CMP_STRIDE + CMP_LEN, (j + 1) * SEL_BLOCK) - np.maximum(i * CMP_STRIDE, j * SEL_BLOCK))
    w = np.where((i < n_cmp) & (j < n_slc), np.clip(ov, 0, None) / CMP_LEN, 0.0)
    return jnp.asarray(w, dtype=BF16)


def _select_blocks(p_slc, qpos, n_slc):
    shape = p_slc.shape
    lane = lax.broadcasted_iota(jnp.int32, shape, 1)
    cur = _idiv(qpos, SEL_BLOCK)
    forced = (lane == 0) | (lane == cur) | (lane == cur - 1)
    causal = lane * SEL_BLOCK <= qpos
    score = jnp.where(forced, SEL_FORCE, jnp.where(causal, p_slc, -SEL_FORCE))
    score = jnp.where(lane < n_slc, score, -3.0 * SEL_FORCE)
    lane_f = lane.astype(F32)
    sel = jnp.zeros(shape, F32)
    for _ in range(min(SEL_TOPK, n_slc)):
        m = jnp.max(score, axis=-1, keepdims=True)
        first = jnp.min(jnp.where(score == m, lane_f, 1e9), axis=-1, keepdims=True)
        hit = lane_f == first
        sel = jnp.where(hit, jnp.where(m > -0.5 * SEL_FORCE, 1.0, 0.0), sel)
        score = jnp.where(hit, -4.0 * SEL_FORCE, score)
    return sel


def _select_blocks_t(p_slc, qpos, n_slc):
    n_rows = -(-n_slc // 8) * 8
    shape = (n_rows, p_slc.shape[1])
    blk = lax.broadcasted_iota(jnp.int32, shape, 0)
    cur = _idiv(qpos, SEL_BLOCK)
    forced = (blk == 0) | (blk == cur) | (blk == cur - 1)
    causal = blk * SEL_BLOCK <= qpos
    score = jnp.where(forced, SEL_FORCE, jnp.where(causal, p_slc[:n_rows], -SEL_FORCE))
    score = jnp.where(blk < n_slc, score, -3.0 * SEL_FORCE)
    tiles = [score[8 * v:8 * v + 8] for v in range(n_rows // 8)]
    sub = lax.broadcasted_iota(jnp.int32, (8, shape[1]), 0)
    ranks = [jnp.zeros((8, shape[1]), F32) for _ in tiles]
    for i in range(n_slc):
        row = tiles[i // 8][i % 8:i % 8 + 1]
        for v, tile in enumerate(tiles):
            ge = jnp.where(row >= tile, 1.0, 0.0)
            gt = jnp.where(row > tile, 1.0, 0.0)
            if 8 * v > i:
                ranks[v] = ranks[v] + ge
            elif 8 * v + 7 <= i:
                ranks[v] = ranks[v] + gt
            else:
                ranks[v] = ranks[v] + jnp.where(sub > i - 8 * v, ge, gt)
    rank = jnp.concatenate(ranks, axis=0)
    return jnp.where(rank < min(SEL_TOPK, n_slc), jnp.where(score > -0.5 * SEL_FORCE, 1.0, 0.0), 0.0)


NSA_TILE = 2 * LANES
ONES_ROWS = 16
LOG2E = 1.4426950408889634
LN2 = 0.6931471805599453


def _nsa_prompt_body(q_ref, kc_ref, vct_ref, ks_ref, vst_ref, kw_ref, vwt_ref, g_ref, wovt_ref, o_ref, m_sc, acc_sc,
                     *, n_cmp, n_slc):
    tq = tk = NSA_TILE
    rep = C_HEADS // C_KV_HEADS
    blk = pl.program_id(1)
    q0 = blk * tq
    n_cmp_pad = kc_ref.shape[1]
    sig_t = jax.nn.sigmoid(g_ref[0])
    q_t = [jnp.transpose(q_ref[0, :, c * LANES:(c + 1) * LANES].astype(F32))
           for c in range(C_WIDTH // LANES)]
    qpos = q0 + lax.broadcasted_iota(jnp.int32, (1, tq), 1)
    cmp_i = lax.broadcasted_iota(jnp.int32, (n_cmp_pad, tq), 0)
    cmp_ok = (cmp_i * CMP_STRIDE + (CMP_LEN - 1) <= qpos) & (cmp_i < n_cmp)
    cmp_bias = jnp.where(cmp_ok, 0.0, NEG_INF)
    any_block = jnp.where(qpos >= CMP_LEN - 1, 1.0, 0.0)
    ahead = lax.broadcasted_iota(jnp.int32, (tk, tq), 0) - lax.broadcasted_iota(jnp.int32, (tk, tq), 1)
    diag_bias = jnp.where(ahead <= 0, 0.0, NEG_INF)
    far_bias = jnp.where(ahead >= 0, 0.0, NEG_INF)
    ones_rows = jnp.where(lax.broadcasted_iota(jnp.int32, (ONES_ROWS, tk), 0) == 0, 1.0, 0.0).astype(BF16)

    def flash_update(scores, v_t):
        v_ext = jnp.concatenate([v_t, ones_rows], axis=0)
        for r, s in enumerate(scores):
            m_old = m_sc[r:r + 1, :]
            m_new = jnp.maximum(m_old, jnp.max(s, axis=0, keepdims=True))
            m_sc[r:r + 1, :] = m_new
            pv = _dot(v_ext, jnp.exp2(s - m_new).astype(BF16))
            acc_sc[r] = jnp.exp2(m_old - m_new) * acc_sc[r] + pv

    def flash_reset():
        m_sc[...] = jnp.full(m_sc.shape, NEG_INF, F32)
        acc_sc[...] = jnp.zeros(acc_sc.shape, F32)

    def flash_result(r):
        acc = acc_sc[r]
        return acc[:HEAD_DIM] * (1.0 / acc[HEAD_DIM:HEAD_DIM + 1])

    outs = []
    for g in range(C_KV_HEADS):
        gc = slice(g * HEAD_DIM, (g + 1) * HEAD_DIM)
        heads = [g * rep + r for r in range(rep)]
        q_heads = [q_t[h // 2][(h % 2) * HEAD_DIM:(h % 2 + 1) * HEAD_DIM] for h in heads]
        q_nat = [(qh * (HEAD_DIM ** -0.5)).astype(BF16) for qh in q_heads]
        q_log2 = [(qh * (HEAD_DIM ** -0.5 * LOG2E)).astype(BF16) for qh in q_heads]
        kcb = kc_ref[0, :, gc].astype(BF16)
        vct = vct_ref[0, gc, :].astype(BF16)
        scores = [_dot(kcb, q_nat[r]) + cmp_bias for r in range(rep)]
        probs = []
        imp = None
        for s in scores:
            p = jnp.exp(s - jnp.max(s, axis=0, keepdims=True))
            p = p * (any_block / jnp.sum(p, axis=0, keepdims=True))
            imp = p if imp is None else imp + p
            probs.append(p.astype(BF16))
        o_cmp = [_dot(vct, p) for p in probs]
        hi = imp.astype(BF16)
        r1 = imp - hi.astype(F32)
        mid = r1.astype(BF16)
        lo = (r1 - mid.astype(F32)).astype(BF16)
        p_slc = _dot(wovt_ref[...], hi) + _dot(wovt_ref[...], mid) + _dot(wovt_ref[...], lo)
        sel_bias = jnp.where(_select_blocks_t(p_slc, qpos, n_slc) > 0.5, 0.0, NEG_INF).astype(BF16)
        q_sel = [jnp.concatenate([qh, sel_bias], axis=0) for qh in q_log2]
        flash_reset()

        def sel_scores(k0):
            k_tile = ks_ref[0, pl.ds(pl.multiple_of(k0, tk), tk), g * LANES:(g + 1) * LANES]
            return [_dot(k_tile, qs) for qs in q_sel]

        def sel_values(k0):
            return vst_ref[0, gc, pl.ds(pl.multiple_of(k0, tk), tk)]

        def past_pair(i, carry):
            k0 = i * (2 * tk)
            first, second = sel_scores(k0), sel_scores(k0 + tk)
            flash_update(first, sel_values(k0))
            flash_update(second, sel_values(k0 + tk))
            return carry

        lax.fori_loop(0, blk // 2, past_pair, 0)
        k0 = (blk // 2) * (2 * tk)

        @pl.when(k0 == q0)
        def _():
            flash_update([s + diag_bias for s in sel_scores(q0)], sel_values(q0))

        @pl.when(k0 != q0)
        def _():
            first, second = sel_scores(k0), [s + diag_bias for s in sel_scores(q0)]
            flash_update(first, sel_values(k0))
            flash_update(second, sel_values(q0))

        o_sel = [flash_result(r) for r in range(rep)]
        flash_reset()
        n_back = C_WINDOW // tk
        win = []
        for back in range(n_back + 1):
            k0 = pl.multiple_of(jnp.maximum(q0 - back * tk, 0), tk)
            k_tile = kw_ref[0, pl.ds(k0, tk), gc]
            absent = jnp.where(blk >= back, 0.0, NEG_INF)
            edge = diag_bias if back == 0 else far_bias if back == n_back else None
            bias = absent if edge is None else edge + absent
            win.append(([_dot(k_tile, qh) + bias for qh in q_log2], vwt_ref[0, gc, pl.ds(k0, tk)]))
        for scores, values in win:
            flash_update(scores, values)
        for r, h in enumerate(heads):
            outs.append(sig_t[3 * h:3 * h + 1] * o_cmp[r] + sig_t[3 * h + 1:3 * h + 2] * o_sel[r]
                        + sig_t[3 * h + 2:3 * h + 3] * flash_result(r))
    for c in range(C_WIDTH // LANES):
        pair = jnp.transpose(jnp.concatenate(outs[2 * c:2 * c + 2], axis=0))
        o_ref[0, :, c * LANES:(c + 1) * LANES] = pair.astype(o_ref.dtype)


def _nsa_prompt(q, kcb, vcb, ks_ext, vs_t, kw, vw_t, gates_t, n_cmp):
    b, t, wq = q.shape
    n_cmp_pad = kcb.shape[1]
    n_slc = -(-t // SEL_BLOCK)
    tile = NSA_TILE
    assert t % (2 * tile) == 0 and n_slc == HEAD_DIM and C_WINDOW % tile == 0
    wov_t = _overlap_weights(n_cmp, n_cmp_pad, n_slc, LANES).T
    vcb_t = jnp.swapaxes(vcb, 1, 2)
    rep = C_HEADS // C_KV_HEADS
    whole = lambda a: pl.BlockSpec((1,) + a.shape[1:], lambda bi, i: (bi, 0, 0))
    return pl.pallas_call(
        functools.partial(_nsa_prompt_body, n_cmp=n_cmp, n_slc=n_slc),
        grid=(b, t // tile),
        in_specs=[pl.BlockSpec((1, tile, wq), lambda bi, i: (bi, i, 0)),
                  whole(kcb), whole(vcb_t), whole(ks_ext), whole(vs_t), whole(kw), whole(vw_t),
                  pl.BlockSpec((1, gates_t.shape[1], tile), lambda bi, i: (bi, 0, i)),
                  pl.BlockSpec(wov_t.shape, lambda bi, i: (0, 0))],
        out_specs=pl.BlockSpec((1, tile, wq), lambda bi, i: (bi, i, 0)),
        out_shape=jax.ShapeDtypeStruct((b, t, wq), BF16),
        scratch_shapes=[pltpu.VMEM((rep, tile), F32), pltpu.VMEM((rep, HEAD_DIM + ONES_ROWS, tile), F32)],
        compiler_params=_params("parallel", "arbitrary"),
        name="nsa_prompt",
    )(q, kcb, vcb_t, ks_ext, vs_t, kw, vw_t, gates_t, wov_t)


def _block_diag(q, blk_of_row, n_blk):
    return jnp.concatenate([jnp.where(blk_of_row == b, q, 0.0) for b in range(n_blk)], axis=-1)


def _diag_blocks(r, blk_of_row, n_blk):
    out = jnp.where(blk_of_row == 0, r[:, 0:HEAD_DIM], 0.0)
    for b in range(1, n_blk):
        out = out + jnp.where(blk_of_row == b, r[:, b * HEAD_DIM:(b + 1) * HEAD_DIM], 0.0)
    return out


def _row_ids(n_rows, n_heads, n_blk):
    row = lax.broadcasted_iota(jnp.int32, (n_rows, 1), 0)
    step = _idiv(row, n_heads)
    blk = _idiv(_imod(row, n_heads), n_heads // n_blk)
    return step, blk


def _pattern_weight(dist, patterns):
    w = jnp.zeros(dist.shape, F32)
    for window, dil in patterns:
        w = w + jnp.where(_imod(dist, dil) == 0, jnp.where(dist <= window, 1.0, 0.0), 0.0)
    return jnp.where(dist >= 0, w, 0.0)


def _decode_body(*refs, n_heads, n_blk, patterns, gate_col):
    q_ref, kc_ref, vc_ref, kn_ref, vn_ref = refs[:5]
    g_ref = refs[5] if gate_col is not None else None
    o_ref = refs[-1]
    n_rows = q_ref.shape[1]
    n_cache = kc_ref.shape[2]
    n_new = kn_ref.shape[1]
    step, blk = _row_ids(n_rows, n_heads, n_blk)
    qbd = _block_diag(q_ref[0] * (HEAD_DIM ** -0.5), blk, n_blk)
    s_c = _dot(qbd.astype(BF16), kc_ref[0].astype(BF16))
    w_c = _pattern_weight(n_cache + step - lax.broadcasted_iota(jnp.int32, (n_rows, n_cache), 1), patterns)
    s_c = jnp.where(w_c > 0, s_c, NEG_INF)
    kn = kn_ref[0]
    vn = vn_ref[0]
    s_n, w_n = [], []
    for c in range(n_new):
        s = jnp.sum(qbd * kn[c:c + 1, :], axis=-1, keepdims=True)
        w = _pattern_weight(step - c, patterns)
        s_n.append(jnp.where(w > 0, s, NEG_INF))
        w_n.append(w)
    m = jnp.max(s_c, axis=-1, keepdims=True)
    for s in s_n:
        m = jnp.maximum(m, s)
    p_c = w_c * jnp.exp(s_c - m)
    l = jnp.sum(p_c, axis=-1, keepdims=True)
    r = _dot_nt(p_c.astype(BF16), vc_ref[0].astype(BF16))
    for c in range(n_new):
        p = w_n[c] * jnp.exp(s_n[c] - m)
        l = l + p
        r = r + p * vn[c:c + 1, :]
    o = _diag_blocks(r, blk, n_blk) / l
    if g_ref is not None:
        o = o * jax.nn.sigmoid(g_ref[0])[:, gate_col:gate_col + 1]
    o_ref[0] = o


def _decode_attention(q, kc, vc, kn, vn, *, n_heads, n_blk, patterns, gate=None, gate_col=None):
    n, n_rows, _ = q.shape
    args = [q, kc, vc, kn, vn]
    if gate is not None:
        args.append(gate)
    whole = lambda a: pl.BlockSpec((1,) + a.shape[1:], lambda i: (i, 0, 0))
    return pl.pallas_call(
        functools.partial(_decode_body, n_heads=n_heads, n_blk=n_blk, patterns=patterns,
                          gate_col=gate_col if gate is not None else None),
        grid=(n,),
        in_specs=[whole(a) for a in args],
        out_specs=pl.BlockSpec((1, n_rows, HEAD_DIM), lambda i: (i, 0, 0)),
        out_shape=jax.ShapeDtypeStruct((n, n_rows, HEAD_DIM), F32),
        compiler_params=_params("parallel"),
        name="decode_attention",
    )(*args)


def _nsa_decode_body(pt_ref, q_ref, kc_ref, vc_ref, ksp_ref, vsp_ref, kn_ref, vn_ref, g_ref, wov_ref, o_ref,
                     kbuf, vbuf, expand, ksem, vsem, *, n_pages, n_cmp, n_slc):
    n = pl.program_id(0)
    slot = n % 2
    n_rows = q_ref.shape[1]
    n_new = kn_ref.shape[1]
    n_past = kbuf.shape[2]
    n_slc_pad = wov_ref.shape[1]
    rep = C_HEADS // C_KV_HEADS

    def gather(seq, dst_slot, start):
        for pool_ref, buf, sem in ((ksp_ref, kbuf, ksem), (vsp_ref, vbuf, vsem)):
            for p in range(n_pages):
                page = pt_ref[seq, p] if start else 0
                cp = pltpu.make_async_copy(pool_ref.at[page], buf.at[dst_slot, :, pl.ds(p * PAGE_SIZE, PAGE_SIZE)],
                                           sem.at[dst_slot])
                if start:
                    cp.start()
                else:
                    cp.wait()

    @pl.when(n == 0)
    def _():
        gather(0, 0, True)
        step_cols = 1024
        for c0 in range(0, n_past, step_cols):
            shape = (n_slc_pad, min(step_cols, n_past - c0))
            blk_of_key = _idiv(c0 + lax.broadcasted_iota(jnp.int32, shape, 1), SEL_BLOCK)
            hit = blk_of_key == lax.broadcasted_iota(jnp.int32, shape, 0)
            expand[:, c0:c0 + shape[1]] = jnp.where(hit, 1.0, 0.0).astype(BF16)

    @pl.when(n + 1 < pl.num_programs(0))
    def _():
        gather(n + 1, 1 - slot, True)

    step, blk = _row_ids(n_rows, C_HEADS, C_KV_HEADS)
    qpos = n_past + step
    qbd = _block_diag(q_ref[0] * (HEAD_DIM ** -0.5), blk, C_KV_HEADS)
    qbd_in = qbd.astype(BF16)
    n_cmp_pad = kc_ref.shape[1]
    cmp_i = lax.broadcasted_iota(jnp.int32, (n_rows, n_cmp_pad), 1)
    cmp_ok = (cmp_i * CMP_STRIDE + (CMP_LEN - 1) <= qpos) & (cmp_i < n_cmp)
    s = jnp.where(cmp_ok, _dot_nt(qbd_in, kc_ref[0].astype(BF16)), NEG_INF)
    m = jnp.max(s, axis=-1, keepdims=True)
    p = jnp.where(cmp_ok, jnp.exp(s - m), 0.0)
    l = jnp.sum(p, axis=-1, keepdims=True)
    p = p / jnp.where(l > 0, l, 1.0)
    o_cmp = _diag_blocks(_dot(p.astype(BF16), vc_ref[0].astype(BF16)), blk, C_KV_HEADS)
    grp = (n_rows // rep, rep, n_cmp_pad)
    imp = jnp.broadcast_to(jnp.sum(p.reshape(grp), axis=1, keepdims=True), grp).reshape(n_rows, n_cmp_pad)
    sel_rows = _select_blocks(_split3_dot(imp, wov_ref[...]), qpos, n_slc)
    gather(n, slot, False)
    blk_bias = jnp.where(sel_rows > 0.5, 0.0, NEG_INF).astype(BF16)
    s_p = _dot(qbd_in, kbuf[slot].astype(BF16)) + _dot(blk_bias, expand[...])
    kn = kn_ref[0]
    vn = vn_ref[0]
    s_n = []
    for c in range(n_new):
        blk_c = (n_past + c) // SEL_BLOCK
        ok = (sel_rows[:, blk_c:blk_c + 1] > 0.5) & (n_past + c <= qpos)
        s_n.append(jnp.where(ok, jnp.sum(qbd * kn[c:c + 1, :], axis=-1, keepdims=True), NEG_INF))
    m = jnp.max(s_p, axis=-1, keepdims=True)
    for sc in s_n:
        m = jnp.maximum(m, sc)
    p_p = jnp.exp(s_p - m)
    l = jnp.sum(p_p, axis=-1, keepdims=True)
    r = _dot_nt(p_p.astype(BF16), vbuf[slot].astype(BF16))
    for c in range(n_new):
        pn = jnp.exp(s_n[c] - m)
        l = l + pn
        r = r + pn * vn[c:c + 1, :]
    o_sel = _diag_blocks(r, blk, C_KV_HEADS) / l
    sig = jax.nn.sigmoid(g_ref[0])
    o_ref[0] = sig[:, 0:1] * o_cmp + sig[:, 1:2] * o_sel


def _nsa_decode(q, kcb, vcb, pool_ks, pool_vs, page_table, kn, vn, gate, n_cmp):
    n, n_rows, _ = q.shape
    n_pages = page_table.shape[1]
    n_past = n_pages * PAGE_SIZE
    n_new = kn.shape[1]
    n_slc = -(-(n_past + n_new) // SEL_BLOCK)
    n_slc_pad = -(-n_slc // LANES) * LANES
    wov = _overlap_weights(n_cmp, kcb.shape[1], n_slc, n_slc_pad)
    whole = lambda a: pl.BlockSpec((1,) + a.shape[1:], lambda i, pt: (i, 0, 0))
    pools = [jnp.transpose(a, (0, 2, 3, 1)).reshape(a.shape[0], C_KV_WIDTH, PAGE_SIZE) for a in (pool_ks, pool_vs)]
    return pl.pallas_call(
        functools.partial(_nsa_decode_body, n_pages=n_pages, n_cmp=n_cmp, n_slc=n_slc),
        grid_spec=pltpu.PrefetchScalarGridSpec(
            num_scalar_prefetch=1,
            grid=(n,),
            in_specs=[whole(q), whole(kcb), whole(vcb), pl.BlockSpec(memory_space=pl.ANY),
                      pl.BlockSpec(memory_space=pl.ANY), whole(kn), whole(vn), whole(gate),
                      pl.BlockSpec(wov.shape, lambda i, pt: (0, 0))],
            out_specs=pl.BlockSpec((1, n_rows, HEAD_DIM), lambda i, pt: (i, 0, 0)),
            scratch_shapes=[pltpu.VMEM((2, C_KV_WIDTH, n_past), F32), pltpu.VMEM((2, C_KV_WIDTH, n_past), F32),
                            pltpu.VMEM((n_slc_pad, n_past), BF16),
                            pltpu.SemaphoreType.DMA((2,)), pltpu.SemaphoreType.DMA((2,))]),
        out_shape=jax.ShapeDtypeStruct((n, n_rows, HEAD_DIM), F32),
        compiler_params=_params("arbitrary"),
        name="nsa_decode",
    )(page_table, q, kcb, vcb, *pools, kn, vn, gate, wov)


def _odd_out_body(x_ref, *refs):
    *branch_refs, z_ref, w_ref, gf_ref, y_ref = refs
    o = branch_refs[0][...].astype(F32)
    for ref in branch_refs[1:]:
        o = o + ref[...].astype(F32)
    mixed = (o * _silu(z_ref[...].astype(F32))).astype(BF16)
    h = x_ref[...] + _dot(mixed, w_ref[...])
    y_ref[...] = h * lax.rsqrt(jnp.mean(h * h, axis=-1, keepdims=True) + RMS_EPS) * gf_ref[...]


def _odd_out(x2d, branches, z, w_out, final_gain, tm):
    m, d = x2d.shape
    row = pl.BlockSpec((tm, d), lambda i: (i, 0))
    return pl.pallas_call(
        _odd_out_body,
        grid=(m // tm,),
        in_specs=[row] * (len(branches) + 2) + [pl.BlockSpec(w_out.shape, lambda i: (0, 0)),
                                                 pl.BlockSpec((1, d), lambda i: (0, 0))],
        out_specs=row,
        out_shape=jax.ShapeDtypeStruct((m, d), F32),
        compiler_params=_params("parallel"),
        name="odd_out",
    )(x2d, *branches, z, w_out, final_gain.reshape(1, d))


def _even_outs(cache_from):
    a, bk = A_WIDTH, B_HEADS * B_DK
    outs = [(0, a, F32, None), (a, a, F32, None), (2 * a, a, F32, None)]
    if cache_from is not None:
        outs += [(a, a, F32, cache_from), (2 * a, a, F32, cache_from)]
    outs += [(3 * a, a, BF16, None),
             (4 * a, bk, F32, None), (4 * a + bk, bk, F32, None),
             (4 * a + 2 * bk, B_WIDTH, F32, None), (4 * a + 2 * bk + B_WIDTH, B_WIDTH, BF16, None)]
    return outs


def _from_feature_major(u, n_heads):
    b, _, length = u.shape
    return jnp.transpose(u.reshape(b, n_heads, HEAD_DIM, length), (0, 3, 1, 2))


def _to_feature_major(u):
    n, length, n_heads, hd = u.shape
    return jnp.transpose(u, (0, 2, 3, 1)).reshape(n, n_heads * hd, length)


def _even_layer_prompt(x, gain, w_in, w_out):
    b, t, d = x.shape
    x2d = x.reshape(b * t, d)
    keep = min(A_PATTERNS[-1][0], t)
    qa, ka, va, ka_t, va_t, za, qb, kb, vb, zb = _norm_proj(x2d, gain, w_in, _even_outs(t - keep), PROJ_ROWS, t)
    seq = lambda u: u.reshape(b, t, u.shape[-1])
    oa = _dilated_attention(seq(qa), seq(ka), seq(va), A_PATTERNS)
    chunk = RET_CHUNK if t % RET_CHUNK == 0 else t
    s0 = jnp.zeros((b, B_HEADS, B_DK, B_DV), F32)
    ob, s_fin = _retention(seq(qb), seq(kb), seq(vb), s0, jnp.arange(t), chunk=chunk, c_real=chunk)
    y = _even_out(x2d, oa.reshape(b * t, A_WIDTH), za, ob.reshape(b * t, B_WIDTH), zb, w_out, OUT_ROWS)
    return y.reshape(b, t, d), _from_feature_major(ka_t, A_HEADS), _from_feature_major(va_t, A_HEADS), s_fin


def _even_layer_sample(x, cache_k, cache_v, state, gain, w_in, w_out, past_len):
    n, s_len, d = x.shape
    x2d = x.reshape(n * s_len, d)
    qa, ka32, va32, za, qb, kb, vb, zb = _norm_proj(x2d, gain, w_in, _even_outs(None), n * s_len, n * s_len)
    seq = lambda u: u.reshape(n, s_len, u.shape[-1])
    oa = _decode_attention(qa.reshape(n, s_len * A_HEADS, HEAD_DIM), _to_feature_major(cache_k),
                           _to_feature_major(cache_v), seq(ka32), seq(va32), n_heads=A_HEADS, n_blk=A_HEADS,
                           patterns=A_PATTERNS)
    oa = oa.reshape(n * s_len, A_WIDTH)
    chunk = 16
    pad = lambda u: jnp.pad(seq(u), ((0, 0), (0, chunk - s_len), (0, 0)))
    pos = past_len + jnp.arange(chunk)
    ob, s_new = _retention(pad(qb), pad(kb), pad(vb), state, pos, chunk=chunk, c_real=s_len)
    ob = ob[:, :s_len].reshape(n * s_len, B_WIDTH)
    y = _even_out(x2d, oa, za, ob, zb, w_out, n * s_len)
    heads = lambda u: u.reshape(n, s_len, A_HEADS, HEAD_DIM)
    return y.reshape(n, s_len, d), heads(ka32), heads(va32), s_new


ODD_KV = ("kc", "vc", "ks", "vs", "kw", "vw")
ODD_GATE_COL = C_WIDTH + len(ODD_KV) * C_KV_WIDTH
ODD_Z_COL = ODD_GATE_COL + 3 * C_HEADS


def _odd_outs_prompt(win_from):
    outs = [(0, C_WIDTH, BF16, None)]
    for i, name in enumerate(ODD_KV):
        off = C_WIDTH + i * C_KV_WIDTH
        outs.append((off, C_KV_WIDTH, BF16, 0 if name in ("vs", "vw") else BLOCK_TAGGED if name == "ks" else None))
        outs.append((off, C_KV_WIDTH, F32, win_from if name in ("kw", "vw") else 0))
    outs += [(ODD_GATE_COL, 3 * C_HEADS, F32, 0), (ODD_Z_COL, C_WIDTH, BF16, None)]
    return outs


def _odd_outs_sample():
    outs = [(0, C_WIDTH, BF16, None)]
    outs += [(C_WIDTH + i * C_KV_WIDTH, C_KV_WIDTH, F32, None) for i in range(len(ODD_KV))]
    outs += [(ODD_GATE_COL, 3 * C_HEADS, F32, None), (ODD_Z_COL, C_WIDTH, BF16, None)]
    return outs


def _odd_layer_prompt(x, gain, w_in, w_out, cw_k, cw_v, final_gain):
    b, t, d = x.shape
    x2d = x.reshape(b * t, d)
    keep = min(C_WINDOW, t)
    (q, kc, kc_t, vc, vc_t, ks, ks_t, vs_tb, vs_t, kw, kw_t, vw_tb, vw_t, gl_t, z) = _norm_proj(
        x2d, gain, w_in, _odd_outs_prompt(t - keep), PROJ_ROWS, t)
    seq = lambda u: u.reshape(b, t, u.shape[-1])
    n_cmp = (t - CMP_LEN) // CMP_STRIDE + 1
    pieces = lambda u: u.reshape(b, t // CMP_STRIDE, CMP_STRIDE * C_KV_WIDTH)
    kcb = _compress(pieces(kc), cw_k)
    vcb = _compress(pieces(vc), cw_v)
    o = _nsa_prompt(seq(q), kcb, vcb, seq(ks), vs_tb, seq(kw), vw_tb, gl_t, n_cmp)
    y = _odd_out(x2d, [o.reshape(b * t, C_WIDTH)], z, w_out, final_gain, OUT_ROWS)
    kv = lambda u: _from_feature_major(u, C_KV_HEADS)
    return y.reshape(b, t, d), kv(kc_t), kv(vc_t), kv(ks_t), kv(vs_t), kv(kw_t), kv(vw_t)


def _odd_layer_sample(x, page_table, pool_kc, pool_vc, pool_ks, pool_vs, buf_kw, buf_vw, gain, w_in, w_out,
                      cw_k, cw_v, final_gain):
    n, s_len, d = x.shape
    x2d = x.reshape(n * s_len, d)
    q, kc, vc, ks32, vs32, kw32, vw32, gl, z = _norm_proj(x2d, gain, w_in, _odd_outs_sample(), n * s_len, n * s_len)
    seq = lambda u: u.reshape(n, s_len, u.shape[-1])
    n_past = page_table.shape[1] * PAGE_SIZE
    n_cmp = (n_past + s_len - CMP_LEN) // CMP_STRIDE + 1
    assert CMP_STRIDE * (n_cmp - 1) + CMP_LEN <= n_past
    kcb = _compress_paged(pool_kc, page_table, cw_k)
    vcb = _compress_paged(pool_vc, page_table, cw_v)
    rows = lambda u: u.astype(F32).reshape(n, s_len * C_HEADS, HEAD_DIM)
    gate = gl.reshape(n, s_len * C_HEADS, 3)
    o_cs = _nsa_decode(rows(q), kcb, vcb, pool_ks, pool_vs, page_table, seq(ks32), seq(vs32), gate, n_cmp)
    o_win = _decode_attention(rows(q), _to_feature_major(buf_kw), _to_feature_major(buf_vw),
                              seq(kw32), seq(vw32), n_heads=C_HEADS, n_blk=C_KV_HEADS,
                              patterns=((C_WINDOW, 1),), gate=gate, gate_col=2)
    y = _odd_out(x2d, [o_cs.reshape(n * s_len, C_WIDTH), o_win.reshape(n * s_len, C_WIDTH)], z, w_out, final_gain,
                 n * s_len)
    kv = lambda u: u.reshape(n, s_len, C_KV_HEADS, HEAD_DIM)
    return y.reshape(n, s_len, d), kv(kc), kv(vc), kv(ks32), kv(vs32), kv(kw32), kv(vw32)


def kernel(x_prompt, x_sample, cache_a_k, cache_a_v, state_ret, cache_c_kcmp, cache_c_vcmp, cache_c_ksel,
           cache_c_vsel, cache_c_kwin, cache_c_vwin, page_table, even_norm, even_w_in, even_w_out, odd_norm,
           odd_w_in, odd_w_out, cmp_pos_k, cmp_w1_k, cmp_w2_k, cmp_pos_v, cmp_w1_v, cmp_w2_v, final_norm):
    assert even_norm.shape[0] == 1 and odd_norm.shape[0] == 1
    past_len = page_table.shape[1] * PAGE_SIZE
    w_in_e = even_w_in[0].T.astype(BF16)
    w_out_e = even_w_out[0].astype(BF16)
    w_in_o = odd_w_in[0].T.astype(BF16)
    w_out_o = odd_w_out[0].astype(BF16)
    cw_k = _compress_weights(cmp_pos_k[0], cmp_w1_k[0], cmp_w2_k[0])
    cw_v = _compress_weights(cmp_pos_v[0], cmp_w1_v[0], cmp_w2_v[0])

    hp, ak_p, av_p, ret_p = _even_layer_prompt(x_prompt, even_norm[0], w_in_e, w_out_e)
    hs, ak_s, av_s, ret_s = _even_layer_sample(x_sample, cache_a_k[0], cache_a_v[0], state_ret[0], even_norm[0],
                                               w_in_e, w_out_e, past_len)
    yp, *rows_p = _odd_layer_prompt(hp, odd_norm[0], w_in_o, w_out_o, cw_k, cw_v, final_norm)
    ys, *rows_s = _odd_layer_sample(hs, page_table, cache_c_kcmp[0], cache_c_vcmp[0], cache_c_ksel[0],
                                    cache_c_vsel[0], cache_c_kwin[0], cache_c_vwin[0], odd_norm[0], w_in_o, w_out_o,
                                    cw_k, cw_v, final_norm)
    lead = lambda u: u[None]
    out = [yp, ys, lead(ak_p), lead(ak_s), lead(av_p), lead(av_s), lead(ret_p), lead(ret_s)]
    for rp, rs in zip(rows_p, rows_s):
        out += [lead(rp), lead(rs)]
    return tuple(out)
```

```python
import functools

import numpy as np
import jax
import jax.numpy as jnp
from jax import lax
from jax.experimental import pallas as pl
from jax.experimental.pallas import tpu as pltpu

F32 = jnp.float32
BF16 = jnp.bfloat16

HEAD_DIM = 64
A_HEADS = 8
A_WIDTH = A_HEADS * HEAD_DIM
A_PATTERNS = ((128, 1), (512, 4), (2048, 16))
B_HEADS = 4
B_DK = 64
B_DV = 128
B_WIDTH = B_HEADS * B_DV
RET_CHUNK = 256
ROPE_BASE = 10000.0
GN_EPS = 1e-5
C_HEADS = 16
C_KV_HEADS = 2
C_WIDTH = C_HEADS * HEAD_DIM
C_KV_WIDTH = C_KV_HEADS * HEAD_DIM
CMP_LEN = 32
CMP_STRIDE = 16
CMP_HIDDEN = 256
SEL_BLOCK = 64
SEL_TOPK = 16
SEL_FORCE = 1e9
C_WINDOW = 512
PAGE_SIZE = 128
RMS_EPS = 1e-6
NEG_INF = -1e30

LANES = 128
ATTN_TQ = 128
PROJ_ROWS = 512
OUT_ROWS = 1024
VMEM_LIMIT = 56 * 1024 * 1024


def _params(*sem):
    return pltpu.CompilerParams(dimension_semantics=sem, vmem_limit_bytes=VMEM_LIMIT)


def _dot(a, b):
    return jnp.dot(a, b, preferred_element_type=F32)


def _dot_nt(a, b):
    return lax.dot_general(a, b, (((1,), (1,)), ((), ())), preferred_element_type=F32)


def _dot_tn(a, b):
    return lax.dot_general(a, b, (((0,), (0,)), ((), ())), preferred_element_type=F32)


def _idiv(x, n):
    assert n & (n - 1) == 0
    return jnp.right_shift(x, n.bit_length() - 1)


def _imod(x, n):
    assert n & (n - 1) == 0
    return jnp.bitwise_and(x, n - 1)


def _split3_dot(a, w):
    hi = a.astype(BF16)
    r1 = a - hi.astype(F32)
    mid = r1.astype(BF16)
    lo = (r1 - mid.astype(F32)).astype(BF16)
    return _dot(hi, w) + _dot(mid, w) + _dot(lo, w)


BLOCK_TAGGED = "block_tagged"


def _norm_proj_body(x_ref, g_ref, wt_ref, *o_refs, outs, tiles, tm):
    x = x_ref[...]
    y = x * lax.rsqrt(jnp.mean(x * x, axis=-1, keepdims=True) + RMS_EPS) * g_ref[...]
    yb = y.astype(BF16)
    tile = pl.program_id(0) % tiles
    done = {}
    for o_ref, (off, width, _, keep_from) in zip(o_refs, outs):
        w_rows = wt_ref[off:off + width, :]
        if keep_from is None or keep_from == BLOCK_TAGGED:
            if (off, width) not in done:
                done[(off, width)] = _dot_nt(yb, w_rows)
            res = done[(off, width)]
            if keep_from is None:
                o_ref[...] = res.astype(o_ref.dtype)
            else:
                lane = lax.broadcasted_iota(jnp.int32, res.shape, 1)
                pos = tile * tm + lax.broadcasted_iota(jnp.int32, res.shape, 0)
                tag = jnp.where(_idiv(pos, SEL_BLOCK) == _imod(lane, HEAD_DIM), 1.0, 0.0)
                low = lane < HEAD_DIM
                o_ref[:, 0:LANES] = jnp.where(low, res, tag).astype(o_ref.dtype)
                o_ref[:, LANES:2 * LANES] = jnp.where(low, pltpu.roll(res, HEAD_DIM, 1), tag).astype(o_ref.dtype)
        elif keep_from == 0:
            if (off, width, 0) not in done:
                done[(off, width, 0)] = _dot_nt(w_rows, yb)
            o_ref[0] = done[(off, width, 0)].astype(o_ref.dtype)
        else:
            @pl.when(tile >= keep_from // tm)
            def _():
                o_ref[0] = _dot_nt(w_rows, yb).astype(o_ref.dtype)


def _norm_proj(x2d, gain, w_t, outs, tm, seq_len):
    m, d = x2d.shape
    n = w_t.shape[0]
    tiles = seq_len // tm
    assert seq_len % tm == 0 and m % seq_len == 0
    out_specs, out_shape = [], []
    for _, width, dt, keep_from in outs:
        if keep_from is None or keep_from == BLOCK_TAGGED:
            if keep_from == BLOCK_TAGGED:
                assert width == C_KV_WIDTH and seq_len <= SEL_BLOCK * HEAD_DIM
                width = 2 * LANES
            out_specs.append(pl.BlockSpec((tm, width), lambda i: (i, 0)))
            out_shape.append(jax.ShapeDtypeStruct((m, width), dt))
        else:
            assert keep_from % tm == 0
            first = keep_from // tm
            out_specs.append(pl.BlockSpec((1, width, tm),
                                          lambda i, first=first: (i // tiles, 0, jnp.maximum(i % tiles - first, 0))))
            out_shape.append(jax.ShapeDtypeStruct((m // seq_len, width, seq_len - keep_from), dt))
    return pl.pallas_call(
        functools.partial(_norm_proj_body, outs=outs, tiles=tiles, tm=tm),
        grid=(m // tm,),
        in_specs=[pl.BlockSpec((tm, d), lambda i: (i, 0)),
                  pl.BlockSpec((1, d), lambda i: (0, 0)),
                  pl.BlockSpec((n, d), lambda i: (0, 0))],
        out_specs=out_specs,
        out_shape=out_shape,
        compiler_params=_params("arbitrary"),
        name="norm_proj",
    )(x2d, gain.reshape(1, d), w_t)


DIL_GROUP = 4


DIL_SPLIT = 4


def _dilated_body(q_ref, k_ref, v_ref, o_ref, o_sc, lse_sc, *split_refs, patterns):
    tq = ATTN_TQ
    blk_len = q_ref.shape[1]
    base = pl.program_id(2) * blk_len
    head_of_lane = _idiv(lax.broadcasted_iota(jnp.int32, (tq, LANES), 1), HEAD_DIM)
    own_lanes = [head_of_lane == h for h in range(LANES // HEAD_DIM)]

    q_split = k_split = v_split = None
    if split_refs:
        q_split, k_split, v_split = split_refs
        for r in range(DIL_SPLIT):
            q_split[r] = q_ref[0, pl.ds(r, blk_len // DIL_SPLIT, stride=DIL_SPLIT), :]

        @pl.when(pl.program_id(2) == 0)
        def _():
            n_rows = k_ref.shape[1] // DIL_SPLIT
            for r in range(DIL_SPLIT):
                k_split[r] = k_ref[0, pl.ds(r, n_rows, stride=DIL_SPLIT), :]
                v_split[r] = v_ref[0, pl.ds(r, n_rows, stride=DIL_SPLIT), :]

    def rows(ref, split, start, size, dil):
        if dil == 1:
            return ref[0, pl.ds(start, size), :]
        if dil <= DIL_SPLIT:
            return ref[0, pl.ds(start, size, stride=dil), :]
        return split[_imod(start, DIL_SPLIT), pl.ds(_idiv(start, DIL_SPLIT), size, stride=dil // DIL_SPLIT), :]

    for p_idx, (window, dil) in enumerate(patterns):
        lookback = window // dil
        n_prev = -(-lookback // tq)
        span = (n_prev + 1) * tq
        n_u = blk_len // (tq * dil)
        n_sub_blocks = dil * n_u
        assert blk_len % (tq * dil) == 0 and n_sub_blocks % DIL_GROUP == 0
        offs = lax.broadcasted_iota(jnp.int32, (tq, span), 0) - lax.broadcasted_iota(jnp.int32, (tq, span), 1)
        biases = []
        for lead in range(n_prev + 1):
            dist = offs + lead * tq
            biases.append(jnp.where(dist >= 0, jnp.where(dist <= lookback, 0.0, NEG_INF), NEG_INF))

        def trip(i, carry, dil=dil, n_prev=n_prev, span=span, p_idx=p_idx, biases=biases):
            work = []
            for j in range(DIL_GROUP):
                idx = i * DIL_GROUP + j
                res = _imod(idx, dil)
                u = _idiv(idx, dil)
                s0 = base // dil + u * tq
                ks = jnp.maximum(s0 - n_prev * tq, 0)
                q_row = res + dil * (u * tq)
                k_row = res + dil * ks
                q2 = rows(q_ref, q_split, q_row, tq, dil) * (HEAD_DIM ** -0.5 * LOG2E)
                k2 = rows(k_ref, k_split, k_row, span, dil).astype(BF16)
                v2 = rows(v_ref, v_split, k_row, span, dil).astype(BF16)
                bias = biases[n_prev]
                for lead in range(n_prev):
                    bias = jnp.where(s0 - ks == lead * tq, biases[lead], bias)
                work.append((q_row, q2, k2, v2, bias))
            scores = [[_dot_nt(jnp.where(own, q2, 0.0).astype(BF16), k2) + bias for own in own_lanes]
                      for _, q2, k2, _, bias in work]
            maxes = [[jnp.max(s, axis=-1, keepdims=True) for s in ss] for ss in scores]
            probs = [[jnp.exp2(s - m) for s, m in zip(ss, mm)] for ss, mm in zip(scores, maxes)]
            sums = [[jnp.sum(p, axis=-1, keepdims=True) for p in pp] for pp in probs]
            outs = [[_dot(p.astype(BF16), w[3]) for p in pp] for pp, w in zip(probs, work)]
            for w, oo, mm, ll in zip(work, outs, maxes, sums):
                o2 = oo[0] * (1.0 / ll[0])
                lse2 = jnp.broadcast_to(mm[0] * LN2 + jnp.log(ll[0]), (tq, LANES))
                for own, o, m, l in list(zip(own_lanes, oo, mm, ll))[1:]:
                    o2 = jnp.where(own, o * (1.0 / l), o2)
                    lse2 = jnp.where(own, m * LN2 + jnp.log(l), lse2)
                idx = pl.ds(w[0], tq) if dil == 1 else pl.ds(w[0], tq, stride=dil)
                o_sc[p_idx, idx, :] = o2
                lse_sc[p_idx, idx, :] = lse2
            return carry

        lax.fori_loop(0, n_sub_blocks // DIL_GROUP, trip, 0)

    lses = [lse_sc[p] for p in range(len(patterns))]
    top = lses[0]
    for l in lses[1:]:
        top = jnp.maximum(top, l)
    weights = [jnp.exp(l - top) for l in lses]
    num = weights[0] * o_sc[0]
    den = weights[0]
    for p in range(1, len(patterns)):
        num = num + weights[p] * o_sc[p]
        den = den + weights[p]
    o_ref[0] = (num / den).astype(o_ref.dtype)


def _dilated_attention(q, k, v, patterns):
    b, t, width = q.shape
    blk_len = min(t, max(ATTN_TQ * dil for _, dil in patterns))
    assert t % blk_len == 0
    for window, dil in patterns:
        assert t // dil >= (-(-(window // dil) // ATTN_TQ) + 1) * ATTN_TQ
    scratch = [pltpu.VMEM((len(patterns), blk_len, LANES), F32), pltpu.VMEM((len(patterns), blk_len, LANES), F32)]
    if any(dil > DIL_SPLIT for _, dil in patterns):
        assert all(dil % DIL_SPLIT == 0 for _, dil in patterns if dil > DIL_SPLIT)
        scratch += [pltpu.VMEM((DIL_SPLIT, blk_len // DIL_SPLIT, LANES), F32),
                    pltpu.VMEM((DIL_SPLIT, t // DIL_SPLIT, LANES), F32),
                    pltpu.VMEM((DIL_SPLIT, t // DIL_SPLIT, LANES), F32)]
    return pl.pallas_call(
        functools.partial(_dilated_body, patterns=patterns),
        grid=(b, width // LANES, t // blk_len),
        in_specs=[pl.BlockSpec((1, blk_len, LANES), lambda bi, hp, i: (bi, i, hp)),
                  pl.BlockSpec((1, t, LANES), lambda bi, hp, i: (bi, 0, hp)),
                  pl.BlockSpec((1, t, LANES), lambda bi, hp, i: (bi, 0, hp))],
        out_specs=pl.BlockSpec((1, blk_len, LANES), lambda bi, hp, i: (bi, i, hp)),
        out_shape=jax.ShapeDtypeStruct((b, t, width), BF16),
        scratch_shapes=scratch,
        compiler_params=_params("parallel", "parallel", "arbitrary"),
        name="dilated_attention",
    )(q, k, v)


def _retention_body(q_ref, k_ref, v_ref, cos_ref, sin_ref, qd_ref, kd_ref, intra_ref, cd_ref, s0_ref,
                    o_ref, sfin_ref, state):
    c = pl.program_id(1)

    @pl.when(c == 0)
    def _():
        state[...] = s0_ref[0]

    cos = cos_ref[...]
    sin = sin_ref[...]
    width = cos.shape[-1]
    half = B_DK // 2
    lane = _imod(lax.broadcasted_iota(jnp.int32, cos.shape, 1), B_DK)

    def rotary(x):
        fwd = pltpu.roll(x, half, 1)
        bwd = pltpu.roll(x, width - half, 1)
        return x * cos + jnp.where(lane < half, -bwd, fwd) * sin

    q = rotary(q_ref[0])
    k = rotary(k_ref[0]) * (B_DK ** -0.5)
    q_in = q.astype(BF16)
    k_in = k.astype(BF16)
    q_st = (q * qd_ref[...]).astype(BF16)
    k_st = (k * kd_ref[...]).astype(BF16)
    v = v_ref[0].astype(BF16)
    kcs = [slice(h * B_DK, (h + 1) * B_DK) for h in range(B_HEADS)]
    vcs = [slice(h * B_DV, (h + 1) * B_DV) for h in range(B_HEADS)]
    states = [state[h] for h in range(B_HEADS)]
    scores = [_dot_nt(q_in[:, kc], k_in[:, kc]) * intra_ref[h] for h, kc in enumerate(kcs)]
    carried = [_dot(q_st[:, kc], st.astype(BF16)) for kc, st in zip(kcs, states)]
    updates = [_dot_tn(k_st[:, kc], v[:, vc]) for kc, vc in zip(kcs, vcs)]
    inner = [_dot(s.astype(BF16), v[:, vc]) for s, vc in zip(scores, vcs)]
    for h in range(B_HEADS):
        o_ref[0, :, vcs[h]] = (inner[h] + carried[h]).astype(o_ref.dtype)
        state[h] = states[h] * cd_ref[h] + updates[h]

    @pl.when(c == pl.num_programs(1) - 1)
    def _():
        sfin_ref[0] = state[...]


def _retention(q, k, v, s0, pos, *, chunk, c_real):
    n, t, _ = q.shape
    half = B_DK // 2
    freqs = ROPE_BASE ** (-jnp.arange(half, dtype=F32) / half)
    ang = pos.astype(F32)[:, None] * freqs[None, :]
    cos = jnp.tile(jnp.cos(ang), (1, 2 * B_HEADS))
    sin = jnp.tile(jnp.sin(ang), (1, 2 * B_HEADS))
    log_g = jnp.log1p(-jnp.exp2(-5.0 - jnp.arange(B_HEADS, dtype=F32)))
    i = jnp.arange(chunk, dtype=F32)
    diff = i[:, None] - i[None, :]
    intra = jnp.where(diff >= 0, jnp.exp(log_g[:, None, None] * jnp.maximum(diff, 0.0)), 0.0)
    q_decay = jnp.repeat(jnp.exp(log_g[None, :] * (i[:, None] + 1.0)), B_DK, axis=1)
    k_decay = jnp.repeat(jnp.exp(log_g[None, :] * (c_real - 1.0 - i[:, None])), B_DK, axis=1)
    chunk_decay = jnp.broadcast_to(jnp.exp(log_g * c_real)[:, None, None], (B_HEADS, 1, B_DV))
    wqk = B_HEADS * B_DK
    return pl.pallas_call(
        _retention_body,
        grid=(n, t // chunk),
        in_specs=[pl.BlockSpec((1, chunk, wqk), lambda b, c: (b, c, 0)),
                  pl.BlockSpec((1, chunk, wqk), lambda b, c: (b, c, 0)),
                  pl.BlockSpec((1, chunk, B_WIDTH), lambda b, c: (b, c, 0)),
                  pl.BlockSpec((chunk, wqk), lambda b, c: (c, 0)),
                  pl.BlockSpec((chunk, wqk), lambda b, c: (c, 0)),
                  pl.BlockSpec((chunk, wqk), lambda b, c: (0, 0)),
                  pl.BlockSpec((chunk, wqk), lambda b, c: (0, 0)),
                  pl.BlockSpec((B_HEADS, chunk, chunk), lambda b, c: (0, 0, 0)),
                  pl.BlockSpec((B_HEADS, 1, B_DV), lambda b, c: (0, 0, 0)),
                  pl.BlockSpec((1, B_HEADS, B_DK, B_DV), lambda b, c: (b, 0, 0, 0))],
        out_specs=[pl.BlockSpec((1, chunk, B_WIDTH), lambda b, c: (b, c, 0)),
                   pl.BlockSpec((1, B_HEADS, B_DK, B_DV), lambda b, c: (b, 0, 0, 0))],
        out_shape=[jax.ShapeDtypeStruct((n, t, B_WIDTH), BF16),
                   jax.ShapeDtypeStruct((n, B_HEADS, B_DK, B_DV), F32)],
        scratch_shapes=[pltpu.VMEM((B_HEADS, B_DK, B_DV), F32)],
        compiler_params=_params("parallel", "arbitrary"),
        name="retention",
    )(q, k, v, cos, sin, q_decay, k_decay, intra, chunk_decay, s0)


def _silu(z):
    return z * jax.nn.sigmoid(z)


def _even_out_body(x_ref, oa_ref, za_ref, ob_ref, zb_ref, w_ref, y_ref):
    ga = (oa_ref[...].astype(F32) * _silu(za_ref[...].astype(F32))).astype(BF16)
    acc = x_ref[...] + _dot(ga, w_ref[0:A_WIDTH, :])
    ob = ob_ref[...].astype(F32)
    zb = zb_ref[...].astype(F32)
    for h in range(B_HEADS):
        cols = slice(h * B_DV, (h + 1) * B_DV)
        seg = ob[:, cols]
        mu = jnp.mean(seg, axis=-1, keepdims=True)
        cen = seg - mu
        var = jnp.mean(cen * cen, axis=-1, keepdims=True)
        gb = (cen * lax.rsqrt(var + GN_EPS) * _silu(zb[:, cols])).astype(BF16)
        acc = acc + _dot(gb, w_ref[A_WIDTH + h * B_DV:A_WIDTH + (h + 1) * B_DV, :])
    y_ref[...] = acc


def _even_out(x2d, oa, za, ob, zb, w_out, tm):
    m, d = x2d.shape
    row = lambda width: pl.BlockSpec((tm, width), lambda i: (i, 0))
    return pl.pallas_call(
        _even_out_body,
        grid=(m // tm,),
        in_specs=[row(d), row(A_WIDTH), row(A_WIDTH), row(B_WIDTH), row(B_WIDTH),
                  pl.BlockSpec(w_out.shape, lambda i: (0, 0))],
        out_specs=row(d),
        out_shape=jax.ShapeDtypeStruct((m, d), F32),
        compiler_params=_params("parallel"),
        name="even_out",
    )(x2d, oa, za, ob, zb, w_out)


def _compress_mlp(pieces, p2_ref, w1_ref, w2_ref):
    n_piece = pieces[0].shape[0]
    hidden = CMP_HIDDEN
    pc = _dot(p2_ref[...], w1_ref[...])
    pos = (pc[0:1, :hidden] + pc[2:3, :hidden]) + (pc[1:2, hidden:] + pc[3:4, hidden:])
    acts = [_dot(x.astype(BF16), w1_ref[...]) for x in pieces]
    hid = [_silu(a[:, :hidden] + pltpu.roll(a[:, hidden:], n_piece - 1, 0) + pos) for a in acts]
    return _dot(jnp.concatenate(hid, axis=-1).astype(BF16), w2_ref[...])


def _split_heads(even, odd, low_half):
    swapped = pltpu.roll(jnp.where(low_half, odd, even), HEAD_DIM, 1)
    return jnp.where(low_half, even, swapped), jnp.where(low_half, swapped, odd)


def _compress_body(x_ref, p2_ref, w1_ref, w2_ref, o_ref):
    x = x_ref[0].astype(F32)
    low_half = lax.broadcasted_iota(jnp.int32, (x.shape[0], LANES), 1) < HEAD_DIM
    slabs = [_split_heads(x[:, l * LANES:(l + 1) * LANES], x[:, (l + 1) * LANES:(l + 2) * LANES], low_half)
             for l in range(0, CMP_STRIDE, 2)]
    pieces = [jnp.concatenate([s[g] for s in slabs], axis=-1) for g in range(C_KV_HEADS)]
    o_ref[0] = _compress_mlp(pieces, p2_ref, w1_ref, w2_ref)


def _compress_weights(pos_emb, w1, w2):
    per = CMP_LEN // CMP_STRIDE
    w1g = jnp.transpose(w1.reshape(per, CMP_STRIDE, HEAD_DIM, CMP_HIDDEN), (1, 2, 0, 3))
    w1g = w1g.reshape(CMP_STRIDE * HEAD_DIM, per * CMP_HIDDEN)
    eye = jnp.eye(C_KV_HEADS, dtype=w2.dtype)
    w2bd = jnp.einsum('fd,gh->gfhd', w2, eye).reshape(C_KV_HEADS * CMP_HIDDEN, C_KV_WIDTH)
    halves = pos_emb.reshape(per, CMP_STRIDE * HEAD_DIM)
    hi = halves.astype(BF16)
    lo = (halves - hi.astype(F32)).astype(BF16)
    p2 = jnp.concatenate([hi, lo, jnp.zeros((16 - 2 * per, halves.shape[1]), BF16)], axis=0)
    return p2, w1g.astype(BF16), w2bd.astype(BF16)


def _compress(pieces, cw):
    n, n_piece, width = pieces.shape
    p2, w1g, w2bd = cw
    const = lambda a: pl.BlockSpec(a.shape, lambda i: (0,) * a.ndim)
    return pl.pallas_call(
        _compress_body,
        grid=(n,),
        in_specs=[pl.BlockSpec((1, n_piece, width), lambda i: (i, 0, 0)), const(p2), const(w1g), const(w2bd)],
        out_specs=pl.BlockSpec((1, n_piece, C_KV_WIDTH), lambda i: (i, 0, 0)),
        out_shape=jax.ShapeDtypeStruct((n, n_piece, C_KV_WIDTH), F32),
        compiler_params=_params("parallel"),
        name="compress",
    )(pieces, p2, w1g, w2bd)


def _page_dma(pt_ref, pool_ref, buf_ref, sem_ref, n, slot, n_pages, start):
    for p in range(n_pages):
        page = pt_ref[n, p] if start else 0
        cp = pltpu.make_async_copy(pool_ref.at[page], buf_ref.at[slot, p], sem_ref.at[slot])
        if start:
            cp.start()
        else:
            cp.wait()


def _compress_paged_body(pt_ref, pool_ref, perm_ref, p2_ref, w1_ref, w2_ref, o_ref, buf, pieces, sem, *, n_pages):
    n = pl.program_id(0)
    slot = n % 2
    rows = PAGE_SIZE // CMP_STRIDE

    @pl.when(n == 0)
    def _():
        _page_dma(pt_ref, pool_ref, buf, sem, 0, 0, n_pages, True)

    @pl.when(n + 1 < pl.num_programs(0))
    def _():
        _page_dma(pt_ref, pool_ref, buf, sem, n + 1, 1 - slot, n_pages, True)

    _page_dma(pt_ref, pool_ref, buf, sem, n, slot, n_pages, False)
    perm = perm_ref[...]

    group = 16
    assert n_pages % group == 0 and CMP_STRIDE % 2 == 0
    low_half = lax.broadcasted_iota(jnp.int32, (rows, LANES), 1) < HEAD_DIM

    def unfold(i, carry):
        pairs = []
        for j in range(0, group, 2):
            two = buf[slot, pl.ds(i * group + j, 2)].reshape(2 * C_KV_WIDTH, PAGE_SIZE)
            pairs.append(_dot_nt(perm, two.astype(BF16)))
        for j in range(group):
            by_row = pairs[j // 2][:, (j % 2) * C_KV_WIDTH:(j % 2 + 1) * C_KV_WIDTH]
            r0 = pl.multiple_of((i * group + j) * rows, rows)
            for l in range(0, CMP_STRIDE, 2):
                even = by_row[l * rows:(l + 1) * rows, :]
                odd = by_row[(l + 1) * rows:(l + 2) * rows, :]
                lanes = slice(l * HEAD_DIM, (l + 2) * HEAD_DIM)
                pieces[0, pl.ds(r0, rows), lanes], pieces[1, pl.ds(r0, rows), lanes] = _split_heads(even, odd, low_half)
        return carry

    lax.fori_loop(0, n_pages // group, unfold, 0)
    o_ref[0] = _compress_mlp([pieces[g] for g in range(C_KV_HEADS)], p2_ref, w1_ref, w2_ref)


def _compress_paged(pool, page_table, cw):
    n, n_pages = page_table.shape
    rows = PAGE_SIZE // CMP_STRIDE
    n_piece = n_pages * rows
    p2g, w1g, w2bd = cw
    perm = np.zeros((PAGE_SIZE, PAGE_SIZE), np.float32)
    for l in range(CMP_STRIDE):
        for piece in range(rows):
            perm[l * rows + piece, CMP_STRIDE * piece + l] = 1.0
    perm = jnp.asarray(perm, BF16)
    pages = jnp.transpose(pool, (0, 2, 3, 1)).reshape(pool.shape[0], C_KV_WIDTH, PAGE_SIZE)
    const = lambda a: pl.BlockSpec(a.shape, lambda i, pt: (0,) * a.ndim)
    return pl.pallas_call(
        functools.partial(_compress_paged_body, n_pages=n_pages),
        grid_spec=pltpu.PrefetchScalarGridSpec(
            num_scalar_prefetch=1,
            grid=(n,),
            in_specs=[pl.BlockSpec(memory_space=pl.ANY), const(perm), const(p2g), const(w1g), const(w2bd)],
            out_specs=pl.BlockSpec((1, n_piece, C_KV_WIDTH), lambda i, pt: (i, 0, 0)),
            scratch_shapes=[pltpu.VMEM((2, n_pages, C_KV_WIDTH, PAGE_SIZE), F32),
                            pltpu.VMEM((C_KV_HEADS, n_piece, CMP_STRIDE * HEAD_DIM), F32),
                            pltpu.SemaphoreType.DMA((2,))]),
        out_shape=jax.ShapeDtypeStruct((n, n_piece, C_KV_WIDTH), F32),
        compiler_params=_params("arbitrary"),
        name="compress_paged",
    )(page_table, pages, perm, p2g, w1g, w2bd)


def _overlap_weights(n_cmp, n_cmp_pad, n_slc, n_slc_pad):
    i = np.arange(n_cmp_pad)[:, None]
    j = np.arange(n_slc_pad)[None, :]
    ov = (np.minimum(i * CMP_STRIDE + CMP_LEN, (j + 1) * SEL_BLOCK) - np.maximum(i * CMP_STRIDE, j * SEL_BLOCK))
    w = np.where((i < n_cmp) & (j < n_slc), np.clip(ov, 0, None) / CMP_LEN, 0.0)
    return jnp.asarray(w, dtype=BF16)


def _select_blocks(p_slc, qpos, n_slc):
    shape = p_slc.shape
    lane = lax.broadcasted_iota(jnp.int32, shape, 1)
    cur = _idiv(qpos, SEL_BLOCK)
    forced = (lane == 0) | (lane == cur) | (lane == cur - 1)
    causal = lane * SEL_BLOCK <= qpos
    score = jnp.where(forced, SEL_FORCE, jnp.where(causal, p_slc, -SEL_FORCE))
    score = jnp.where(lane < n_slc, score, -3.0 * SEL_FORCE)
    lane_f = lane.astype(F32)
    sel = jnp.zeros(shape, F32)
    for _ in range(min(SEL_TOPK, n_slc)):
        m = jnp.max(score, axis=-1, keepdims=True)
        first = jnp.min(jnp.where(score == m, lane_f, 1e9), axis=-1, keepdims=True)
        hit = lane_f == first
        sel = jnp.where(hit, jnp.where(m > -0.5 * SEL_FORCE, 1.0, 0.0), sel)
        score = jnp.where(hit, -4.0 * SEL_FORCE, score)
    return sel


def _select_blocks_t(p_slc, qpos, n_slc):
    n_rows = -(-n_slc // 8) * 8
    shape = (n_rows, p_slc.shape[1])
    blk = lax.broadcasted_iota(jnp.int32, shape, 0)
    cur = _idiv(qpos, SEL_BLOCK)
    forced = (blk == 0) | (blk == cur) | (blk == cur - 1)
    causal = blk * SEL_BLOCK <= qpos
    score = jnp.where(forced, SEL_FORCE, jnp.where(causal, p_slc[:n_rows], -SEL_FORCE))
    score = jnp.where(blk < n_slc, score, -3.0 * SEL_FORCE)
    tiles = [score[8 * v:8 * v + 8] for v in range(n_rows // 8)]
    sub = lax.broadcasted_iota(jnp.int32, (8, shape[1]), 0)
    ranks = [jnp.zeros((8, shape[1]), F32) for _ in tiles]
    for i in range(n_slc):
        row = tiles[i // 8][i % 8:i % 8 + 1]
        for v, tile in enumerate(tiles):
            ge = jnp.where(row >= tile, 1.0, 0.0)
            gt = jnp.where(row > tile, 1.0, 0.0)
            if 8 * v > i:
                ranks[v] = ranks[v] + ge
            elif 8 * v + 7 <= i:
                ranks[v] = ranks[v] + gt
            else:
                ranks[v] = ranks[v] + jnp.where(sub > i - 8 * v, ge, gt)
    rank = jnp.concatenate(ranks, axis=0)
    return jnp.where(rank < min(SEL_TOPK, n_slc), jnp.where(score > -0.5 * SEL_FORCE, 1.0, 0.0), 0.0)


NSA_TILE = 2 * LANES
ONES_ROWS = 16
LOG2E = 1.4426950408889634
LN2 = 0.6931471805599453


def _nsa_prompt_body(q_ref, kc_ref, vct_ref, ks_ref, vst_ref, kw_ref, vwt_ref, g_ref, wovt_ref, o_ref, m_sc, acc_sc,
                     *, n_cmp, n_slc):
    tq = tk = NSA_TILE
    rep = C_HEADS // C_KV_HEADS
    blk = pl.program_id(1)
    q0 = blk * tq
    n_cmp_pad = kc_ref.shape[1]
    sig_t = jax.nn.sigmoid(g_ref[0])
    q_t = [jnp.transpose(q_ref[0, :, c * LANES:(c + 1) * LANES].astype(F32))
           for c in range(C_WIDTH // LANES)]
    qpos = q0 + lax.broadcasted_iota(jnp.int32, (1, tq), 1)
    cmp_i = lax.broadcasted_iota(jnp.int32, (n_cmp_pad, tq), 0)
    cmp_ok = (cmp_i * CMP_STRIDE + (CMP_LEN - 1) <= qpos) & (cmp_i < n_cmp)
    cmp_bias = jnp.where(cmp_ok, 0.0, NEG_INF)
    any_block = jnp.where(qpos >= CMP_LEN - 1, 1.0, 0.0)
    ahead = lax.broadcasted_iota(jnp.int32, (tk, tq), 0) - lax.broadcasted_iota(jnp.int32, (tk, tq), 1)
    diag_bias = jnp.where(ahead <= 0, 0.0, NEG_INF)
    far_bias = jnp.where(ahead >= 0, 0.0, NEG_INF)
    ones_rows = jnp.where(lax.broadcasted_iota(jnp.int32, (ONES_ROWS, tk), 0) == 0, 1.0, 0.0).astype(BF16)

    def flash_update(scores, v_t):
        v_ext = jnp.concatenate([v_t, ones_rows], axis=0)
        for r, s in enumerate(scores):
            m_old = m_sc[r:r + 1, :]
            m_new = jnp.maximum(m_old, jnp.max(s, axis=0, keepdims=True))
            m_sc[r:r + 1, :] = m_new
            pv = _dot(v_ext, jnp.exp2(s - m_new).astype(BF16))
            acc_sc[r] = jnp.exp2(m_old - m_new) * acc_sc[r] + pv

    def flash_reset():
        m_sc[...] = jnp.full(m_sc.shape, NEG_INF, F32)
        acc_sc[...] = jnp.zeros(acc_sc.shape, F32)

    def flash_result(r):
        acc = acc_sc[r]
        return acc[:HEAD_DIM] * (1.0 / acc[HEAD_DIM:HEAD_DIM + 1])

    outs = []
    for g in range(C_KV_HEADS):
        gc = slice(g * HEAD_DIM, (g + 1) * HEAD_DIM)
        heads = [g * rep + r for r in range(rep)]
        q_heads = [q_t[h // 2][(h % 2) * HEAD_DIM:(h % 2 + 1) * HEAD_DIM] for h in heads]
        q_nat = [(qh * (HEAD_DIM ** -0.5)).astype(BF16) for qh in q_heads]
        q_log2 = [(qh * (HEAD_DIM ** -0.5 * LOG2E)).astype(BF16) for qh in q_heads]
        kcb = kc_ref[0, :, gc].astype(BF16)
        vct = vct_ref[0, gc, :].astype(BF16)
        scores = [_dot(kcb, q_nat[r]) + cmp_bias for r in range(rep)]
        probs = []
        imp = None
        for s in scores:
            p = jnp.exp(s - jnp.max(s, axis=0, keepdims=True))
            p = p * (any_block / jnp.sum(p, axis=0, keepdims=True))
            imp = p if imp is None else imp + p
            probs.append(p.astype(BF16))
        o_cmp = [_dot(vct, p) for p in probs]
        hi = imp.astype(BF16)
        r1 = imp - hi.astype(F32)
        mid = r1.astype(BF16)
        lo = (r1 - mid.astype(F32)).astype(BF16)
        p_slc = _dot(wovt_ref[...], hi) + _dot(wovt_ref[...], mid) + _dot(wovt_ref[...], lo)
        sel_bias = jnp.where(_select_blocks_t(p_slc, qpos, n_slc) > 0.5, 0.0, NEG_INF).astype(BF16)
        q_sel = [jnp.concatenate([qh, sel_bias], axis=0) for qh in q_log2]
        flash_reset()

        def sel_scores(k0):
            k_tile = ks_ref[0, pl.ds(pl.multiple_of(k0, tk), tk), g * LANES:(g + 1) * LANES]
            return [_dot(k_tile, qs) for qs in q_sel]

        def sel_values(k0):
            return vst_ref[0, gc, pl.ds(pl.multiple_of(k0, tk), tk)]

        def past_pair(i, carry):
            k0 = i * (2 * tk)
            first, second = sel_scores(k0), sel_scores(k0 + tk)
            flash_update(first, sel_values(k0))
            flash_update(second, sel_values(k0 + tk))
            return carry

        lax.fori_loop(0, blk // 2, past_pair, 0)
        k0 = (blk // 2) * (2 * tk)

        @pl.when(k0 == q0)
        def _():
            flash_update([s + diag_bias for s in sel_scores(q0)], sel_values(q0))

        @pl.when(k0 != q0)
        def _():
            first, second = sel_scores(k0), [s + diag_bias for s in sel_scores(q0)]
            flash_update(first, sel_values(k0))
            flash_update(second, sel_values(q0))

        o_sel = [flash_result(r) for r in range(rep)]
        flash_reset()
        n_back = C_WINDOW // tk

        def window(n_tiles):
            win = []
            for back in range(n_tiles):
                k0 = pl.multiple_of(q0 - back * tk, tk)
                scores = [_dot(kw_ref[0, pl.ds(k0, tk), gc], qh) for qh in q_log2]
                edge = diag_bias if back == 0 else far_bias if back == n_back else None
                if edge is not None:
                    scores = [s + edge for s in scores]
                win.append((scores, vwt_ref[0, gc, pl.ds(k0, tk)]))
            for scores, values in win:
                flash_update(scores, values)

        for n_tiles in range(1, n_back + 2):
            pl.when(blk == n_tiles - 1 if n_tiles <= n_back else blk >= n_back)(functools.partial(window, n_tiles))
        for r, h in enumerate(heads):
            outs.append(sig_t[3 * h:3 * h + 1] * o_cmp[r] + sig_t[3 * h + 1:3 * h + 2] * o_sel[r]
                        + sig_t[3 * h + 2:3 * h + 3] * flash_result(r))
    for c in range(C_WIDTH // LANES):
        pair = jnp.transpose(jnp.concatenate(outs[2 * c:2 * c + 2], axis=0))
        o_ref[0, :, c * LANES:(c + 1) * LANES] = pair.astype(o_ref.dtype)


def _nsa_prompt(q, kcb, vcb, ks_ext, vs_t, kw, vw_t, gates_t, n_cmp):
    b, t, wq = q.shape
    n_cmp_pad = kcb.shape[1]
    n_slc = -(-t // SEL_BLOCK)
    tile = NSA_TILE
    assert t % (2 * tile) == 0 and n_slc == HEAD_DIM and C_WINDOW % tile == 0
    wov_t = _overlap_weights(n_cmp, n_cmp_pad, n_slc, LANES).T
    vcb_t = jnp.swapaxes(vcb, 1, 2)
    rep = C_HEADS // C_KV_HEADS
    whole = lambda a: pl.BlockSpec((1,) + a.shape[1:], lambda bi, i: (bi, 0, 0))
    return pl.pallas_call(
        functools.partial(_nsa_prompt_body, n_cmp=n_cmp, n_slc=n_slc),
        grid=(b, t // tile),
        in_specs=[pl.BlockSpec((1, tile, wq), lambda bi, i: (bi, i, 0)),
                  whole(kcb), whole(vcb_t), whole(ks_ext), whole(vs_t), whole(kw), whole(vw_t),
                  pl.BlockSpec((1, gates_t.shape[1], tile), lambda bi, i: (bi, 0, i)),
                  pl.BlockSpec(wov_t.shape, lambda bi, i: (0, 0))],
        out_specs=pl.BlockSpec((1, tile, wq), lambda bi, i: (bi, i, 0)),
        out_shape=jax.ShapeDtypeStruct((b, t, wq), BF16),
        scratch_shapes=[pltpu.VMEM((rep, tile), F32), pltpu.VMEM((rep, HEAD_DIM + ONES_ROWS, tile), F32)],
        compiler_params=_params("parallel", "arbitrary"),
        name="nsa_prompt",
    )(q, kcb, vcb_t, ks_ext, vs_t, kw, vw_t, gates_t, wov_t)


def _block_diag(q, blk_of_row, n_blk):
    return jnp.concatenate([jnp.where(blk_of_row == b, q, 0.0) for b in range(n_blk)], axis=-1)


def _diag_blocks(r, blk_of_row, n_blk):
    out = jnp.where(blk_of_row == 0, r[:, 0:HEAD_DIM], 0.0)
    for b in range(1, n_blk):
        out = out + jnp.where(blk_of_row == b, r[:, b * HEAD_DIM:(b + 1) * HEAD_DIM], 0.0)
    return out


def _row_ids(n_rows, n_heads, n_blk):
    row = lax.broadcasted_iota(jnp.int32, (n_rows, 1), 0)
    step = _idiv(row, n_heads)
    blk = _idiv(_imod(row, n_heads), n_heads // n_blk)
    return step, blk


def _pattern_weight(dist, patterns):
    w = jnp.zeros(dist.shape, F32)
    for window, dil in patterns:
        w = w + jnp.where(_imod(dist, dil) == 0, jnp.where(dist <= window, 1.0, 0.0), 0.0)
    return jnp.where(dist >= 0, w, 0.0)


def _decode_body(*refs, n_heads, n_blk, patterns, gate_col):
    q_ref, kc_ref, vc_ref, kn_ref, vn_ref = refs[:5]
    g_ref = refs[5] if gate_col is not None else None
    o_ref = refs[-1]
    n_rows = q_ref.shape[1]
    n_cache = kc_ref.shape[2]
    n_new = kn_ref.shape[1]
    step, blk = _row_ids(n_rows, n_heads, n_blk)
    qbd = _block_diag(q_ref[0] * (HEAD_DIM ** -0.5), blk, n_blk)
    s_c = _dot(qbd.astype(BF16), kc_ref[0].astype(BF16))
    w_c = _pattern_weight(n_cache + step - lax.broadcasted_iota(jnp.int32, (n_rows, n_cache), 1), patterns)
    s_c = jnp.where(w_c > 0, s_c, NEG_INF)
    kn = kn_ref[0]
    vn = vn_ref[0]
    s_n, w_n = [], []
    for c in range(n_new):
        s = jnp.sum(qbd * kn[c:c + 1, :], axis=-1, keepdims=True)
        w = _pattern_weight(step - c, patterns)
        s_n.append(jnp.where(w > 0, s, NEG_INF))
        w_n.append(w)
    m = jnp.max(s_c, axis=-1, keepdims=True)
    for s in s_n:
        m = jnp.maximum(m, s)
    p_c = w_c * jnp.exp(s_c - m)
    l = jnp.sum(p_c, axis=-1, keepdims=True)
    r = _dot_nt(p_c.astype(BF16), vc_ref[0].astype(BF16))
    for c in range(n_new):
        p = w_n[c] * jnp.exp(s_n[c] - m)
        l = l + p
        r = r + p * vn[c:c + 1, :]
    o = _diag_blocks(r, blk, n_blk) / l
    if g_ref is not None:
        o = o * jax.nn.sigmoid(g_ref[0])[:, gate_col:gate_col + 1]
    o_ref[0] = o


def _decode_attention(q, kc, vc, kn, vn, *, n_heads, n_blk, patterns, gate=None, gate_col=None):
    n, n_rows, _ = q.shape
    args = [q, kc, vc, kn, vn]
    if gate is not None:
        args.append(gate)
    whole = lambda a: pl.BlockSpec((1,) + a.shape[1:], lambda i: (i, 0, 0))
    return pl.pallas_call(
        functools.partial(_decode_body, n_heads=n_heads, n_blk=n_blk, patterns=patterns,
                          gate_col=gate_col if gate is not None else None),
        grid=(n,),
        in_specs=[whole(a) for a in args],
        out_specs=pl.BlockSpec((1, n_rows, HEAD_DIM), lambda i: (i, 0, 0)),
        out_shape=jax.ShapeDtypeStruct((n, n_rows, HEAD_DIM), F32),
        compiler_params=_params("parallel"),
        name="decode_attention",
    )(*args)


def _nsa_decode_body(pt_ref, q_ref, kc_ref, vc_ref, ksp_ref, vsp_ref, kn_ref, vn_ref, g_ref, wov_ref, o_ref,
                     kbuf, vbuf, expand, ksem, vsem, *, n_pages, n_cmp, n_slc):
    n = pl.program_id(0)
    slot = n % 2
    n_rows = q_ref.shape[1]
    n_new = kn_ref.shape[1]
    n_past = kbuf.shape[2]
    n_slc_pad = wov_ref.shape[1]
    rep = C_HEADS // C_KV_HEADS

    def gather(seq, dst_slot, start):
        for pool_ref, buf, sem in ((ksp_ref, kbuf, ksem), (vsp_ref, vbuf, vsem)):
            for p in range(n_pages):
                page = pt_ref[seq, p] if start else 0
                cp = pltpu.make_async_copy(pool_ref.at[page], buf.at[dst_slot, :, pl.ds(p * PAGE_SIZE, PAGE_SIZE)],
                                           sem.at[dst_slot])
                if start:
                    cp.start()
                else:
                    cp.wait()

    @pl.when(n == 0)
    def _():
        gather(0, 0, True)
        step_cols = 1024
        for c0 in range(0, n_past, step_cols):
            shape = (n_slc_pad, min(step_cols, n_past - c0))
            blk_of_key = _idiv(c0 + lax.broadcasted_iota(jnp.int32, shape, 1), SEL_BLOCK)
            hit = blk_of_key == lax.broadcasted_iota(jnp.int32, shape, 0)
            expand[:, c0:c0 + shape[1]] = jnp.where(hit, 1.0, 0.0).astype(BF16)

    @pl.when(n + 1 < pl.num_programs(0))
    def _():
        gather(n + 1, 1 - slot, True)

    step, blk = _row_ids(n_rows, C_HEADS, C_KV_HEADS)
    qpos = n_past + step
    qbd = _block_diag(q_ref[0] * (HEAD_DIM ** -0.5), blk, C_KV_HEADS)
    qbd_in = qbd.astype(BF16)
    n_cmp_pad = kc_ref.shape[1]
    cmp_i = lax.broadcasted_iota(jnp.int32, (n_rows, n_cmp_pad), 1)
    cmp_ok = (cmp_i * CMP_STRIDE + (CMP_LEN - 1) <= qpos) & (cmp_i < n_cmp)
    s = jnp.where(cmp_ok, _dot_nt(qbd_in, kc_ref[0].astype(BF16)), NEG_INF)
    m = jnp.max(s, axis=-1, keepdims=True)
    p = jnp.where(cmp_ok, jnp.exp(s - m), 0.0)
    l = jnp.sum(p, axis=-1, keepdims=True)
    p = p / jnp.where(l > 0, l, 1.0)
    o_cmp = _diag_blocks(_dot(p.astype(BF16), vc_ref[0].astype(BF16)), blk, C_KV_HEADS)
    grp = (n_rows // rep, rep, n_cmp_pad)
    imp = jnp.broadcast_to(jnp.sum(p.reshape(grp), axis=1, keepdims=True), grp).reshape(n_rows, n_cmp_pad)
    sel_rows = _select_blocks(_split3_dot(imp, wov_ref[...]), qpos, n_slc)
    gather(n, slot, False)
    blk_bias = jnp.where(sel_rows > 0.5, 0.0, NEG_INF).astype(BF16)
    s_p = _dot(qbd_in, kbuf[slot].astype(BF16)) + _dot(blk_bias, expand[...])
    kn = kn_ref[0]
    vn = vn_ref[0]
    s_n = []
    for c in range(n_new):
        blk_c = (n_past + c) // SEL_BLOCK
        ok = (sel_rows[:, blk_c:blk_c + 1] > 0.5) & (n_past + c <= qpos)
        s_n.append(jnp.where(ok, jnp.sum(qbd * kn[c:c + 1, :], axis=-1, keepdims=True), NEG_INF))
    m = jnp.max(s_p, axis=-1, keepdims=True)
    for sc in s_n:
        m = jnp.maximum(m, sc)
    p_p = jnp.exp(s_p - m)
    l = jnp.sum(p_p, axis=-1, keepdims=True)
    r = _dot_nt(p_p.astype(BF16), vbuf[slot].astype(BF16))
    for c in range(n_new):
        pn = jnp.exp(s_n[c] - m)
        l = l + pn
        r = r + pn * vn[c:c + 1, :]
    o_sel = _diag_blocks(r, blk, C_KV_HEADS) / l
    sig = jax.nn.sigmoid(g_ref[0])
    o_ref[0] = sig[:, 0:1] * o_cmp + sig[:, 1:2] * o_sel


def _nsa_decode(q, kcb, vcb, pool_ks, pool_vs, page_table, kn, vn, gate, n_cmp):
    n, n_rows, _ = q.shape
    n_pages = page_table.shape[1]
    n_past = n_pages * PAGE_SIZE
    n_new = kn.shape[1]
    n_slc = -(-(n_past + n_new) // SEL_BLOCK)
    n_slc_pad = -(-n_slc // LANES) * LANES
    wov = _overlap_weights(n_cmp, kcb.shape[1], n_slc, n_slc_pad)
    whole = lambda a: pl.BlockSpec((1,) + a.shape[1:], lambda i, pt: (i, 0, 0))
    pools = [jnp.transpose(a, (0, 2, 3, 1)).reshape(a.shape[0], C_KV_WIDTH, PAGE_SIZE) for a in (pool_ks, pool_vs)]
    return pl.pallas_call(
        functools.partial(_nsa_decode_body, n_pages=n_pages, n_cmp=n_cmp, n_slc=n_slc),
        grid_spec=pltpu.PrefetchScalarGridSpec(
            num_scalar_prefetch=1,
            grid=(n,),
            in_specs=[whole(q), whole(kcb), whole(vcb), pl.BlockSpec(memory_space=pl.ANY),
                      pl.BlockSpec(memory_space=pl.ANY), whole(kn), whole(vn), whole(gate),
                      pl.BlockSpec(wov.shape, lambda i, pt: (0, 0))],
            out_specs=pl.BlockSpec((1, n_rows, HEAD_DIM), lambda i, pt: (i, 0, 0)),
            scratch_shapes=[pltpu.VMEM((2, C_KV_WIDTH, n_past), F32), pltpu.VMEM((2, C_KV_WIDTH, n_past), F32),
                            pltpu.VMEM((n_slc_pad, n_past), BF16),
                            pltpu.SemaphoreType.DMA((2,)), pltpu.SemaphoreType.DMA((2,))]),
        out_shape=jax.ShapeDtypeStruct((n, n_rows, HEAD_DIM), F32),
        compiler_params=_params("arbitrary"),
        name="nsa_decode",
    )(page_table, q, kcb, vcb, *pools, kn, vn, gate, wov)


def _odd_out_body(x_ref, *refs):
    *branch_refs, z_ref, w_ref, gf_ref, y_ref = refs
    o = branch_refs[0][...].astype(F32)
    for ref in branch_refs[1:]:
        o = o + ref[...].astype(F32)
    mixed = (o * _silu(z_ref[...].astype(F32))).astype(BF16)
    h = x_ref[...] + _dot(mixed, w_ref[...])
    y_ref[...] = h * lax.rsqrt(jnp.mean(h * h, axis=-1, keepdims=True) + RMS_EPS) * gf_ref[...]


def _odd_out(x2d, branches, z, w_out, final_gain, tm):
    m, d = x2d.shape
    row = pl.BlockSpec((tm, d), lambda i: (i, 0))
    return pl.pallas_call(
        _odd_out_body,
        grid=(m // tm,),
        in_specs=[row] * (len(branches) + 2) + [pl.BlockSpec(w_out.shape, lambda i: (0, 0)),
                                                 pl.BlockSpec((1, d), lambda i: (0, 0))],
        out_specs=row,
        out_shape=jax.ShapeDtypeStruct((m, d), F32),
        compiler_params=_params("parallel"),
        name="odd_out",
    )(x2d, *branches, z, w_out, final_gain.reshape(1, d))


def _even_outs(cache_from):
    a, bk = A_WIDTH, B_HEADS * B_DK
    outs = [(0, a, F32, None), (a, a, F32, None), (2 * a, a, F32, None)]
    if cache_from is not None:
        outs += [(a, a, F32, cache_from), (2 * a, a, F32, cache_from)]
    outs += [(3 * a, a, BF16, None),
             (4 * a, bk, F32, None), (4 * a + bk, bk, F32, None),
             (4 * a + 2 * bk, B_WIDTH, F32, None), (4 * a + 2 * bk + B_WIDTH, B_WIDTH, BF16, None)]
    return outs


def _from_feature_major(u, n_heads):
    b, _, length = u.shape
    return jnp.transpose(u.reshape(b, n_heads, HEAD_DIM, length), (0, 3, 1, 2))


def _to_feature_major(u):
    n, length, n_heads, hd = u.shape
    return jnp.transpose(u, (0, 2, 3, 1)).reshape(n, n_heads * hd, length)


def _even_layer_prompt(x, gain, w_in, w_out):
    b, t, d = x.shape
    x2d = x.reshape(b * t, d)
    keep = min(A_PATTERNS[-1][0], t)
    qa, ka, va, ka_t, va_t, za, qb, kb, vb, zb = _norm_proj(x2d, gain, w_in, _even_outs(t - keep), PROJ_ROWS, t)
    seq = lambda u: u.reshape(b, t, u.shape[-1])
    oa = _dilated_attention(seq(qa), seq(ka), seq(va), A_PATTERNS)
    chunk = RET_CHUNK if t % RET_CHUNK == 0 else t
    s0 = jnp.zeros((b, B_HEADS, B_DK, B_DV), F32)
    ob, s_fin = _retention(seq(qb), seq(kb), seq(vb), s0, jnp.arange(t), chunk=chunk, c_real=chunk)
    y = _even_out(x2d, oa.reshape(b * t, A_WIDTH), za, ob.reshape(b * t, B_WIDTH), zb, w_out, OUT_ROWS)
    return y.reshape(b, t, d), _from_feature_major(ka_t, A_HEADS), _from_feature_major(va_t, A_HEADS), s_fin


def _even_layer_sample(x, cache_k, cache_v, state, gain, w_in, w_out, past_len):
    n, s_len, d = x.shape
    x2d = x.reshape(n * s_len, d)
    qa, ka32, va32, za, qb, kb, vb, zb = _norm_proj(x2d, gain, w_in, _even_outs(None), n * s_len, n * s_len)
    seq = lambda u: u.reshape(n, s_len, u.shape[-1])
    oa = _decode_attention(qa.reshape(n, s_len * A_HEADS, HEAD_DIM), _to_feature_major(cache_k),
                           _to_feature_major(cache_v), seq(ka32), seq(va32), n_heads=A_HEADS, n_blk=A_HEADS,
                           patterns=A_PATTERNS)
    oa = oa.reshape(n * s_len, A_WIDTH)
    chunk = 16
    pad = lambda u: jnp.pad(seq(u), ((0, 0), (0, chunk - s_len), (0, 0)))
    pos = past_len + jnp.arange(chunk)
    ob, s_new = _retention(pad(qb), pad(kb), pad(vb), state, pos, chunk=chunk, c_real=s_len)
    ob = ob[:, :s_len].reshape(n * s_len, B_WIDTH)
    y = _even_out(x2d, oa, za, ob, zb, w_out, n * s_len)
    heads = lambda u: u.reshape(n, s_len, A_HEADS, HEAD_DIM)
    return y.reshape(n, s_len, d), heads(ka32), heads(va32), s_new


ODD_KV = ("kc", "vc", "ks", "vs", "kw", "vw")
ODD_GATE_COL = C_WIDTH + len(ODD_KV) * C_KV_WIDTH
ODD_Z_COL = ODD_GATE_COL + 3 * C_HEADS


def _odd_outs_prompt(win_from):
    outs = [(0, C_WIDTH, BF16, None)]
    for i, name in enumerate(ODD_KV):
        off = C_WIDTH + i * C_KV_WIDTH
        outs.append((off, C_KV_WIDTH, BF16, 0 if name in ("vs", "vw") else BLOCK_TAGGED if name == "ks" else None))
        outs.append((off, C_KV_WIDTH, F32, win_from if name in ("kw", "vw") else 0))
    outs += [(ODD_GATE_COL, 3 * C_HEADS, F32, 0), (ODD_Z_COL, C_WIDTH, BF16, None)]
    return outs


def _odd_outs_sample():
    outs = [(0, C_WIDTH, BF16, None)]
    outs += [(C_WIDTH + i * C_KV_WIDTH, C_KV_WIDTH, F32, None) for i in range(len(ODD_KV))]
    outs += [(ODD_GATE_COL, 3 * C_HEADS, F32, None), (ODD_Z_COL, C_WIDTH, BF16, None)]
    return outs


def _odd_layer_prompt(x, gain, w_in, w_out, cw_k, cw_v, final_gain):
    b, t, d = x.shape
    x2d = x.reshape(b * t, d)
    keep = min(C_WINDOW, t)
    (q, kc, kc_t, vc, vc_t, ks, ks_t, vs_tb, vs_t, kw, kw_t, vw_tb, vw_t, gl_t, z) = _norm_proj(
        x2d, gain, w_in, _odd_outs_prompt(t - keep), PROJ_ROWS, t)
    seq = lambda u: u.reshape(b, t, u.shape[-1])
    n_cmp = (t - CMP_LEN) // CMP_STRIDE + 1
    pieces = lambda u: u.reshape(b, t // CMP_STRIDE, CMP_STRIDE * C_KV_WIDTH)
    kcb = _compress(pieces(kc), cw_k)
    vcb = _compress(pieces(vc), cw_v)
    o = _nsa_prompt(seq(q), kcb, vcb, seq(ks), vs_tb, seq(kw), vw_tb, gl_t, n_cmp)
    y = _odd_out(x2d, [o.reshape(b * t, C_WIDTH)], z, w_out, final_gain, OUT_ROWS)
    kv = lambda u: _from_feature_major(u, C_KV_HEADS)
    return y.reshape(b, t, d), kv(kc_t), kv(vc_t), kv(ks_t), kv(vs_t), kv(kw_t), kv(vw_t)


def _odd_layer_sample(x, page_table, pool_kc, pool_vc, pool_ks, pool_vs, buf_kw, buf_vw, gain, w_in, w_out,
                      cw_k, cw_v, final_gain):
    n, s_len, d = x.shape
    x2d = x.reshape(n * s_len, d)
    q, kc, vc, ks32, vs32, kw32, vw32, gl, z = _norm_proj(x2d, gain, w_in, _odd_outs_sample(), n * s_len, n * s_len)
    seq = lambda u: u.reshape(n, s_len, u.shape[-1])
    n_past = page_table.shape[1] * PAGE_SIZE
    n_cmp = (n_past + s_len - CMP_LEN) // CMP_STRIDE + 1
    assert CMP_STRIDE * (n_cmp - 1) + CMP_LEN <= n_past
    kcb = _compress_paged(pool_kc, page_table, cw_k)
    vcb = _compress_paged(pool_vc, page_table, cw_v)
    rows = lambda u: u.astype(F32).reshape(n, s_len * C_HEADS, HEAD_DIM)
    gate = gl.reshape(n, s_len * C_HEADS, 3)
    o_cs = _nsa_decode(rows(q), kcb, vcb, pool_ks, pool_vs, page_table, seq(ks32), seq(vs32), gate, n_cmp)
    o_win = _decode_attention(rows(q), _to_feature_major(buf_kw), _to_feature_major(buf_vw),
                              seq(kw32), seq(vw32), n_heads=C_HEADS, n_blk=C_KV_HEADS,
                              patterns=((C_WINDOW, 1),), gate=gate, gate_col=2)
    y = _odd_out(x2d, [o_cs.reshape(n * s_len, C_WIDTH), o_win.reshape(n * s_len, C_WIDTH)], z, w_out, final_gain,
                 n * s_len)
    kv = lambda u: u.reshape(n, s_len, C_KV_HEADS, HEAD_DIM)
    return y.reshape(n, s_len, d), kv(kc), kv(vc), kv(ks32), kv(vs32), kv(kw32), kv(vw32)


def kernel(x_prompt, x_sample, cache_a_k, cache_a_v, state_ret, cache_c_kcmp, cache_c_vcmp, cache_c_ksel,
           cache_c_vsel, cache_c_kwin, cache_c_vwin, page_table, even_norm, even_w_in, even_w_out, odd_norm,
           odd_w_in, odd_w_out, cmp_pos_k, cmp_w1_k, cmp_w2_k, cmp_pos_v, cmp_w1_v, cmp_w2_v, final_norm):
    assert even_norm.shape[0] == 1 and odd_norm.shape[0] == 1
    past_len = page_table.shape[1] * PAGE_SIZE
    w_in_e = even_w_in[0].T.astype(BF16)
    w_out_e = even_w_out[0].astype(BF16)
    w_in_o = odd_w_in[0].T.astype(BF16)
    w_out_o = odd_w_out[0].astype(BF16)
    cw_k = _compress_weights(cmp_pos_k[0], cmp_w1_k[0], cmp_w2_k[0])
    cw_v = _compress_weights(cmp_pos_v[0], cmp_w1_v[0], cmp_w2_v[0])

    hp, ak_p, av_p, ret_p = _even_layer_prompt(x_prompt, even_norm[0], w_in_e, w_out_e)
    hs, ak_s, av_s, ret_s = _even_layer_sample(x_sample, cache_a_k[0], cache_a_v[0], state_ret[0], even_norm[0],
                                               w_in_e, w_out_e, past_len)
    yp, *rows_p = _odd_layer_prompt(hp, odd_norm[0], w_in_o, w_out_o, cw_k, cw_v, final_norm)
    ys, *rows_s = _odd_layer_sample(hs, page_table, cache_c_kcmp[0], cache_c_vcmp[0], cache_c_ksel[0],
                                    cache_c_vsel[0], cache_c_kwin[0], cache_c_vwin[0], odd_norm[0], w_in_o, w_out_o,
                                    cw_k, cw_v, final_norm)
    lead = lambda u: u[None]
    out = [yp, ys, lead(ak_p), lead(ak_s), lead(av_p), lead(av_s), lead(ret_p), lead(ret_s)]
    for rp, rs in zip(rows_p, rows_s):
        out += [lead(rp), lead(rs)]
    return tuple(out)
```

```python
import functools

import numpy as np
import jax
import jax.numpy as jnp
from jax import lax
from jax.experimental import pallas as pl
from jax.experimental.pallas import tpu as pltpu

F32 = jnp.float32
BF16 = jnp.bfloat16

HEAD_DIM = 64
A_HEADS = 8
A_WIDTH = A_HEADS * HEAD_DIM
A_PATTERNS = ((128, 1), (512, 4), (2048, 16))
B_HEADS = 4
B_DK = 64
B_DV = 128
B_WIDTH = B_HEADS * B_DV
RET_CHUNK = 256
ROPE_BASE = 10000.0
GN_EPS = 1e-5
C_HEADS = 16
C_KV_HEADS = 2
C_WIDTH = C_HEADS * HEAD_DIM
C_KV_WIDTH = C_KV_HEADS * HEAD_DIM
CMP_LEN = 32
CMP_STRIDE = 16
CMP_HIDDEN = 256
SEL_BLOCK = 64
SEL_TOPK = 16
SEL_FORCE = 1e9
C_WINDOW = 512
PAGE_SIZE = 128
RMS_EPS = 1e-6
NEG_INF = -1e30

LANES = 128
ATTN_TQ = 128
PROJ_ROWS = 512
OUT_ROWS = 1024
VMEM_LIMIT = 56 * 1024 * 1024


def _params(*sem):
    return pltpu.CompilerParams(dimension_semantics=sem, vmem_limit_bytes=VMEM_LIMIT)


def _dot(a, b):
    return jnp.dot(a, b, preferred_element_type=F32)


def _dot_nt(a, b):
    return lax.dot_general(a, b, (((1,), (1,)), ((), ())), preferred_element_type=F32)


def _dot_tn(a, b):
    return lax.dot_general(a, b, (((0,), (0,)), ((), ())), preferred_element_type=F32)


def _idiv(x, n):
    assert n & (n - 1) == 0
    return jnp.right_shift(x, n.bit_length() - 1)


def _imod(x, n):
    assert n & (n - 1) == 0
    return jnp.bitwise_and(x, n - 1)


def _split3_dot(a, w):
    hi = a.astype(BF16)
    r1 = a - hi.astype(F32)
    mid = r1.astype(BF16)
    lo = (r1 - mid.astype(F32)).astype(BF16)
    return _dot(hi, w) + _dot(mid, w) + _dot(lo, w)


BLOCK_TAGGED = "block_tagged"


def _norm_proj_body(x_ref, g_ref, wt_ref, *o_refs, outs, tiles, tm):
    x = x_ref[...]
    y = x * lax.rsqrt(jnp.mean(x * x, axis=-1, keepdims=True) + RMS_EPS) * g_ref[...]
    yb = y.astype(BF16)
    tile = pl.program_id(0) % tiles
    done = {}
    for o_ref, (off, width, _, keep_from) in zip(o_refs, outs):
        w_rows = wt_ref[off:off + width, :]
        if keep_from is None or keep_from == BLOCK_TAGGED:
            if (off, width) not in done:
                done[(off, width)] = _dot_nt(yb, w_rows)
            res = done[(off, width)]
            if keep_from is None:
                o_ref[...] = res.astype(o_ref.dtype)
            else:
                lane = lax.broadcasted_iota(jnp.int32, res.shape, 1)
                pos = tile * tm + lax.broadcasted_iota(jnp.int32, res.shape, 0)
                tag = jnp.where(_idiv(pos, SEL_BLOCK) == _imod(lane, HEAD_DIM), 1.0, 0.0)
                low = lane < HEAD_DIM
                o_ref[:, 0:LANES] = jnp.where(low, res, tag).astype(o_ref.dtype)
                o_ref[:, LANES:2 * LANES] = jnp.where(low, pltpu.roll(res, HEAD_DIM, 1), tag).astype(o_ref.dtype)
        elif keep_from == 0:
            if (off, width, 0) not in done:
                done[(off, width, 0)] = _dot_nt(w_rows, yb)
            o_ref[0] = done[(off, width, 0)].astype(o_ref.dtype)
        else:
            @pl.when(tile >= keep_from // tm)
            def _():
                o_ref[0] = _dot_nt(w_rows, yb).astype(o_ref.dtype)


def _norm_proj(x2d, gain, w_t, outs, tm, seq_len):
    m, d = x2d.shape
    n = w_t.shape[0]
    tiles = seq_len // tm
    assert seq_len % tm == 0 and m % seq_len == 0
    out_specs, out_shape = [], []
    for _, width, dt, keep_from in outs:
        if keep_from is None or keep_from == BLOCK_TAGGED:
            if keep_from == BLOCK_TAGGED:
                assert width == C_KV_WIDTH and seq_len <= SEL_BLOCK * HEAD_DIM
                width = 2 * LANES
            out_specs.append(pl.BlockSpec((tm, width), lambda i: (i, 0)))
            out_shape.append(jax.ShapeDtypeStruct((m, width), dt))
        else:
            assert keep_from % tm == 0
            first = keep_from // tm
            out_specs.append(pl.BlockSpec((1, width, tm),
                                          lambda i, first=first: (i // tiles, 0, jnp.maximum(i % tiles - first, 0))))
            out_shape.append(jax.ShapeDtypeStruct((m // seq_len, width, seq_len - keep_from), dt))
    return pl.pallas_call(
        functools.partial(_norm_proj_body, outs=outs, tiles=tiles, tm=tm),
        grid=(m // tm,),
        in_specs=[pl.BlockSpec((tm, d), lambda i: (i, 0)),
                  pl.BlockSpec((1, d), lambda i: (0, 0)),
                  pl.BlockSpec((n, d), lambda i: (0, 0))],
        out_specs=out_specs,
        out_shape=out_shape,
        compiler_params=_params("arbitrary"),
        name="norm_proj",
    )(x2d, gain.reshape(1, d), w_t)


DIL_GROUP = 4


DIL_SPLIT = 4


def _dilated_body(q_ref, k_ref, v_ref, o_ref, o_sc, lse_sc, *split_refs, patterns):
    tq = ATTN_TQ
    blk_len = q_ref.shape[1]
    base = pl.program_id(2) * blk_len
    head_of_lane = _idiv(lax.broadcasted_iota(jnp.int32, (tq, LANES), 1), HEAD_DIM)
    own_lanes = [head_of_lane == h for h in range(LANES // HEAD_DIM)]

    q_split = k_split = v_split = None
    if split_refs:
        q_split, k_split, v_split = split_refs
        for r in range(DIL_SPLIT):
            q_split[r] = q_ref[0, pl.ds(r, blk_len // DIL_SPLIT, stride=DIL_SPLIT), :]

        @pl.when(pl.program_id(2) == 0)
        def _():
            n_rows = k_ref.shape[1] // DIL_SPLIT
            for r in range(DIL_SPLIT):
                k_split[r] = k_ref[0, pl.ds(r, n_rows, stride=DIL_SPLIT), :]
                v_split[r] = v_ref[0, pl.ds(r, n_rows, stride=DIL_SPLIT), :]

    def rows(ref, split, start, size, dil):
        if dil == 1:
            return ref[0, pl.ds(start, size), :]
        if dil <= DIL_SPLIT:
            return ref[0, pl.ds(start, size, stride=dil), :]
        return split[_imod(start, DIL_SPLIT), pl.ds(_idiv(start, DIL_SPLIT), size, stride=dil // DIL_SPLIT), :]

    for p_idx, (window, dil) in enumerate(patterns):
        lookback = window // dil
        n_prev = -(-lookback // tq)
        span = (n_prev + 1) * tq
        n_u = blk_len // (tq * dil)
        n_sub_blocks = dil * n_u
        assert blk_len % (tq * dil) == 0 and n_sub_blocks % DIL_GROUP == 0
        offs = lax.broadcasted_iota(jnp.int32, (tq, span), 0) - lax.broadcasted_iota(jnp.int32, (tq, span), 1)
        biases = []
        for lead in range(n_prev + 1):
            dist = offs + lead * tq
            biases.append(jnp.where(dist >= 0, jnp.where(dist <= lookback, 0.0, NEG_INF), NEG_INF))

        def trip(i, carry, dil=dil, n_prev=n_prev, span=span, p_idx=p_idx, biases=biases):
            work = []
            for j in range(DIL_GROUP):
                idx = i * DIL_GROUP + j
                res = _imod(idx, dil)
                u = _idiv(idx, dil)
                s0 = base // dil + u * tq
                ks = jnp.maximum(s0 - n_prev * tq, 0)
                q_row = res + dil * (u * tq)
                k_row = res + dil * ks
                q2 = rows(q_ref, q_split, q_row, tq, dil) * (HEAD_DIM ** -0.5 * LOG2E)
                k2 = rows(k_ref, k_split, k_row, span, dil).astype(BF16)
                v2 = rows(v_ref, v_split, k_row, span, dil).astype(BF16)
                bias = biases[n_prev]
                for lead in range(n_prev):
                    bias = jnp.where(s0 - ks == lead * tq, biases[lead], bias)
                work.append((q_row, q2, k2, v2, bias))
            scores = [[_dot_nt(jnp.where(own, q2, 0.0).astype(BF16), k2) + bias for own in own_lanes]
                      for _, q2, k2, _, bias in work]
            maxes = [[jnp.max(s, axis=-1, keepdims=True) for s in ss] for ss in scores]
            probs = [[jnp.exp2(s - m) for s, m in zip(ss, mm)] for ss, mm in zip(scores, maxes)]
            sums = [[jnp.sum(p, axis=-1, keepdims=True) for p in pp] for pp in probs]
            outs = [[_dot(p.astype(BF16), w[3]) for p in pp] for pp, w in zip(probs, work)]
            for w, oo, mm, ll in zip(work, outs, maxes, sums):
                o2 = oo[0] * (1.0 / ll[0])
                lse2 = jnp.broadcast_to(mm[0] * LN2 + jnp.log(ll[0]), (tq, LANES))
                for own, o, m, l in list(zip(own_lanes, oo, mm, ll))[1:]:
                    o2 = jnp.where(own, o * (1.0 / l), o2)
                    lse2 = jnp.where(own, m * LN2 + jnp.log(l), lse2)
                idx = pl.ds(w[0], tq) if dil == 1 else pl.ds(w[0], tq, stride=dil)
                o_sc[p_idx, idx, :] = o2
                lse_sc[p_idx, idx, :] = lse2
            return carry

        lax.fori_loop(0, n_sub_blocks // DIL_GROUP, trip, 0)

    lses = [lse_sc[p] for p in range(len(patterns))]
    top = lses[0]
    for l in lses[1:]:
        top = jnp.maximum(top, l)
    weights = [jnp.exp(l - top) for l in lses]
    num = weights[0] * o_sc[0]
    den = weights[0]
    for p in range(1, len(patterns)):
        num = num + weights[p] * o_sc[p]
        den = den + weights[p]
    o_ref[0] = (num / den).astype(o_ref.dtype)


def _dilated_attention(q, k, v, patterns):
    b, t, width = q.shape
    blk_len = min(t, max(ATTN_TQ * dil for _, dil in patterns))
    assert t % blk_len == 0
    for window, dil in patterns:
        assert t // dil >= (-(-(window // dil) // ATTN_TQ) + 1) * ATTN_TQ
    scratch = [pltpu.VMEM((len(patterns), blk_len, LANES), F32), pltpu.VMEM((len(patterns), blk_len, LANES), F32)]
    if any(dil > DIL_SPLIT for _, dil in patterns):
        assert all(dil % DIL_SPLIT == 0 for _, dil in patterns if dil > DIL_SPLIT)
        scratch += [pltpu.VMEM((DIL_SPLIT, blk_len // DIL_SPLIT, LANES), F32),
                    pltpu.VMEM((DIL_SPLIT, t // DIL_SPLIT, LANES), F32),
                    pltpu.VMEM((DIL_SPLIT, t // DIL_SPLIT, LANES), F32)]
    return pl.pallas_call(
        functools.partial(_dilated_body, patterns=patterns),
        grid=(b, width // LANES, t // blk_len),
        in_specs=[pl.BlockSpec((1, blk_len, LANES), lambda bi, hp, i: (bi, i, hp)),
                  pl.BlockSpec((1, t, LANES), lambda bi, hp, i: (bi, 0, hp)),
                  pl.BlockSpec((1, t, LANES), lambda bi, hp, i: (bi, 0, hp))],
        out_specs=pl.BlockSpec((1, blk_len, LANES), lambda bi, hp, i: (bi, i, hp)),
        out_shape=jax.ShapeDtypeStruct((b, t, width), BF16),
        scratch_shapes=scratch,
        compiler_params=_params("parallel", "parallel", "arbitrary"),
        name="dilated_attention",
    )(q, k, v)


def _retention_body(q_ref, k_ref, v_ref, cos_ref, sin_ref, qd_ref, kd_ref, intra_ref, cd_ref, s0_ref,
                    o_ref, sfin_ref, state):
    c = pl.program_id(1)

    @pl.when(c == 0)
    def _():
        state[...] = s0_ref[0]

    cos = cos_ref[...]
    sin = sin_ref[...]
    width = cos.shape[-1]
    half = B_DK // 2
    lane = _imod(lax.broadcasted_iota(jnp.int32, cos.shape, 1), B_DK)

    def rotary(x):
        fwd = pltpu.roll(x, half, 1)
        bwd = pltpu.roll(x, width - half, 1)
        return x * cos + jnp.where(lane < half, -bwd, fwd) * sin

    q = rotary(q_ref[0])
    k = rotary(k_ref[0]) * (B_DK ** -0.5)
    q_in = q.astype(BF16)
    k_in = k.astype(BF16)
    q_st = (q * qd_ref[...]).astype(BF16)
    k_st = (k * kd_ref[...]).astype(BF16)
    v = v_ref[0].astype(BF16)
    kcs = [slice(h * B_DK, (h + 1) * B_DK) for h in range(B_HEADS)]
    vcs = [slice(h * B_DV, (h + 1) * B_DV) for h in range(B_HEADS)]
    states = [state[h] for h in range(B_HEADS)]
    scores = [_dot_nt(q_in[:, kc], k_in[:, kc]) * intra_ref[h] for h, kc in enumerate(kcs)]
    carried = [_dot(q_st[:, kc], st.astype(BF16)) for kc, st in zip(kcs, states)]
    updates = [_dot_tn(k_st[:, kc], v[:, vc]) for kc, vc in zip(kcs, vcs)]
    inner = [_dot(s.astype(BF16), v[:, vc]) for s, vc in zip(scores, vcs)]
    for h in range(B_HEADS):
        o_ref[0, :, vcs[h]] = (inner[h] + carried[h]).astype(o_ref.dtype)
        state[h] = states[h] * cd_ref[h] + updates[h]

    @pl.when(c == pl.num_programs(1) - 1)
    def _():
        sfin_ref[0] = state[...]


def _retention(q, k, v, s0, pos, *, chunk, c_real):
    n, t, _ = q.shape
    half = B_DK // 2
    freqs = ROPE_BASE ** (-jnp.arange(half, dtype=F32) / half)
    ang = pos.astype(F32)[:, None] * freqs[None, :]
    cos = jnp.tile(jnp.cos(ang), (1, 2 * B_HEADS))
    sin = jnp.tile(jnp.sin(ang), (1, 2 * B_HEADS))
    log_g = jnp.log1p(-jnp.exp2(-5.0 - jnp.arange(B_HEADS, dtype=F32)))
    i = jnp.arange(chunk, dtype=F32)
    diff = i[:, None] - i[None, :]
    intra = jnp.where(diff >= 0, jnp.exp(log_g[:, None, None] * jnp.maximum(diff, 0.0)), 0.0)
    q_decay = jnp.repeat(jnp.exp(log_g[None, :] * (i[:, None] + 1.0)), B_DK, axis=1)
    k_decay = jnp.repeat(jnp.exp(log_g[None, :] * (c_real - 1.0 - i[:, None])), B_DK, axis=1)
    chunk_decay = jnp.broadcast_to(jnp.exp(log_g * c_real)[:, None, None], (B_HEADS, 1, B_DV))
    wqk = B_HEADS * B_DK
    return pl.pallas_call(
        _retention_body,
        grid=(n, t // chunk),
        in_specs=[pl.BlockSpec((1, chunk, wqk), lambda b, c: (b, c, 0)),
                  pl.BlockSpec((1, chunk, wqk), lambda b, c: (b, c, 0)),
                  pl.BlockSpec((1, chunk, B_WIDTH), lambda b, c: (b, c, 0)),
                  pl.BlockSpec((chunk, wqk), lambda b, c: (c, 0)),
                  pl.BlockSpec((chunk, wqk), lambda b, c: (c, 0)),
                  pl.BlockSpec((chunk, wqk), lambda b, c: (0, 0)),
                  pl.BlockSpec((chunk, wqk), lambda b, c: (0, 0)),
                  pl.BlockSpec((B_HEADS, chunk, chunk), lambda b, c: (0, 0, 0)),
                  pl.BlockSpec((B_HEADS, 1, B_DV), lambda b, c: (0, 0, 0)),
                  pl.BlockSpec((1, B_HEADS, B_DK, B_DV), lambda b, c: (b, 0, 0, 0))],
        out_specs=[pl.BlockSpec((1, chunk, B_WIDTH), lambda b, c: (b, c, 0)),
                   pl.BlockSpec((1, B_HEADS, B_DK, B_DV), lambda b, c: (b, 0, 0, 0))],
        out_shape=[jax.ShapeDtypeStruct((n, t, B_WIDTH), BF16),
                   jax.ShapeDtypeStruct((n, B_HEADS, B_DK, B_DV), F32)],
        scratch_shapes=[pltpu.VMEM((B_HEADS, B_DK, B_DV), F32)],
        compiler_params=_params("parallel", "arbitrary"),
        name="retention",
    )(q, k, v, cos, sin, q_decay, k_decay, intra, chunk_decay, s0)


def _silu(z):
    return z * jax.nn.sigmoid(z)


def _even_out_body(x_ref, oa_ref, za_ref, ob_ref, zb_ref, w_ref, y_ref):
    ga = (oa_ref[...].astype(F32) * _silu(za_ref[...].astype(F32))).astype(BF16)
    acc = x_ref[...] + _dot(ga, w_ref[0:A_WIDTH, :])
    ob = ob_ref[...].astype(F32)
    zb = zb_ref[...].astype(F32)
    for h in range(B_HEADS):
        cols = slice(h * B_DV, (h + 1) * B_DV)
        seg = ob[:, cols]
        mu = jnp.mean(seg, axis=-1, keepdims=True)
        cen = seg - mu
        var = jnp.mean(cen * cen, axis=-1, keepdims=True)
        gb = (cen * lax.rsqrt(var + GN_EPS) * _silu(zb[:, cols])).astype(BF16)
        acc = acc + _dot(gb, w_ref[A_WIDTH + h * B_DV:A_WIDTH + (h + 1) * B_DV, :])
    y_ref[...] = acc


def _even_out(x2d, oa, za, ob, zb, w_out, tm):
    m, d = x2d.shape
    row = lambda width: pl.BlockSpec((tm, width), lambda i: (i, 0))
    return pl.pallas_call(
        _even_out_body,
        grid=(m // tm,),
        in_specs=[row(d), row(A_WIDTH), row(A_WIDTH), row(B_WIDTH), row(B_WIDTH),
                  pl.BlockSpec(w_out.shape, lambda i: (0, 0))],
        out_specs=row(d),
        out_shape=jax.ShapeDtypeStruct((m, d), F32),
        compiler_params=_params("parallel"),
        name="even_out",
    )(x2d, oa, za, ob, zb, w_out)


def _compress_mlp(pieces, p2_ref, w1_ref, w2_ref):
    n_piece = pieces[0].shape[0]
    hidden = CMP_HIDDEN
    pc = _dot(p2_ref[...], w1_ref[...])
    pos = (pc[0:1, :hidden] + pc[2:3, :hidden]) + (pc[1:2, hidden:] + pc[3:4, hidden:])
    acts = [_dot(x.astype(BF16), w1_ref[...]) for x in pieces]
    hid = [_silu(a[:, :hidden] + pltpu.roll(a[:, hidden:], n_piece - 1, 0) + pos) for a in acts]
    return _dot(jnp.concatenate(hid, axis=-1).astype(BF16), w2_ref[...])


def _split_heads(even, odd, low_half):
    swapped = pltpu.roll(jnp.where(low_half, odd, even), HEAD_DIM, 1)
    return jnp.where(low_half, even, swapped), jnp.where(low_half, swapped, odd)


def _compress_body(x_ref, p2_ref, w1_ref, w2_ref, o_ref):
    x = x_ref[0].astype(F32)
    low_half = lax.broadcasted_iota(jnp.int32, (x.shape[0], LANES), 1) < HEAD_DIM
    slabs = [_split_heads(x[:, l * LANES:(l + 1) * LANES], x[:, (l + 1) * LANES:(l + 2) * LANES], low_half)
             for l in range(0, CMP_STRIDE, 2)]
    pieces = [jnp.concatenate([s[g] for s in slabs], axis=-1) for g in range(C_KV_HEADS)]
    o_ref[0] = _compress_mlp(pieces, p2_ref, w1_ref, w2_ref)


def _compress_weights(pos_emb, w1, w2):
    per = CMP_LEN // CMP_STRIDE
    w1g = jnp.transpose(w1.reshape(per, CMP_STRIDE, HEAD_DIM, CMP_HIDDEN), (1, 2, 0, 3))
    w1g = w1g.reshape(CMP_STRIDE * HEAD_DIM, per * CMP_HIDDEN)
    eye = jnp.eye(C_KV_HEADS, dtype=w2.dtype)
    w2bd = jnp.einsum('fd,gh->gfhd', w2, eye).reshape(C_KV_HEADS * CMP_HIDDEN, C_KV_WIDTH)
    halves = pos_emb.reshape(per, CMP_STRIDE * HEAD_DIM)
    hi = halves.astype(BF16)
    lo = (halves - hi.astype(F32)).astype(BF16)
    p2 = jnp.concatenate([hi, lo, jnp.zeros((16 - 2 * per, halves.shape[1]), BF16)], axis=0)
    return p2, w1g.astype(BF16), w2bd.astype(BF16)


def _compress(pieces, cw):
    n, n_piece, width = pieces.shape
    p2, w1g, w2bd = cw
    const = lambda a: pl.BlockSpec(a.shape, lambda i: (0,) * a.ndim)
    return pl.pallas_call(
        _compress_body,
        grid=(n,),
        in_specs=[pl.BlockSpec((1, n_piece, width), lambda i: (i, 0, 0)), const(p2), const(w1g), const(w2bd)],
        out_specs=pl.BlockSpec((1, n_piece, C_KV_WIDTH), lambda i: (i, 0, 0)),
        out_shape=jax.ShapeDtypeStruct((n, n_piece, C_KV_WIDTH), F32),
        compiler_params=_params("parallel"),
        name="compress",
    )(pieces, p2, w1g, w2bd)


def _page_dma(pt_ref, pool_ref, buf_ref, sem_ref, n, slot, n_pages, start):
    for p in range(n_pages):
        page = pt_ref[n, p] if start else 0
        cp = pltpu.make_async_copy(pool_ref.at[page], buf_ref.at[slot, p], sem_ref.at[slot])
        if start:
            cp.start()
        else:
            cp.wait()


def _compress_paged_body(pt_ref, pool_ref, perm_ref, p2_ref, w1_ref, w2_ref, o_ref, buf, pieces, sem, *, n_pages):
    n = pl.program_id(0)
    slot = n % 2
    rows = PAGE_SIZE // CMP_STRIDE

    @pl.when(n == 0)
    def _():
        _page_dma(pt_ref, pool_ref, buf, sem, 0, 0, n_pages, True)

    @pl.when(n + 1 < pl.num_programs(0))
    def _():
        _page_dma(pt_ref, pool_ref, buf, sem, n + 1, 1 - slot, n_pages, True)

    _page_dma(pt_ref, pool_ref, buf, sem, n, slot, n_pages, False)
    perm = perm_ref[...]

    group = 32
    assert n_pages % group == 0 and CMP_STRIDE % 2 == 0
    low_half = lax.broadcasted_iota(jnp.int32, (rows, LANES), 1) < HEAD_DIM

    def unfold(i, carry):
        pairs = []
        for j in range(0, group, 2):
            two = buf[slot, pl.ds(i * group + j, 2)].reshape(2 * C_KV_WIDTH, PAGE_SIZE)
            pairs.append(_dot_nt(perm, two.astype(BF16)))
        for j in range(group):
            by_row = pairs[j // 2][:, (j % 2) * C_KV_WIDTH:(j % 2 + 1) * C_KV_WIDTH]
            r0 = pl.multiple_of((i * group + j) * rows, rows)
            for l in range(0, CMP_STRIDE, 2):
                even = by_row[l * rows:(l + 1) * rows, :]
                odd = by_row[(l + 1) * rows:(l + 2) * rows, :]
                lanes = slice(l * HEAD_DIM, (l + 2) * HEAD_DIM)
                pieces[0, pl.ds(r0, rows), lanes], pieces[1, pl.ds(r0, rows), lanes] = _split_heads(even, odd, low_half)
        return carry

    lax.fori_loop(0, n_pages // group, unfold, 0)
    o_ref[0] = _compress_mlp([pieces[g] for g in range(C_KV_HEADS)], p2_ref, w1_ref, w2_ref)


def _compress_paged(pool, page_table, cw):
    n, n_pages = page_table.shape
    rows = PAGE_SIZE // CMP_STRIDE
    n_piece = n_pages * rows
    p2g, w1g, w2bd = cw
    perm = np.zeros((PAGE_SIZE, PAGE_SIZE), np.float32)
    for l in range(CMP_STRIDE):
        for piece in range(rows):
            perm[l * rows + piece, CMP_STRIDE * piece + l] = 1.0
    perm = jnp.asarray(perm, BF16)
    pages = jnp.transpose(pool, (0, 2, 3, 1)).reshape(pool.shape[0], C_KV_WIDTH, PAGE_SIZE)
    const = lambda a: pl.BlockSpec(a.shape, lambda i, pt: (0,) * a.ndim)
    return pl.pallas_call(
        functools.partial(_compress_paged_body, n_pages=n_pages),
        grid_spec=pltpu.PrefetchScalarGridSpec(
            num_scalar_prefetch=1,
            grid=(n,),
            in_specs=[pl.BlockSpec(memory_space=pl.ANY), const(perm), const(p2g), const(w1g), const(w2bd)],
            out_specs=pl.BlockSpec((1, n_piece, C_KV_WIDTH), lambda i, pt: (i, 0, 0)),
            scratch_shapes=[pltpu.VMEM((2, n_pages, C_KV_WIDTH, PAGE_SIZE), F32),
                            pltpu.VMEM((C_KV_HEADS, n_piece, CMP_STRIDE * HEAD_DIM), F32),
                            pltpu.SemaphoreType.DMA((2,))]),
        out_shape=jax.ShapeDtypeStruct((n, n_piece, C_KV_WIDTH), F32),
        compiler_params=_params("arbitrary"),
        name="compress_paged",
    )(page_table, pages, perm, p2g, w1g, w2bd)


def _overlap_weights(n_cmp, n_cmp_pad, n_slc, n_slc_pad):
    i = np.arange(n_cmp_pad)[:, None]
    j = np.arange(n_slc_pad)[None, :]
    ov = (np.minimum(i * CMP_STRIDE + CMP_LEN, (j + 1) * SEL_BLOCK) - np.maximum(i * CMP_STRIDE, j * SEL_BLOCK))
    w = np.where((i < n_cmp) & (j < n_slc), np.clip(ov, 0, None) / CMP_LEN, 0.0)
    return jnp.asarray(w, dtype=BF16)


def _select_blocks(p_slc, qpos, n_slc):
    shape = p_slc.shape
    lane = lax.broadcasted_iota(jnp.int32, shape, 1)
    cur = _idiv(qpos, SEL_BLOCK)
    forced = (lane == 0) | (lane == cur) | (lane == cur - 1)
    causal = lane * SEL_BLOCK <= qpos
    score = jnp.where(forced, SEL_FORCE, jnp.where(causal, p_slc, -SEL_FORCE))
    score = jnp.where(lane < n_slc, score, -3.0 * SEL_FORCE)
    lane_f = lane.astype(F32)
    sel = jnp.zeros(shape, F32)
    for _ in range(min(SEL_TOPK, n_slc)):
        m = jnp.max(score, axis=-1, keepdims=True)
        first = jnp.min(jnp.where(score == m, lane_f, 1e9), axis=-1, keepdims=True)
        hit = lane_f == first
        sel = jnp.where(hit, jnp.where(m > -0.5 * SEL_FORCE, 1.0, 0.0), sel)
        score = jnp.where(hit, -4.0 * SEL_FORCE, score)
    return sel


def _select_blocks_t(p_slc, qpos, n_slc):
    n_rows = -(-n_slc // 8) * 8
    shape = (n_rows, p_slc.shape[1])
    blk = lax.broadcasted_iota(jnp.int32, shape, 0)
    cur = _idiv(qpos, SEL_BLOCK)
    forced = (blk == 0) | (blk == cur) | (blk == cur - 1)
    causal = blk * SEL_BLOCK <= qpos
    score = jnp.where(forced, SEL_FORCE, jnp.where(causal, p_slc[:n_rows], -SEL_FORCE))
    score = jnp.where(blk < n_slc, score, -3.0 * SEL_FORCE)
    tiles = [score[8 * v:8 * v + 8] for v in range(n_rows // 8)]
    sub = lax.broadcasted_iota(jnp.int32, (8, shape[1]), 0)
    ranks = [jnp.zeros((8, shape[1]), F32) for _ in tiles]
    for i in range(n_slc):
        row = tiles[i // 8][i % 8:i % 8 + 1]
        for v, tile in enumerate(tiles):
            ge = jnp.where(row >= tile, 1.0, 0.0)
            gt = jnp.where(row > tile, 1.0, 0.0)
            if 8 * v > i:
                ranks[v] = ranks[v] + ge
            elif 8 * v + 7 <= i:
                ranks[v] = ranks[v] + gt
            else:
                ranks[v] = ranks[v] + jnp.where(sub > i - 8 * v, ge, gt)
    rank = jnp.concatenate(ranks, axis=0)
    return jnp.where(rank < min(SEL_TOPK, n_slc), jnp.where(score > -0.5 * SEL_FORCE, 1.0, 0.0), 0.0)


NSA_TILE = 2 * LANES
ONES_ROWS = 16
LOG2E = 1.4426950408889634
LN2 = 0.6931471805599453


def _nsa_prompt_body(q_ref, kc_ref, vct_ref, ks_ref, vst_ref, kw_ref, vwt_ref, g_ref, wovt_ref, o_ref, m_sc, acc_sc,
                     *, n_cmp, n_slc):
    tq = tk = NSA_TILE
    rep = C_HEADS // C_KV_HEADS
    blk = pl.program_id(1)
    q0 = blk * tq
    n_cmp_pad = kc_ref.shape[1]
    sig_t = jax.nn.sigmoid(g_ref[0])
    q_t = [jnp.transpose(q_ref[0, :, c * LANES:(c + 1) * LANES].astype(F32))
           for c in range(C_WIDTH // LANES)]
    qpos = q0 + lax.broadcasted_iota(jnp.int32, (1, tq), 1)
    cmp_i = lax.broadcasted_iota(jnp.int32, (n_cmp_pad, tq), 0)
    cmp_ok = (cmp_i * CMP_STRIDE + (CMP_LEN - 1) <= qpos) & (cmp_i < n_cmp)
    cmp_bias = jnp.where(cmp_ok, 0.0, NEG_INF)
    any_block = jnp.where(qpos >= CMP_LEN - 1, 1.0, 0.0)
    ahead = lax.broadcasted_iota(jnp.int32, (tk, tq), 0) - lax.broadcasted_iota(jnp.int32, (tk, tq), 1)
    diag_bias = jnp.where(ahead <= 0, 0.0, NEG_INF)
    far_bias = jnp.where(ahead >= 0, 0.0, NEG_INF)
    ones_rows = jnp.where(lax.broadcasted_iota(jnp.int32, (ONES_ROWS, tk), 0) == 0, 1.0, 0.0).astype(BF16)

    def flash_update(scores, v_t):
        v_ext = jnp.concatenate([v_t, ones_rows], axis=0)
        for r, s in enumerate(scores):
            m_old = m_sc[r:r + 1, :]
            m_new = jnp.maximum(m_old, jnp.max(s, axis=0, keepdims=True))
            m_sc[r:r + 1, :] = m_new
            pv = _dot(v_ext, jnp.exp2(s - m_new).astype(BF16))
            acc_sc[r] = jnp.exp2(m_old - m_new) * acc_sc[r] + pv

    def flash_reset():
        m_sc[...] = jnp.full(m_sc.shape, NEG_INF, F32)
        acc_sc[...] = jnp.zeros(acc_sc.shape, F32)

    def flash_result(r):
        acc = acc_sc[r]
        return acc[:HEAD_DIM] * (1.0 / acc[HEAD_DIM:HEAD_DIM + 1])

    outs = []
    for g in range(C_KV_HEADS):
        gc = slice(g * HEAD_DIM, (g + 1) * HEAD_DIM)
        heads = [g * rep + r for r in range(rep)]
        q_heads = [q_t[h // 2][(h % 2) * HEAD_DIM:(h % 2 + 1) * HEAD_DIM] for h in heads]
        q_nat = [(qh * (HEAD_DIM ** -0.5)).astype(BF16) for qh in q_heads]
        q_log2 = [(qh * (HEAD_DIM ** -0.5 * LOG2E)).astype(BF16) for qh in q_heads]
        kcb = kc_ref[0, :, gc].astype(BF16)
        vct = vct_ref[0, gc, :].astype(BF16)
        scores = [_dot(kcb, q_nat[r]) + cmp_bias for r in range(rep)]
        probs = []
        imp = None
        for s in scores:
            p = jnp.exp(s - jnp.max(s, axis=0, keepdims=True))
            p = p * (any_block / jnp.sum(p, axis=0, keepdims=True))
            imp = p if imp is None else imp + p
            probs.append(p.astype(BF16))
        o_cmp = [_dot(vct, p) for p in probs]
        hi = imp.astype(BF16)
        r1 = imp - hi.astype(F32)
        mid = r1.astype(BF16)
        lo = (r1 - mid.astype(F32)).astype(BF16)
        p_slc = _dot(wovt_ref[...], hi) + _dot(wovt_ref[...], mid) + _dot(wovt_ref[...], lo)
        sel_bias = jnp.where(_select_blocks_t(p_slc, qpos, n_slc) > 0.5, 0.0, NEG_INF).astype(BF16)
        q_sel = [jnp.concatenate([qh, sel_bias], axis=0) for qh in q_log2]
        flash_reset()

        def sel_scores(k0):
            k_tile = ks_ref[0, pl.ds(pl.multiple_of(k0, tk), tk), g * LANES:(g + 1) * LANES]
            return [_dot(k_tile, qs) for qs in q_sel]

        def sel_values(k0):
            return vst_ref[0, gc, pl.ds(pl.multiple_of(k0, tk), tk)]

        def past_pair(i, carry):
            k0 = i * (2 * tk)
            first, second = sel_scores(k0), sel_scores(k0 + tk)
            flash_update(first, sel_values(k0))
            flash_update(second, sel_values(k0 + tk))
            return carry

        lax.fori_loop(0, blk // 2, past_pair, 0)
        k0 = (blk // 2) * (2 * tk)

        @pl.when(k0 == q0)
        def _():
            flash_update([s + diag_bias for s in sel_scores(q0)], sel_values(q0))

        @pl.when(k0 != q0)
        def _():
            first, second = sel_scores(k0), [s + diag_bias for s in sel_scores(q0)]
            flash_update(first, sel_values(k0))
            flash_update(second, sel_values(q0))

        o_sel = [flash_result(r) for r in range(rep)]
        flash_reset()
        n_back = C_WINDOW // tk
        win = []
        for back in range(n_back + 1):
            k0 = pl.multiple_of(jnp.maximum(q0 - back * tk, 0), tk)
            k_tile = kw_ref[0, pl.ds(k0, tk), gc]
            absent = jnp.where(blk >= back, 0.0, NEG_INF)
            edge = diag_bias if back == 0 else far_bias if back == n_back else None
            bias = absent if edge is None else edge + absent
            win.append(([_dot(k_tile, qh) + bias for qh in q_log2], vwt_ref[0, gc, pl.ds(k0, tk)]))
        for scores, values in win:
            flash_update(scores, values)
        for r, h in enumerate(heads):
            outs.append(sig_t[3 * h:3 * h + 1] * o_cmp[r] + sig_t[3 * h + 1:3 * h + 2] * o_sel[r]
                        + sig_t[3 * h + 2:3 * h + 3] * flash_result(r))
    for c in range(C_WIDTH // LANES):
        pair = jnp.transpose(jnp.concatenate(outs[2 * c:2 * c + 2], axis=0))
        o_ref[0, :, c * LANES:(c + 1) * LANES] = pair.astype(o_ref.dtype)


def _nsa_prompt(q, kcb, vcb, ks_ext, vs_t, kw, vw_t, gates_t, n_cmp):
    b, t, wq = q.shape
    n_cmp_pad = kcb.shape[1]
    n_slc = -(-t // SEL_BLOCK)
    tile = NSA_TILE
    assert t % (2 * tile) == 0 and n_slc == HEAD_DIM and C_WINDOW % tile == 0
    wov_t = _overlap_weights(n_cmp, n_cmp_pad, n_slc, LANES).T
    vcb_t = jnp.swapaxes(vcb, 1, 2)
    rep = C_HEADS // C_KV_HEADS
    whole = lambda a: pl.BlockSpec((1,) + a.shape[1:], lambda bi, i: (bi, 0, 0))
    return pl.pallas_call(
        functools.partial(_nsa_prompt_body, n_cmp=n_cmp, n_slc=n_slc),
        grid=(b, t // tile),
        in_specs=[pl.BlockSpec((1, tile, wq), lambda bi, i: (bi, i, 0)),
                  whole(kcb), whole(vcb_t), whole(ks_ext), whole(vs_t), whole(kw), whole(vw_t),
                  pl.BlockSpec((1, gates_t.shape[1], tile), lambda bi, i: (bi, 0, i)),
                  pl.BlockSpec(wov_t.shape, lambda bi, i: (0, 0))],
        out_specs=pl.BlockSpec((1, tile, wq), lambda bi, i: (bi, i, 0)),
        out_shape=jax.ShapeDtypeStruct((b, t, wq), BF16),
        scratch_shapes=[pltpu.VMEM((rep, tile), F32), pltpu.VMEM((rep, HEAD_DIM + ONES_ROWS, tile), F32)],
        compiler_params=_params("parallel", "arbitrary"),
        name="nsa_prompt",
    )(q, kcb, vcb_t, ks_ext, vs_t, kw, vw_t, gates_t, wov_t)


def _block_diag(q, blk_of_row, n_blk):
    return jnp.concatenate([jnp.where(blk_of_row == b, q, 0.0) for b in range(n_blk)], axis=-1)


def _diag_blocks(r, blk_of_row, n_blk):
    out = jnp.where(blk_of_row == 0, r[:, 0:HEAD_DIM], 0.0)
    for b in range(1, n_blk):
        out = out + jnp.where(blk_of_row == b, r[:, b * HEAD_DIM:(b + 1) * HEAD_DIM], 0.0)
    return out


def _row_ids(n_rows, n_heads, n_blk):
    row = lax.broadcasted_iota(jnp.int32, (n_rows, 1), 0)
    step = _idiv(row, n_heads)
    blk = _idiv(_imod(row, n_heads), n_heads // n_blk)
    return step, blk


def _pattern_weight(dist, patterns):
    w = jnp.zeros(dist.shape, F32)
    for window, dil in patterns:
        w = w + jnp.where(_imod(dist, dil) == 0, jnp.where(dist <= window, 1.0, 0.0), 0.0)
    return jnp.where(dist >= 0, w, 0.0)


def _decode_body(*refs, n_heads, n_blk, patterns, gate_col):
    q_ref, kc_ref, vc_ref, kn_ref, vn_ref = refs[:5]
    g_ref = refs[5] if gate_col is not None else None
    o_ref = refs[-1]
    n_rows = q_ref.shape[1]
    n_cache = kc_ref.shape[2]
    n_new = kn_ref.shape[1]
    step, blk = _row_ids(n_rows, n_heads, n_blk)
    qbd = _block_diag(q_ref[0] * (HEAD_DIM ** -0.5), blk, n_blk)
    s_c = _dot(qbd.astype(BF16), kc_ref[0].astype(BF16))
    w_c = _pattern_weight(n_cache + step - lax.broadcasted_iota(jnp.int32, (n_rows, n_cache), 1), patterns)
    s_c = jnp.where(w_c > 0, s_c, NEG_INF)
    kn = kn_ref[0]
    vn = vn_ref[0]
    s_n, w_n = [], []
    for c in range(n_new):
        s = jnp.sum(qbd * kn[c:c + 1, :], axis=-1, keepdims=True)
        w = _pattern_weight(step - c, patterns)
        s_n.append(jnp.where(w > 0, s, NEG_INF))
        w_n.append(w)
    m = jnp.max(s_c, axis=-1, keepdims=True)
    for s in s_n:
        m = jnp.maximum(m, s)
    p_c = w_c * jnp.exp(s_c - m)
    l = jnp.sum(p_c, axis=-1, keepdims=True)
    r = _dot_nt(p_c.astype(BF16), vc_ref[0].astype(BF16))
    for c in range(n_new):
        p = w_n[c] * jnp.exp(s_n[c] - m)
        l = l + p
        r = r + p * vn[c:c + 1, :]
    o = _diag_blocks(r, blk, n_blk) / l
    if g_ref is not None:
        o = o * jax.nn.sigmoid(g_ref[0])[:, gate_col:gate_col + 1]
    o_ref[0] = o


def _decode_attention(q, kc, vc, kn, vn, *, n_heads, n_blk, patterns, gate=None, gate_col=None):
    n, n_rows, _ = q.shape
    args = [q, kc, vc, kn, vn]
    if gate is not None:
        args.append(gate)
    whole = lambda a: pl.BlockSpec((1,) + a.shape[1:], lambda i: (i, 0, 0))
    return pl.pallas_call(
        functools.partial(_decode_body, n_heads=n_heads, n_blk=n_blk, patterns=patterns,
                          gate_col=gate_col if gate is not None else None),
        grid=(n,),
        in_specs=[whole(a) for a in args],
        out_specs=pl.BlockSpec((1, n_rows, HEAD_DIM), lambda i: (i, 0, 0)),
        out_shape=jax.ShapeDtypeStruct((n, n_rows, HEAD_DIM), F32),
        compiler_params=_params("parallel"),
        name="decode_attention",
    )(*args)


def _nsa_decode_body(pt_ref, q_ref, kc_ref, vc_ref, ksp_ref, vsp_ref, kn_ref, vn_ref, g_ref, wov_ref, o_ref,
                     kbuf, vbuf, expand, ksem, vsem, *, n_pages, n_cmp, n_slc):
    n = pl.program_id(0)
    slot = n % 2
    n_rows = q_ref.shape[1]
    n_new = kn_ref.shape[1]
    n_past = kbuf.shape[2]
    n_slc_pad = wov_ref.shape[1]
    rep = C_HEADS // C_KV_HEADS

    def gather(seq, dst_slot, start):
        for pool_ref, buf, sem in ((ksp_ref, kbuf, ksem), (vsp_ref, vbuf, vsem)):
            for p in range(n_pages):
                page = pt_ref[seq, p] if start else 0
                cp = pltpu.make_async_copy(pool_ref.at[page], buf.at[dst_slot, :, pl.ds(p * PAGE_SIZE, PAGE_SIZE)],
                                           sem.at[dst_slot])
                if start:
                    cp.start()
                else:
                    cp.wait()

    @pl.when(n == 0)
    def _():
        gather(0, 0, True)
        step_cols = 1024
        for c0 in range(0, n_past, step_cols):
            shape = (n_slc_pad, min(step_cols, n_past - c0))
            blk_of_key = _idiv(c0 + lax.broadcasted_iota(jnp.int32, shape, 1), SEL_BLOCK)
            hit = blk_of_key == lax.broadcasted_iota(jnp.int32, shape, 0)
            expand[:, c0:c0 + shape[1]] = jnp.where(hit, 1.0, 0.0).astype(BF16)

    @pl.when(n + 1 < pl.num_programs(0))
    def _():
        gather(n + 1, 1 - slot, True)

    step, blk = _row_ids(n_rows, C_HEADS, C_KV_HEADS)
    qpos = n_past + step
    qbd = _block_diag(q_ref[0] * (HEAD_DIM ** -0.5), blk, C_KV_HEADS)
    qbd_in = qbd.astype(BF16)
    n_cmp_pad = kc_ref.shape[1]
    cmp_i = lax.broadcasted_iota(jnp.int32, (n_rows, n_cmp_pad), 1)
    cmp_ok = (cmp_i * CMP_STRIDE + (CMP_LEN - 1) <= qpos) & (cmp_i < n_cmp)
    s = jnp.where(cmp_ok, _dot_nt(qbd_in, kc_ref[0].astype(BF16)), NEG_INF)
    m = jnp.max(s, axis=-1, keepdims=True)
    p = jnp.where(cmp_ok, jnp.exp(s - m), 0.0)
    l = jnp.sum(p, axis=-1, keepdims=True)
    p = p / jnp.where(l > 0, l, 1.0)
    o_cmp = _diag_blocks(_dot(p.astype(BF16), vc_ref[0].astype(BF16)), blk, C_KV_HEADS)
    grp = (n_rows // rep, rep, n_cmp_pad)
    imp = jnp.broadcast_to(jnp.sum(p.reshape(grp), axis=1, keepdims=True), grp).reshape(n_rows, n_cmp_pad)
    sel_rows = _select_blocks(_split3_dot(imp, wov_ref[...]), qpos, n_slc)
    gather(n, slot, False)
    blk_bias = jnp.where(sel_rows > 0.5, 0.0, NEG_INF).astype(BF16)
    s_p = _dot(qbd_in, kbuf[slot].astype(BF16)) + _dot(blk_bias, expand[...])
    kn = kn_ref[0]
    vn = vn_ref[0]
    s_n = []
    for c in range(n_new):
        blk_c = (n_past + c) // SEL_BLOCK
        ok = (sel_rows[:, blk_c:blk_c + 1] > 0.5) & (n_past + c <= qpos)
        s_n.append(jnp.where(ok, jnp.sum(qbd * kn[c:c + 1, :], axis=-1, keepdims=True), NEG_INF))
    m = jnp.max(s_p, axis=-1, keepdims=True)
    for sc in s_n:
        m = jnp.maximum(m, sc)
    p_p = jnp.exp(s_p - m)
    l = jnp.sum(p_p, axis=-1, keepdims=True)
    r = _dot_nt(p_p.astype(BF16), vbuf[slot].astype(BF16))
    for c in range(n_new):
        pn = jnp.exp(s_n[c] - m)
        l = l + pn
        r = r + pn * vn[c:c + 1, :]
    o_sel = _diag_blocks(r, blk, C_KV_HEADS) / l
    sig = jax.nn.sigmoid(g_ref[0])
    o_ref[0] = sig[:, 0:1] * o_cmp + sig[:, 1:2] * o_sel


def _nsa_decode(q, kcb, vcb, pool_ks, pool_vs, page_table, kn, vn, gate, n_cmp):
    n, n_rows, _ = q.shape
    n_pages = page_table.shape[1]
    n_past = n_pages * PAGE_SIZE
    n_new = kn.shape[1]
    n_slc = -(-(n_past + n_new) // SEL_BLOCK)
    n_slc_pad = -(-n_slc // LANES) * LANES
    wov = _overlap_weights(n_cmp, kcb.shape[1], n_slc, n_slc_pad)
    whole = lambda a: pl.BlockSpec((1,) + a.shape[1:], lambda i, pt: (i, 0, 0))
    pools = [jnp.transpose(a, (0, 2, 3, 1)).reshape(a.shape[0], C_KV_WIDTH, PAGE_SIZE) for a in (pool_ks, pool_vs)]
    return pl.pallas_call(
        functools.partial(_nsa_decode_body, n_pages=n_pages, n_cmp=n_cmp, n_slc=n_slc),
        grid_spec=pltpu.PrefetchScalarGridSpec(
            num_scalar_prefetch=1,
            grid=(n,),
            in_specs=[whole(q), whole(kcb), whole(vcb), pl.BlockSpec(memory_space=pl.ANY),
                      pl.BlockSpec(memory_space=pl.ANY), whole(kn), whole(vn), whole(gate),
                      pl.BlockSpec(wov.shape, lambda i, pt: (0, 0))],
            out_specs=pl.BlockSpec((1, n_rows, HEAD_DIM), lambda i, pt: (i, 0, 0)),
            scratch_shapes=[pltpu.VMEM((2, C_KV_WIDTH, n_past), F32), pltpu.VMEM((2, C_KV_WIDTH, n_past), F32),
                            pltpu.VMEM((n_slc_pad, n_past), BF16),
                            pltpu.SemaphoreType.DMA((2,)), pltpu.SemaphoreType.DMA((2,))]),
        out_shape=jax.ShapeDtypeStruct((n, n_rows, HEAD_DIM), F32),
        compiler_params=_params("arbitrary"),
        name="nsa_decode",
    )(page_table, q, kcb, vcb, *pools, kn, vn, gate, wov)


def _odd_out_body(x_ref, *refs):
    *branch_refs, z_ref, w_ref, gf_ref, y_ref = refs
    o = branch_refs[0][...].astype(F32)
    for ref in branch_refs[1:]:
        o = o + ref[...].astype(F32)
    mixed = (o * _silu(z_ref[...].astype(F32))).astype(BF16)
    h = x_ref[...] + _dot(mixed, w_ref[...])
    y_ref[...] = h * lax.rsqrt(jnp.mean(h * h, axis=-1, keepdims=True) + RMS_EPS) * gf_ref[...]


def _odd_out(x2d, branches, z, w_out, final_gain, tm):
    m, d = x2d.shape
    row = pl.BlockSpec((tm, d), lambda i: (i, 0))
    return pl.pallas_call(
        _odd_out_body,
        grid=(m // tm,),
        in_specs=[row] * (len(branches) + 2) + [pl.BlockSpec(w_out.shape, lambda i: (0, 0)),
                                                 pl.BlockSpec((1, d), lambda i: (0, 0))],
        out_specs=row,
        out_shape=jax.ShapeDtypeStruct((m, d), F32),
        compiler_params=_params("parallel"),
        name="odd_out",
    )(x2d, *branches, z, w_out, final_gain.reshape(1, d))


def _even_outs(cache_from):
    a, bk = A_WIDTH, B_HEADS * B_DK
    outs = [(0, a, F32, None), (a, a, F32, None), (2 * a, a, F32, None)]
    if cache_from is not None:
        outs += [(a, a, F32, cache_from), (2 * a, a, F32, cache_from)]
    outs += [(3 * a, a, BF16, None),
             (4 * a, bk, F32, None), (4 * a + bk, bk, F32, None),
             (4 * a + 2 * bk, B_WIDTH, F32, None), (4 * a + 2 * bk + B_WIDTH, B_WIDTH, BF16, None)]
    return outs


def _from_feature_major(u, n_heads):
    b, _, length = u.shape
    return jnp.transpose(u.reshape(b, n_heads, HEAD_DIM, length), (0, 3, 1, 2))


def _to_feature_major(u):
    n, length, n_heads, hd = u.shape
    return jnp.transpose(u, (0, 2, 3, 1)).reshape(n, n_heads * hd, length)


def _even_layer_prompt(x, gain, w_in, w_out):
    b, t, d = x.shape
    x2d = x.reshape(b * t, d)
    keep = min(A_PATTERNS[-1][0], t)
    qa, ka, va, ka_t, va_t, za, qb, kb, vb, zb = _norm_proj(x2d, gain, w_in, _even_outs(t - keep), PROJ_ROWS, t)
    seq = lambda u: u.reshape(b, t, u.shape[-1])
    oa = _dilated_attention(seq(qa), seq(ka), seq(va), A_PATTERNS)
    chunk = RET_CHUNK if t % RET_CHUNK == 0 else t
    s0 = jnp.zeros((b, B_HEADS, B_DK, B_DV), F32)
    ob, s_fin = _retention(seq(qb), seq(kb), seq(vb), s0, jnp.arange(t), chunk=chunk, c_real=chunk)
    y = _even_out(x2d, oa.reshape(b * t, A_WIDTH), za, ob.reshape(b * t, B_WIDTH), zb, w_out, OUT_ROWS)
    return y.reshape(b, t, d), _from_feature_major(ka_t, A_HEADS), _from_feature_major(va_t, A_HEADS), s_fin


def _even_layer_sample(x, cache_k, cache_v, state, gain, w_in, w_out, past_len):
    n, s_len, d = x.shape
    x2d = x.reshape(n * s_len, d)
    qa, ka32, va32, za, qb, kb, vb, zb = _norm_proj(x2d, gain, w_in, _even_outs(None), n * s_len, n * s_len)
    seq = lambda u: u.reshape(n, s_len, u.shape[-1])
    oa = _decode_attention(qa.reshape(n, s_len * A_HEADS, HEAD_DIM), _to_feature_major(cache_k),
                           _to_feature_major(cache_v), seq(ka32), seq(va32), n_heads=A_HEADS, n_blk=A_HEADS,
                           patterns=A_PATTERNS)
    oa = oa.reshape(n * s_len, A_WIDTH)
    chunk = 16
    pad = lambda u: jnp.pad(seq(u), ((0, 0), (0, chunk - s_len), (0, 0)))
    pos = past_len + jnp.arange(chunk)
    ob, s_new = _retention(pad(qb), pad(kb), pad(vb), state, pos, chunk=chunk, c_real=s_len)
    ob = ob[:, :s_len].reshape(n * s_len, B_WIDTH)
    y = _even_out(x2d, oa, za, ob, zb, w_out, n * s_len)
    heads = lambda u: u.reshape(n, s_len, A_HEADS, HEAD_DIM)
    return y.reshape(n, s_len, d), heads(ka32), heads(va32), s_new


ODD_KV = ("kc", "vc", "ks", "vs", "kw", "vw")
ODD_GATE_COL = C_WIDTH + len(ODD_KV) * C_KV_WIDTH
ODD_Z_COL = ODD_GATE_COL + 3 * C_HEADS


def _odd_outs_prompt(win_from):
    outs = [(0, C_WIDTH, BF16, None)]
    for i, name in enumerate(ODD_KV):
        off = C_WIDTH + i * C_KV_WIDTH
        outs.append((off, C_KV_WIDTH, BF16, 0 if name in ("vs", "vw") else BLOCK_TAGGED if name == "ks" else None))
        outs.append((off, C_KV_WIDTH, F32, win_from if name in ("kw", "vw") else 0))
    outs += [(ODD_GATE_COL, 3 * C_HEADS, F32, 0), (ODD_Z_COL, C_WIDTH, BF16, None)]
    return outs


def _odd_outs_sample():
    outs = [(0, C_WIDTH, BF16, None)]
    outs += [(C_WIDTH + i * C_KV_WIDTH, C_KV_WIDTH, F32, None) for i in range(len(ODD_KV))]
    outs += [(ODD_GATE_COL, 3 * C_HEADS, F32, None), (ODD_Z_COL, C_WIDTH, BF16, None)]
    return outs


def _odd_layer_prompt(x, gain, w_in, w_out, cw_k, cw_v, final_gain):
    b, t, d = x.shape
    x2d = x.reshape(b * t, d)
    keep = min(C_WINDOW, t)
    (q, kc, kc_t, vc, vc_t, ks, ks_t, vs_tb, vs_t, kw, kw_t, vw_tb, vw_t, gl_t, z) = _norm_proj(
        x2d, gain, w_in, _odd_outs_prompt(t - keep), PROJ_ROWS, t)
    seq = lambda u: u.reshape(b, t, u.shape[-1])
    n_cmp = (t - CMP_LEN) // CMP_STRIDE + 1
    pieces = lambda u: u.reshape(b, t // CMP_STRIDE, CMP_STRIDE * C_KV_WIDTH)
    kcb = _compress(pieces(kc), cw_k)
    vcb = _compress(pieces(vc), cw_v)
    o = _nsa_prompt(seq(q), kcb, vcb, seq(ks), vs_tb, seq(kw), vw_tb, gl_t, n_cmp)
    y = _odd_out(x2d, [o.reshape(b * t, C_WIDTH)], z, w_out, final_gain, OUT_ROWS)
    kv = lambda u: _from_feature_major(u, C_KV_HEADS)
    return y.reshape(b, t, d), kv(kc_t), kv(vc_t), kv(ks_t), kv(vs_t), kv(kw_t), kv(vw_t)


def _odd_layer_sample(x, page_table, pool_kc, pool_vc, pool_ks, pool_vs, buf_kw, buf_vw, gain, w_in, w_out,
                      cw_k, cw_v, final_gain):
    n, s_len, d = x.shape
    x2d = x.reshape(n * s_len, d)
    q, kc, vc, ks32, vs32, kw32, vw32, gl, z = _norm_proj(x2d, gain, w_in, _odd_outs_sample(), n * s_len, n * s_len)
    seq = lambda u: u.reshape(n, s_len, u.shape[-1])
    n_past = page_table.shape[1] * PAGE_SIZE
    n_cmp = (n_past + s_len - CMP_LEN) // CMP_STRIDE + 1
    assert CMP_STRIDE * (n_cmp - 1) + CMP_LEN <= n_past
    kcb = _compress_paged(pool_kc, page_table, cw_k)
    vcb = _compress_paged(pool_vc, page_table, cw_v)
    rows = lambda u: u.astype(F32).reshape(n, s_len * C_HEADS, HEAD_DIM)
    gate = gl.reshape(n, s_len * C_HEADS, 3)
    o_cs = _nsa_decode(rows(q), kcb, vcb, pool_ks, pool_vs, page_table, seq(ks32), seq(vs32), gate, n_cmp)
    o_win = _decode_attention(rows(q), _to_feature_major(buf_kw), _to_feature_major(buf_vw),
                              seq(kw32), seq(vw32), n_heads=C_HEADS, n_blk=C_KV_HEADS,
                              patterns=((C_WINDOW, 1),), gate=gate, gate_col=2)
    y = _odd_out(x2d, [o_cs.reshape(n * s_len, C_WIDTH), o_win.reshape(n * s_len, C_WIDTH)], z, w_out, final_gain,
                 n * s_len)
    kv = lambda u: u.reshape(n, s_len, C_KV_HEADS, HEAD_DIM)
    return y.reshape(n, s_len, d), kv(kc), kv(vc), kv(ks32), kv(vs32), kv(kw32), kv(vw32)


def kernel(x_prompt, x_sample, cache_a_k, cache_a_v, state_ret, cache_c_kcmp, cache_c_vcmp, cache_c_ksel,
           cache_c_vsel, cache_c_kwin, cache_c_vwin, page_table, even_norm, even_w_in, even_w_out, odd_norm,
           odd_w_in, odd_w_out, cmp_pos_k, cmp_w1_k, cmp_w2_k, cmp_pos_v, cmp_w1_v, cmp_w2_v, final_norm):
    assert even_norm.shape[0] == 1 and odd_norm.shape[0] == 1
    past_len = page_table.shape[1] * PAGE_SIZE
    w_in_e = even_w_in[0].T.astype(BF16)
    w_out_e = even_w_out[0].astype(BF16)
    w_in_o = odd_w_in[0].T.astype(BF16)
    w_out_o = odd_w_out[0].astype(BF16)
    cw_k = _compress_weights(cmp_pos_k[0], cmp_w1_k[0], cmp_w2_k[0])
    cw_v = _compress_weights(cmp_pos_v[0], cmp_w1_v[0], cmp_w2_v[0])

    hp, ak_p, av_p, ret_p = _even_layer_prompt(x_prompt, even_norm[0], w_in_e, w_out_e)
    hs, ak_s, av_s, ret_s = _even_layer_sample(x_sample, cache_a_k[0], cache_a_v[0], state_ret[0], even_norm[0],
                                               w_in_e, w_out_e, past_len)
    yp, *rows_p = _odd_layer_prompt(hp, odd_norm[0], w_in_o, w_out_o, cw_k, cw_v, final_norm)
    ys, *rows_s = _odd_layer_sample(hs, page_table, cache_c_kcmp[0], cache_c_vcmp[0], cache_c_ksel[0],
                                    cache_c_vsel[0], cache_c_kwin[0], cache_c_vwin[0], odd_norm[0], w_in_o, w_out_o,
                                    cw_k, cw_v, final_norm)
    lead = lambda u: u[None]
    out = [yp, ys, lead(ak_p), lead(ak_s), lead(av_p), lead(av_s), lead(ret_p), lead(ret_s)]
    for rp, rs in zip(rows_p, rows_s):
        out += [lead(rp), lead(rs)]
    return tuple(out)
```
